```python
import math
import jax
import jax.numpy as jnp
from jax import lax
import numpy as np

D_MODEL = 2048
BATCH = 4
SEQ = 2048
DEPTH = 2
DEC_BATCH = 128
DEC_SEQ = 8
PAST_LEN = 16384
PAGE_SIZE = 128

N_META = 16
N_EVEN = (DEPTH + 1) // 2
N_ODD = DEPTH // 2
MIX_W = D_MODEL // 2
W_A = MIX_W
CONV_W = 31
S5_WIDTH = MIX_W
S5_GROUP = 16
S5_GROUPS = S5_WIDTH // S5_GROUP
S5_STATE = 64
W_C = MIX_W
M_HEADS = 4
M_DK = W_C // M_HEADS
M_DV = W_C // M_HEADS
M_CHUNK = 64
W_D = MIX_W
R_K = 64
R_HEADS = W_D // R_K
R_DECAY_LORA = 64
R_A_LORA = 64
R_SHIFT_SIZES = (W_D, W_D, W_D, R_DECAY_LORA, R_A_LORA)
R_SHIFT_COLS = sum(R_SHIFT_SIZES)
EVEN_SIZES = (W_A, W_A, W_A, S5_WIDTH, S5_WIDTH)
EVEN_IN = sum(EVEN_SIZES)
ODD_SIZES = (W_C, W_C, W_C, W_C, M_HEADS, M_HEADS, W_C, R_SHIFT_COLS, W_D)
ODD_IN = sum(ODD_SIZES)
LN_EPS = 1e-5
R_LN_EPS = 64e-5
ALPHA = (2 * DEPTH) ** 0.25
BETA = (8 * DEPTH) ** -0.25

kernel_name = 'hybrid_conv_s5_mlstm_rwkv7_step'


def _split(t, sizes):
    return jnp.split(t, np.cumsum(sizes)[:-1].tolist(), axis=-1)


def _ln(x, g, b, eps=LN_EPS):
    xf = x.astype(jnp.float32)
    mu = jnp.mean(xf, -1, keepdims=True)
    var = jnp.mean(jnp.square(xf - mu), -1, keepdims=True)
    return (xf - mu) * lax.rsqrt(var + eps) * g.astype(jnp.float32) + b.astype(jnp.float32)


def _head_norm(t, eps):
    mu = jnp.mean(t, -1, keepdims=True)
    var = jnp.mean(jnp.square(t - mu), -1, keepdims=True)
    return (t - mu) * lax.rsqrt(var + eps)


def _conformer_conv(u, g, buf, conv_w, conv_b, ln_g, ln_b, pw):
    h = u * jax.nn.sigmoid(g)
    hp = jnp.concatenate([buf.astype(h.dtype), h], axis=1)
    y = lax.conv_general_dilated(hp, conv_w.astype(h.dtype)[:, None, :], window_strides=(1,), padding='VALID',
                                 dimension_numbers=('NWC', 'WIO', 'NWC'), feature_group_count=W_A)
    y = y + conv_b.astype(h.dtype)
    y = jax.nn.silu(_ln(y, ln_g, ln_b)).astype(u.dtype)
    y = y @ pw.astype(u.dtype)
    return y, hp[:, -(CONV_W - 1):]


def _s5(u, x0_re, x0_im, lam_re, lam_im, log_dt, b_re, b_im, c_re, c_im, d, glu_w, glu_b):
    f32 = jnp.float32
    n, l, _ = u.shape
    uf = u.astype(f32).reshape(n, l, S5_GROUPS, S5_GROUP)
    dt = jnp.exp(log_dt.astype(f32))[:, None]
    lr = lam_re.astype(f32)
    li = lam_im.astype(f32)
    mag = jnp.exp(lr * dt)
    ar = mag * jnp.cos(li * dt)
    ai = mag * jnp.sin(li * dt)
    den = lr * lr + li * li
    qr = ((ar - 1.0) * lr + ai * li) / den
    qi = (ai * lr - (ar - 1.0) * li) / den
    br = b_re.astype(f32)
    bi = b_im.astype(f32)
    bbr = qr[..., None] * br - qi[..., None] * bi
    bbi = qr[..., None] * bi + qi[..., None] * br
    ur = jnp.einsum('nlgh,gph->nlgp', uf, bbr)
    ui = jnp.einsum('nlgh,gph->nlgp', uf, bbi)
    x0r = x0_re.astype(f32)
    x0i = x0_im.astype(f32)
    ur = ur.at[:, 0].add(ar * x0r - ai * x0i)
    ui = ui.at[:, 0].add(ar * x0i + ai * x0r)
    a_r = jnp.broadcast_to(ar, ur.shape)
    a_i = jnp.broadcast_to(ai, ui.shape)

    def combine(e1, e2):
        a1r, a1i, b1r, b1i = e1
        a2r, a2i, b2r, b2i = e2
        return (a1r * a2r - a1i * a2i, a1r * a2i + a1i * a2r,
                a2r * b1r - a2i * b1i + b2r, a2r * b1i + a2i * b1r + b2i)

    _, _, xr, xi = lax.associative_scan(combine, (a_r, a_i, ur, ui), axis=1)
    y = (jnp.einsum('ghp,nlgp->nlgh', c_re.astype(f32), xr)
         - jnp.einsum('ghp,nlgp->nlgh', c_im.astype(f32), xi))
    y = y.reshape(n, l, S5_WIDTH) + d.astype(f32) * u.astype(f32)
    y = jax.nn.gelu(y).astype(u.dtype)
    vg = y @ glu_w.astype(u.dtype) + glu_b.astype(u.dtype)
    v, gt = jnp.split(vg, 2, axis=-1)
    return v * jax.nn.sigmoid(gt), xr[:, -1], xi[:, -1]


def _mlstm_chunk(carry, inp):
    c, nv, m = carry
    q, k, v, ig, lf = inp
    t = q.shape[1]
    b = jnp.cumsum(lf, axis=1)
    m_t = b + jnp.maximum(m[:, None, :], lax.cummax(ig - b, axis=1))
    causal = jnp.tril(jnp.ones((t, t), dtype=bool))
    logd = b[:, :, None, :] - b[:, None, :, :] + ig[:, None, :, :] - m_t[:, :, None, :]
    dmat = jnp.exp(jnp.where(causal[None, :, :, None], logd, -jnp.inf))
    s = jnp.einsum('nthd,nshd->ntsh', q, k) * dmat
    h_intra = jnp.einsum('ntsh,nshd->nthd', s, v)
    n_intra = jnp.sum(s, axis=2)
    inter = jnp.exp(m[:, None, :] + b - m_t)
    h_inter = jnp.einsum('nthk,nhkv->nthv', q, c) * inter[..., None]
    n_inter = jnp.einsum('nthk,nhk->nth', q, nv) * inter
    denom = jnp.maximum(jnp.abs(n_intra + n_inter), jnp.exp(-m_t))
    h = (h_intra + h_inter) / denom[..., None]
    m_new = m_t[:, -1]
    b_end = b[:, -1]
    dec = jnp.exp(m + b_end - m_new)
    w_s = jnp.exp(b_end[:, None, :] - b + ig - m_new[:, None, :])
    c_new = c * dec[..., None, None] + jnp.einsum('nthk,nthv,nth->nhkv', k, v, w_s)
    n_new = nv * dec[..., None] + jnp.einsum('nthk,nth->nhk', k, w_s)
    return (c_new, n_new, m_new), h


def _mlstm(q, k, v, ig, lf, c0, n0, m0, lead):
    seqs = (q, k, v, ig, lf)
    state = (c0, n0, m0)
    hs = []
    if lead > 0:
        state, h_lead = _mlstm_chunk(state, tuple(a[:, :lead] for a in seqs))
        hs.append(h_lead)
    rest = q.shape[1] - lead
    t = M_CHUNK if rest % M_CHUNK == 0 else rest
    nc = rest // t

    def to_chunks(a):
        a = a[:, lead:]
        a = a.reshape((a.shape[0], nc, t) + a.shape[2:])
        return jnp.moveaxis(a, 1, 0)

    state, h_rest = lax.scan(_mlstm_chunk, state, tuple(to_chunks(a) for a in seqs))
    h_rest = jnp.moveaxis(h_rest, 0, 1)
    hs.append(h_rest.reshape((h_rest.shape[0], rest) + h_rest.shape[3:]))
    return jnp.concatenate(hs, axis=1), state


def _rwkv7(p, prev, s0, od, j):
    f32 = jnp.float32
    n, l, _ = p.shape
    pf = p.astype(f32)
    p_prev = jnp.concatenate([prev.astype(f32)[:, None, :], pf[:, :-1]], axis=1)
    pm = pf + (p_prev - pf) * od['mu'][j].astype(f32)
    r, k, v, w1, a1 = _split(pm, R_SHIFT_SIZES)
    w = -jax.nn.softplus(-(od['w0'][j].astype(f32) + jnp.tanh(w1) @ od['w2'][j].astype(f32))) - 0.5
    decay = jnp.exp(-jnp.exp(w))
    a = jax.nn.sigmoid(od['a0'][j].astype(f32) + a1 @ od['a2'][j].astype(f32))
    heads = lambda z: z.reshape(n, l, R_HEADS, R_K)
    kk = heads(k * od['kk'][j].astype(f32))
    kk = kk / jnp.maximum(jnp.sqrt(jnp.sum(kk * kk, -1, keepdims=True)), 1e-12)
    k = k * (1.0 + (a - 1.0) * od['ka'][j].astype(f32))
    r, k, v, decay, a = heads(r), heads(k), heads(v), heads(decay), heads(a)

    def step(s, inp):
        r_t, k_t, v_t, w_t, kk_t, a_t = inp
        sa = -jnp.einsum('nhvk,nhk->nhv', s, kk_t)
        s = (s * w_t[:, :, None, :] + sa[..., None] * (kk_t * a_t)[:, :, None, :]
             + v_t[..., None] * k_t[:, :, None, :])
        return s, jnp.einsum('nhvk,nhk->nhv', s, r_t)

    xs = tuple(jnp.moveaxis(z, 1, 0) for z in (r, k, v, decay, kk, a))
    s_fin, y = lax.scan(step, s0.astype(f32), xs)
    y = jnp.moveaxis(y, 0, 1)
    y = _head_norm(y, R_LN_EPS).reshape(n, l, W_D) * od['r_ln_g'][j].astype(f32) + od['r_ln_b'][j].astype(f32)
    bonus = jnp.sum(r * k * od['rk'][j].astype(f32), -1, keepdims=True) * v
    y = y + bonus.reshape(n, l, W_D)
    return y, s_fin, p[:, -1]


def _even_layer(x, conv_buf, s_re, s_im, ev, j):
    proj = x @ ev['w_in'][j].astype(x.dtype)
    u_a, g_a, z_a, u_b, z_b = _split(proj, EVEN_SIZES)
    y_a, new_buf = _conformer_conv(u_a, g_a, conv_buf, ev['conv_w'][j], ev['conv_b'][j],
                                   ev['a_ln_g'][j], ev['a_ln_b'][j], ev['pw'][j])
    y_b, new_re, new_im = _s5(u_b, s_re, s_im, ev['lam_re'][j], ev['lam_im'][j], ev['log_dt'][j],
                              ev['b_re'][j], ev['b_im'][j], ev['c_re'][j], ev['c_im'][j], ev['d'][j],
                              ev['glu_w'][j], ev['glu_b'][j])
    mix = jnp.concatenate([y_a * jax.nn.silu(z_a), y_b * jax.nn.silu(z_b)], axis=-1).astype(x.dtype)
    out = mix @ ev['w_out'][j].astype(x.dtype)
    x = _ln(ALPHA * x + out, ev['ln_g'][j], ev['ln_b'][j]).astype(x.dtype)
    return x, new_buf, new_re, new_im


def _odd_layer(x, mc, mn, mm, rs, rsh, od, j, lead):
    f32 = jnp.float32
    n, l, _ = x.shape
    proj = x @ od['w_in'][j].astype(x.dtype)
    q, k, v, o, ig, fg, z_c, p_d, z_d = _split(proj, ODD_SIZES)
    hd = lambda z, dd: z.astype(f32).reshape(n, l, M_HEADS, dd)
    qh = hd(q, M_DK) * (M_DK ** -0.5)
    kh = hd(k, M_DK)
    vh = hd(v, M_DV)
    igp = ig.astype(f32) + od['ig_b'][j].astype(f32)
    lf = jax.nn.log_sigmoid(fg.astype(f32) + od['fg_b'][j].astype(f32))
    h, (c_new, n_new, m_new) = _mlstm(qh, kh, vh, igp, lf, mc.astype(f32), mn.astype(f32), mm.astype(f32), lead)
    h = _head_norm(h, LN_EPS).reshape(n, l, W_C) * od['hn_g'][j].astype(f32)
    y_c = h * jax.nn.sigmoid(o.astype(f32)) * jax.nn.silu(z_c.astype(f32))
    y_d, s_new, shift_new = _rwkv7(p_d, rsh, rs, od, j)
    y_d = y_d * jax.nn.silu(z_d.astype(f32))
    mix = jnp.concatenate([y_c, y_d], axis=-1).astype(x.dtype)
    out = mix @ od['w_out'][j].astype(x.dtype)
    x = _ln(ALPHA * x + out, od['ln_g'][j], od['ln_b'][j]).astype(x.dtype)
    return x, c_new, n_new, m_new, s_new, shift_new


def _trunk(x, conv, ssm_re, ssm_im, mc, mn, mm, rs, rsh, lead, ev, od):
    ev_new = ([], [], [])
    od_new = ([], [], [], [], [])
    for layer in range(DEPTH):
        j = layer // 2
        if layer % 2 == 0:
            x, b_new, re_new, im_new = _even_layer(x, conv[j], ssm_re[j], ssm_im[j], ev, j)
            for lst, val in zip(ev_new, (b_new, re_new, im_new)):
                lst.append(val.astype(x.dtype))
        else:
            x, c_new, n_new, m_new, s_new, sh_new = _odd_layer(x, mc[j], mn[j], mm[j], rs[j], rsh[j], od, j, lead)
            for lst, val in zip(od_new, (c_new, n_new, m_new, s_new, sh_new)):
                lst.append(val.astype(x.dtype))
    st = [jnp.stack(lst, axis=0) for lst in ev_new + od_new]
    return x, st[0], st[1], st[2], st[3], st[4], st[5], st[6], st[7]


def setup_inputs(seed: int = 0) -> dict:
    key = jax.random.key(seed)
    it = iter(jax.random.split(key, 64))
    f32 = jnp.float32

    def nrm(shape, scale):
        return scale * jax.random.normal(next(it), shape, f32)

    def uni(shape, lo, hi):
        return jax.random.uniform(next(it), shape, f32, lo, hi)

    inp = {}
    inp['x_prompt'] = nrm((BATCH, SEQ, D_MODEL), 1.0)
    inp['x_sample'] = nrm((DEC_BATCH, DEC_SEQ, D_MODEL), 1.0)
    inp['state_conv'] = nrm((N_EVEN, DEC_BATCH, CONV_W - 1, W_A), 0.5)
    inp['state_ssm_re'] = nrm((N_EVEN, DEC_BATCH, S5_GROUPS, S5_STATE), 0.05)
    inp['state_ssm_im'] = nrm((N_EVEN, DEC_BATCH, S5_GROUPS, S5_STATE), 0.05)
    inp['state_mlstm_c'] = nrm((N_ODD, DEC_BATCH, M_HEADS, M_DK, M_DV), 0.05)
    inp['state_mlstm_n'] = nrm((N_ODD, DEC_BATCH, M_HEADS, M_DK), 0.05)
    inp['state_mlstm_m'] = nrm((N_ODD, DEC_BATCH, M_HEADS), 1.0)
    inp['state_rwkv_s'] = nrm((N_ODD, DEC_BATCH, R_HEADS, R_K, R_K), 0.1)
    inp['state_rwkv_shift'] = nrm((N_ODD, DEC_BATCH, R_SHIFT_COLS), 1.0)
    inp['meta_tokens'] = nrm((N_META, D_MODEL), 1.0)
    inp['ev_w_in'] = nrm((N_EVEN, D_MODEL, EVEN_IN), D_MODEL ** -0.5)
    inp['a_conv_w'] = nrm((N_EVEN, CONV_W, W_A), CONV_W ** -0.5)
    inp['a_conv_b'] = nrm((N_EVEN, W_A), 0.02)
    inp['a_ln_g'] = 1.0 + nrm((N_EVEN, W_A), 0.02)
    inp['a_ln_b'] = nrm((N_EVEN, W_A), 0.02)
    inp['a_pw'] = nrm((N_EVEN, W_A, W_A), W_A ** -0.5)
    inp['s5_lambda_re'] = -0.5 + nrm((N_EVEN, S5_GROUPS, S5_STATE), 0.01)
    inp['s5_lambda_im'] = (jnp.pi * jnp.arange(S5_STATE, dtype=f32))[None, None, :] + nrm((N_EVEN, S5_GROUPS, S5_STATE), 0.01)
    inp['s5_log_dt'] = uni((N_EVEN, S5_GROUPS), math.log(1e-3), math.log(1e-1))
    inp['s5_b_re'] = nrm((N_EVEN, S5_GROUPS, S5_STATE, S5_GROUP), (2 * S5_GROUP) ** -0.5)
    inp['s5_b_im'] = nrm((N_EVEN, S5_GROUPS, S5_STATE, S5_GROUP), (2 * S5_GROUP) ** -0.5)
    inp['s5_c_re'] = nrm((N_EVEN, S5_GROUPS, S5_GROUP, S5_STATE), S5_STATE ** -0.5)
    inp['s5_c_im'] = nrm((N_EVEN, S5_GROUPS, S5_GROUP, S5_STATE), S5_STATE ** -0.5)
    inp['s5_d'] = nrm((N_EVEN, S5_WIDTH), 1.0)
    inp['s5_glu_w'] = nrm((N_EVEN, S5_WIDTH, 2 * S5_WIDTH), S5_WIDTH ** -0.5)
    inp['s5_glu_b'] = nrm((N_EVEN, 2 * S5_WIDTH), 0.02)
    inp['ev_w_out'] = nrm((N_EVEN, W_A + S5_WIDTH, D_MODEL), BETA * (W_A + S5_WIDTH) ** -0.5)
    inp['ev_ln_g'] = 1.0 + nrm((N_EVEN, D_MODEL), 0.02)
    inp['ev_ln_b'] = nrm((N_EVEN, D_MODEL), 0.02)
    inp['od_w_in'] = nrm((N_ODD, D_MODEL, ODD_IN), D_MODEL ** -0.5)
    inp['m_ig_b'] = nrm((N_ODD, M_HEADS), 0.1)
    inp['m_fg_b'] = jnp.linspace(3.0, 6.0, M_HEADS, dtype=f32)[None, :] + nrm((N_ODD, M_HEADS), 0.01)
    inp['m_hn_g'] = 1.0 + nrm((N_ODD, W_C), 0.02)
    inp['r_mu'] = uni((N_ODD, R_SHIFT_COLS), 0.0, 1.0)
    w0_base = jnp.broadcast_to(jnp.linspace(-6.0, -1.0, R_K, dtype=f32)[None, :], (R_HEADS, R_K)).reshape(W_D)
    inp['r_w0'] = w0_base[None, :] + nrm((N_ODD, W_D), 0.1)
    inp['r_w2'] = nrm((N_ODD, R_DECAY_LORA, W_D), 0.5 * R_DECAY_LORA ** -0.5)
    inp['r_a0'] = nrm((N_ODD, W_D), 0.1)
    inp['r_a2'] = nrm((N_ODD, R_A_LORA, W_D), 0.5 * R_A_LORA ** -0.5)
    inp['r_kk'] = 0.85 + nrm((N_ODD, W_D), 0.02)
    inp['r_ka'] = 1.0 + nrm((N_ODD, W_D), 0.02)
    inp['r_rk'] = nrm((N_ODD, R_HEADS, R_K), 0.1)
    inp['r_ln_g'] = 1.0 + nrm((N_ODD, W_D), 0.02)
    inp['r_ln_b'] = nrm((N_ODD, W_D), 0.02)
    inp['od_w_out'] = nrm((N_ODD, W_C + W_D, D_MODEL), BETA * (W_C + W_D) ** -0.5)
    inp['od_ln_g'] = 1.0 + nrm((N_ODD, D_MODEL), 0.02)
    inp['od_ln_b'] = nrm((N_ODD, D_MODEL), 0.02)
    return inp


def reference(x_prompt, x_sample, state_conv, state_ssm_re, state_ssm_im, state_mlstm_c, state_mlstm_n,
              state_mlstm_m, state_rwkv_s, state_rwkv_shift, meta_tokens,
              ev_w_in, a_conv_w, a_conv_b, a_ln_g, a_ln_b, a_pw, s5_lambda_re, s5_lambda_im, s5_log_dt,
              s5_b_re, s5_b_im, s5_c_re, s5_c_im, s5_d, s5_glu_w, s5_glu_b, ev_w_out, ev_ln_g, ev_ln_b,
              od_w_in, m_ig_b, m_fg_b, m_hn_g, r_mu, r_w0, r_w2, r_a0, r_a2, r_kk, r_ka, r_rk,
              r_ln_g, r_ln_b, od_w_out, od_ln_g, od_ln_b):
    ev = {'w_in': ev_w_in, 'conv_w': a_conv_w, 'conv_b': a_conv_b, 'a_ln_g': a_ln_g, 'a_ln_b': a_ln_b,
          'pw': a_pw, 'lam_re': s5_lambda_re, 'lam_im': s5_lambda_im, 'log_dt': s5_log_dt,
          'b_re': s5_b_re, 'b_im': s5_b_im, 'c_re': s5_c_re, 'c_im': s5_c_im, 'd': s5_d,
          'glu_w': s5_glu_w, 'glu_b': s5_glu_b, 'w_out': ev_w_out, 'ln_g': ev_ln_g, 'ln_b': ev_ln_b}
    od = {'w_in': od_w_in, 'ig_b': m_ig_b, 'fg_b': m_fg_b, 'hn_g': m_hn_g, 'mu': r_mu, 'w0': r_w0,
          'w2': r_w2, 'a0': r_a0, 'a2': r_a2, 'kk': r_kk, 'ka': r_ka, 'rk': r_rk,
          'r_ln_g': r_ln_g, 'r_ln_b': r_ln_b, 'w_out': od_w_out, 'ln_g': od_ln_g, 'ln_b': od_ln_b}
    nb = x_prompt.shape[0]
    zf = lambda shape: jnp.zeros(shape, jnp.float32)
    meta = jnp.broadcast_to(meta_tokens.astype(x_prompt.dtype)[None], (nb, N_META, D_MODEL))
    xp = jnp.concatenate([meta, x_prompt], axis=1)
    (h_p, p_conv, p_sre, p_sim, p_mc, p_mn, p_mm, p_rs, p_rsh) = _trunk(
        xp, zf((N_EVEN, nb, CONV_W - 1, W_A)), zf((N_EVEN, nb, S5_GROUPS, S5_STATE)),
        zf((N_EVEN, nb, S5_GROUPS, S5_STATE)), zf((N_ODD, nb, M_HEADS, M_DK, M_DV)),
        zf((N_ODD, nb, M_HEADS, M_DK)), zf((N_ODD, nb, M_HEADS)), zf((N_ODD, nb, R_HEADS, R_K, R_K)),
        zf((N_ODD, nb, R_SHIFT_COLS)), N_META, ev, od)
    (y_sample, s_conv, s_sre, s_sim, s_mc, s_mn, s_mm, s_rs, s_rsh) = _trunk(
        x_sample, state_conv, state_ssm_re, state_ssm_im, state_mlstm_c, state_mlstm_n, state_mlstm_m,
        state_rwkv_s, state_rwkv_shift, 0, ev, od)
    y_prompt = h_p[:, N_META:]
    return (y_prompt, y_sample, p_conv, p_sre, p_sim, p_mc, p_mn, p_mm, p_rs, p_rsh,
            s_conv, s_sre, s_sim, s_mc, s_mn, s_mm, s_rs, s_rsh)
```

```python
import functools
import math

import jax
import jax.numpy as jnp
from jax import lax
from jax.experimental import pallas as pl
from jax.experimental.pallas import tpu as pltpu

F32 = jnp.float32
BF16 = jnp.bfloat16

D_MODEL = 2048
MIX_W = 1024
N_META = 16
CONV_W = 31
S5_GROUP = 16
S5_GROUPS = 64
S5_STATE = 64
M_HEADS = 4
M_D = 256
R_HEADS = 16
R_K = 64
LN_EPS = 1e-5
R_LN_EPS = 64e-5
DEPTH = 2
ALPHA = (2 * DEPTH) ** 0.25

BATCH = 4
SEQ = 2048
DEC_BATCH = 128
DEC_SEQ = 8
R_MAIN = BATCH * SEQ
R_SAMP0 = R_MAIN
R_SAMP = DEC_BATCH * DEC_SEQ
R_LEAD0 = R_SAMP0 + R_SAMP
R_LEAD = BATCH * N_META
R_ALL = R_LEAD0 + R_LEAD

P1_Q, P1_K, P1_V, P1_O, P1_ZC = 0, 1024, 2048, 3072, 4096
P1_R, P1_RK, P1_RV, P1_ZD, P1_WA, P1_GATE = 5120, 6144, 7168, 8192, 9216, 9344
P1_N = 9728

VMEM_LIMIT = 48 * 1024 * 1024


def _cp(sem):
    return pltpu.CompilerParams(dimension_semantics=sem, vmem_limit_bytes=VMEM_LIMIT)


def _dot(a, b):
    return jnp.dot(a, b, preferred_element_type=F32)


def _dot_nt(a, b):
    return lax.dot_general(a, b, (((1,), (1,)), ((), ())), preferred_element_type=F32)


def _dot_tn(a, b):
    return lax.dot_general(a, b, (((0,), (0,)), ((), ())), preferred_element_type=F32)


def _split2(x):
    hi = x.astype(BF16)
    lo = (x - hi.astype(F32)).astype(BF16)
    return hi, lo


def _split3(x):
    hi = x.astype(BF16)
    r1 = x - hi.astype(F32)
    mid = r1.astype(BF16)
    lo = (r1 - mid.astype(F32)).astype(BF16)
    return hi, mid, lo


def _dot3(a, b):
    ah, al = _split2(a)
    bh, bl = _split2(b)
    return _dot(ah, bh) + _dot(ah, bl) + _dot(al, bh)


def _sigmoid(x):
    return jax.nn.sigmoid(x)


def _silu(x):
    return x * jax.nn.sigmoid(x)


def _softplus(x):
    return jnp.maximum(x, 0.0) + jnp.log(1.0 + jnp.exp(-jnp.abs(x)))


def _gelu_tanh(x):
    c = math.sqrt(2.0 / math.pi)
    return x * (0.5 * (1.0 + jnp.tanh(c * (x + 0.044715 * (x * x * x)))))


def _iota(shape, axis):
    return lax.broadcasted_iota(jnp.int32, shape, axis)


def _shr(x, k):
    return lax.shift_right_logical(x, jnp.int32(k))


def _log2(n):
    k = int(round(math.log2(n)))
    assert 1 << k == n
    return k


def _block_ones(n, seg, dtype):
    r = _shr(_iota((n, n), 0), _log2(seg))
    c = _shr(_iota((n, n), 1), _log2(seg))
    return (r == c).astype(dtype)


def _segsum(x, seg):
    g = _block_ones(256, seg, BF16)
    outs = []
    for j in range(x.shape[1] // 256):
        hi, lo = _split2(x[:, 256 * j:256 * (j + 1)])
        outs.append(_dot(hi, g) + _dot(lo, g))
    return jnp.concatenate(outs, axis=1)


def _row_cumsum(x, period):
    rows = x.shape[0]
    rid = jnp.bitwise_and(_iota(x.shape, 0), period - 1)
    d = 1
    while d < min(period, rows):
        x = x + jnp.where(rid >= d, pltpu.roll(x, d, 0), 0.0)
        d *= 2
    return x


def _row_cummax(x, period):
    rows = x.shape[0]
    rid = jnp.bitwise_and(_iota(x.shape, 0), period - 1)
    d = 1
    while d < min(period, rows):
        x = jnp.maximum(x, jnp.where(rid >= d, pltpu.roll(x, d, 0), -jnp.inf))
        d *= 2
    return x


def _mm_kernel(x_ref, w_ref, o_ref):
    o_ref[...] = _dot(x_ref[...].astype(BF16), w_ref[...])


def _matmul(x, w, tm, tn):
    r, k = x.shape
    n = w.shape[1]
    return pl.pallas_call(
        _mm_kernel,
        out_shape=jax.ShapeDtypeStruct((r, n), F32),
        grid=(pl.cdiv(r, tm), n // tn),
        in_specs=[pl.BlockSpec((tm, k), lambda i, j: (i, 0)),
                  pl.BlockSpec((k, tn), lambda i, j: (0, j))],
        out_specs=pl.BlockSpec((tm, tn), lambda i, j: (i, j)),
        compiler_params=_cp(("parallel", "arbitrary")),
    )(x, w)


def _pw_kernel(a_ref, w_ref, z_ref, o_ref):
    o_ref[...] = _dot(a_ref[...].astype(BF16), w_ref[...]) * _silu(z_ref[...])


def _pw_gate(act, pw, proj0, tm=464, tn=512):
    r = act.shape[0]
    zb = 2048 // tn
    return pl.pallas_call(
        _pw_kernel,
        out_shape=jax.ShapeDtypeStruct((r, MIX_W), F32),
        grid=(pl.cdiv(r, tm), MIX_W // tn),
        in_specs=[pl.BlockSpec((tm, MIX_W), lambda i, j: (i, 0)),
                  pl.BlockSpec((MIX_W, tn), lambda i, j: (0, j)),
                  pl.BlockSpec((tm, tn), lambda i, j: (i, zb + j))],
        out_specs=pl.BlockSpec((tm, tn), lambda i, j: (i, j)),
        compiler_params=_cp(("parallel", "arbitrary")),
    )(act, pw, proj0)


def _glu_kernel(y_ref, wv_ref, wg_ref, bv_ref, bg_ref, z_ref, o_ref):
    y = y_ref[...].astype(BF16)
    v = _dot(y, wv_ref[...]) + bv_ref[...]
    g = _dot(y, wg_ref[...]) + bg_ref[...]
    o_ref[...] = v * _sigmoid(g) * _silu(z_ref[...])


def _glu_gate(yb, glu_w, glu_b, proj0, tm=464, tn=512):
    r = yb.shape[0]
    nb = MIX_W // tn
    zb = 4096 // tn
    return pl.pallas_call(
        _glu_kernel,
        out_shape=jax.ShapeDtypeStruct((r, MIX_W), F32),
        grid=(pl.cdiv(r, tm), nb),
        in_specs=[pl.BlockSpec((tm, MIX_W), lambda i, j: (i, 0)),
                  pl.BlockSpec((MIX_W, tn), lambda i, j: (0, j)),
                  pl.BlockSpec((MIX_W, tn), lambda i, j: (0, nb + j)),
                  pl.BlockSpec((1, tn), lambda i, j: (0, j)),
                  pl.BlockSpec((1, tn), lambda i, j: (0, nb + j)),
                  pl.BlockSpec((tm, tn), lambda i, j: (i, zb + j))],
        out_specs=pl.BlockSpec((tm, tn), lambda i, j: (i, j)),
        compiler_params=_cp(("parallel", "arbitrary")),
    )(yb, glu_w, glu_w, glu_b, glu_b, proj0)


def _out_ln_kernel(x_ref, ma_ref, mb_ref, wa_ref, wb_ref, g_ref, b_ref, o_ref):
    out = _dot(ma_ref[...].astype(BF16), wa_ref[...]) + _dot(mb_ref[...].astype(BF16), wb_ref[...])
    y = ALPHA * x_ref[...] + out
    mu = jnp.mean(y, axis=-1, keepdims=True)
    yc = y - mu
    var = jnp.mean(yc * yc, axis=-1, keepdims=True)
    o_ref[...] = yc * lax.rsqrt(var + LN_EPS) * g_ref[...] + b_ref[...]


def _out_ln(x, mix_a, mix_b, w_out, ln_g, ln_b, tm=232):
    r = x.shape[0]
    return pl.pallas_call(
        _out_ln_kernel,
        out_shape=jax.ShapeDtypeStruct((r, D_MODEL), F32),
        grid=(pl.cdiv(r, tm),),
        in_specs=[pl.BlockSpec((tm, D_MODEL), lambda i: (i, 0)),
                  pl.BlockSpec((tm, MIX_W), lambda i: (i, 0)),
                  pl.BlockSpec((tm, MIX_W), lambda i: (i, 0)),
                  pl.BlockSpec((MIX_W, D_MODEL), lambda i: (0, 0)),
                  pl.BlockSpec((MIX_W, D_MODEL), lambda i: (1, 0)),
                  pl.BlockSpec((1, D_MODEL), lambda i: (0, 0)),
                  pl.BlockSpec((1, D_MODEL), lambda i: (0, 0))],
        out_specs=pl.BlockSpec((tm, D_MODEL), lambda i: (i, 0)),
        compiler_params=_cp(("parallel",)),
    )(x, mix_a, mix_b, w_out, w_out, ln_g, ln_b)


def _conv_kernel(*refs, NB, TL, T, chained):
    if chained:
        refs = refs[1:]
    (u_ref, g_ref, st_ref, w_ref, cb_ref, lg_ref, lb_ref, act_ref, nst_ref, hp_ref, hs_ref) = refs
    t = pl.program_id(1)
    rc = 16 if TL % 16 == 0 else 8
    for nb in range(NB):
        base = nb * TL

        @pl.when(t == 0)
        def _():
            hp_ref[nb, 0:2, :] = jnp.zeros((2, MIX_W), F32)
            hp_ref[nb, 2:32, :] = st_ref[nb]

        hp_ref[nb, TL + 32:TL + 40, :] = jnp.zeros((8, MIX_W), F32)
        hp_ref[nb, 32:32 + TL, :] = u_ref[base:base + TL, :] * _sigmoid(g_ref[base:base + TL, :])
        for b in range(8):
            hs_ref[b] = hp_ref[nb, b:b + TL + 32, :]

        def chunk(c, carry):
            r0 = pl.multiple_of(c * rc, rc)
            acc = jnp.zeros((rc, MIX_W), F32)
            for j in range(CONV_W):
                o = j + 2
                acc = acc + w_ref[j:j + 1, :] * hs_ref[o % 8, pl.ds(r0 + 8 * (o // 8), rc), :]
            y = acc + cb_ref[...]
            mu = jnp.mean(y, axis=-1, keepdims=True)
            yc = y - mu
            var = jnp.mean(yc * yc, axis=-1, keepdims=True)
            yn = yc * lax.rsqrt(var + LN_EPS) * lg_ref[...] + lb_ref[...]
            act_ref[pl.ds(base + r0, rc), :] = _silu(yn)
            return carry

        lax.fori_loop(0, TL // rc, chunk, 0)

        @pl.when(t == T - 1)
        def _():
            nst_ref[nb] = hp_ref[nb, TL + 2:TL + 32, :]

        if T > 1:
            hp_ref[nb, 0:32, :] = hp_ref[nb, TL:TL + 32, :]


def _conv_call(prev_act, proj0, state, conv_w, conv_b, ln_g, ln_b, *, row0, N, L, NB, TL):
    T = L // TL
    RB = NB * TL
    assert NB == 1 or T == 1
    rb0 = row0 // RB
    assert rb0 * RB == row0
    chained = prev_act is not None
    in_specs = [pl.BlockSpec((RB, MIX_W), lambda i, t: (rb0 + i * T + t, 0)),
                pl.BlockSpec((RB, MIX_W), lambda i, t: (rb0 + i * T + t, 1)),
                pl.BlockSpec((NB, CONV_W - 1, MIX_W), lambda i, t: (i, 0, 0)),
                pl.BlockSpec((CONV_W, MIX_W), lambda i, t: (0, 0)),
                pl.BlockSpec((1, MIX_W), lambda i, t: (0, 0)),
                pl.BlockSpec((1, MIX_W), lambda i, t: (0, 0)),
                pl.BlockSpec((1, MIX_W), lambda i, t: (0, 0))]
    args = [proj0, proj0, state, conv_w, conv_b, ln_g, ln_b]
    aliases = {}
    if chained:
        in_specs = [pl.BlockSpec(memory_space=pl.ANY)] + in_specs
        args = [prev_act] + args
        aliases = {0: 0}
    return pl.pallas_call(
        functools.partial(_conv_kernel, NB=NB, TL=TL, T=T, chained=chained),
        out_shape=(jax.ShapeDtypeStruct((R_ALL, MIX_W), F32),
                   jax.ShapeDtypeStruct((N, CONV_W - 1, MIX_W), F32)),
        grid=(N // NB, T),
        in_specs=in_specs,
        out_specs=(pl.BlockSpec((RB, MIX_W), lambda i, t: (rb0 + i * T + t, 0)),
                   pl.BlockSpec((NB, CONV_W - 1, MIX_W), lambda i, t: (i, 0, 0))),
        scratch_shapes=[pltpu.VMEM((NB, TL + 40, MIX_W), F32),
                        pltpu.VMEM((8, TL + 32, MIX_W), F32)],
        input_output_aliases=aliases,
        compiler_params=_cp(("arbitrary", "arbitrary")),
    )(*args)


def _s5_kernel(*refs, NB, TL, T, chained):
    if chained:
        refs = refs[1:]
    (u_ref, wb_ref, wc_ref, d_ref, ar_ref, ai_ref, x0r_ref, x0i_ref,
     y_ref, xfr_ref, xfi_ref, xs_ref, cr_ref, ci_ref) = refs
    t = pl.program_id(2)
    RB = NB * TL
    GL = TL // 8
    u = u_ref[...]
    big = _dot(u.astype(BF16), wb_ref[0])
    xr = big[:, :1024]
    xi = big[:, 1024:]
    ar = ar_ref[...]
    ai = ai_ref[...]

    def cmul(pr, pi, qr, qi):
        return pr * qr - pi * qi, pr * qi + pi * qr

    a1 = (ar, ai)
    a2 = cmul(*a1, *a1)
    a4 = cmul(*a2, *a2)
    rid = jnp.bitwise_and(_iota((RB, 1024), 0), 7)
    for d, (pr, pi) in ((1, a1), (2, a2), (4, a4)):
        sr = pltpu.roll(xr, d, 0)
        si = pltpu.roll(xi, d, 0)
        m = rid >= d
        xr, xi = (xr + jnp.where(m, pr * sr - pi * si, 0.0),
                  xi + jnp.where(m, pr * si + pi * sr, 0.0))
    xs_ref[0] = xr
    xs_ref[1] = xi
    a3 = cmul(*a2, *a1)
    a5 = cmul(*a4, *a1)
    a6 = cmul(*a4, *a2)
    a7 = cmul(*a6, *a1)
    a8 = cmul(*a4, *a4)
    r8 = _iota((8, 1024), 0)
    pwr = jnp.zeros((8, 1024), F32)
    pwi = jnp.zeros((8, 1024), F32)
    for k, (pr, pi) in enumerate((a1, a2, a3, a4, a5, a6, a7, a8)):
        pwr = jnp.where(r8 == k, pr, pwr)
        pwi = jnp.where(r8 == k, pi, pwi)

    first = t == 0

    def seq_body(nb, carry0):
        x0r = x0r_ref[nb]
        x0i = x0i_ref[nb]
        if T > 1:
            c_r = jnp.where(first, x0r, cr_ref[0:1, :])
            c_i = jnp.where(first, x0i, ci_ref[0:1, :])
        else:
            c_r, c_i = x0r, x0i

        def grp(g, c):
            c_r, c_i = c
            off = pl.multiple_of(nb * TL + g * 8, 8)
            vr = xs_ref[0, pl.ds(off, 8), :]
            vi = xs_ref[1, pl.ds(off, 8), :]
            br = jnp.broadcast_to(c_r, (8, 1024))
            bi = jnp.broadcast_to(c_i, (8, 1024))
            nr = vr + pwr * br - pwi * bi
            ni = vi + pwr * bi + pwi * br
            xs_ref[0, pl.ds(off, 8), :] = nr
            xs_ref[1, pl.ds(off, 8), :] = ni
            return nr[7:8, :], ni[7:8, :]

        c_r, c_i = lax.fori_loop(0, GL, grp, (c_r, c_i))
        if T > 1:
            cr_ref[...] = jnp.broadcast_to(c_r, (8, 1024))
            ci_ref[...] = jnp.broadcast_to(c_i, (8, 1024))

        @pl.when(t == T - 1)
        def _():
            xfr_ref[nb] = c_r
            xfi_ref[nb] = c_i

        return carry0

    lax.fori_loop(0, NB, seq_body, 0)
    y = (_dot(xs_ref[0].astype(BF16), wc_ref[0, 0:1024, :])
         + _dot(xs_ref[1].astype(BF16), wc_ref[0, 1024:2048, :]))
    y_ref[...] = _gelu_tanh(y + d_ref[...] * u)


def _s5_call(prev_y, proj0, wb, wc, dvec, ar, ai, x0r, x0i, *, row0, N, L, NB, TL):
    T = L // TL
    RB = NB * TL
    assert NB == 1 or T == 1
    rb0 = row0 // RB
    assert rb0 * RB == row0
    chained = prev_y is not None
    ub = 3072 // 256
    in_specs = [pl.BlockSpec((RB, 256), lambda i, j, t: (rb0 + i * T + t, ub + j)),
                pl.BlockSpec((1, 256, 2048), lambda i, j, t: (j, 0, 0)),
                pl.BlockSpec((1, 2048, 256), lambda i, j, t: (j, 0, 0)),
                pl.BlockSpec((1, 256), lambda i, j, t: (0, j)),
                pl.BlockSpec((1, 1024), lambda i, j, t: (0, j)),
                pl.BlockSpec((1, 1024), lambda i, j, t: (0, j)),
                pl.BlockSpec((NB, 1, 1024), lambda i, j, t: (i, 0, j)),
                pl.BlockSpec((NB, 1, 1024), lambda i, j, t: (i, 0, j))]
    args = [proj0, wb, wc, dvec, ar, ai, x0r, x0i]
    aliases = {}
    if chained:
        in_specs = [pl.BlockSpec(memory_space=pl.ANY)] + in_specs
        args = [prev_y] + args
        aliases = {0: 0}
    st = jax.ShapeDtypeStruct((N, 1, 4096), F32)
    return pl.pallas_call(
        functools.partial(_s5_kernel, NB=NB, TL=TL, T=T, chained=chained),
        out_shape=(jax.ShapeDtypeStruct((R_ALL, MIX_W), F32), st, st),
        grid=(N // NB, 4, T),
        in_specs=in_specs,
        out_specs=(pl.BlockSpec((RB, 256), lambda i, j, t: (rb0 + i * T + t, j)),
                   pl.BlockSpec((NB, 1, 1024), lambda i, j, t: (i, 0, j)),
                   pl.BlockSpec((NB, 1, 1024), lambda i, j, t: (i, 0, j))),
        scratch_shapes=[pltpu.VMEM((2, RB, 1024), F32),
                        pltpu.VMEM((8, 1024), F32),
                        pltpu.VMEM((8, 1024), F32)],
        input_output_aliases=aliases,
        compiler_params=_cp(("arbitrary", "arbitrary", "arbitrary")),
    )(*args)


def _mlstm_kernel(*refs, TL, T, chained):
    if chained:
        refs = refs[1:]
    (q_ref, k_ref, v_ref, o_ref, z_ref, gt_ref, gb_ref, hg_ref, c0_ref, n0_ref, m0_ref,
     y_ref, c_ref, n_ref, m_ref, cs_ref, ns_ref, ms_ref) = refs
    t = pl.program_id(1)

    @pl.when(t == 0)
    def _():
        cs_ref[...] = c0_ref[0]
        ns_ref[...] = n0_ref[0]
        ms_ref[...] = m0_ref[0]

    G = gt_ref[...] + gb_ref[...]
    B = _row_cumsum(-_softplus(-G), TL)
    Bs = pltpu.roll(B, 124, 1)
    A = G - Bs
    CM = _row_cummax(A, TL)
    ms = ms_ref[...]
    dg = _iota((8, 128), 0) == _iota((8, 128), 1)
    mrow = jnp.sum(jnp.where(dg, ms, 0.0), axis=0, keepdims=True)
    M = jnp.maximum(mrow, CM)
    MT = Bs + M
    sel = dg.astype(BF16)
    a_hi, a_mid, a_lo = _split3(A)
    Arow = _dot_nt(sel, a_hi) + _dot_nt(sel, a_mid) + _dot_nt(sel, a_lo)
    causal = _iota((TL, TL), 0) >= _iota((TL, TL), 1)
    for h in range(M_HEADS):
        sl = slice(M_D * h, M_D * (h + 1))
        q = (q_ref[:, sl] * (M_D ** -0.5))
        qb = q.astype(BF16)
        kf = k_ref[:, sl]
        kb = kf.astype(BF16)
        vf = v_ref[:, sl]
        m_col = M[:, h:h + 1]
        mt_col = MT[:, h:h + 1]
        b_col = Bs[:, h:h + 1]
        ig_col = G[:, h:h + 1]
        logd = Arow[h:h + 1, :] - m_col
        dm = jnp.exp(jnp.where(causal, logd, -jnp.inf))
        s = _dot_nt(qb, kb) * dm
        h_intra = _dot(s.astype(BF16), vf.astype(BF16))
        n_intra = jnp.sum(s, axis=1, keepdims=True)
        m_prev = mrow[:, h:h + 1]
        inter = jnp.exp(m_prev - m_col)
        c_old = cs_ref[h]
        n_old = ns_ref[h:h + 1, :]
        h_inter = _dot(qb, c_old.astype(BF16)) * inter
        n_inter = jnp.sum(q * n_old, axis=1, keepdims=True) * inter
        denom = jnp.maximum(jnp.abs(n_intra + n_inter), jnp.exp(-mt_col))
        hh = (h_intra + h_inter) / denom
        m_new = mt_col[TL - 1:TL, :]
        b_end = b_col[TL - 1:TL, :]
        dec = jnp.exp(m_prev + b_end - m_new)
        w_s = jnp.exp(b_end - b_col + ig_col - m_new)
        cs_ref[h] = c_old * dec + _dot_tn(kb, (vf * w_s).astype(BF16))
        ns_ref[h:h + 1, :] = n_old * dec + jnp.sum(kf * w_s, axis=0, keepdims=True)
        ms_ref[h:h + 1, :] = jnp.broadcast_to(m_new, (1, 128))
        mu = jnp.mean(hh, axis=-1, keepdims=True)
        hc = hh - mu
        var = jnp.mean(hc * hc, axis=-1, keepdims=True)
        hn = hc * lax.rsqrt(var + LN_EPS) * hg_ref[:, sl]
        y_ref[:, sl] = hn * _sigmoid(o_ref[:, sl]) * _silu(z_ref[:, sl])

    @pl.when(t == T - 1)
    def _():
        c_ref[0] = cs_ref[...]
        n_ref[0] = ns_ref[...]
        m_ref[0] = ms_ref[...]


def _mlstm_call(prev_y, proj1, gate_b, hn_g, c0, n0, m0, *, row0, N, L, TL):
    T = L // TL
    rb0 = row0 // TL
    assert rb0 * TL == row0
    chained = prev_y is not None

    def col(cb):
        return pl.BlockSpec((TL, MIX_W), lambda i, t: (rb0 + i * T + t, cb))

    in_specs = [col(0), col(1), col(2), col(3), col(4),
                pl.BlockSpec((TL, 128), lambda i, t: (rb0 + i * T + t, P1_GATE // 128)),
                pl.BlockSpec((1, 128), lambda i, t: (0, 0)),
                pl.BlockSpec((1, MIX_W), lambda i, t: (0, 0)),
                pl.BlockSpec((1, M_HEADS, M_D, M_D), lambda i, t: (i, 0, 0, 0)),
                pl.BlockSpec((1, M_HEADS, M_D), lambda i, t: (i, 0, 0)),
                pl.BlockSpec((1, 8, 128), lambda i, t: (i, 0, 0))]
    args = [proj1] * 6 + [gate_b, hn_g, c0, n0, m0]
    aliases = {}
    if chained:
        in_specs = [pl.BlockSpec(memory_space=pl.ANY)] + in_specs
        args = [prev_y] + args
        aliases = {0: 0}
    return pl.pallas_call(
        functools.partial(_mlstm_kernel, TL=TL, T=T, chained=chained),
        out_shape=(jax.ShapeDtypeStruct((R_ALL, MIX_W), F32),
                   jax.ShapeDtypeStruct((N, M_HEADS, M_D, M_D), F32),
                   jax.ShapeDtypeStruct((N, M_HEADS, M_D), F32),
                   jax.ShapeDtypeStruct((N, 8, 128), F32)),
        grid=(N, T),
        in_specs=in_specs,
        out_specs=(pl.BlockSpec((TL, MIX_W), lambda i, t: (rb0 + i * T + t, 0)),
                   pl.BlockSpec((1, M_HEADS, M_D, M_D), lambda i, t: (i, 0, 0, 0)),
                   pl.BlockSpec((1, M_HEADS, M_D), lambda i, t: (i, 0, 0)),
                   pl.BlockSpec((1, 8, 128), lambda i, t: (i, 0, 0))),
        scratch_shapes=[pltpu.VMEM((M_HEADS, M_D, M_D), F32),
                        pltpu.VMEM((M_HEADS, M_D), F32),
                        pltpu.VMEM((8, 128), F32)],
        input_output_aliases=aliases,
        compiler_params=_cp(("arbitrary", "arbitrary")),
    )(*args)


N_RA_OUT = 9


def _rwkv_a_kernel(*refs, RB, CT, short, chained):
    if chained:
        refs = refs[N_RA_OUT:]
    (pr_ref, pk_ref, pv_ref, pwa_ref, hr_ref, hk_ref, hv_ref, hwa_ref, st_ref, stwa_ref,
     mu_ref, muwa_ref, w0_ref, w2_ref, a0_ref, a2_ref, kkp_ref, ka_ref, rk_ref,
     ah_ref, rh_ref, bh_ref, kh_ref, vo_ref, ul_ref, yl_ref, dc_ref, bo_ref,
     s_at, s_rt, s_bt, s_kt, s_v, s_cum) = refs
    HS = R_HEADS * CT
    NCH = RB // CT
    rid = _iota((RB, 1), 0)
    if short:
        is_first = jnp.bitwise_and(rid, CT - 1) == 0
    else:
        is_first = rid == 0
        t0 = pl.program_id(1) == 0

    def shifted(p_ref, h_ref, s_ref, lo, hi, mu):
        p = p_ref[...]
        prev = pltpu.roll(p, 1, 0)
        if short:
            fp = s_ref[:, lo:hi]
        else:
            fp = jnp.where(t0, s_ref[0, :, lo:hi], h_ref[7:8, :])
        prev = jnp.where(is_first, fp, prev)
        return p + (prev - p) * mu

    r = shifted(pr_ref, hr_ref, st_ref, 0, 1024, mu_ref[:, 0:1024])
    k = shifted(pk_ref, hk_ref, st_ref, 1024, 2048, mu_ref[:, 1024:2048])
    v = shifted(pv_ref, hv_ref, st_ref, 2048, 3072, mu_ref[:, 2048:3072])
    wa = shifted(pwa_ref, hwa_ref, stwa_ref, 0, 128, muwa_ref[...])
    w = -_softplus(-(w0_ref[...] + _dot(jnp.tanh(wa).astype(BF16), w2_ref[...]))) - 0.5
    wlog = -jnp.exp(w)
    a = _sigmoid(a0_ref[...] + _dot(wa.astype(BF16), a2_ref[...]))
    kk = k * kkp_ref[...]
    kk = kk / jnp.maximum(jnp.sqrt(_segsum(kk * kk, R_K)), 1e-12)
    kmod = k * (1.0 + (a - 1.0) * ka_ref[...])
    bo_ref[...] = _segsum(r * kmod * rk_ref[...], R_K) * v
    cum = _row_cumsum(wlog, CT)
    einv = jnp.exp(-cum)
    s_at[...] = (-kk) * jnp.exp(cum - wlog)
    s_rt[...] = r * jnp.exp(cum)
    s_bt[...] = kk * a * einv
    s_kt[...] = kmod * einv
    s_v[...] = v
    s_cum[...] = cum

    be_mask = (_shr(_iota((HS, 1024), 0), _log2(CT)) == _shr(_iota((HS, 1024), 1), _log2(R_K))).astype(F32)
    bd_mask = (_shr(_iota((HS, HS), 0), _log2(CT)) == _shr(_iota((HS, HS), 1), _log2(CT))).astype(F32)
    tt = _iota((CT, HS), 0)
    ss = jnp.bitwise_and(_iota((CT, HS), 1), CT - 1)
    strict = tt > ss
    incl = tt >= ss
    eye_c = (tt == ss).astype(F32)

    def blockexp(x):
        return (jnp.concatenate([x] * R_HEADS, axis=0) * be_mask).astype(BF16)

    def bdiag(x):
        return jnp.concatenate([x] * R_HEADS, axis=0) * bd_mask

    def chunk(c, carry):
        r0 = pl.multiple_of(c * CT, CT)
        rows = pl.ds(r0, CT)
        at = s_at[rows, :]
        rt = s_rt[rows, :]
        bt = s_bt[rows, :]
        kt = s_kt[rows, :]
        vv = s_v[rows, :]
        cm = s_cum[rows, :]
        lhs = jnp.concatenate([at, rt], axis=0).astype(BF16)
        rhs_t = jnp.concatenate([blockexp(bt), blockexp(kt)], axis=0)
        amat = _dot_nt(lhs, rhs_t)
        a_ab = jnp.where(strict, amat[:CT, :HS], 0.0)
        a_ak = jnp.where(strict, amat[:CT, HS:], 0.0)
        a_rb = jnp.where(incl, amat[CT:, :HS], 0.0)
        a_rk = jnp.where(incl, amat[CT:, HS:], 0.0)
        p = eye_c + a_ab
        x = a_ab
        d = 1
        while 2 * d < CT:
            x = _dot3(x, bdiag(x))
            p = p + _dot3(p, bdiag(x))
            d *= 2
        tinv = p
        bd_ak = bdiag(a_ak).astype(BF16)
        ta = _dot(tinv.astype(BF16), bd_ak)
        art = _dot(a_rb.astype(BF16), bdiag(tinv).astype(BF16))
        yc = _dot(art.astype(BF16), bd_ak) + a_rk
        o1 = _dot(jnp.concatenate([tinv, art], axis=0).astype(BF16), blockexp(at))
        o2 = _dot(jnp.concatenate([ta, yc], axis=0).astype(BF16), blockexp(vv))
        ect = jnp.exp(cm[CT - 1:CT, :])
        ah_ref[rows, :] = o1[:CT]
        rh_ref[rows, :] = rt + o1[CT:]
        ul_ref[rows, :] = o2[:CT]
        yl_ref[rows, :] = o2[CT:]
        bh_ref[rows, :] = bt * ect
        kh_ref[rows, :] = kt * ect
        vo_ref[rows, :] = vv
        dc_ref[rows, :] = jnp.broadcast_to(ect, (CT, 1024))
        return carry

    lax.fori_loop(0, NCH, chunk, 0)


def _rwkv_a_call(prev, proj1, st_rkv, st_wa, wts, *, row0, N, L, RB, CT, short):
    chained = prev is not None
    rb0 = row0 // RB
    assert rb0 * RB == row0
    if short:
        assert L == CT
        grid = (N * L // RB, 1)
        T = 1
    else:
        T = L // RB
        grid = (N, T)

    def rowblk(i, t):
        return rb0 + i * T + t

    def col(cb, width=MIX_W):
        return pl.BlockSpec((RB, width), lambda i, t: (rowblk(i, t), cb))

    def halo(cb, width=MIX_W):
        return pl.BlockSpec((8, width),
                            lambda i, t: (jnp.maximum(rowblk(i, t) * (RB // 8) - 1, 0), cb))

    if short:
        st_specs = [pl.BlockSpec((RB, 3072), lambda i, t: (i, 0)),
                    pl.BlockSpec((RB, 128), lambda i, t: (i, 0))]
    else:
        st_specs = [pl.BlockSpec((1, 1, 3072), lambda i, t: (i, 0, 0)),
                    pl.BlockSpec((1, 1, 128), lambda i, t: (i, 0, 0))]

    def full(shape):
        return pl.BlockSpec(shape, lambda i, t: (0,) * len(shape))

    in_specs = ([col(P1_R // 1024), col(P1_RK // 1024), col(P1_RV // 1024), col(P1_WA // 128, 128),
                 halo(P1_R // 1024), halo(P1_RK // 1024), halo(P1_RV // 1024), halo(P1_WA // 128, 128)]
                + st_specs
                + [full((1, 3072)), full((1, 128)), full((1, 1024)), full((128, 1024)), full((1, 1024)),
                   full((128, 1024)), full((1, 1024)), full((1, 1024)), full((1, 1024))])
    args = [proj1] * 8 + [st_rkv, st_wa] + list(wts)
    aliases = {}
    if chained:
        in_specs = [pl.BlockSpec(memory_space=pl.ANY)] * N_RA_OUT + in_specs
        args = list(prev) + args
        aliases = {i: i for i in range(N_RA_OUT)}
    o_spec = pl.BlockSpec((RB, MIX_W), lambda i, t: (rowblk(i, t), 0))
    o_shape = jax.ShapeDtypeStruct((R_ALL, MIX_W), F32)
    return pl.pallas_call(
        functools.partial(_rwkv_a_kernel, RB=RB, CT=CT, short=short, chained=chained),
        out_shape=(o_shape,) * N_RA_OUT,
        grid=grid,
        in_specs=in_specs,
        out_specs=(o_spec,) * N_RA_OUT,
        scratch_shapes=[pltpu.VMEM((RB, MIX_W), F32)] * 6,
        input_output_aliases=aliases,
        compiler_params=_cp(("arbitrary", "arbitrary")),
    )(*args)


def _rwkv_b_kernel(*refs, RB, CT, T, short, chained):
    if chained:
        refs = refs[1:]
    (ah_ref, rh_ref, bh_ref, kh_ref, v_ref, ul_ref, yl_ref, dc_ref, bo_ref, z_ref, lg_ref, lb_ref,
     s0_ref, y_ref, so_ref, sbd_ref, yb_ref) = refs
    NCH = RB // CT
    t = pl.program_id(1)
    bd_mask = (_shr(_iota((256, 256), 0), 6) == _shr(_iota((256, 256), 1), 6)).astype(F32)
    e_tile = (_iota((64, 256), 0) == jnp.bitwise_and(_iota((64, 256), 1), 63)).astype(BF16)
    e_fold = (jnp.bitwise_and(_iota((256, 64), 0), 63) == _iota((256, 64), 1)).astype(BF16)

    def load_state(src):
        for j in range(4):
            hi, mid, lo = _split3(src[256 * j:256 * (j + 1), :])
            full = _dot(hi, e_tile) + _dot(mid, e_tile) + _dot(lo, e_tile)
            sbd_ref[j] = full * bd_mask

    def store_state(dst_store):
        outs = []
        for j in range(4):
            hi, mid, lo = _split3(sbd_ref[j])
            outs.append(_dot(hi, e_fold) + _dot(mid, e_fold) + _dot(lo, e_fold))
        dst_store(jnp.concatenate(outs, axis=0))

    def chunk_step(r0):
        rows = pl.ds(r0, CT)
        for j in range(4):
            cs = slice(256 * j, 256 * (j + 1))
            sb = sbd_ref[j]
            lhs = jnp.concatenate([ah_ref[rows, cs], rh_ref[rows, cs]], axis=0).astype(BF16)
            o = _dot_nt(lhs, sb.astype(BF16))
            vv = v_ref[rows, cs]
            ut = o[:CT] + ul_ref[rows, cs]
            yb_ref[rows, cs] = o[CT:] + yl_ref[rows, cs]
            uv = jnp.concatenate([ut, vv], axis=0).astype(BF16)
            bk = jnp.concatenate([bh_ref[rows, cs], kh_ref[rows, cs]], axis=0).astype(BF16)
            upd = _dot_tn(uv, bk)
            sbd_ref[j] = sb * dc_ref[pl.ds(r0, 1), cs] + upd * bd_mask

    if short:
        def seq(nb, carry):
            load_state(s0_ref[nb])

            def st(val):
                so_ref[nb] = val

            chunk_step(pl.multiple_of(nb * CT, CT))
            store_state(st)
            return carry

        lax.fori_loop(0, NCH, seq, 0)
    else:
        @pl.when(t == 0)
        def _():
            load_state(s0_ref[0])

        def body(c, carry):
            chunk_step(pl.multiple_of(c * CT, CT))
            return carry

        lax.fori_loop(0, NCH, body, 0)

        @pl.when(t == T - 1)
        def _():
            def st(val):
                so_ref[0] = val

            store_state(st)

    y = yb_ref[...]
    mu = _segsum(y, R_K) * (1.0 / R_K)
    yc = y - mu
    var = _segsum(yc * yc, R_K) * (1.0 / R_K)
    yn = yc * lax.rsqrt(var + R_LN_EPS) * lg_ref[...] + lb_ref[...] + bo_ref[...]
    y_ref[...] = yn * _silu(z_ref[...])


def _rwkv_b_call(prev_y, ra, proj1, ln_g, ln_b, s0, *, row0, N, L, RB, CT, short):
    chained = prev_y is not None
    rb0 = row0 // RB
    assert rb0 * RB == row0
    if short:
        NB = RB // CT
        T = 1
        grid = (N // NB, 1)
        s_spec = pl.BlockSpec((NB, 1024, 64), lambda i, t: (i, 0, 0))
    else:
        NB = 1
        T = L // RB
        grid = (N, T)
        s_spec = pl.BlockSpec((1, 1024, 64), lambda i, t: (i, 0, 0))
    rspec = pl.BlockSpec((RB, MIX_W), lambda i, t: (rb0 + i * T + t, 0))
    in_specs = ([rspec] * N_RA_OUT
                + [pl.BlockSpec((RB, MIX_W), lambda i, t: (rb0 + i * T + t, P1_ZD // 1024)),
                   pl.BlockSpec((1, MIX_W), lambda i, t: (0, 0)),
                   pl.BlockSpec((1, MIX_W), lambda i, t: (0, 0)),
                   s_spec])
    args = list(ra) + [proj1, ln_g, ln_b, s0]
    aliases = {}
    if chained:
        in_specs = [pl.BlockSpec(memory_space=pl.ANY)] + in_specs
        args = [prev_y] + args
        aliases = {0: 0}
    return pl.pallas_call(
        functools.partial(_rwkv_b_kernel, RB=RB, CT=CT, T=T, short=short, chained=chained),
        out_shape=(jax.ShapeDtypeStruct((R_ALL, MIX_W), F32),
                   jax.ShapeDtypeStruct((N, 1024, 64), F32)),
        grid=grid,
        in_specs=in_specs,
        out_specs=(rspec, s_spec),
        scratch_shapes=[pltpu.VMEM((4, 256, 256), F32),
                        pltpu.VMEM((RB, MIX_W), F32)],
        input_output_aliases=aliases,
        compiler_params=_cp(("arbitrary", "arbitrary")),
    )(*args)


def _s5_weights(lam_re, lam_im, log_dt, b_re, b_im, c_re, c_im):
    dt = jnp.exp(log_dt)[:, None]
    mag = jnp.exp(lam_re * dt)
    ar = mag * jnp.cos(lam_im * dt)
    ai = mag * jnp.sin(lam_im * dt)
    den = lam_re * lam_re + lam_im * lam_im
    qr = ((ar - 1.0) * lam_re + ai * lam_im) / den
    qi = (ai * lam_re - (ar - 1.0) * lam_im) / den
    bbr = qr[..., None] * b_re - qi[..., None] * b_im
    bbi = qr[..., None] * b_im + qi[..., None] * b_re
    eye = jnp.eye(16, dtype=F32)

    def in_blocks(bb):
        bb = bb.reshape(4, 16, S5_STATE, S5_GROUP)
        return jnp.einsum('jgph,gk->jghkp', bb, eye).reshape(4, 256, 1024)

    def out_blocks(cc):
        cc = cc.reshape(4, 16, S5_GROUP, S5_STATE)
        return jnp.einsum('jghp,gk->jgpkh', cc, eye).reshape(4, 1024, 256)

    wb = jnp.concatenate([in_blocks(bbr), in_blocks(bbi)], axis=2).astype(BF16)
    wc = jnp.concatenate([out_blocks(c_re), out_blocks(-c_im)], axis=1).astype(BF16)
    return ar.reshape(1, 4096), ai.reshape(1, 4096), wb, wc


GROUPS = (
    ("lead", R_LEAD0, BATCH, N_META),
    ("main", 0, BATCH, SEQ),
    ("samp", R_SAMP0, DEC_BATCH, DEC_SEQ),
)


def kernel(x_prompt, x_sample, state_conv, state_ssm_re, state_ssm_im, state_mlstm_c, state_mlstm_n, state_mlstm_m, state_rwkv_s, state_rwkv_shift, meta_tokens, ev_w_in, a_conv_w, a_conv_b, a_ln_g, a_ln_b, a_pw, s5_lambda_re, s5_lambda_im, s5_log_dt, s5_b_re, s5_b_im, s5_c_re, s5_c_im, s5_d, s5_glu_w, s5_glu_b, ev_w_out, ev_ln_g, ev_ln_b, od_w_in, m_ig_b, m_fg_b, m_hn_g, r_mu, r_w0, r_w2, r_a0, r_a2, r_kk, r_ka, r_rk, r_ln_g, r_ln_b, od_w_out, od_ln_g, od_ln_b):
    nb = x_prompt.shape[0]
    x_all = jnp.concatenate([x_prompt.reshape(R_MAIN, D_MODEL),
                             x_sample.reshape(R_SAMP, D_MODEL),
                             jnp.broadcast_to(meta_tokens[None], (nb, N_META, D_MODEL)).reshape(R_LEAD, D_MODEL)],
                            axis=0)

    proj0 = _matmul(x_all, ev_w_in[0].astype(BF16), 928, 512)

    zeros = lambda *s: jnp.zeros(s, F32)
    row = lambda vec: vec.reshape(1, -1)

    conv_cfg = {"lead": dict(NB=4, TL=16), "main": dict(NB=1, TL=256), "samp": dict(NB=16, TL=8)}
    act = None
    conv_st = {}
    for name, row0, n, l in GROUPS:
        st_in = {"lead": zeros(nb, CONV_W - 1, MIX_W), "main": conv_st.get("lead"), "samp": state_conv[0]}[name]
        act, conv_st[name] = _conv_call(act, proj0, st_in, a_conv_w[0], row(a_conv_b[0]), row(a_ln_g[0]),
                                        row(a_ln_b[0]), row0=row0, N=n, L=l, **conv_cfg[name])
    mix_a = _pw_gate(act, a_pw[0].astype(BF16), proj0)

    ar, ai, wb, wc = _s5_weights(s5_lambda_re[0], s5_lambda_im[0], s5_log_dt[0], s5_b_re[0], s5_b_im[0],
                                 s5_c_re[0], s5_c_im[0])
    s5_cfg = {"lead": dict(NB=4, TL=16), "main": dict(NB=1, TL=256), "samp": dict(NB=32, TL=8)}
    yb = None
    s5_st = {}
    for name, row0, n, l in GROUPS:
        if name == "lead":
            x0r, x0i = zeros(nb, 1, 4096), zeros(nb, 1, 4096)
        elif name == "main":
            x0r, x0i = s5_st["lead"]
        else:
            x0r, x0i = state_ssm_re[0].reshape(n, 1, 4096), state_ssm_im[0].reshape(n, 1, 4096)
        yb, xr, xi = _s5_call(yb, proj0, wb, wc, row(s5_d[0]), ar, ai, x0r, x0i,
                              row0=row0, N=n, L=l, **s5_cfg[name])
        s5_st[name] = (xr, xi)
    mix_b = _glu_gate(yb, s5_glu_w[0].astype(BF16), row(s5_glu_b[0]), proj0)

    x1 = _out_ln(x_all, mix_a, mix_b, ev_w_out[0].astype(BF16), row(ev_ln_g[0]), row(ev_ln_b[0]))

    w1 = od_w_in[0]
    w1p = jnp.concatenate([w1[:, 0:4096], w1[:, 4104:5128], w1[:, 5128:8200], w1[:, 8328:9352],
                           w1[:, 8200:8328], w1[:, 4096:4104],
                           jnp.zeros((D_MODEL, P1_N - 9352), F32)], axis=1).astype(BF16)
    proj1 = _matmul(x1, w1p, 928, 512)

    gate_b = jnp.concatenate([m_ig_b[0], m_fg_b[0], jnp.zeros((120,), F32)]).reshape(1, 128)
    m_cfg = {"lead": 16, "main": 256, "samp": 8}

    def m_pad(m):
        return jnp.pad(jnp.broadcast_to(m[:, :, None], m.shape + (128,)), ((0, 0), (0, 4), (0, 0)))

    mix_c = None
    ml_st = {}
    for name, row0, n, l in GROUPS:
        if name == "lead":
            c0, n0, m0 = zeros(nb, M_HEADS, M_D, M_D), zeros(nb, M_HEADS, M_D), zeros(nb, 8, 128)
        elif name == "main":
            c0, n0, m0 = ml_st["lead"]
        else:
            c0, n0, m0 = state_mlstm_c[0], state_mlstm_n[0], m_pad(state_mlstm_m[0])
        mix_c, c_new, n_new, m_new = _mlstm_call(mix_c, proj1, gate_b, row(m_hn_g[0]), c0, n0, m0,
                                                 row0=row0, N=n, L=l, TL=m_cfg[name])
        ml_st[name] = (c_new, n_new, m_new)

    mu = r_mu[0]
    w2p = jnp.concatenate([r_w2[0], jnp.zeros((64, MIX_W), F32)], axis=0).astype(BF16)
    a2p = jnp.concatenate([jnp.zeros((64, MIX_W), F32), r_a2[0]], axis=0).astype(BF16)
    r_wts = [row(mu[:3072]), row(mu[3072:]), row(r_w0[0]), w2p, row(r_a0[0]), a2p,
             row(r_kk[0]), row(r_ka[0]), row(r_rk[0])]

    def shift_rows(rows):
        return jnp.concatenate([proj1[rows, P1_R:P1_R + 3072], proj1[rows, P1_WA:P1_WA + 128]], axis=1)

    lead_last = R_LEAD0 + N_META * jnp.arange(nb) + (N_META - 1)
    main_last = SEQ * jnp.arange(nb) + (SEQ - 1)
    samp_last = R_SAMP0 + DEC_SEQ * jnp.arange(DEC_BATCH) + (DEC_SEQ - 1)
    sh_lead = shift_rows(lead_last)
    sh_samp0 = state_rwkv_shift[0]
    ra_cfg = {"lead": dict(RB=64, CT=16, short=True), "main": dict(RB=256, CT=16, short=False),
              "samp": dict(RB=256, CT=8, short=True)}
    ra = None
    for name, row0, n, l in GROUPS:
        if name == "lead":
            st = zeros(n * l, 3200)
            st_rkv, st_wa = st[:, :3072], st[:, 3072:]
        elif name == "main":
            st_rkv, st_wa = sh_lead[:, None, :3072], sh_lead[:, None, 3072:]
        else:
            st = jnp.repeat(sh_samp0, l, axis=0)
            st_rkv, st_wa = st[:, :3072], st[:, 3072:]
        ra = _rwkv_a_call(ra, proj1, st_rkv, st_wa, r_wts, row0=row0, N=n, L=l, **ra_cfg[name])

    rb_cfg = {"lead": dict(RB=64, CT=16, short=True), "main": dict(RB=256, CT=16, short=False),
              "samp": dict(RB=64, CT=8, short=True)}
    mix_d = None
    rs_st = {}
    for name, row0, n, l in GROUPS:
        if name == "lead":
            s0 = zeros(nb, 1024, 64)
        elif name == "main":
            s0 = rs_st["lead"]
        else:
            s0 = state_rwkv_s[0].reshape(n, 1024, 64)
        mix_d, rs_st[name] = _rwkv_b_call(mix_d, ra, proj1, row(r_ln_g[0]), row(r_ln_b[0]), s0,
                                          row0=row0, N=n, L=l, **rb_cfg[name])

    y_all = _out_ln(x1, mix_c, mix_d, od_w_out[0].astype(BF16), row(od_ln_g[0]), row(od_ln_b[0]))

    y_prompt = y_all[:R_MAIN].reshape(nb, SEQ, D_MODEL)
    y_sample = y_all[R_SAMP0:R_LEAD0].reshape(DEC_BATCH, DEC_SEQ, D_MODEL)

    def pack(group):
        xr, xi = s5_st[group]
        c_new, n_new, m_new = ml_st[group]
        n = xr.shape[0]
        last = main_last if group == "main" else samp_last
        return (conv_st[group][None],
                xr.reshape(n, S5_GROUPS, S5_STATE)[None],
                xi.reshape(n, S5_GROUPS, S5_STATE)[None],
                c_new[None], n_new[None], m_new[:, :M_HEADS, 0][None],
                rs_st[group].reshape(n, R_HEADS, R_K, R_K)[None],
                shift_rows(last)[None])

    return (y_prompt, y_sample) + pack("main") + pack("samp")
```

```python
import functools
import math

import jax
import jax.numpy as jnp
from jax import lax
from jax.experimental import pallas as pl
from jax.experimental.pallas import tpu as pltpu

F32 = jnp.float32
BF16 = jnp.bfloat16

D_MODEL = 2048
MIX_W = 1024
N_META = 16
CONV_W = 31
S5_GROUP = 16
S5_GROUPS = 64
S5_STATE = 64
M_HEADS = 4
M_D = 256
R_HEADS = 16
R_K = 64
LN_EPS = 1e-5
R_LN_EPS = 64e-5
DEPTH = 2
ALPHA = (2 * DEPTH) ** 0.25

BATCH = 4
SEQ = 2048
DEC_BATCH = 128
DEC_SEQ = 8
R_MAIN = BATCH * SEQ
R_SAMP0 = R_MAIN
R_SAMP = DEC_BATCH * DEC_SEQ
R_LEAD0 = R_SAMP0 + R_SAMP
R_LEAD = BATCH * N_META
R_ALL = R_LEAD0 + R_LEAD

P1_Q, P1_K, P1_V, P1_O, P1_ZC = 0, 1024, 2048, 3072, 4096
P1_R, P1_RK, P1_RV, P1_ZD, P1_WA, P1_GATE = 5120, 6144, 7168, 8192, 9216, 9344
P1_N = 9728

VMEM_LIMIT = 48 * 1024 * 1024


def _cp(sem):
    return pltpu.CompilerParams(dimension_semantics=sem, vmem_limit_bytes=VMEM_LIMIT)


def _dot(a, b):
    return jnp.dot(a, b, preferred_element_type=F32)


def _dot_nt(a, b):
    return lax.dot_general(a, b, (((1,), (1,)), ((), ())), preferred_element_type=F32)


def _dot_tn(a, b):
    return lax.dot_general(a, b, (((0,), (0,)), ((), ())), preferred_element_type=F32)


def _split2(x):
    hi = x.astype(BF16)
    lo = (x - hi.astype(F32)).astype(BF16)
    return hi, lo


def _split3(x):
    hi = x.astype(BF16)
    r1 = x - hi.astype(F32)
    mid = r1.astype(BF16)
    lo = (r1 - mid.astype(F32)).astype(BF16)
    return hi, mid, lo


def _dot3(a, b):
    ah, al = _split2(a)
    bh, bl = _split2(b)
    return _dot(ah, bh) + _dot(ah, bl) + _dot(al, bh)


def _sigmoid(x):
    return jax.nn.sigmoid(x)


def _silu(x):
    return x * jax.nn.sigmoid(x)


def _softplus(x):
    return jnp.maximum(x, 0.0) + jnp.log(1.0 + jnp.exp(-jnp.abs(x)))


def _gelu_tanh(x):
    c = math.sqrt(2.0 / math.pi)
    return x * (0.5 * (1.0 + jnp.tanh(c * (x + 0.044715 * (x * x * x)))))


def _iota(shape, axis):
    return lax.broadcasted_iota(jnp.int32, shape, axis)


def _shr(x, k):
    return lax.shift_right_logical(x, jnp.int32(k))


def _log2(n):
    k = int(round(math.log2(n)))
    assert 1 << k == n
    return k


def _block_ones(n, seg, dtype):
    r = _shr(_iota((n, n), 0), _log2(seg))
    c = _shr(_iota((n, n), 1), _log2(seg))
    return (r == c).astype(dtype)


def _segsum(x, seg):
    g = _block_ones(256, seg, BF16)
    outs = []
    for j in range(x.shape[1] // 256):
        hi, lo = _split2(x[:, 256 * j:256 * (j + 1)])
        outs.append(_dot(hi, g) + _dot(lo, g))
    return jnp.concatenate(outs, axis=1)


def _row_cumsum(x, period):
    rows = x.shape[0]
    rid = jnp.bitwise_and(_iota(x.shape, 0), period - 1)
    d = 1
    while d < min(period, rows):
        x = x + jnp.where(rid >= d, pltpu.roll(x, d, 0), 0.0)
        d *= 2
    return x


def _row_cummax(x, period):
    rows = x.shape[0]
    rid = jnp.bitwise_and(_iota(x.shape, 0), period - 1)
    d = 1
    while d < min(period, rows):
        x = jnp.maximum(x, jnp.where(rid >= d, pltpu.roll(x, d, 0), -jnp.inf))
        d *= 2
    return x


def _mm_kernel(x_ref, w_ref, o_ref):
    o_ref[...] = _dot(x_ref[...].astype(BF16), w_ref[...])


def _matmul(x, w, tm, tn):
    r, k = x.shape
    n = w.shape[1]
    return pl.pallas_call(
        _mm_kernel,
        out_shape=jax.ShapeDtypeStruct((r, n), F32),
        grid=(pl.cdiv(r, tm), n // tn),
        in_specs=[pl.BlockSpec((tm, k), lambda i, j: (i, 0)),
                  pl.BlockSpec((k, tn), lambda i, j: (0, j))],
        out_specs=pl.BlockSpec((tm, tn), lambda i, j: (i, j)),
        compiler_params=_cp(("parallel", "arbitrary")),
    )(x, w)


def _pw_kernel(a_ref, w_ref, z_ref, o_ref):
    o_ref[...] = _dot(a_ref[...].astype(BF16), w_ref[...]) * _silu(z_ref[...])


def _pw_gate(act, pw, proj0, tm=464, tn=512):
    r = act.shape[0]
    zb = 2048 // tn
    return pl.pallas_call(
        _pw_kernel,
        out_shape=jax.ShapeDtypeStruct((r, MIX_W), F32),
        grid=(pl.cdiv(r, tm), MIX_W // tn),
        in_specs=[pl.BlockSpec((tm, MIX_W), lambda i, j: (i, 0)),
                  pl.BlockSpec((MIX_W, tn), lambda i, j: (0, j)),
                  pl.BlockSpec((tm, tn), lambda i, j: (i, zb + j))],
        out_specs=pl.BlockSpec((tm, tn), lambda i, j: (i, j)),
        compiler_params=_cp(("parallel", "arbitrary")),
    )(act, pw, proj0)


def _glu_kernel(y_ref, wv_ref, wg_ref, bv_ref, bg_ref, z_ref, o_ref):
    y = y_ref[...].astype(BF16)
    v = _dot(y, wv_ref[...]) + bv_ref[...]
    g = _dot(y, wg_ref[...]) + bg_ref[...]
    o_ref[...] = v * _sigmoid(g) * _silu(z_ref[...])


def _glu_gate(yb, glu_w, glu_b, proj0, tm=464, tn=512):
    r = yb.shape[0]
    nb = MIX_W // tn
    zb = 4096 // tn
    return pl.pallas_call(
        _glu_kernel,
        out_shape=jax.ShapeDtypeStruct((r, MIX_W), F32),
        grid=(pl.cdiv(r, tm), nb),
        in_specs=[pl.BlockSpec((tm, MIX_W), lambda i, j: (i, 0)),
                  pl.BlockSpec((MIX_W, tn), lambda i, j: (0, j)),
                  pl.BlockSpec((MIX_W, tn), lambda i, j: (0, nb + j)),
                  pl.BlockSpec((1, tn), lambda i, j: (0, j)),
                  pl.BlockSpec((1, tn), lambda i, j: (0, nb + j)),
                  pl.BlockSpec((tm, tn), lambda i, j: (i, zb + j))],
        out_specs=pl.BlockSpec((tm, tn), lambda i, j: (i, j)),
        compiler_params=_cp(("parallel", "arbitrary")),
    )(yb, glu_w, glu_w, glu_b, glu_b, proj0)


def _out_ln_kernel(x_ref, ma_ref, mb_ref, wa_ref, wb_ref, g_ref, b_ref, o_ref):
    out = _dot(ma_ref[...].astype(BF16), wa_ref[...]) + _dot(mb_ref[...].astype(BF16), wb_ref[...])
    y = ALPHA * x_ref[...] + out
    mu = jnp.mean(y, axis=-1, keepdims=True)
    yc = y - mu
    var = jnp.mean(yc * yc, axis=-1, keepdims=True)
    o_ref[...] = yc * lax.rsqrt(var + LN_EPS) * g_ref[...] + b_ref[...]


def _out_ln(x, mix_a, mix_b, w_out, ln_g, ln_b, tm=232):
    r = x.shape[0]
    return pl.pallas_call(
        _out_ln_kernel,
        out_shape=jax.ShapeDtypeStruct((r, D_MODEL), F32),
        grid=(pl.cdiv(r, tm),),
        in_specs=[pl.BlockSpec((tm, D_MODEL), lambda i: (i, 0)),
                  pl.BlockSpec((tm, MIX_W), lambda i: (i, 0)),
                  pl.BlockSpec((tm, MIX_W), lambda i: (i, 0)),
                  pl.BlockSpec((MIX_W, D_MODEL), lambda i: (0, 0)),
                  pl.BlockSpec((MIX_W, D_MODEL), lambda i: (1, 0)),
                  pl.BlockSpec((1, D_MODEL), lambda i: (0, 0)),
                  pl.BlockSpec((1, D_MODEL), lambda i: (0, 0))],
        out_specs=pl.BlockSpec((tm, D_MODEL), lambda i: (i, 0)),
        compiler_params=_cp(("parallel",)),
    )(x, mix_a, mix_b, w_out, w_out, ln_g, ln_b)


def _conv_kernel(*refs, NB, TL, T, chained):
    if chained:
        refs = refs[1:]
    (u_ref, g_ref, st_ref, w_ref, cb_ref, lg_ref, lb_ref, act_ref, nst_ref, hp_ref, hs_ref) = refs
    t = pl.program_id(1)
    rc = 16 if TL % 16 == 0 else 8
    for nb in range(NB):
        base = nb * TL

        @pl.when(t == 0)
        def _():
            hp_ref[nb, 0:2, :] = jnp.zeros((2, MIX_W), F32)
            hp_ref[nb, 2:32, :] = st_ref[nb]

        hp_ref[nb, TL + 32:TL + 40, :] = jnp.zeros((8, MIX_W), F32)
        hp_ref[nb, 32:32 + TL, :] = u_ref[base:base + TL, :] * _sigmoid(g_ref[base:base + TL, :])
        for b in range(8):
            hs_ref[b] = hp_ref[nb, b:b + TL + 32, :]

        def chunk(c, carry):
            r0 = pl.multiple_of(c * rc, rc)
            acc = jnp.zeros((rc, MIX_W), F32)
            for j in range(CONV_W):
                o = j + 2
                acc = acc + w_ref[j:j + 1, :] * hs_ref[o % 8, pl.ds(r0 + 8 * (o // 8), rc), :]
            y = acc + cb_ref[...]
            mu = jnp.mean(y, axis=-1, keepdims=True)
            yc = y - mu
            var = jnp.mean(yc * yc, axis=-1, keepdims=True)
            yn = yc * lax.rsqrt(var + LN_EPS) * lg_ref[...] + lb_ref[...]
            act_ref[pl.ds(base + r0, rc), :] = _silu(yn)
            return carry

        lax.fori_loop(0, TL // rc, chunk, 0)

        @pl.when(t == T - 1)
        def _():
            nst_ref[nb] = hp_ref[nb, TL + 2:TL + 32, :]

        if T > 1:
            hp_ref[nb, 0:32, :] = hp_ref[nb, TL:TL + 32, :]


def _conv_call(prev_act, proj0, state, conv_w, conv_b, ln_g, ln_b, *, row0, N, L, NB, TL):
    T = L // TL
    RB = NB * TL
    assert NB == 1 or T == 1
    rb0 = row0 // RB
    assert rb0 * RB == row0
    chained = prev_act is not None
    in_specs = [pl.BlockSpec((RB, MIX_W), lambda i, t: (rb0 + i * T + t, 0)),
                pl.BlockSpec((RB, MIX_W), lambda i, t: (rb0 + i * T + t, 1)),
                pl.BlockSpec((NB, CONV_W - 1, MIX_W), lambda i, t: (i, 0, 0)),
                pl.BlockSpec((CONV_W, MIX_W), lambda i, t: (0, 0)),
                pl.BlockSpec((1, MIX_W), lambda i, t: (0, 0)),
                pl.BlockSpec((1, MIX_W), lambda i, t: (0, 0)),
                pl.BlockSpec((1, MIX_W), lambda i, t: (0, 0))]
    args = [proj0, proj0, state, conv_w, conv_b, ln_g, ln_b]
    aliases = {}
    if chained:
        in_specs = [pl.BlockSpec(memory_space=pl.ANY)] + in_specs
        args = [prev_act] + args
        aliases = {0: 0}
    return pl.pallas_call(
        functools.partial(_conv_kernel, NB=NB, TL=TL, T=T, chained=chained),
        out_shape=(jax.ShapeDtypeStruct((R_ALL, MIX_W), F32),
                   jax.ShapeDtypeStruct((N, CONV_W - 1, MIX_W), F32)),
        grid=(N // NB, T),
        in_specs=in_specs,
        out_specs=(pl.BlockSpec((RB, MIX_W), lambda i, t: (rb0 + i * T + t, 0)),
                   pl.BlockSpec((NB, CONV_W - 1, MIX_W), lambda i, t: (i, 0, 0))),
        scratch_shapes=[pltpu.VMEM((NB, TL + 40, MIX_W), F32),
                        pltpu.VMEM((8, TL + 32, MIX_W), F32)],
        input_output_aliases=aliases,
        compiler_params=_cp(("arbitrary", "arbitrary")),
    )(*args)


def _s5_kernel(*refs, NB, TL, T, chained):
    if chained:
        refs = refs[1:]
    (u_ref, wb_ref, wc_ref, d_ref, ar_ref, ai_ref, x0r_ref, x0i_ref,
     y_ref, xfr_ref, xfi_ref, xs_ref, cr_ref, ci_ref) = refs
    t = pl.program_id(2)
    RB = NB * TL
    GL = TL // 8
    u = u_ref[...]
    big = _dot(u.astype(BF16), wb_ref[0])
    xr = big[:, :1024]
    xi = big[:, 1024:]
    ar = ar_ref[...]
    ai = ai_ref[...]

    def cmul(pr, pi, qr, qi):
        return pr * qr - pi * qi, pr * qi + pi * qr

    a1 = (ar, ai)
    a2 = cmul(*a1, *a1)
    a4 = cmul(*a2, *a2)
    rid = jnp.bitwise_and(_iota((RB, 1024), 0), 7)
    for d, (pr, pi) in ((1, a1), (2, a2), (4, a4)):
        sr = pltpu.roll(xr, d, 0)
        si = pltpu.roll(xi, d, 0)
        m = rid >= d
        xr, xi = (xr + jnp.where(m, pr * sr - pi * si, 0.0),
                  xi + jnp.where(m, pr * si + pi * sr, 0.0))
    xs_ref[0] = xr
    xs_ref[1] = xi
    a3 = cmul(*a2, *a1)
    a5 = cmul(*a4, *a1)
    a6 = cmul(*a4, *a2)
    a7 = cmul(*a6, *a1)
    a8 = cmul(*a4, *a4)
    r8 = _iota((8, 1024), 0)
    pwr = jnp.zeros((8, 1024), F32)
    pwi = jnp.zeros((8, 1024), F32)
    for k, (pr, pi) in enumerate((a1, a2, a3, a4, a5, a6, a7, a8)):
        pwr = jnp.where(r8 == k, pr, pwr)
        pwi = jnp.where(r8 == k, pi, pwi)

    first = t == 0

    def seq_body(nb, carry0):
        x0r = x0r_ref[nb]
        x0i = x0i_ref[nb]
        if T > 1:
            c_r = jnp.where(first, x0r, cr_ref[0:1, :])
            c_i = jnp.where(first, x0i, ci_ref[0:1, :])
        else:
            c_r, c_i = x0r, x0i

        def grp(g, c):
            c_r, c_i = c
            off = pl.multiple_of(nb * TL + g * 8, 8)
            vr = xs_ref[0, pl.ds(off, 8), :]
            vi = xs_ref[1, pl.ds(off, 8), :]
            br = jnp.broadcast_to(c_r, (8, 1024))
            bi = jnp.broadcast_to(c_i, (8, 1024))
            nr = vr + pwr * br - pwi * bi
            ni = vi + pwr * bi + pwi * br
            xs_ref[0, pl.ds(off, 8), :] = nr
            xs_ref[1, pl.ds(off, 8), :] = ni
            return nr[7:8, :], ni[7:8, :]

        c_r, c_i = lax.fori_loop(0, GL, grp, (c_r, c_i))
        if T > 1:
            cr_ref[...] = jnp.broadcast_to(c_r, (8, 1024))
            ci_ref[...] = jnp.broadcast_to(c_i, (8, 1024))

        @pl.when(t == T - 1)
        def _():
            xfr_ref[nb] = c_r
            xfi_ref[nb] = c_i

        return carry0

    lax.fori_loop(0, NB, seq_body, 0)
    y = (_dot(xs_ref[0].astype(BF16), wc_ref[0, 0:1024, :])
         + _dot(xs_ref[1].astype(BF16), wc_ref[0, 1024:2048, :]))
    y_ref[...] = _gelu_tanh(y + d_ref[...] * u)


def _s5_call(prev_y, proj0, wb, wc, dvec, ar, ai, x0r, x0i, *, row0, N, L, NB, TL):
    T = L // TL
    RB = NB * TL
    assert NB == 1 or T == 1
    rb0 = row0 // RB
    assert rb0 * RB == row0
    chained = prev_y is not None
    ub = 3072 // 256
    in_specs = [pl.BlockSpec((RB, 256), lambda i, j, t: (rb0 + i * T + t, ub + j)),
                pl.BlockSpec((1, 256, 2048), lambda i, j, t: (j, 0, 0)),
                pl.BlockSpec((1, 2048, 256), lambda i, j, t: (j, 0, 0)),
                pl.BlockSpec((1, 256), lambda i, j, t: (0, j)),
                pl.BlockSpec((1, 1024), lambda i, j, t: (0, j)),
                pl.BlockSpec((1, 1024), lambda i, j, t: (0, j)),
                pl.BlockSpec((NB, 1, 1024), lambda i, j, t: (i, 0, j)),
                pl.BlockSpec((NB, 1, 1024), lambda i, j, t: (i, 0, j))]
    args = [proj0, wb, wc, dvec, ar, ai, x0r, x0i]
    aliases = {}
    if chained:
        in_specs = [pl.BlockSpec(memory_space=pl.ANY)] + in_specs
        args = [prev_y] + args
        aliases = {0: 0}
    st = jax.ShapeDtypeStruct((N, 1, 4096), F32)
    return pl.pallas_call(
        functools.partial(_s5_kernel, NB=NB, TL=TL, T=T, chained=chained),
        out_shape=(jax.ShapeDtypeStruct((R_ALL, MIX_W), F32), st, st),
        grid=(N // NB, 4, T),
        in_specs=in_specs,
        out_specs=(pl.BlockSpec((RB, 256), lambda i, j, t: (rb0 + i * T + t, j)),
                   pl.BlockSpec((NB, 1, 1024), lambda i, j, t: (i, 0, j)),
                   pl.BlockSpec((NB, 1, 1024), lambda i, j, t: (i, 0, j))),
        scratch_shapes=[pltpu.VMEM((2, RB, 1024), F32),
                        pltpu.VMEM((8, 1024), F32),
                        pltpu.VMEM((8, 1024), F32)],
        input_output_aliases=aliases,
        compiler_params=_cp(("arbitrary", "arbitrary", "arbitrary")),
    )(*args)


def _mlstm_kernel(*refs, TL, T, chained):
    if chained:
        refs = refs[1:]
    (q_ref, k_ref, v_ref, o_ref, z_ref, gt_ref, gb_ref, hg_ref, c0_ref, n0_ref, m0_ref,
     y_ref, c_ref, n_ref, m_ref, cs_ref, ns_ref, ms_ref) = refs
    t = pl.program_id(1)

    @pl.when(t == 0)
    def _():
        cs_ref[...] = c0_ref[0]
        ns_ref[...] = n0_ref[0]
        ms_ref[...] = m0_ref[0]

    G = gt_ref[...] + gb_ref[...]
    B = _row_cumsum(-_softplus(-G), TL)
    Bs = pltpu.roll(B, 124, 1)
    A = G - Bs
    CM = _row_cummax(A, TL)
    ms = ms_ref[...]
    dg = _iota((8, 128), 0) == _iota((8, 128), 1)
    mrow = jnp.sum(jnp.where(dg, ms, 0.0), axis=0, keepdims=True)
    M = jnp.maximum(mrow, CM)
    MT = Bs + M
    sel = dg.astype(BF16)
    a_hi, a_mid, a_lo = _split3(A)
    Arow = _dot_nt(sel, a_hi) + _dot_nt(sel, a_mid) + _dot_nt(sel, a_lo)
    causal = _iota((TL, TL), 0) >= _iota((TL, TL), 1)
    for h in range(M_HEADS):
        sl = slice(M_D * h, M_D * (h + 1))
        q = (q_ref[:, sl] * (M_D ** -0.5))
        qb = q.astype(BF16)
        kf = k_ref[:, sl]
        kb = kf.astype(BF16)
        vf = v_ref[:, sl]
        m_col = M[:, h:h + 1]
        mt_col = MT[:, h:h + 1]
        b_col = Bs[:, h:h + 1]
        ig_col = G[:, h:h + 1]
        logd = Arow[h:h + 1, :] - m_col
        dm = jnp.exp(jnp.where(causal, logd, -jnp.inf))
        s = _dot_nt(qb, kb) * dm
        h_intra = _dot(s.astype(BF16), vf.astype(BF16))
        n_intra = jnp.sum(s, axis=1, keepdims=True)
        m_prev = mrow[:, h:h + 1]
        inter = jnp.exp(m_prev - m_col)
        c_old = cs_ref[h]
        n_old = ns_ref[h:h + 1, :]
        h_inter = _dot(qb, c_old.astype(BF16)) * inter
        n_inter = jnp.sum(q * n_old, axis=1, keepdims=True) * inter
        denom = jnp.maximum(jnp.abs(n_intra + n_inter), jnp.exp(-mt_col))
        hh = (h_intra + h_inter) / denom
        m_new = mt_col[TL - 1:TL, :]
        b_end = b_col[TL - 1:TL, :]
        dec = jnp.exp(m_prev + b_end - m_new)
        w_s = jnp.exp(b_end - b_col + ig_col - m_new)
        cs_ref[h] = c_old * dec + _dot_tn(kb, (vf * w_s).astype(BF16))
        ns_ref[h:h + 1, :] = n_old * dec + jnp.sum(kf * w_s, axis=0, keepdims=True)
        ms_ref[h:h + 1, :] = jnp.broadcast_to(m_new, (1, 128))
        mu = jnp.mean(hh, axis=-1, keepdims=True)
        hc = hh - mu
        var = jnp.mean(hc * hc, axis=-1, keepdims=True)
        hn = hc * lax.rsqrt(var + LN_EPS) * hg_ref[:, sl]
        y_ref[:, sl] = hn * _sigmoid(o_ref[:, sl]) * _silu(z_ref[:, sl])

    @pl.when(t == T - 1)
    def _():
        c_ref[0] = cs_ref[...]
        n_ref[0] = ns_ref[...]
        m_ref[0] = ms_ref[...]


def _mlstm_call(prev_y, proj1, gate_b, hn_g, c0, n0, m0, *, row0, N, L, TL):
    T = L // TL
    rb0 = row0 // TL
    assert rb0 * TL == row0
    chained = prev_y is not None

    def col(cb):
        return pl.BlockSpec((TL, MIX_W), lambda i, t: (rb0 + i * T + t, cb))

    in_specs = [col(0), col(1), col(2), col(3), col(4),
                pl.BlockSpec((TL, 128), lambda i, t: (rb0 + i * T + t, P1_GATE // 128)),
                pl.BlockSpec((1, 128), lambda i, t: (0, 0)),
                pl.BlockSpec((1, MIX_W), lambda i, t: (0, 0)),
                pl.BlockSpec((1, M_HEADS, M_D, M_D), lambda i, t: (i, 0, 0, 0)),
                pl.BlockSpec((1, M_HEADS, M_D), lambda i, t: (i, 0, 0)),
                pl.BlockSpec((1, 8, 128), lambda i, t: (i, 0, 0))]
    args = [proj1] * 6 + [gate_b, hn_g, c0, n0, m0]
    aliases = {}
    if chained:
        in_specs = [pl.BlockSpec(memory_space=pl.ANY)] + in_specs
        args = [prev_y] + args
        aliases = {0: 0}
    return pl.pallas_call(
        functools.partial(_mlstm_kernel, TL=TL, T=T, chained=chained),
        out_shape=(jax.ShapeDtypeStruct((R_ALL, MIX_W), F32),
                   jax.ShapeDtypeStruct((N, M_HEADS, M_D, M_D), F32),
                   jax.ShapeDtypeStruct((N, M_HEADS, M_D), F32),
                   jax.ShapeDtypeStruct((N, 8, 128), F32)),
        grid=(N, T),
        in_specs=in_specs,
        out_specs=(pl.BlockSpec((TL, MIX_W), lambda i, t: (rb0 + i * T + t, 0)),
                   pl.BlockSpec((1, M_HEADS, M_D, M_D), lambda i, t: (i, 0, 0, 0)),
                   pl.BlockSpec((1, M_HEADS, M_D), lambda i, t: (i, 0, 0)),
                   pl.BlockSpec((1, 8, 128), lambda i, t: (i, 0, 0))),
        scratch_shapes=[pltpu.VMEM((M_HEADS, M_D, M_D), F32),
                        pltpu.VMEM((M_HEADS, M_D), F32),
                        pltpu.VMEM((8, 128), F32)],
        input_output_aliases=aliases,
        compiler_params=_cp(("arbitrary", "arbitrary")),
    )(*args)


N_RA_OUT = 9


def _rwkv_a_kernel(*refs, RB, CT, short, chained):
    if chained:
        refs = refs[N_RA_OUT:]
    (pr_ref, pk_ref, pv_ref, pwa_ref, hr_ref, hk_ref, hv_ref, hwa_ref, st_ref, stwa_ref,
     mu_ref, muwa_ref, w0_ref, w2_ref, a0_ref, a2_ref, kkp_ref, ka_ref, rk_ref,
     ah_ref, rh_ref, bh_ref, kh_ref, vo_ref, ul_ref, yl_ref, dc_ref, bo_ref,
     s_at, s_rt, s_bt, s_kt, s_v, s_cum) = refs
    HS = R_HEADS * CT
    NCH = RB // CT
    rid = _iota((RB, 1), 0)
    if short:
        is_first = jnp.bitwise_and(rid, CT - 1) == 0
    else:
        is_first = rid == 0
        t0 = pl.program_id(1) == 0

    def shifted(p_ref, h_ref, s_ref, lo, hi, mu):
        p = p_ref[...]
        prev = pltpu.roll(p, 1, 0)
        if short:
            fp = s_ref[:, lo:hi]
        else:
            fp = jnp.where(t0, s_ref[0, :, lo:hi], h_ref[7:8, :])
        prev = jnp.where(is_first, fp, prev)
        return p + (prev - p) * mu

    r = shifted(pr_ref, hr_ref, st_ref, 0, 1024, mu_ref[:, 0:1024])
    k = shifted(pk_ref, hk_ref, st_ref, 1024, 2048, mu_ref[:, 1024:2048])
    v = shifted(pv_ref, hv_ref, st_ref, 2048, 3072, mu_ref[:, 2048:3072])
    wa = shifted(pwa_ref, hwa_ref, stwa_ref, 0, 128, muwa_ref[...])
    w = -_softplus(-(w0_ref[...] + _dot(jnp.tanh(wa).astype(BF16), w2_ref[...]))) - 0.5
    wlog = -jnp.exp(w)
    a = _sigmoid(a0_ref[...] + _dot(wa.astype(BF16), a2_ref[...]))
    kk = k * kkp_ref[...]
    kk = kk / jnp.maximum(jnp.sqrt(_segsum(kk * kk, R_K)), 1e-12)
    kmod = k * (1.0 + (a - 1.0) * ka_ref[...])
    bo_ref[...] = _segsum(r * kmod * rk_ref[...], R_K) * v
    cum = _row_cumsum(wlog, CT)
    einv = jnp.exp(-cum)
    s_at[...] = (-kk) * jnp.exp(cum - wlog)
    s_rt[...] = r * jnp.exp(cum)
    s_bt[...] = kk * a * einv
    s_kt[...] = kmod * einv
    s_v[...] = v
    s_cum[...] = cum

    be_mask = (_shr(_iota((HS, 1024), 0), _log2(CT)) == _shr(_iota((HS, 1024), 1), _log2(R_K))).astype(F32)
    bd_mask = (_shr(_iota((HS, HS), 0), _log2(CT)) == _shr(_iota((HS, HS), 1), _log2(CT))).astype(F32)
    tt = _iota((CT, HS), 0)
    ss = jnp.bitwise_and(_iota((CT, HS), 1), CT - 1)
    strict = tt > ss
    incl = tt >= ss
    eye_c = (tt == ss).astype(F32)

    def blockexp(x):
        return (jnp.concatenate([x] * R_HEADS, axis=0) * be_mask).astype(BF16)

    def bdiag(x):
        return jnp.concatenate([x] * R_HEADS, axis=0) * bd_mask

    U = 8 if NCH % 8 == 0 else 4

    def chunks(i, carry):
        rows = [pl.ds(pl.multiple_of((i * U + u) * CT, CT), CT) for u in range(U)]
        ld = lambda ref: [ref[rw, :] for rw in rows]
        at, rt, bt, kt, vv, cm = ld(s_at), ld(s_rt), ld(s_bt), ld(s_kt), ld(s_v), ld(s_cum)
        each = lambda f, *xs: [f(*a) for a in zip(*xs)]
        amat = each(lambda a, r_, b_, k_: _dot_nt(
            jnp.concatenate([a, r_], axis=0).astype(BF16),
            jnp.concatenate([blockexp(b_), blockexp(k_)], axis=0)), at, rt, bt, kt)
        a_ab = each(lambda m: jnp.where(strict, m[:CT, :HS], 0.0), amat)
        a_ak = each(lambda m: jnp.where(strict, m[:CT, HS:], 0.0), amat)
        a_rb = each(lambda m: jnp.where(incl, m[CT:, :HS], 0.0), amat)
        a_rk = each(lambda m: jnp.where(incl, m[CT:, HS:], 0.0), amat)
        p = each(lambda m: eye_c + m, a_ab)
        x = a_ab
        d = 1
        while 2 * d < CT:
            x = each(lambda m: _dot3(m, bdiag(m)), x)
            p = each(lambda q, m: q + _dot3(q, bdiag(m)), p, x)
            d *= 2
        tinv = p
        bd_ak = each(lambda m: bdiag(m).astype(BF16), a_ak)
        ta = each(lambda q, m: _dot(q.astype(BF16), m), tinv, bd_ak)
        art = each(lambda m, q: _dot(m.astype(BF16), bdiag(q).astype(BF16)), a_rb, tinv)
        yc = each(lambda m, n_, q: _dot(m.astype(BF16), n_) + q, art, bd_ak, a_rk)
        o1 = each(lambda q, m, a: _dot(jnp.concatenate([q, m], axis=0).astype(BF16), blockexp(a)), tinv, art, at)
        o2 = each(lambda q, m, v_: _dot(jnp.concatenate([q, m], axis=0).astype(BF16), blockexp(v_)), ta, yc, vv)
        ect = each(lambda c_: jnp.exp(c_[CT - 1:CT, :]), cm)
        for u, rw in enumerate(rows):
            ah_ref[rw, :] = o1[u][:CT]
            rh_ref[rw, :] = rt[u] + o1[u][CT:]
            ul_ref[rw, :] = o2[u][:CT]
            yl_ref[rw, :] = o2[u][CT:]
            bh_ref[rw, :] = bt[u] * ect[u]
            kh_ref[rw, :] = kt[u] * ect[u]
            vo_ref[rw, :] = vv[u]
            dc_ref[rw, :] = jnp.broadcast_to(ect[u], (CT, 1024))
        return carry

    lax.fori_loop(0, NCH // U, chunks, 0)


def _rwkv_a_call(prev, proj1, st_rkv, st_wa, wts, *, row0, N, L, RB, CT, short):
    chained = prev is not None
    rb0 = row0 // RB
    assert rb0 * RB == row0
    if short:
        assert L == CT
        grid = (N * L // RB, 1)
        T = 1
    else:
        T = L // RB
        grid = (N, T)

    def rowblk(i, t):
        return rb0 + i * T + t

    def col(cb, width=MIX_W):
        return pl.BlockSpec((RB, width), lambda i, t: (rowblk(i, t), cb))

    def halo(cb, width=MIX_W):
        return pl.BlockSpec((8, width),
                            lambda i, t: (jnp.maximum(rowblk(i, t) * (RB // 8) - 1, 0), cb))

    if short:
        st_specs = [pl.BlockSpec((RB, 3072), lambda i, t: (i, 0)),
                    pl.BlockSpec((RB, 128), lambda i, t: (i, 0))]
    else:
        st_specs = [pl.BlockSpec((1, 1, 3072), lambda i, t: (i, 0, 0)),
                    pl.BlockSpec((1, 1, 128), lambda i, t: (i, 0, 0))]

    def full(shape):
        return pl.BlockSpec(shape, lambda i, t: (0,) * len(shape))

    in_specs = ([col(P1_R // 1024), col(P1_RK // 1024), col(P1_RV // 1024), col(P1_WA // 128, 128),
                 halo(P1_R // 1024), halo(P1_RK // 1024), halo(P1_RV // 1024), halo(P1_WA // 128, 128)]
                + st_specs
                + [full((1, 3072)), full((1, 128)), full((1, 1024)), full((128, 1024)), full((1, 1024)),
                   full((128, 1024)), full((1, 1024)), full((1, 1024)), full((1, 1024))])
    args = [proj1] * 8 + [st_rkv, st_wa] + list(wts)
    aliases = {}
    if chained:
        in_specs = [pl.BlockSpec(memory_space=pl.ANY)] * N_RA_OUT + in_specs
        args = list(prev) + args
        aliases = {i: i for i in range(N_RA_OUT)}
    o_spec = pl.BlockSpec((RB, MIX_W), lambda i, t: (rowblk(i, t), 0))
    o_shape = jax.ShapeDtypeStruct((R_ALL, MIX_W), F32)
    return pl.pallas_call(
        functools.partial(_rwkv_a_kernel, RB=RB, CT=CT, short=short, chained=chained),
        out_shape=(o_shape,) * N_RA_OUT,
        grid=grid,
        in_specs=in_specs,
        out_specs=(o_spec,) * N_RA_OUT,
        scratch_shapes=[pltpu.VMEM((RB, MIX_W), F32)] * 6,
        input_output_aliases=aliases,
        compiler_params=_cp(("arbitrary", "arbitrary")),
    )(*args)


def _rwkv_b_kernel(*refs, RB, CT, T, short, chained):
    if chained:
        refs = refs[1:]
    (ah_ref, rh_ref, bh_ref, kh_ref, v_ref, ul_ref, yl_ref, dc_ref, bo_ref, z_ref, lg_ref, lb_ref,
     s0_ref, y_ref, so_ref, sbd_ref, yb_ref) = refs
    NCH = RB // CT
    t = pl.program_id(1)
    bd_mask = (_shr(_iota((256, 256), 0), 6) == _shr(_iota((256, 256), 1), 6)).astype(F32)
    e_tile = (_iota((64, 256), 0) == jnp.bitwise_and(_iota((64, 256), 1), 63)).astype(BF16)
    e_fold = (jnp.bitwise_and(_iota((256, 64), 0), 63) == _iota((256, 64), 1)).astype(BF16)

    def load_state(src):
        for j in range(4):
            hi, mid, lo = _split3(src[256 * j:256 * (j + 1), :])
            full = _dot(hi, e_tile) + _dot(mid, e_tile) + _dot(lo, e_tile)
            sbd_ref[j] = full * bd_mask

    def store_state(dst_store):
        outs = []
        for j in range(4):
            hi, mid, lo = _split3(sbd_ref[j])
            outs.append(_dot(hi, e_fold) + _dot(mid, e_fold) + _dot(lo, e_fold))
        dst_store(jnp.concatenate(outs, axis=0))

    def chunk_step(r0):
        rows = pl.ds(r0, CT)
        for j in range(4):
            cs = slice(256 * j, 256 * (j + 1))
            sb = sbd_ref[j]
            lhs = jnp.concatenate([ah_ref[rows, cs], rh_ref[rows, cs]], axis=0).astype(BF16)
            o = _dot_nt(lhs, sb.astype(BF16))
            vv = v_ref[rows, cs]
            ut = o[:CT] + ul_ref[rows, cs]
            yb_ref[rows, cs] = o[CT:] + yl_ref[rows, cs]
            uv = jnp.concatenate([ut, vv], axis=0).astype(BF16)
            bk = jnp.concatenate([bh_ref[rows, cs], kh_ref[rows, cs]], axis=0).astype(BF16)
            upd = _dot_tn(uv, bk)
            sbd_ref[j] = sb * dc_ref[pl.ds(r0, 1), cs] + upd * bd_mask

    if short:
        def seq(nb, carry):
            load_state(s0_ref[nb])

            def st(val):
                so_ref[nb] = val

            chunk_step(pl.multiple_of(nb * CT, CT))
            store_state(st)
            return carry

        lax.fori_loop(0, NCH, seq, 0)
    else:
        @pl.when(t == 0)
        def _():
            load_state(s0_ref[0])

        def body(c, carry):
            chunk_step(pl.multiple_of(c * CT, CT))
            return carry

        lax.fori_loop(0, NCH, body, 0)

        @pl.when(t == T - 1)
        def _():
            def st(val):
                so_ref[0] = val

            store_state(st)

    y = yb_ref[...]
    mu = _segsum(y, R_K) * (1.0 / R_K)
    yc = y - mu
    var = _segsum(yc * yc, R_K) * (1.0 / R_K)
    yn = yc * lax.rsqrt(var + R_LN_EPS) * lg_ref[...] + lb_ref[...] + bo_ref[...]
    y_ref[...] = yn * _silu(z_ref[...])


def _rwkv_b_call(prev_y, ra, proj1, ln_g, ln_b, s0, *, row0, N, L, RB, CT, short):
    chained = prev_y is not None
    rb0 = row0 // RB
    assert rb0 * RB == row0
    if short:
        NB = RB // CT
        T = 1
        grid = (N // NB, 1)
        s_spec = pl.BlockSpec((NB, 1024, 64), lambda i, t: (i, 0, 0))
    else:
        NB = 1
        T = L // RB
        grid = (N, T)
        s_spec = pl.BlockSpec((1, 1024, 64), lambda i, t: (i, 0, 0))
    rspec = pl.BlockSpec((RB, MIX_W), lambda i, t: (rb0 + i * T + t, 0))
    in_specs = ([rspec] * N_RA_OUT
                + [pl.BlockSpec((RB, MIX_W), lambda i, t: (rb0 + i * T + t, P1_ZD // 1024)),
                   pl.BlockSpec((1, MIX_W), lambda i, t: (0, 0)),
                   pl.BlockSpec((1, MIX_W), lambda i, t: (0, 0)),
                   s_spec])
    args = list(ra) + [proj1, ln_g, ln_b, s0]
    aliases = {}
    if chained:
        in_specs = [pl.BlockSpec(memory_space=pl.ANY)] + in_specs
        args = [prev_y] + args
        aliases = {0: 0}
    return pl.pallas_call(
        functools.partial(_rwkv_b_kernel, RB=RB, CT=CT, T=T, short=short, chained=chained),
        out_shape=(jax.ShapeDtypeStruct((R_ALL, MIX_W), F32),
                   jax.ShapeDtypeStruct((N, 1024, 64), F32)),
        grid=grid,
        in_specs=in_specs,
        out_specs=(rspec, s_spec),
        scratch_shapes=[pltpu.VMEM((4, 256, 256), F32),
                        pltpu.VMEM((RB, MIX_W), F32)],
        input_output_aliases=aliases,
        compiler_params=_cp(("arbitrary", "arbitrary")),
    )(*args)


def _s5_weights(lam_re, lam_im, log_dt, b_re, b_im, c_re, c_im):
    dt = jnp.exp(log_dt)[:, None]
    mag = jnp.exp(lam_re * dt)
    ar = mag * jnp.cos(lam_im * dt)
    ai = mag * jnp.sin(lam_im * dt)
    den = lam_re * lam_re + lam_im * lam_im
    qr = ((ar - 1.0) * lam_re + ai * lam_im) / den
    qi = (ai * lam_re - (ar - 1.0) * lam_im) / den
    bbr = qr[..., None] * b_re - qi[..., None] * b_im
    bbi = qr[..., None] * b_im + qi[..., None] * b_re
    eye = jnp.eye(16, dtype=F32)

    def in_blocks(bb):
        bb = bb.reshape(4, 16, S5_STATE, S5_GROUP)
        return jnp.einsum('jgph,gk->jghkp', bb, eye).reshape(4, 256, 1024)

    def out_blocks(cc):
        cc = cc.reshape(4, 16, S5_GROUP, S5_STATE)
        return jnp.einsum('jghp,gk->jgpkh', cc, eye).reshape(4, 1024, 256)

    wb = jnp.concatenate([in_blocks(bbr), in_blocks(bbi)], axis=2).astype(BF16)
    wc = jnp.concatenate([out_blocks(c_re), out_blocks(-c_im)], axis=1).astype(BF16)
    return ar.reshape(1, 4096), ai.reshape(1, 4096), wb, wc


GROUPS = (
    ("lead", R_LEAD0, BATCH, N_META),
    ("main", 0, BATCH, SEQ),
    ("samp", R_SAMP0, DEC_BATCH, DEC_SEQ),
)


def kernel(x_prompt, x_sample, state_conv, state_ssm_re, state_ssm_im, state_mlstm_c, state_mlstm_n, state_mlstm_m, state_rwkv_s, state_rwkv_shift, meta_tokens, ev_w_in, a_conv_w, a_conv_b, a_ln_g, a_ln_b, a_pw, s5_lambda_re, s5_lambda_im, s5_log_dt, s5_b_re, s5_b_im, s5_c_re, s5_c_im, s5_d, s5_glu_w, s5_glu_b, ev_w_out, ev_ln_g, ev_ln_b, od_w_in, m_ig_b, m_fg_b, m_hn_g, r_mu, r_w0, r_w2, r_a0, r_a2, r_kk, r_ka, r_rk, r_ln_g, r_ln_b, od_w_out, od_ln_g, od_ln_b):
    nb = x_prompt.shape[0]
    x_all = jnp.concatenate([x_prompt.reshape(R_MAIN, D_MODEL),
                             x_sample.reshape(R_SAMP, D_MODEL),
                             jnp.broadcast_to(meta_tokens[None], (nb, N_META, D_MODEL)).reshape(R_LEAD, D_MODEL)],
                            axis=0)

    proj0 = _matmul(x_all, ev_w_in[0].astype(BF16), 928, 512)

    zeros = lambda *s: jnp.zeros(s, F32)
    row = lambda vec: vec.reshape(1, -1)

    conv_cfg = {"lead": dict(NB=4, TL=16), "main": dict(NB=1, TL=256), "samp": dict(NB=16, TL=8)}
    act = None
    conv_st = {}
    for name, row0, n, l in GROUPS:
        st_in = {"lead": zeros(nb, CONV_W - 1, MIX_W), "main": conv_st.get("lead"), "samp": state_conv[0]}[name]
        act, conv_st[name] = _conv_call(act, proj0, st_in, a_conv_w[0], row(a_conv_b[0]), row(a_ln_g[0]),
                                        row(a_ln_b[0]), row0=row0, N=n, L=l, **conv_cfg[name])
    mix_a = _pw_gate(act, a_pw[0].astype(BF16), proj0)

    ar, ai, wb, wc = _s5_weights(s5_lambda_re[0], s5_lambda_im[0], s5_log_dt[0], s5_b_re[0], s5_b_im[0],
                                 s5_c_re[0], s5_c_im[0])
    s5_cfg = {"lead": dict(NB=4, TL=16), "main": dict(NB=1, TL=256), "samp": dict(NB=32, TL=8)}
    yb = None
    s5_st = {}
    for name, row0, n, l in GROUPS:
        if name == "lead":
            x0r, x0i = zeros(nb, 1, 4096), zeros(nb, 1, 4096)
        elif name == "main":
            x0r, x0i = s5_st["lead"]
        else:
            x0r, x0i = state_ssm_re[0].reshape(n, 1, 4096), state_ssm_im[0].reshape(n, 1, 4096)
        yb, xr, xi = _s5_call(yb, proj0, wb, wc, row(s5_d[0]), ar, ai, x0r, x0i,
                              row0=row0, N=n, L=l, **s5_cfg[name])
        s5_st[name] = (xr, xi)
    mix_b = _glu_gate(yb, s5_glu_w[0].astype(BF16), row(s5_glu_b[0]), proj0)

    x1 = _out_ln(x_all, mix_a, mix_b, ev_w_out[0].astype(BF16), row(ev_ln_g[0]), row(ev_ln_b[0]))

    w1 = od_w_in[0]
    w1p = jnp.concatenate([w1[:, 0:4096], w1[:, 4104:5128], w1[:, 5128:8200], w1[:, 8328:9352],
                           w1[:, 8200:8328], w1[:, 4096:4104],
                           jnp.zeros((D_MODEL, P1_N - 9352), F32)], axis=1).astype(BF16)
    proj1 = _matmul(x1, w1p, 928, 512)

    gate_b = jnp.concatenate([m_ig_b[0], m_fg_b[0], jnp.zeros((120,), F32)]).reshape(1, 128)
    m_cfg = {"lead": 16, "main": 256, "samp": 8}

    def m_pad(m):
        return jnp.pad(jnp.broadcast_to(m[:, :, None], m.shape + (128,)), ((0, 0), (0, 4), (0, 0)))

    mix_c = None
    ml_st = {}
    for name, row0, n, l in GROUPS:
        if name == "lead":
            c0, n0, m0 = zeros(nb, M_HEADS, M_D, M_D), zeros(nb, M_HEADS, M_D), zeros(nb, 8, 128)
        elif name == "main":
            c0, n0, m0 = ml_st["lead"]
        else:
            c0, n0, m0 = state_mlstm_c[0], state_mlstm_n[0], m_pad(state_mlstm_m[0])
        mix_c, c_new, n_new, m_new = _mlstm_call(mix_c, proj1, gate_b, row(m_hn_g[0]), c0, n0, m0,
                                                 row0=row0, N=n, L=l, TL=m_cfg[name])
        ml_st[name] = (c_new, n_new, m_new)

    mu = r_mu[0]
    w2p = jnp.concatenate([r_w2[0], jnp.zeros((64, MIX_W), F32)], axis=0).astype(BF16)
    a2p = jnp.concatenate([jnp.zeros((64, MIX_W), F32), r_a2[0]], axis=0).astype(BF16)
    r_wts = [row(mu[:3072]), row(mu[3072:]), row(r_w0[0]), w2p, row(r_a0[0]), a2p,
             row(r_kk[0]), row(r_ka[0]), row(r_rk[0])]

    def shift_rows(rows):
        return jnp.concatenate([proj1[rows, P1_R:P1_R + 3072], proj1[rows, P1_WA:P1_WA + 128]], axis=1)

    lead_last = slice(R_LEAD0 + N_META - 1, R_ALL, N_META)
    main_last = slice(SEQ - 1, R_MAIN, SEQ)
    samp_last = slice(R_SAMP0 + DEC_SEQ - 1, R_LEAD0, DEC_SEQ)
    sh_lead = shift_rows(lead_last)
    sh_samp0 = state_rwkv_shift[0]
    ra_cfg = {"lead": dict(RB=64, CT=16, short=True), "main": dict(RB=256, CT=16, short=False),
              "samp": dict(RB=256, CT=8, short=True)}
    ra = None
    for name, row0, n, l in GROUPS:
        if name == "lead":
            st = zeros(n * l, 3200)
            st_rkv, st_wa = st[:, :3072], st[:, 3072:]
        elif name == "main":
            st_rkv, st_wa = sh_lead[:, None, :3072], sh_lead[:, None, 3072:]
        else:
            st = jnp.repeat(sh_samp0, l, axis=0)
            st_rkv, st_wa = st[:, :3072], st[:, 3072:]
        ra = _rwkv_a_call(ra, proj1, st_rkv, st_wa, r_wts, row0=row0, N=n, L=l, **ra_cfg[name])

    rb_cfg = {"lead": dict(RB=64, CT=16, short=True), "main": dict(RB=256, CT=16, short=False),
              "samp": dict(RB=64, CT=8, short=True)}
    mix_d = None
    rs_st = {}
    for name, row0, n, l in GROUPS:
        if name == "lead":
            s0 = zeros(nb, 1024, 64)
        elif name == "main":
            s0 = rs_st["lead"]
        else:
            s0 = state_rwkv_s[0].reshape(n, 1024, 64)
        mix_d, rs_st[name] = _rwkv_b_call(mix_d, ra, proj1, row(r_ln_g[0]), row(r_ln_b[0]), s0,
                                          row0=row0, N=n, L=l, **rb_cfg[name])

    y_all = _out_ln(x1, mix_c, mix_d, od_w_out[0].astype(BF16), row(od_ln_g[0]), row(od_ln_b[0]))

    y_prompt = y_all[:R_MAIN].reshape(nb, SEQ, D_MODEL)
    y_sample = y_all[R_SAMP0:R_LEAD0].reshape(DEC_BATCH, DEC_SEQ, D_MODEL)

    def pack(group):
        xr, xi = s5_st[group]
        c_new, n_new, m_new = ml_st[group]
        n = xr.shape[0]
        last = main_last if group == "main" else samp_last
        return (conv_st[group][None],
                xr.reshape(n, S5_GROUPS, S5_STATE)[None],
                xi.reshape(n, S5_GROUPS, S5_STATE)[None],
                c_new[None], n_new[None], m_new[:, :M_HEADS, 0][None],
                rs_st[group].reshape(n, R_HEADS, R_K, R_K)[None],
                shift_rows(last)[None])

    return (y_prompt, y_sample) + pack("main") + pack("samp")
```

```python
import functools
import math

import jax
import jax.numpy as jnp
from jax import lax
from jax.experimental import pallas as pl
from jax.experimental.pallas import tpu as pltpu

F32 = jnp.float32
BF16 = jnp.bfloat16

D_MODEL = 2048
MIX_W = 1024
N_META = 16
CONV_W = 31
S5_GROUP = 16
S5_GROUPS = 64
S5_STATE = 64
M_HEADS = 4
M_D = 256
R_HEADS = 16
R_K = 64
LN_EPS = 1e-5
R_LN_EPS = 64e-5
DEPTH = 2
ALPHA = (2 * DEPTH) ** 0.25

BATCH = 4
SEQ = 2048
P_LEN = N_META + SEQ
DEC_BATCH = 128
DEC_SEQ = 8

P1_Q, P1_K, P1_V, P1_O, P1_ZC = 0, 1024, 2048, 3072, 4096
P1_R, P1_RK, P1_RV, P1_ZD, P1_WA, P1_GATE = 5120, 6144, 7168, 8192, 9216, 9344
P1_N = 9728

VMEM_LIMIT = 48 * 1024 * 1024


def _cp(sem):
    return pltpu.CompilerParams(dimension_semantics=sem, vmem_limit_bytes=VMEM_LIMIT)


def _dot(a, b):
    return jnp.dot(a, b, preferred_element_type=F32)


def _dot_nt(a, b):
    return lax.dot_general(a, b, (((1,), (1,)), ((), ())), preferred_element_type=F32)


def _dot_tn(a, b):
    return lax.dot_general(a, b, (((0,), (0,)), ((), ())), preferred_element_type=F32)


def _split2(x):
    hi = x.astype(BF16)
    lo = (x - hi.astype(F32)).astype(BF16)
    return hi, lo


def _split3(x):
    hi = x.astype(BF16)
    r1 = x - hi.astype(F32)
    mid = r1.astype(BF16)
    lo = (r1 - mid.astype(F32)).astype(BF16)
    return hi, mid, lo


def _sigmoid(x):
    return jax.nn.sigmoid(x)


def _silu(x):
    return x * jax.nn.sigmoid(x)


def _softplus(x):
    return jnp.maximum(x, 0.0) + jnp.log(1.0 + jnp.exp(-jnp.abs(x)))


def _gelu_tanh(x):
    c = math.sqrt(2.0 / math.pi)
    return x * (0.5 * (1.0 + jnp.tanh(c * (x + 0.044715 * (x * x * x)))))


def _iota(shape, axis):
    return lax.broadcasted_iota(jnp.int32, shape, axis)


def _shr(x, k):
    return lax.shift_right_logical(x, jnp.int32(k))


def _log2(n):
    k = int(round(math.log2(n)))
    assert 1 << k == n
    return k


def _block_ones(n, seg, dtype):
    r = _shr(_iota((n, n), 0), _log2(seg))
    c = _shr(_iota((n, n), 1), _log2(seg))
    return (r == c).astype(dtype)


def _segsum(x, seg):
    g = _block_ones(256, seg, BF16)
    outs = []
    for j in range(x.shape[1] // 256):
        hi, lo = _split2(x[:, 256 * j:256 * (j + 1)])
        outs.append(_dot(hi, g) + _dot(lo, g))
    return jnp.concatenate(outs, axis=1)


def _row_cumsum(x, period):
    rows = x.shape[0]
    rid = _iota(x.shape, 0)
    if period < rows:
        rid = jnp.bitwise_and(rid, period - 1)
    d = 1
    while d < min(period, rows):
        x = x + jnp.where(rid >= d, pltpu.roll(x, d, 0), 0.0)
        d *= 2
    return x


def _row_cummax(x, period):
    rows = x.shape[0]
    rid = _iota(x.shape, 0)
    if period < rows:
        rid = jnp.bitwise_and(rid, period - 1)
    d = 1
    while d < min(period, rows):
        x = jnp.maximum(x, jnp.where(rid >= d, pltpu.roll(x, d, 0), -jnp.inf))
        d *= 2
    return x


def _mm_kernel(x_ref, w_ref, o_ref):
    o_ref[...] = _dot(x_ref[...].astype(BF16), w_ref[...])


def _matmul(x, w, tm, tn):
    r, k = x.shape
    n = w.shape[1]
    return pl.pallas_call(
        _mm_kernel,
        out_shape=jax.ShapeDtypeStruct((r, n), F32),
        grid=(pl.cdiv(r, tm), n // tn),
        in_specs=[pl.BlockSpec((tm, k), lambda i, j: (i, 0)),
                  pl.BlockSpec((k, tn), lambda i, j: (0, j))],
        out_specs=pl.BlockSpec((tm, tn), lambda i, j: (i, j)),
        compiler_params=_cp(("parallel", "arbitrary")),
    )(x, w)


def _pw_kernel(a_ref, w_ref, z_ref, o_ref):
    o_ref[...] = _dot(a_ref[...].astype(BF16), w_ref[...]) * _silu(z_ref[...])


def _pw_gate(act, pw, proj0, tm, tn=512):
    r = act.shape[0]
    zb = 2048 // tn
    return pl.pallas_call(
        _pw_kernel,
        out_shape=jax.ShapeDtypeStruct((r, MIX_W), F32),
        grid=(pl.cdiv(r, tm), MIX_W // tn),
        in_specs=[pl.BlockSpec((tm, MIX_W), lambda i, j: (i, 0)),
                  pl.BlockSpec((MIX_W, tn), lambda i, j: (0, j)),
                  pl.BlockSpec((tm, tn), lambda i, j: (i, zb + j))],
        out_specs=pl.BlockSpec((tm, tn), lambda i, j: (i, j)),
        compiler_params=_cp(("parallel", "arbitrary")),
    )(act, pw, proj0)


def _glu_kernel(y_ref, wv_ref, wg_ref, bv_ref, bg_ref, z_ref, o_ref):
    y = y_ref[...].astype(BF16)
    v = _dot(y, wv_ref[...]) + bv_ref[...]
    g = _dot(y, wg_ref[...]) + bg_ref[...]
    o_ref[...] = v * _sigmoid(g) * _silu(z_ref[...])


def _glu_gate(yb, glu_w, glu_b, proj0, tm, tn=512):
    r = yb.shape[0]
    nb = MIX_W // tn
    zb = 4096 // tn
    return pl.pallas_call(
        _glu_kernel,
        out_shape=jax.ShapeDtypeStruct((r, MIX_W), F32),
        grid=(pl.cdiv(r, tm), nb),
        in_specs=[pl.BlockSpec((tm, MIX_W), lambda i, j: (i, 0)),
                  pl.BlockSpec((MIX_W, tn), lambda i, j: (0, j)),
                  pl.BlockSpec((MIX_W, tn), lambda i, j: (0, nb + j)),
                  pl.BlockSpec((1, tn), lambda i, j: (0, j)),
                  pl.BlockSpec((1, tn), lambda i, j: (0, nb + j)),
                  pl.BlockSpec((tm, tn), lambda i, j: (i, zb + j))],
        out_specs=pl.BlockSpec((tm, tn), lambda i, j: (i, j)),
        compiler_params=_cp(("parallel", "arbitrary")),
    )(yb, glu_w, glu_w, glu_b, glu_b, proj0)


def _out_ln_kernel(x_ref, ma_ref, mb_ref, wa_ref, wb_ref, g_ref, b_ref, o_ref):
    out = _dot(ma_ref[...].astype(BF16), wa_ref[...]) + _dot(mb_ref[...].astype(BF16), wb_ref[...])
    y = ALPHA * x_ref[...] + out
    mu = jnp.mean(y, axis=-1, keepdims=True)
    yc = y - mu
    var = jnp.mean(yc * yc, axis=-1, keepdims=True)
    o_ref[...] = yc * lax.rsqrt(var + LN_EPS) * g_ref[...] + b_ref[...]


def _out_ln(x, mix_a, mix_b, w_out, ln_g, ln_b, tm):
    r = x.shape[0]
    return pl.pallas_call(
        _out_ln_kernel,
        out_shape=jax.ShapeDtypeStruct((r, D_MODEL), F32),
        grid=(pl.cdiv(r, tm),),
        in_specs=[pl.BlockSpec((tm, D_MODEL), lambda i: (i, 0)),
                  pl.BlockSpec((tm, MIX_W), lambda i: (i, 0)),
                  pl.BlockSpec((tm, MIX_W), lambda i: (i, 0)),
                  pl.BlockSpec((MIX_W, D_MODEL), lambda i: (0, 0)),
                  pl.BlockSpec((MIX_W, D_MODEL), lambda i: (1, 0)),
                  pl.BlockSpec((1, D_MODEL), lambda i: (0, 0)),
                  pl.BlockSpec((1, D_MODEL), lambda i: (0, 0))],
        out_specs=pl.BlockSpec((tm, D_MODEL), lambda i: (i, 0)),
        compiler_params=_cp(("parallel",)),
    )(x, mix_a, mix_b, w_out, w_out, ln_g, ln_b)


def _conv_kernel(u_ref, g_ref, st_ref, w_ref, cb_ref, lg_ref, lb_ref, act_ref, nst_ref, hp_ref, hs_ref,
                 *, NB, TL, T):
    t = pl.program_id(1)
    rc = 16 if TL % 16 == 0 else 8
    for nb in range(NB):
        base = nb * TL

        @pl.when(t == 0)
        def _():
            hp_ref[nb, 0:2, :] = jnp.zeros((2, MIX_W), F32)
            hp_ref[nb, 2:32, :] = st_ref[nb]

        hp_ref[nb, TL + 32:TL + 40, :] = jnp.zeros((8, MIX_W), F32)
        hp_ref[nb, 32:32 + TL, :] = u_ref[base:base + TL, :] * _sigmoid(g_ref[base:base + TL, :])
        for b in range(8):
            hs_ref[b] = hp_ref[nb, b:b + TL + 32, :]

        def chunk(c, carry):
            r0 = pl.multiple_of(c * rc, rc)
            acc = jnp.zeros((rc, MIX_W), F32)
            for j in range(CONV_W):
                o = j + 2
                acc = acc + w_ref[j:j + 1, :] * hs_ref[o % 8, pl.ds(r0 + 8 * (o // 8), rc), :]
            y = acc + cb_ref[...]
            mu = jnp.mean(y, axis=-1, keepdims=True)
            yc = y - mu
            var = jnp.mean(yc * yc, axis=-1, keepdims=True)
            yn = yc * lax.rsqrt(var + LN_EPS) * lg_ref[...] + lb_ref[...]
            act_ref[pl.ds(base + r0, rc), :] = _silu(yn)
            return carry

        lax.fori_loop(0, TL // rc, chunk, 0)

        @pl.when(t == T - 1)
        def _():
            nst_ref[nb] = hp_ref[nb, TL + 2:TL + 32, :]

        if T > 1:
            hp_ref[nb, 0:32, :] = hp_ref[nb, TL:TL + 32, :]


def _conv_call(proj0, state, conv_w, conv_b, ln_g, ln_b, *, N, L, NB, TL):
    T = L // TL
    RB = NB * TL
    assert NB == 1 or T == 1
    return pl.pallas_call(
        functools.partial(_conv_kernel, NB=NB, TL=TL, T=T),
        out_shape=(jax.ShapeDtypeStruct((N * L, MIX_W), F32),
                   jax.ShapeDtypeStruct((N, CONV_W - 1, MIX_W), F32)),
        grid=(N // NB, T),
        in_specs=[pl.BlockSpec((RB, MIX_W), lambda i, t: (i * T + t, 0)),
                  pl.BlockSpec((RB, MIX_W), lambda i, t: (i * T + t, 1)),
                  pl.BlockSpec((NB, CONV_W - 1, MIX_W), lambda i, t: (i, 0, 0)),
                  pl.BlockSpec((CONV_W, MIX_W), lambda i, t: (0, 0)),
                  pl.BlockSpec((1, MIX_W), lambda i, t: (0, 0)),
                  pl.BlockSpec((1, MIX_W), lambda i, t: (0, 0)),
                  pl.BlockSpec((1, MIX_W), lambda i, t: (0, 0))],
        out_specs=(pl.BlockSpec((RB, MIX_W), lambda i, t: (i * T + t, 0)),
                   pl.BlockSpec((NB, CONV_W - 1, MIX_W), lambda i, t: (i, 0, 0))),
        scratch_shapes=[pltpu.VMEM((NB, TL + 40, MIX_W), F32),
                        pltpu.VMEM((8, TL + 32, MIX_W), F32)],
        compiler_params=_cp(("arbitrary", "arbitrary")),
    )(proj0, proj0, state, conv_w, conv_b, ln_g, ln_b)


def _s5_kernel(u_ref, wb_ref, wc_ref, d_ref, ar_ref, ai_ref, x0r_ref, x0i_ref,
               y_ref, xfr_ref, xfi_ref, xs_ref, cr_ref, ci_ref, *, NB, TL, T):
    t = pl.program_id(2)
    RB = NB * TL
    GL = TL // 8
    u = u_ref[...]
    big = _dot(u.astype(BF16), wb_ref[0])
    xr = big[:, :1024]
    xi = big[:, 1024:]
    ar = ar_ref[...]
    ai = ai_ref[...]

    def cmul(pr, pi, qr, qi):
        return pr * qr - pi * qi, pr * qi + pi * qr

    a1 = (ar, ai)
    a2 = cmul(*a1, *a1)
    a4 = cmul(*a2, *a2)
    rid = jnp.bitwise_and(_iota((RB, 1024), 0), 7)
    for d, (pr, pi) in ((1, a1), (2, a2), (4, a4)):
        sr = pltpu.roll(xr, d, 0)
        si = pltpu.roll(xi, d, 0)
        m = rid >= d
        xr, xi = (xr + jnp.where(m, pr * sr - pi * si, 0.0),
                  xi + jnp.where(m, pr * si + pi * sr, 0.0))
    xs_ref[0] = xr
    xs_ref[1] = xi
    a3 = cmul(*a2, *a1)
    a5 = cmul(*a4, *a1)
    a6 = cmul(*a4, *a2)
    a7 = cmul(*a6, *a1)
    a8 = cmul(*a4, *a4)
    r8 = _iota((8, 1024), 0)
    pwr = jnp.zeros((8, 1024), F32)
    pwi = jnp.zeros((8, 1024), F32)
    for k, (pr, pi) in enumerate((a1, a2, a3, a4, a5, a6, a7, a8)):
        pwr = jnp.where(r8 == k, pr, pwr)
        pwi = jnp.where(r8 == k, pi, pwi)

    first = t == 0

    def seq_body(nb, carry0):
        x0r = x0r_ref[nb]
        x0i = x0i_ref[nb]
        if T > 1:
            c_r = jnp.where(first, x0r, cr_ref[0:1, :])
            c_i = jnp.where(first, x0i, ci_ref[0:1, :])
        else:
            c_r, c_i = x0r, x0i

        def grp(g, c):
            c_r, c_i = c
            off = pl.multiple_of(nb * TL + g * 8, 8)
            vr = xs_ref[0, pl.ds(off, 8), :]
            vi = xs_ref[1, pl.ds(off, 8), :]
            br = jnp.broadcast_to(c_r, (8, 1024))
            bi = jnp.broadcast_to(c_i, (8, 1024))
            nr = vr + pwr * br - pwi * bi
            ni = vi + pwr * bi + pwi * br
            xs_ref[0, pl.ds(off, 8), :] = nr
            xs_ref[1, pl.ds(off, 8), :] = ni
            return nr[7:8, :], ni[7:8, :]

        c_r, c_i = lax.fori_loop(0, GL, grp, (c_r, c_i))
        if T > 1:
            cr_ref[...] = jnp.broadcast_to(c_r, (8, 1024))
            ci_ref[...] = jnp.broadcast_to(c_i, (8, 1024))

        @pl.when(t == T - 1)
        def _():
            xfr_ref[nb] = c_r
            xfi_ref[nb] = c_i

        return carry0

    lax.fori_loop(0, NB, seq_body, 0)
    y = (_dot(xs_ref[0].astype(BF16), wc_ref[0, 0:1024, :])
         + _dot(xs_ref[1].astype(BF16), wc_ref[0, 1024:2048, :]))
    y_ref[...] = _gelu_tanh(y + d_ref[...] * u)


def _s5_call(proj0, wb, wc, dvec, ar, ai, x0r, x0i, *, N, L, NB, TL):
    T = L // TL
    RB = NB * TL
    assert NB == 1 or T == 1
    ub = 3072 // 256
    st = jax.ShapeDtypeStruct((N, 1, 4096), F32)
    return pl.pallas_call(
        functools.partial(_s5_kernel, NB=NB, TL=TL, T=T),
        out_shape=(jax.ShapeDtypeStruct((N * L, MIX_W), F32), st, st),
        grid=(N // NB, 4, T),
        in_specs=[pl.BlockSpec((RB, 256), lambda i, j, t: (i * T + t, ub + j)),
                  pl.BlockSpec((1, 256, 2048), lambda i, j, t: (j, 0, 0)),
                  pl.BlockSpec((1, 2048, 256), lambda i, j, t: (j, 0, 0)),
                  pl.BlockSpec((1, 256), lambda i, j, t: (0, j)),
                  pl.BlockSpec((1, 1024), lambda i, j, t: (0, j)),
                  pl.BlockSpec((1, 1024), lambda i, j, t: (0, j)),
                  pl.BlockSpec((NB, 1, 1024), lambda i, j, t: (i, 0, j)),
                  pl.BlockSpec((NB, 1, 1024), lambda i, j, t: (i, 0, j))],
        out_specs=(pl.BlockSpec((RB, 256), lambda i, j, t: (i * T + t, j)),
                   pl.BlockSpec((NB, 1, 1024), lambda i, j, t: (i, 0, j)),
                   pl.BlockSpec((NB, 1, 1024), lambda i, j, t: (i, 0, j))),
        scratch_shapes=[pltpu.VMEM((2, RB, 1024), F32),
                        pltpu.VMEM((8, 1024), F32),
                        pltpu.VMEM((8, 1024), F32)],
        compiler_params=_cp(("arbitrary", "arbitrary", "arbitrary")),
    )(proj0, wb, wc, dvec, ar, ai, x0r, x0i)


def _mlstm_kernel(q_ref, k_ref, v_ref, o_ref, z_ref, gt_ref, gb_ref, hg_ref, c0_ref, n0_ref, m0_ref,
                  y_ref, c_ref, n_ref, m_ref, cs_ref, ns_ref, ms_ref, *, TL, T):
    t = pl.program_id(1)

    @pl.when(t == 0)
    def _():
        cs_ref[...] = c0_ref[0]
        ns_ref[...] = n0_ref[0]
        ms_ref[...] = m0_ref[0]

    G = gt_ref[...] + gb_ref[...]
    B = _row_cumsum(-_softplus(-G), TL)
    Bs = pltpu.roll(B, 124, 1)
    A = G - Bs
    CM = _row_cummax(A, TL)
    ms = ms_ref[...]
    dg = _iota((8, 128), 0) == _iota((8, 128), 1)
    mrow = jnp.sum(jnp.where(dg, ms, 0.0), axis=0, keepdims=True)
    M = jnp.maximum(mrow, CM)
    MT = Bs + M
    sel = dg.astype(BF16)
    a_hi, a_mid, a_lo = _split3(A)
    Arow = _dot_nt(sel, a_hi) + _dot_nt(sel, a_mid) + _dot_nt(sel, a_lo)
    causal = _iota((TL, TL), 0) >= _iota((TL, TL), 1)
    for h in range(M_HEADS):
        sl = slice(M_D * h, M_D * (h + 1))
        q = (q_ref[:, sl] * (M_D ** -0.5))
        qb = q.astype(BF16)
        kf = k_ref[:, sl]
        kb = kf.astype(BF16)
        vf = v_ref[:, sl]
        m_col = M[:, h:h + 1]
        mt_col = MT[:, h:h + 1]
        b_col = Bs[:, h:h + 1]
        ig_col = G[:, h:h + 1]
        logd = Arow[h:h + 1, :] - m_col
        dm = jnp.exp(jnp.where(causal, logd, -jnp.inf))
        s = _dot_nt(qb, kb) * dm
        h_intra = _dot(s.astype(BF16), vf.astype(BF16))
        n_intra = jnp.sum(s, axis=1, keepdims=True)
        m_prev = mrow[:, h:h + 1]
        inter = jnp.exp(m_prev - m_col)
        c_old = cs_ref[h]
        n_old = ns_ref[h:h + 1, :]
        h_inter = _dot(qb, c_old.astype(BF16)) * inter
        n_inter = jnp.sum(q * n_old, axis=1, keepdims=True) * inter
        denom = jnp.maximum(jnp.abs(n_intra + n_inter), jnp.exp(-mt_col))
        hh = (h_intra + h_inter) / denom
        m_new = mt_col[TL - 1:TL, :]
        b_end = b_col[TL - 1:TL, :]
        dec = jnp.exp(m_prev + b_end - m_new)
        w_s = jnp.exp(b_end - b_col + ig_col - m_new)
        cs_ref[h] = c_old * dec + _dot_tn(kb, (vf * w_s).astype(BF16))
        ns_ref[h:h + 1, :] = n_old * dec + jnp.sum(kf * w_s, axis=0, keepdims=True)
        ms_ref[h:h + 1, :] = jnp.broadcast_to(m_new, (1, 128))
        mu = jnp.mean(hh, axis=-1, keepdims=True)
        hc = hh - mu
        var = jnp.mean(hc * hc, axis=-1, keepdims=True)
        hn = hc * lax.rsqrt(var + LN_EPS) * hg_ref[:, sl]
        y_ref[:, sl] = hn * _sigmoid(o_ref[:, sl]) * _silu(z_ref[:, sl])

    @pl.when(t == T - 1)
    def _():
        c_ref[0] = cs_ref[...]
        n_ref[0] = ns_ref[...]
        m_ref[0] = ms_ref[...]


def _mlstm_call(proj1, gate_b, hn_g, c0, n0, m0, *, N, L, TL):
    T = L // TL

    def col(cb):
        return pl.BlockSpec((TL, MIX_W), lambda i, t: (i * T + t, cb))

    return pl.pallas_call(
        functools.partial(_mlstm_kernel, TL=TL, T=T),
        out_shape=(jax.ShapeDtypeStruct((N * L, MIX_W), F32),
                   jax.ShapeDtypeStruct((N, M_HEADS, M_D, M_D), F32),
                   jax.ShapeDtypeStruct((N, M_HEADS, M_D), F32),
                   jax.ShapeDtypeStruct((N, 8, 128), F32)),
        grid=(N, T),
        in_specs=[col(0), col(1), col(2), col(3), col(4),
                  pl.BlockSpec((TL, 128), lambda i, t: (i * T + t, P1_GATE // 128)),
                  pl.BlockSpec((1, 128), lambda i, t: (0, 0)),
                  pl.BlockSpec((1, MIX_W), lambda i, t: (0, 0)),
                  pl.BlockSpec((1, M_HEADS, M_D, M_D), lambda i, t: (i, 0, 0, 0)),
                  pl.BlockSpec((1, M_HEADS, M_D), lambda i, t: (i, 0, 0)),
                  pl.BlockSpec((1, 8, 128), lambda i, t: (i, 0, 0))],
        out_specs=(pl.BlockSpec((TL, MIX_W), lambda i, t: (i * T + t, 0)),
                   pl.BlockSpec((1, M_HEADS, M_D, M_D), lambda i, t: (i, 0, 0, 0)),
                   pl.BlockSpec((1, M_HEADS, M_D), lambda i, t: (i, 0, 0)),
                   pl.BlockSpec((1, 8, 128), lambda i, t: (i, 0, 0))),
        scratch_shapes=[pltpu.VMEM((M_HEADS, M_D, M_D), F32),
                        pltpu.VMEM((M_HEADS, M_D), F32),
                        pltpu.VMEM((8, 128), F32)],
        compiler_params=_cp(("arbitrary", "arbitrary")),
    )(proj1, proj1, proj1, proj1, proj1, proj1, gate_b, hn_g, c0, n0, m0)


N_RA_OUT = 9


def _rwkv_a_kernel(pr_ref, pk_ref, pv_ref, pwa_ref, hr_ref, hk_ref, hv_ref, hwa_ref, st_ref, stwa_ref,
                   mu_ref, muwa_ref, w0_ref, w2_ref, a0_ref, a2_ref, kkp_ref, ka_ref, rk_ref,
                   ah_ref, rh_ref, bh_ref, kh_ref, vo_ref, ul_ref, yl_ref, dc_ref, bo_ref,
                   s_at, s_rt, s_bt, s_kt, s_v, s_cum, *, RB, CT, N, L, U):
    HS = R_HEADS * CT
    HG = 128 // CT
    NG = R_HEADS // HG
    GW = HG * R_K
    NCH = RB // CT
    short = L == CT
    rid = _iota((RB, 1), 0)
    grow = pl.program_id(0) * RB + rid

    def shifted(p_ref, h_ref, s_ref, lo, hi, mu):
        p = p_ref[...]
        prev = pltpu.roll(p, 1, 0)
        if short:
            prev = jnp.where(jnp.bitwise_and(rid, CT - 1) == 0, s_ref[:, lo:hi], prev)
        else:
            prev = jnp.where(rid == 0, h_ref[7:8, :], prev)
            for n in range(N):
                prev = jnp.where(grow == n * L, s_ref[n:n + 1, lo:hi], prev)
        return p + (prev - p) * mu

    r = shifted(pr_ref, hr_ref, st_ref, 0, 1024, mu_ref[:, 0:1024])
    k = shifted(pk_ref, hk_ref, st_ref, 1024, 2048, mu_ref[:, 1024:2048])
    v = shifted(pv_ref, hv_ref, st_ref, 2048, 3072, mu_ref[:, 2048:3072])
    wa = shifted(pwa_ref, hwa_ref, stwa_ref, 0, 128, muwa_ref[...])
    w = -_softplus(-(w0_ref[...] + _dot(jnp.tanh(wa).astype(BF16), w2_ref[...]))) - 0.5
    wlog = -jnp.exp(w)
    a = _sigmoid(a0_ref[...] + _dot(wa.astype(BF16), a2_ref[...]))
    kk = k * kkp_ref[...]
    kk = kk / jnp.maximum(jnp.sqrt(_segsum(kk * kk, R_K)), 1e-12)
    kmod = k * (1.0 + (a - 1.0) * ka_ref[...])
    bo_ref[...] = _segsum(r * kmod * rk_ref[...], R_K) * v
    cum = _row_cumsum(wlog, CT)
    einv = jnp.exp(-cum)
    s_at[...] = (-kk) * jnp.exp(cum - wlog)
    s_rt[...] = r * jnp.exp(cum)
    s_bt[...] = kk * a * einv
    s_kt[...] = kmod * einv
    s_v[...] = v
    s_cum[...] = cum

    be_mask = (_shr(_iota((128, GW), 0), _log2(CT)) == _shr(_iota((128, GW), 1), _log2(R_K))).astype(F32)
    bd_mask = (_shr(_iota((HS, HS), 0), _log2(CT)) == _shr(_iota((HS, HS), 1), _log2(CT))).astype(F32)
    tt = _iota((CT, HS), 0)
    ss = jnp.bitwise_and(_iota((CT, HS), 1), CT - 1)
    strict = tt > ss
    incl = tt >= ss
    eye_c = (tt == ss).astype(F32)
    cat0 = lambda *xs: jnp.concatenate(xs, axis=0)

    def blockexp(x):
        return [(jnp.concatenate([x[:, GW * g:GW * (g + 1)]] * HG, axis=0) * be_mask).astype(BF16)
                for g in range(NG)]

    def gram(lhs, be):
        lb = lhs.astype(BF16)
        return jnp.concatenate([_dot_nt(lb[:, GW * g:GW * (g + 1)], be[g]) for g in range(NG)], axis=1)

    def apply(cmp, be):
        cb = cmp.astype(BF16)
        return jnp.concatenate([_dot(cb[:, 128 * g:128 * (g + 1)], be[g]) for g in range(NG)], axis=1)

    def bdiag(x):
        return jnp.concatenate([x] * R_HEADS, axis=0) * bd_mask

    def mm_hl(stack, wh, wl):
        sh, sl = _split2(stack)
        n = stack.shape[0]
        full = _dot(cat0(sh, sl), wh)
        return full[:n] + full[n:] + _dot(sh, wl)

    def chunks(i, carry):
        rows = [pl.ds(pl.multiple_of((i * U + u) * CT, CT), CT) for u in range(U)]
        ld = lambda ref: [ref[rw, :] for rw in rows]
        at, rt, bt, kt, vv, cm = ld(s_at), ld(s_rt), ld(s_bt), ld(s_kt), ld(s_v), ld(s_cum)
        each = lambda f, *xs: [f(*a_) for a_ in zip(*xs)]
        ar_ = each(cat0, at, rt)
        gb = each(lambda l_, y_: gram(l_, blockexp(y_)), ar_, bt)
        gk = each(lambda l_, y_: gram(l_, blockexp(y_)), ar_, kt)
        a_ab = each(lambda m: jnp.where(strict, m[:CT], 0.0), gb)
        a_rb = each(lambda m: jnp.where(incl, m[CT:], 0.0), gb)
        a_ak = each(lambda m: jnp.where(strict, m[:CT], 0.0), gk)
        a_rk = each(lambda m: jnp.where(incl, m[CT:], 0.0), gk)
        p = each(lambda m: eye_c + m, a_ab)
        x = a_ab
        q = a_rb
        w_hl = each(lambda m: _split2(bdiag(m)), x)
        res = each(lambda x_, q_, w_: mm_hl(cat0(x_, q_), *w_), x, q, w_hl)
        x = each(lambda r_: r_[:CT], res)
        q = each(lambda q_, r_: q_ + r_[CT:], q, res)
        pw = 2
        while pw < CT:
            w_hl = each(lambda m: _split2(bdiag(m)), x)
            if 2 * pw >= CT:
                res = each(lambda p_, q_, w_: mm_hl(cat0(p_, q_), *w_), p, q, w_hl)
                q = each(lambda q_, r_: q_ + r_[CT:], q, res)
            else:
                res = each(lambda p_, x_, q_, w_: mm_hl(cat0(p_, x_, q_), *w_), p, x, q, w_hl)
                x = each(lambda r_: r_[CT:2 * CT], res)
                q = each(lambda q_, r_: q_ + r_[2 * CT:], q, res)
            p = each(lambda p_, r_: p_ + r_[:CT], p, res)
            pw *= 2
        tq = each(cat0, p, q)
        res = each(lambda m, k_: _dot(m.astype(BF16), bdiag(k_).astype(BF16)), tq, a_ak)
        ty = each(lambda r_, k_: cat0(r_[:CT], r_[CT:] + k_), res, a_rk)
        o1 = each(lambda m, y_: apply(m, blockexp(y_)), tq, at)
        o2 = each(lambda m, y_: apply(m, blockexp(y_)), ty, vv)
        ect = each(lambda c_: jnp.exp(c_[CT - 1:CT, :]), cm)
        for u, rw in enumerate(rows):
            ah_ref[rw, :] = o1[u][:CT]
            rh_ref[rw, :] = rt[u] + o1[u][CT:]
            ul_ref[rw, :] = o2[u][:CT]
            yl_ref[rw, :] = o2[u][CT:]
            bh_ref[rw, :] = bt[u] * ect[u]
            kh_ref[rw, :] = kt[u] * ect[u]
            vo_ref[rw, :] = vv[u]
            dc_ref[rw, :] = jnp.broadcast_to(ect[u], (CT, 1024))
        return carry

    lax.fori_loop(0, NCH // U, chunks, 0)


def _rwkv_a_call(proj1, st_rkv, st_wa, wts, *, N, L, RB, CT, U):
    short = L == CT
    rows = N * L
    assert rows % RB == 0 and (RB // CT) % U == 0

    def col(cb, width=MIX_W):
        return pl.BlockSpec((RB, width), lambda i: (i, cb))

    def halo(cb, width=MIX_W):
        return pl.BlockSpec((8, width), lambda i: (jnp.maximum(i * (RB // 8) - 1, 0), cb))

    if short:
        st_specs = [pl.BlockSpec((RB, 3072), lambda i: (i, 0)), pl.BlockSpec((RB, 128), lambda i: (i, 0))]
    else:
        st_specs = [pl.BlockSpec((N, 3072), lambda i: (0, 0)), pl.BlockSpec((N, 128), lambda i: (0, 0))]

    def full(shape):
        return pl.BlockSpec(shape, lambda i: (0,) * len(shape))

    o_spec = pl.BlockSpec((RB, MIX_W), lambda i: (i, 0))
    return pl.pallas_call(
        functools.partial(_rwkv_a_kernel, RB=RB, CT=CT, N=N, L=L, U=U),
        out_shape=(jax.ShapeDtypeStruct((rows, MIX_W), F32),) * N_RA_OUT,
        grid=(rows // RB,),
        in_specs=([col(P1_R // 1024), col(P1_RK // 1024), col(P1_RV // 1024), col(P1_WA // 128, 128),
                   halo(P1_R // 1024), halo(P1_RK // 1024), halo(P1_RV // 1024), halo(P1_WA // 128, 128)]
                  + st_specs
                  + [full((1, 3072)), full((1, 128)), full((1, 1024)), full((128, 1024)), full((1, 1024)),
                     full((128, 1024)), full((1, 1024)), full((1, 1024)), full((1, 1024))]),
        out_specs=(o_spec,) * N_RA_OUT,
        scratch_shapes=[pltpu.VMEM((RB, MIX_W), F32)] * 6,
        compiler_params=_cp(("arbitrary",)),
    )(*([proj1] * 8 + [st_rkv, st_wa] + list(wts)))


def _rwkv_b_kernel(ah_ref, rh_ref, bh_ref, kh_ref, v_ref, ul_ref, yl_ref, dc_ref, bo_ref, z_ref, lg_ref, lb_ref,
                   s0_ref, y_ref, so_ref, sbd_ref, yb_ref, *, NBLK, TLB, CT, T):
    t = pl.program_id(1)
    bd_mask = (_shr(_iota((256, 256), 0), 6) == _shr(_iota((256, 256), 1), 6)).astype(F32)
    e_tile = (_iota((64, 256), 0) == jnp.bitwise_and(_iota((64, 256), 1), 63)).astype(BF16)
    e_fold = (jnp.bitwise_and(_iota((256, 64), 0), 63) == _iota((256, 64), 1)).astype(BF16)

    @pl.when(t == 0)
    def _():
        for nb in range(NBLK):
            for j in range(4):
                hi, mid, lo = _split3(s0_ref[nb, 256 * j:256 * (j + 1), :])
                full = _dot(hi, e_tile) + _dot(mid, e_tile) + _dot(lo, e_tile)
                sbd_ref[4 * nb + j] = full * bd_mask

    chains = [(nb, j, slice(256 * j, 256 * (j + 1))) for nb in range(NBLK) for j in range(4)]
    for c in range(TLB // CT):
        rows = slice(c * CT, (c + 1) * CT)
        sbs = [sbd_ref[4 * nb + j] for nb, j, cs in chains]
        outs = [_dot_nt(jnp.concatenate([ah_ref[nb, rows, cs], rh_ref[nb, rows, cs]], axis=0).astype(BF16),
                        sb.astype(BF16))
                for (nb, j, cs), sb in zip(chains, sbs)]
        upds = [_dot_tn(jnp.concatenate([o[:CT] + ul_ref[nb, rows, cs], v_ref[nb, rows, cs]],
                                        axis=0).astype(BF16),
                        jnp.concatenate([bh_ref[nb, rows, cs], kh_ref[nb, rows, cs]], axis=0).astype(BF16))
                for (nb, j, cs), o in zip(chains, outs)]
        for (nb, j, cs), sb, o, upd in zip(chains, sbs, outs, upds):
            yb_ref[nb, rows, cs] = o[CT:] + yl_ref[nb, rows, cs]
            sbd_ref[4 * nb + j] = sb * dc_ref[nb, c * CT:c * CT + 1, cs] + upd * bd_mask

    @pl.when(t == T - 1)
    def _():
        for nb in range(NBLK):
            outs = []
            for j in range(4):
                hi, mid, lo = _split3(sbd_ref[4 * nb + j])
                outs.append(_dot(hi, e_fold) + _dot(mid, e_fold) + _dot(lo, e_fold))
            so_ref[nb] = jnp.concatenate(outs, axis=0)

    cat = lambda ref: jnp.concatenate([ref[nb] for nb in range(NBLK)], axis=0)
    y = cat(yb_ref)
    mu = _segsum(y, R_K) * (1.0 / R_K)
    yc = y - mu
    var = _segsum(yc * yc, R_K) * (1.0 / R_K)
    yn = yc * lax.rsqrt(var + R_LN_EPS) * lg_ref[...] + lb_ref[...] + cat(bo_ref)
    out = yn * _silu(cat(z_ref))
    for nb in range(NBLK):
        y_ref[nb] = out[nb * TLB:(nb + 1) * TLB, :]


def _rwkv_b_call(ra, proj1, ln_g, ln_b, s0, *, N, L, NBLK, TLB, CT):
    T = L // TLB
    blk = lambda cb: pl.BlockSpec((NBLK, TLB, MIX_W), lambda i, t: (i, t, cb))
    s_spec = pl.BlockSpec((NBLK, 1024, 64), lambda i, t: (i, 0, 0))
    ra3 = [a.reshape(N, L, MIX_W) for a in ra]
    y, s_new = pl.pallas_call(
        functools.partial(_rwkv_b_kernel, NBLK=NBLK, TLB=TLB, CT=CT, T=T),
        out_shape=(jax.ShapeDtypeStruct((N, L, MIX_W), F32),
                   jax.ShapeDtypeStruct((N, 1024, 64), F32)),
        grid=(N // NBLK, T),
        in_specs=([blk(0)] * N_RA_OUT
                  + [blk(P1_ZD // 1024),
                     pl.BlockSpec((1, MIX_W), lambda i, t: (0, 0)),
                     pl.BlockSpec((1, MIX_W), lambda i, t: (0, 0)),
                     s_spec]),
        out_specs=(blk(0), s_spec),
        scratch_shapes=[pltpu.VMEM((4 * NBLK, 256, 256), F32),
                        pltpu.VMEM((NBLK, TLB, MIX_W), F32)],
        compiler_params=_cp(("arbitrary", "arbitrary")),
    )(*(ra3 + [proj1.reshape(N, L, P1_N), ln_g, ln_b, s0]))
    return y.reshape(N * L, MIX_W), s_new


def _s5_weights(lam_re, lam_im, log_dt, b_re, b_im, c_re, c_im):
    dt = jnp.exp(log_dt)[:, None]
    mag = jnp.exp(lam_re * dt)
    ar = mag * jnp.cos(lam_im * dt)
    ai = mag * jnp.sin(lam_im * dt)
    den = lam_re * lam_re + lam_im * lam_im
    qr = ((ar - 1.0) * lam_re + ai * lam_im) / den
    qi = (ai * lam_re - (ar - 1.0) * lam_im) / den
    bbr = qr[..., None] * b_re - qi[..., None] * b_im
    bbi = qr[..., None] * b_im + qi[..., None] * b_re
    eye = jnp.eye(16, dtype=F32)

    def in_blocks(bb):
        bb = bb.reshape(4, 16, S5_STATE, S5_GROUP)
        return jnp.einsum('jgph,gk->jghkp', bb, eye).reshape(4, 256, 1024)

    def out_blocks(cc):
        cc = cc.reshape(4, 16, S5_GROUP, S5_STATE)
        return jnp.einsum('jghp,gk->jgpkh', cc, eye).reshape(4, 1024, 256)

    wb = jnp.concatenate([in_blocks(bbr), in_blocks(bbi)], axis=2).astype(BF16)
    wc = jnp.concatenate([out_blocks(c_re), out_blocks(-c_im)], axis=1).astype(BF16)
    return ar.reshape(1, 4096), ai.reshape(1, 4096), wb, wc


CFG = {
    "P": dict(N=BATCH, L=P_LEN, tm=688, tm_ln=344,
              conv=dict(NB=1, TL=344), s5=dict(NB=1, TL=344), mlstm=dict(TL=344),
              ra=dict(RB=192, CT=16, U=6), rb=dict(NBLK=4, TLB=48, CT=16)),
    "S": dict(N=DEC_BATCH, L=DEC_SEQ, tm=512, tm_ln=256,
              conv=dict(NB=16, TL=8), s5=dict(NB=32, TL=8), mlstm=dict(TL=8),
              ra=dict(RB=256, CT=8, U=8), rb=dict(NBLK=8, TLB=8, CT=8)),
}


def _trunk(x, st, w, cfg):
    n, l = cfg["N"], cfg["L"]
    proj0 = _matmul(x, w["w_in0"], cfg["tm"], 512)
    act, conv_new = _conv_call(proj0, st["conv"], w["conv_w"], w["conv_b"], w["a_ln_g"], w["a_ln_b"],
                               N=n, L=l, **cfg["conv"])
    mix_a = _pw_gate(act, w["pw"], proj0, tm=cfg["tm"])
    yb, xr, xi = _s5_call(proj0, w["s5_wb"], w["s5_wc"], w["s5_d"], w["s5_ar"], w["s5_ai"],
                          st["ssm_re"].reshape(n, 1, 4096), st["ssm_im"].reshape(n, 1, 4096),
                          N=n, L=l, **cfg["s5"])
    mix_b = _glu_gate(yb, w["glu_w"], w["glu_b"], proj0, tm=cfg["tm"])
    x1 = _out_ln(x, mix_a, mix_b, w["w_out0"], w["ln_g0"], w["ln_b0"], tm=cfg["tm_ln"])

    proj1 = _matmul(x1, w["w_in1"], cfg["tm"], 512)
    m0 = jnp.pad(jnp.broadcast_to(st["m"][:, :, None], (n, M_HEADS, 128)), ((0, 0), (0, 4), (0, 0)))
    mix_c, c_new, n_new, m_new = _mlstm_call(proj1, w["gate_b"], w["hn_g"], st["c"], st["n"], m0,
                                             N=n, L=l, **cfg["mlstm"])
    sh = st["shift"]
    if l == cfg["ra"]["CT"]:
        sh = jnp.repeat(sh, l, axis=0)
    ra = _rwkv_a_call(proj1, sh[:, :3072], sh[:, 3072:], w["rwkv"], N=n, L=l, **cfg["ra"])
    mix_d, s_new = _rwkv_b_call(ra, proj1, w["r_ln_g"], w["r_ln_b"], st["s"].reshape(n, 1024, 64),
                                N=n, L=l, **cfg["rb"])
    y = _out_ln(x1, mix_c, mix_d, w["w_out1"], w["ln_g1"], w["ln_b1"], tm=cfg["tm_ln"])

    last = proj1[l - 1::l]
    shift_new = jnp.concatenate([last[:, P1_R:P1_R + 3072], last[:, P1_WA:P1_WA + 128]], axis=1)
    states = (conv_new[None],
              xr.reshape(n, S5_GROUPS, S5_STATE)[None],
              xi.reshape(n, S5_GROUPS, S5_STATE)[None],
              c_new[None], n_new[None], m_new[:, :M_HEADS, 0][None],
              s_new.reshape(n, R_HEADS, R_K, R_K)[None],
              shift_new[None])
    return y, states


def kernel(x_prompt, x_sample, state_conv, state_ssm_re, state_ssm_im, state_mlstm_c, state_mlstm_n, state_mlstm_m, state_rwkv_s, state_rwkv_shift, meta_tokens, ev_w_in, a_conv_w, a_conv_b, a_ln_g, a_ln_b, a_pw, s5_lambda_re, s5_lambda_im, s5_log_dt, s5_b_re, s5_b_im, s5_c_re, s5_c_im, s5_d, s5_glu_w, s5_glu_b, ev_w_out, ev_ln_g, ev_ln_b, od_w_in, m_ig_b, m_fg_b, m_hn_g, r_mu, r_w0, r_w2, r_a0, r_a2, r_kk, r_ka, r_rk, r_ln_g, r_ln_b, od_w_out, od_ln_g, od_ln_b):
    nb = x_prompt.shape[0]
    row = lambda vec: vec.reshape(1, -1)
    zeros = lambda *s: jnp.zeros(s, F32)

    ar, ai, wb, wc = _s5_weights(s5_lambda_re[0], s5_lambda_im[0], s5_log_dt[0], s5_b_re[0], s5_b_im[0],
                                 s5_c_re[0], s5_c_im[0])
    w1 = od_w_in[0]
    w_in1 = jnp.concatenate([w1[:, 0:4096], w1[:, 4104:5128], w1[:, 5128:8200], w1[:, 8328:9352],
                             w1[:, 8200:8328], w1[:, 4096:4104],
                             jnp.zeros((D_MODEL, P1_N - 9352), F32)], axis=1).astype(BF16)
    mu = r_mu[0]
    w = dict(
        w_in0=ev_w_in[0].astype(BF16), conv_w=a_conv_w[0], conv_b=row(a_conv_b[0]),
        a_ln_g=row(a_ln_g[0]), a_ln_b=row(a_ln_b[0]), pw=a_pw[0].astype(BF16),
        s5_wb=wb, s5_wc=wc, s5_d=row(s5_d[0]), s5_ar=ar, s5_ai=ai,
        glu_w=s5_glu_w[0].astype(BF16), glu_b=row(s5_glu_b[0]),
        w_out0=ev_w_out[0].astype(BF16), ln_g0=row(ev_ln_g[0]), ln_b0=row(ev_ln_b[0]),
        w_in1=w_in1,
        gate_b=jnp.concatenate([m_ig_b[0], m_fg_b[0], jnp.zeros((120,), F32)]).reshape(1, 128),
        hn_g=row(m_hn_g[0]),
        rwkv=[row(mu[:3072]), row(mu[3072:]), row(r_w0[0]),
              jnp.concatenate([r_w2[0], jnp.zeros((64, MIX_W), F32)], axis=0).astype(BF16),
              row(r_a0[0]),
              jnp.concatenate([jnp.zeros((64, MIX_W), F32), r_a2[0]], axis=0).astype(BF16),
              row(r_kk[0]), row(r_ka[0]), row(r_rk[0])],
        r_ln_g=row(r_ln_g[0]), r_ln_b=row(r_ln_b[0]),
        w_out1=od_w_out[0].astype(BF16), ln_g1=row(od_ln_g[0]), ln_b1=row(od_ln_b[0]),
    )

    x_p = jnp.concatenate([jnp.broadcast_to(meta_tokens[None], (nb, N_META, D_MODEL)), x_prompt],
                          axis=1).reshape(nb * P_LEN, D_MODEL)
    st_p = dict(conv=zeros(nb, CONV_W - 1, MIX_W), ssm_re=zeros(nb, 4096), ssm_im=zeros(nb, 4096),
                c=zeros(nb, M_HEADS, M_D, M_D), n=zeros(nb, M_HEADS, M_D), m=zeros(nb, M_HEADS),
                s=zeros(nb, R_HEADS, R_K, R_K), shift=zeros(nb, 3200))
    y_p, states_p = _trunk(x_p, st_p, w, CFG["P"])

    st_s = dict(conv=state_conv[0], ssm_re=state_ssm_re[0], ssm_im=state_ssm_im[0],
                c=state_mlstm_c[0], n=state_mlstm_n[0], m=state_mlstm_m[0],
                s=state_rwkv_s[0], shift=state_rwkv_shift[0])
    y_s, states_s = _trunk(x_sample.reshape(DEC_BATCH * DEC_SEQ, D_MODEL), st_s, w, CFG["S"])

    y_prompt = y_p.reshape(nb, P_LEN, D_MODEL)[:, N_META:]
    y_sample = y_s.reshape(DEC_BATCH, DEC_SEQ, D_MODEL)
    return (y_prompt, y_sample) + states_p + states_s
```

```python
import functools
import math

import jax
import jax.numpy as jnp
from jax import lax
from jax.experimental import pallas as pl
from jax.experimental.pallas import tpu as pltpu

F32 = jnp.float32
BF16 = jnp.bfloat16

D_MODEL = 2048
MIX_W = 1024
N_META = 16
CONV_W = 31
S5_GROUP = 16
S5_GROUPS = 64
S5_STATE = 64
M_HEADS = 4
M_D = 256
R_HEADS = 16
R_K = 64
LN_EPS = 1e-5
R_LN_EPS = 64e-5
DEPTH = 2
ALPHA = (2 * DEPTH) ** 0.25

BATCH = 4
SEQ = 2048
P_LEN = N_META + SEQ
DEC_BATCH = 128
DEC_SEQ = 8

P1_Q, P1_K, P1_V, P1_O, P1_ZC = 0, 1024, 2048, 3072, 4096
P1_R, P1_RK, P1_RV, P1_ZD, P1_WA, P1_GATE = 5120, 6144, 7168, 8192, 9216, 9344
P1_N = 9728

VMEM_LIMIT = 48 * 1024 * 1024


def _cp(sem):
    return pltpu.CompilerParams(dimension_semantics=sem, vmem_limit_bytes=VMEM_LIMIT)


def _dot(a, b):
    return jnp.dot(a, b, preferred_element_type=F32)


def _dot_nt(a, b):
    return lax.dot_general(a, b, (((1,), (1,)), ((), ())), preferred_element_type=F32)


def _dot_tn(a, b):
    return lax.dot_general(a, b, (((0,), (0,)), ((), ())), preferred_element_type=F32)


def _split2(x):
    hi = x.astype(BF16)
    lo = (x - hi.astype(F32)).astype(BF16)
    return hi, lo


def _split3(x):
    hi = x.astype(BF16)
    r1 = x - hi.astype(F32)
    mid = r1.astype(BF16)
    lo = (r1 - mid.astype(F32)).astype(BF16)
    return hi, mid, lo


def _sigmoid(x):
    return jax.nn.sigmoid(x)


def _silu(x):
    return x * jax.nn.sigmoid(x)


def _softplus(x):
    return jnp.maximum(x, 0.0) + jnp.log(1.0 + jnp.exp(-jnp.abs(x)))


def _gelu_tanh(x):
    c = math.sqrt(2.0 / math.pi)
    return x * (0.5 * (1.0 + jnp.tanh(c * (x + 0.044715 * (x * x * x)))))


def _iota(shape, axis):
    return lax.broadcasted_iota(jnp.int32, shape, axis)


def _shr(x, k):
    return lax.shift_right_logical(x, jnp.int32(k))


def _log2(n):
    k = int(round(math.log2(n)))
    assert 1 << k == n
    return k


def _block_ones(n, seg, dtype):
    r = _shr(_iota((n, n), 0), _log2(seg))
    c = _shr(_iota((n, n), 1), _log2(seg))
    return (r == c).astype(dtype)


def _segsum(x, seg):
    g = _block_ones(256, seg, BF16)
    outs = []
    for j in range(x.shape[1] // 256):
        hi, lo = _split2(x[:, 256 * j:256 * (j + 1)])
        outs.append(_dot(hi, g) + _dot(lo, g))
    return jnp.concatenate(outs, axis=1)


def _row_cumsum(x, period):
    rows = x.shape[0]
    rid = _iota(x.shape, 0)
    if period < rows:
        rid = jnp.bitwise_and(rid, period - 1)
    d = 1
    while d < min(period, rows):
        x = x + jnp.where(rid >= d, pltpu.roll(x, d, 0), 0.0)
        d *= 2
    return x


def _row_cummax(x, period):
    rows = x.shape[0]
    rid = _iota(x.shape, 0)
    if period < rows:
        rid = jnp.bitwise_and(rid, period - 1)
    d = 1
    while d < min(period, rows):
        x = jnp.maximum(x, jnp.where(rid >= d, pltpu.roll(x, d, 0), -jnp.inf))
        d *= 2
    return x


def _mm_kernel(x_ref, w_ref, o_ref):
    o_ref[...] = _dot(x_ref[...].astype(BF16), w_ref[...])


def _matmul(x, w, tm, tn):
    r, k = x.shape
    n = w.shape[1]
    return pl.pallas_call(
        _mm_kernel,
        out_shape=jax.ShapeDtypeStruct((r, n), F32),
        grid=(pl.cdiv(r, tm), n // tn),
        in_specs=[pl.BlockSpec((tm, k), lambda i, j: (i, 0)),
                  pl.BlockSpec((k, tn), lambda i, j: (0, j))],
        out_specs=pl.BlockSpec((tm, tn), lambda i, j: (i, j)),
        compiler_params=_cp(("parallel", "arbitrary")),
    )(x, w)


def _pw_kernel(a_ref, w_ref, z_ref, o_ref):
    o_ref[...] = _dot(a_ref[...].astype(BF16), w_ref[...]) * _silu(z_ref[...])


def _pw_gate(act, pw, proj0, tm, tn=512):
    r = act.shape[0]
    zb = 2048 // tn
    return pl.pallas_call(
        _pw_kernel,
        out_shape=jax.ShapeDtypeStruct((r, MIX_W), F32),
        grid=(pl.cdiv(r, tm), MIX_W // tn),
        in_specs=[pl.BlockSpec((tm, MIX_W), lambda i, j: (i, 0)),
                  pl.BlockSpec((MIX_W, tn), lambda i, j: (0, j)),
                  pl.BlockSpec((tm, tn), lambda i, j: (i, zb + j))],
        out_specs=pl.BlockSpec((tm, tn), lambda i, j: (i, j)),
        compiler_params=_cp(("parallel", "arbitrary")),
    )(act, pw, proj0)


def _glu_kernel(y_ref, wv_ref, wg_ref, bv_ref, bg_ref, z_ref, o_ref):
    y = y_ref[...].astype(BF16)
    v = _dot(y, wv_ref[...]) + bv_ref[...]
    g = _dot(y, wg_ref[...]) + bg_ref[...]
    o_ref[...] = v * _sigmoid(g) * _silu(z_ref[...])


def _glu_gate(yb, glu_w, glu_b, proj0, tm, tn=512):
    r = yb.shape[0]
    nb = MIX_W // tn
    zb = 4096 // tn
    return pl.pallas_call(
        _glu_kernel,
        out_shape=jax.ShapeDtypeStruct((r, MIX_W), F32),
        grid=(pl.cdiv(r, tm), nb),
        in_specs=[pl.BlockSpec((tm, MIX_W), lambda i, j: (i, 0)),
                  pl.BlockSpec((MIX_W, tn), lambda i, j: (0, j)),
                  pl.BlockSpec((MIX_W, tn), lambda i, j: (0, nb + j)),
                  pl.BlockSpec((1, tn), lambda i, j: (0, j)),
                  pl.BlockSpec((1, tn), lambda i, j: (0, nb + j)),
                  pl.BlockSpec((tm, tn), lambda i, j: (i, zb + j))],
        out_specs=pl.BlockSpec((tm, tn), lambda i, j: (i, j)),
        compiler_params=_cp(("parallel", "arbitrary")),
    )(yb, glu_w, glu_w, glu_b, glu_b, proj0)


def _out_ln_kernel(x_ref, ma_ref, mb_ref, wa_ref, wb_ref, g_ref, b_ref, o_ref):
    out = _dot(ma_ref[...].astype(BF16), wa_ref[...]) + _dot(mb_ref[...].astype(BF16), wb_ref[...])
    y = ALPHA * x_ref[...] + out
    mu = jnp.mean(y, axis=-1, keepdims=True)
    yc = y - mu
    var = jnp.mean(yc * yc, axis=-1, keepdims=True)
    o_ref[...] = yc * lax.rsqrt(var + LN_EPS) * g_ref[...] + b_ref[...]


def _out_ln(x, mix_a, mix_b, w_out, ln_g, ln_b, tm):
    r = x.shape[0]
    return pl.pallas_call(
        _out_ln_kernel,
        out_shape=jax.ShapeDtypeStruct((r, D_MODEL), F32),
        grid=(pl.cdiv(r, tm),),
        in_specs=[pl.BlockSpec((tm, D_MODEL), lambda i: (i, 0)),
                  pl.BlockSpec((tm, MIX_W), lambda i: (i, 0)),
                  pl.BlockSpec((tm, MIX_W), lambda i: (i, 0)),
                  pl.BlockSpec((MIX_W, D_MODEL), lambda i: (0, 0)),
                  pl.BlockSpec((MIX_W, D_MODEL), lambda i: (1, 0)),
                  pl.BlockSpec((1, D_MODEL), lambda i: (0, 0)),
                  pl.BlockSpec((1, D_MODEL), lambda i: (0, 0))],
        out_specs=pl.BlockSpec((tm, D_MODEL), lambda i: (i, 0)),
        compiler_params=_cp(("parallel",)),
    )(x, mix_a, mix_b, w_out, w_out, ln_g, ln_b)


def _conv_kernel(u_ref, g_ref, st_ref, w_ref, cb_ref, lg_ref, lb_ref, act_ref, nst_ref, hp_ref, hs_ref, wb_ref,
                 *, NB, TL, T):
    t = pl.program_id(1)
    rc = 8
    for j in range(CONV_W):
        wb_ref[j] = jnp.broadcast_to(w_ref[j:j + 1, :], (8, MIX_W))
    for nb in range(NB):
        base = nb * TL

        @pl.when(t == 0)
        def _():
            hp_ref[nb, 0:2, :] = jnp.zeros((2, MIX_W), F32)
            hp_ref[nb, 2:32, :] = st_ref[nb]

        hp_ref[nb, TL + 32:TL + 40, :] = jnp.zeros((8, MIX_W), F32)
        hp_ref[nb, 32:32 + TL, :] = u_ref[base:base + TL, :] * _sigmoid(g_ref[base:base + TL, :])
        for b in range(8):
            hs_ref[b] = hp_ref[nb, b:b + TL + 32, :]

        def chunk(c, carry):
            r0 = pl.multiple_of(c * rc, rc)
            acc0 = wb_ref[0] * hs_ref[2, pl.ds(r0, rc), :]
            acc1 = wb_ref[1] * hs_ref[3, pl.ds(r0, rc), :]
            for j in range(2, CONV_W):
                o = j + 2
                term = wb_ref[j] * hs_ref[o % 8, pl.ds(r0 + 8 * (o // 8), rc), :]
                if j % 2 == 0:
                    acc0 = acc0 + term
                else:
                    acc1 = acc1 + term
            act_ref[pl.ds(base + r0, rc), :] = acc0 + acc1
            return carry

        lax.fori_loop(0, TL // rc, chunk, 0, unroll=2 if (TL // rc) % 2 == 0 else 1)

        @pl.when(t == T - 1)
        def _():
            nst_ref[nb] = hp_ref[nb, TL + 2:TL + 32, :]

        if T > 1:
            hp_ref[nb, 0:32, :] = hp_ref[nb, TL:TL + 32, :]

    y = act_ref[...] + cb_ref[...]
    mu = jnp.mean(y, axis=-1, keepdims=True)
    yc = y - mu
    var = jnp.mean(yc * yc, axis=-1, keepdims=True)
    act_ref[...] = _silu(yc * lax.rsqrt(var + LN_EPS) * lg_ref[...] + lb_ref[...])


def _conv_call(proj0, state, conv_w, conv_b, ln_g, ln_b, *, N, L, NB, TL):
    T = L // TL
    RB = NB * TL
    assert NB == 1 or T == 1
    return pl.pallas_call(
        functools.partial(_conv_kernel, NB=NB, TL=TL, T=T),
        out_shape=(jax.ShapeDtypeStruct((N * L, MIX_W), F32),
                   jax.ShapeDtypeStruct((N, CONV_W - 1, MIX_W), F32)),
        grid=(N // NB, T),
        in_specs=[pl.BlockSpec((RB, MIX_W), lambda i, t: (i * T + t, 0)),
                  pl.BlockSpec((RB, MIX_W), lambda i, t: (i * T + t, 1)),
                  pl.BlockSpec((NB, CONV_W - 1, MIX_W), lambda i, t: (i, 0, 0)),
                  pl.BlockSpec((CONV_W, MIX_W), lambda i, t: (0, 0)),
                  pl.BlockSpec((1, MIX_W), lambda i, t: (0, 0)),
                  pl.BlockSpec((1, MIX_W), lambda i, t: (0, 0)),
                  pl.BlockSpec((1, MIX_W), lambda i, t: (0, 0))],
        out_specs=(pl.BlockSpec((RB, MIX_W), lambda i, t: (i * T + t, 0)),
                   pl.BlockSpec((NB, CONV_W - 1, MIX_W), lambda i, t: (i, 0, 0))),
        scratch_shapes=[pltpu.VMEM((NB, TL + 40, MIX_W), F32),
                        pltpu.VMEM((8, TL + 32, MIX_W), F32),
                        pltpu.VMEM((CONV_W, 8, MIX_W), F32)],
        compiler_params=_cp(("arbitrary", "arbitrary")),
    )(proj0, proj0, state, conv_w, conv_b, ln_g, ln_b)


def _s5_kernel(u_ref, wb_ref, wk_ref, wc_ref, d_ref, ar_ref, ai_ref, x0r_ref, x0i_ref,
               y_ref, xfr_ref, xfi_ref, xs_ref, cr_ref, ci_ref, *, NB, TL, T):
    t = pl.program_id(2)
    RB = NB * TL
    GL = TL // 8
    u = u_ref[...]
    ub = u.astype(BF16)
    big = _dot(ub, wb_ref[0])
    xs_ref[0] = big[:, :1024]
    xs_ref[1] = big[:, 1024:]
    rid = jnp.bitwise_and(_iota((RB, 256), 0), 7)
    lags = [ub] + [jnp.where(rid >= d, pltpu.roll(u, d, 0), 0.0).astype(BF16) for d in range(1, 8)]
    y_loc = _dot(jnp.concatenate(lags, axis=1), wk_ref[0])
    ar = ar_ref[...]
    ai = ai_ref[...]

    def cmul(pr, pi, qr, qi):
        return pr * qr - pi * qi, pr * qi + pi * qr

    a1 = (ar, ai)
    a2 = cmul(*a1, *a1)
    a4 = cmul(*a2, *a2)
    a3 = cmul(*a2, *a1)
    a5 = cmul(*a4, *a1)
    a6 = cmul(*a4, *a2)
    a7 = cmul(*a6, *a1)
    a8 = cmul(*a4, *a4)
    a0 = (jnp.ones_like(ar), jnp.zeros_like(ai))
    r8 = _iota((8, 1024), 0)

    def table(powers):
        tr = jnp.zeros((8, 1024), F32)
        ti = jnp.zeros((8, 1024), F32)
        for k, (pr, pi) in enumerate(powers):
            tr = jnp.where(r8 == k, pr, tr)
            ti = jnp.where(r8 == k, pi, ti)
        return tr, ti

    pwr, pwi = table((a1, a2, a3, a4, a5, a6, a7, a8))
    qwr, qwi = table((a7, a6, a5, a4, a3, a2, a1, a0))
    a8r, a8i = a8

    first = t == 0

    def seq_body(nb, carry0):
        x0r = x0r_ref[nb]
        x0i = x0i_ref[nb]
        if T > 1:
            c_r = jnp.where(first, x0r, cr_ref[0:1, :])
            c_i = jnp.where(first, x0i, ci_ref[0:1, :])
        else:
            c_r, c_i = x0r, x0i

        def grp(g, c):
            c_r, c_i = c
            off = pl.multiple_of(nb * TL + g * 8, 8)
            vr = xs_ref[0, pl.ds(off, 8), :]
            vi = xs_ref[1, pl.ds(off, 8), :]
            er = jnp.sum(qwr * vr - qwi * vi, axis=0, keepdims=True)
            ei = jnp.sum(qwr * vi + qwi * vr, axis=0, keepdims=True)
            br = jnp.broadcast_to(c_r, (8, 1024))
            bi = jnp.broadcast_to(c_i, (8, 1024))
            xs_ref[0, pl.ds(off, 8), :] = pwr * br - pwi * bi
            xs_ref[1, pl.ds(off, 8), :] = pwr * bi + pwi * br
            return a8r * c_r - a8i * c_i + er, a8r * c_i + a8i * c_r + ei

        c_r, c_i = lax.fori_loop(0, GL, grp, (c_r, c_i))
        if T > 1:
            cr_ref[...] = jnp.broadcast_to(c_r, (8, 1024))
            ci_ref[...] = jnp.broadcast_to(c_i, (8, 1024))

        @pl.when(t == T - 1)
        def _():
            xfr_ref[nb] = c_r
            xfi_ref[nb] = c_i

        return carry0

    lax.fori_loop(0, NB, seq_body, 0)
    y = (_dot(xs_ref[0].astype(BF16), wc_ref[0, 0:1024, :])
         + _dot(xs_ref[1].astype(BF16), wc_ref[0, 1024:2048, :]))
    y_ref[...] = _gelu_tanh(y + y_loc + d_ref[...] * u)


def _s5_call(proj0, wb, wk, wc, dvec, ar, ai, x0r, x0i, *, N, L, NB, TL):
    T = L // TL
    RB = NB * TL
    assert NB == 1 or T == 1
    ub = 3072 // 256
    st = jax.ShapeDtypeStruct((N, 1, 4096), F32)
    return pl.pallas_call(
        functools.partial(_s5_kernel, NB=NB, TL=TL, T=T),
        out_shape=(jax.ShapeDtypeStruct((N * L, MIX_W), F32), st, st),
        grid=(N // NB, 4, T),
        in_specs=[pl.BlockSpec((RB, 256), lambda i, j, t: (i * T + t, ub + j)),
                  pl.BlockSpec((1, 256, 2048), lambda i, j, t: (j, 0, 0)),
                  pl.BlockSpec((1, 2048, 256), lambda i, j, t: (j, 0, 0)),
                  pl.BlockSpec((1, 2048, 256), lambda i, j, t: (j, 0, 0)),
                  pl.BlockSpec((1, 256), lambda i, j, t: (0, j)),
                  pl.BlockSpec((1, 1024), lambda i, j, t: (0, j)),
                  pl.BlockSpec((1, 1024), lambda i, j, t: (0, j)),
                  pl.BlockSpec((NB, 1, 1024), lambda i, j, t: (i, 0, j)),
                  pl.BlockSpec((NB, 1, 1024), lambda i, j, t: (i, 0, j))],
        out_specs=(pl.BlockSpec((RB, 256), lambda i, j, t: (i * T + t, j)),
                   pl.BlockSpec((NB, 1, 1024), lambda i, j, t: (i, 0, j)),
                   pl.BlockSpec((NB, 1, 1024), lambda i, j, t: (i, 0, j))),
        scratch_shapes=[pltpu.VMEM((2, RB, 1024), F32),
                        pltpu.VMEM((8, 1024), F32),
                        pltpu.VMEM((8, 1024), F32)],
        compiler_params=_cp(("arbitrary", "arbitrary", "arbitrary")),
    )(proj0, wb, wk, wc, dvec, ar, ai, x0r, x0i)


def _mlstm_kernel(q_ref, k_ref, v_ref, o_ref, z_ref, gt_ref, gb_ref, hg_ref, c0_ref, n0_ref, m0_ref,
                  y_ref, c_ref, n_ref, m_ref, cs_ref, ns_ref, ms_ref, *, TL, T):
    t = pl.program_id(1)

    @pl.when(t == 0)
    def _():
        cs_ref[...] = c0_ref[0]
        ns_ref[...] = n0_ref[0]
        ms_ref[...] = m0_ref[0]

    G = gt_ref[...] + gb_ref[...]
    B = _row_cumsum(-_softplus(-G), TL)
    Bs = pltpu.roll(B, 124, 1)
    A = G - Bs
    CM = _row_cummax(A, TL)
    ms = ms_ref[...]
    dg = _iota((8, 128), 0) == _iota((8, 128), 1)
    mrow = jnp.sum(jnp.where(dg, ms, 0.0), axis=0, keepdims=True)
    M = jnp.maximum(mrow, CM)
    MT = Bs + M
    sel = dg.astype(BF16)
    a_hi, a_mid, a_lo = _split3(A)
    Arow = _dot_nt(sel, a_hi) + _dot_nt(sel, a_mid) + _dot_nt(sel, a_lo)
    causal = _iota((TL, TL), 0) >= _iota((TL, TL), 1)
    for h in range(M_HEADS):
        sl = slice(M_D * h, M_D * (h + 1))
        q = (q_ref[:, sl] * (M_D ** -0.5))
        qb = q.astype(BF16)
        kf = k_ref[:, sl]
        kb = kf.astype(BF16)
        vf = v_ref[:, sl]
        m_col = M[:, h:h + 1]
        mt_col = MT[:, h:h + 1]
        b_col = Bs[:, h:h + 1]
        ig_col = G[:, h:h + 1]
        logd = Arow[h:h + 1, :] - m_col
        dm = jnp.exp(jnp.where(causal, logd, -jnp.inf))
        s = _dot_nt(qb, kb) * dm
        h_intra = _dot(s.astype(BF16), vf.astype(BF16))
        n_intra = jnp.sum(s, axis=1, keepdims=True)
        m_prev = mrow[:, h:h + 1]
        inter = jnp.exp(m_prev - m_col)
        c_old = cs_ref[h]
        n_old = ns_ref[h:h + 1, :]
        h_inter = _dot(qb, c_old.astype(BF16)) * inter
        n_inter = jnp.sum(q * n_old, axis=1, keepdims=True) * inter
        denom = jnp.maximum(jnp.abs(n_intra + n_inter), jnp.exp(-mt_col))
        hh = (h_intra + h_inter) / denom
        m_new = mt_col[TL - 1:TL, :]
        b_end = b_col[TL - 1:TL, :]
        dec = jnp.exp(m_prev + b_end - m_new)
        w_s = jnp.exp(b_end - b_col + ig_col - m_new)
        cs_ref[h] = c_old * dec + _dot_tn(kb, (vf * w_s).astype(BF16))
        ns_ref[h:h + 1, :] = n_old * dec + jnp.sum(kf * w_s, axis=0, keepdims=True)
        ms_ref[h:h + 1, :] = jnp.broadcast_to(m_new, (1, 128))
        mu = jnp.mean(hh, axis=-1, keepdims=True)
        hc = hh - mu
        var = jnp.mean(hc * hc, axis=-1, keepdims=True)
        hn = hc * lax.rsqrt(var + LN_EPS) * hg_ref[:, sl]
        y_ref[:, sl] = hn * _sigmoid(o_ref[:, sl]) * _silu(z_ref[:, sl])

    @pl.when(t == T - 1)
    def _():
        c_ref[0] = cs_ref[...]
        n_ref[0] = ns_ref[...]
        m_ref[0] = ms_ref[...]


def _mlstm_call(proj1, gate_b, hn_g, c0, n0, m0, *, N, L, TL):
    T = L // TL

    def col(cb):
        return pl.BlockSpec((TL, MIX_W), lambda i, t: (i * T + t, cb))

    return pl.pallas_call(
        functools.partial(_mlstm_kernel, TL=TL, T=T),
        out_shape=(jax.ShapeDtypeStruct((N * L, MIX_W), F32),
                   jax.ShapeDtypeStruct((N, M_HEADS, M_D, M_D), F32),
                   jax.ShapeDtypeStruct((N, M_HEADS, M_D), F32),
                   jax.ShapeDtypeStruct((N, 8, 128), F32)),
        grid=(N, T),
        in_specs=[col(0), col(1), col(2), col(3), col(4),
                  pl.BlockSpec((TL, 128), lambda i, t: (i * T + t, P1_GATE // 128)),
                  pl.BlockSpec((1, 128), lambda i, t: (0, 0)),
                  pl.BlockSpec((1, MIX_W), lambda i, t: (0, 0)),
                  pl.BlockSpec((1, M_HEADS, M_D, M_D), lambda i, t: (i, 0, 0, 0)),
                  pl.BlockSpec((1, M_HEADS, M_D), lambda i, t: (i, 0, 0)),
                  pl.BlockSpec((1, 8, 128), lambda i, t: (i, 0, 0))],
        out_specs=(pl.BlockSpec((TL, MIX_W), lambda i, t: (i * T + t, 0)),
                   pl.BlockSpec((1, M_HEADS, M_D, M_D), lambda i, t: (i, 0, 0, 0)),
                   pl.BlockSpec((1, M_HEADS, M_D), lambda i, t: (i, 0, 0)),
                   pl.BlockSpec((1, 8, 128), lambda i, t: (i, 0, 0))),
        scratch_shapes=[pltpu.VMEM((M_HEADS, M_D, M_D), F32),
                        pltpu.VMEM((M_HEADS, M_D), F32),
                        pltpu.VMEM((8, 128), F32)],
        compiler_params=_cp(("arbitrary", "arbitrary")),
    )(proj1, proj1, proj1, proj1, proj1, proj1, gate_b, hn_g, c0, n0, m0)


N_RA_OUT = 9


def _rwkv_a_kernel(pr_ref, pk_ref, pv_ref, pwa_ref, hr_ref, hk_ref, hv_ref, hwa_ref, st_ref, stwa_ref,
                   mu_ref, muwa_ref, w0_ref, w2_ref, a0_ref, a2_ref, kkp_ref, ka_ref, rk_ref,
                   ah_ref, rh_ref, bh_ref, kh_ref, vo_ref, ul_ref, yl_ref, dc_ref, bo_ref, shr_ref, shw_ref,
                   s_at, s_rt, s_bt, s_kt, s_v, s_cum, *, RB, CT, N, L, U):
    HS = R_HEADS * CT
    HG = 128 // CT
    NG = R_HEADS // HG
    GW = HG * R_K
    NCH = RB // CT
    short = L == CT
    rid = _iota((RB, 1), 0)
    grow = pl.program_id(0) * RB + rid

    def shifted(p_ref, h_ref, s_ref, lo, hi, mu):
        p = p_ref[...]
        prev = pltpu.roll(p, 1, 0)
        if short:
            prev = jnp.where(jnp.bitwise_and(rid, CT - 1) == 0, s_ref[:, lo:hi], prev)
        else:
            prev = jnp.where(rid == 0, h_ref[7:8, :], prev)
            for n in range(N):
                prev = jnp.where(grow == n * L, s_ref[n:n + 1, lo:hi], prev)
        return p + (prev - p) * mu

    raw = ((pr_ref, shr_ref, 0), (pk_ref, shr_ref, 1024), (pv_ref, shr_ref, 2048), (pwa_ref, shw_ref, 0))
    if short:
        nseq = RB // CT
        sel = (_iota((nseq, RB), 1) == _iota((nseq, RB), 0) * CT + (CT - 1)).astype(BF16)
        for src, dst, lo in raw:
            hi, mid, low = _split3(src[...])
            dst[:, lo:lo + src.shape[1]] = _dot(sel, hi) + _dot(sel, mid) + _dot(sel, low)
    else:
        for n in range(N):
            tile, off = divmod(n * L + L - 1, RB)

            @pl.when(pl.program_id(0) == tile)
            def _():
                for src, dst, lo in raw:
                    dst[n:n + 1, lo:lo + src.shape[1]] = src[off:off + 1, :]

    r = shifted(pr_ref, hr_ref, st_ref, 0, 1024, mu_ref[:, 0:1024])
    k = shifted(pk_ref, hk_ref, st_ref, 1024, 2048, mu_ref[:, 1024:2048])
    v = shifted(pv_ref, hv_ref, st_ref, 2048, 3072, mu_ref[:, 2048:3072])
    wa = shifted(pwa_ref, hwa_ref, stwa_ref, 0, 128, muwa_ref[...])
    w = -_softplus(-(w0_ref[...] + _dot(jnp.tanh(wa).astype(BF16), w2_ref[...]))) - 0.5
    wlog = -jnp.exp(w)
    a = _sigmoid(a0_ref[...] + _dot(wa.astype(BF16), a2_ref[...]))
    kk = k * kkp_ref[...]
    kk = kk / jnp.maximum(jnp.sqrt(_segsum(kk * kk, R_K)), 1e-12)
    kmod = k * (1.0 + (a - 1.0) * ka_ref[...])
    bo_ref[...] = _segsum(r * kmod * rk_ref[...], R_K) * v
    cum = _row_cumsum(wlog, CT)
    einv = jnp.exp(-cum)
    s_at[...] = (-kk) * jnp.exp(cum - wlog)
    s_rt[...] = r * jnp.exp(cum)
    s_bt[...] = kk * a * einv
    s_kt[...] = kmod * einv
    s_v[...] = v
    s_cum[...] = cum

    be_mask = (_shr(_iota((128, GW), 0), _log2(CT)) == _shr(_iota((128, GW), 1), _log2(R_K))).astype(F32)
    bd_mask = (_shr(_iota((HS, HS), 0), _log2(CT)) == _shr(_iota((HS, HS), 1), _log2(CT))).astype(F32)
    tt = _iota((CT, HS), 0)
    ss = jnp.bitwise_and(_iota((CT, HS), 1), CT - 1)
    strict = tt > ss
    incl = tt >= ss
    eye_c = (tt == ss).astype(F32)
    cat0 = lambda *xs: jnp.concatenate(xs, axis=0)

    def blockexp(x):
        return [(jnp.concatenate([x[:, GW * g:GW * (g + 1)]] * HG, axis=0) * be_mask).astype(BF16)
                for g in range(NG)]

    def gram(lhs, be):
        lb = lhs.astype(BF16)
        return jnp.concatenate([_dot_nt(lb[:, GW * g:GW * (g + 1)], be[g]) for g in range(NG)], axis=1)

    def apply(cmp, be):
        cb = cmp.astype(BF16)
        return jnp.concatenate([_dot(cb[:, 128 * g:128 * (g + 1)], be[g]) for g in range(NG)], axis=1)

    def bdiag(x):
        return jnp.concatenate([x] * R_HEADS, axis=0) * bd_mask

    def mm_hl(stack, wh, wl):
        sh, sl = _split2(stack)
        n = stack.shape[0]
        full = _dot(cat0(sh, sl), wh)
        return full[:n] + full[n:] + _dot(sh, wl)

    def chunks(i, carry):
        rows = [pl.ds(pl.multiple_of((i * U + u) * CT, CT), CT) for u in range(U)]
        ld = lambda ref: [ref[rw, :] for rw in rows]
        at, rt, bt, kt, vv, cm = ld(s_at), ld(s_rt), ld(s_bt), ld(s_kt), ld(s_v), ld(s_cum)
        each = lambda f, *xs: [f(*a_) for a_ in zip(*xs)]
        ar_ = each(cat0, at, rt)
        gb = each(lambda l_, y_: gram(l_, blockexp(y_)), ar_, bt)
        gk = each(lambda l_, y_: gram(l_, blockexp(y_)), ar_, kt)
        a_ab = each(lambda m: jnp.where(strict, m[:CT], 0.0), gb)
        a_rb = each(lambda m: jnp.where(incl, m[CT:], 0.0), gb)
        a_ak = each(lambda m: jnp.where(strict, m[:CT], 0.0), gk)
        a_rk = each(lambda m: jnp.where(incl, m[CT:], 0.0), gk)
        p = each(lambda m: eye_c + m, a_ab)
        x = a_ab
        q = a_rb
        w_hl = each(lambda m: _split2(bdiag(m)), x)
        res = each(lambda x_, q_, w_: mm_hl(cat0(x_, q_), *w_), x, q, w_hl)
        x = each(lambda r_: r_[:CT], res)
        q = each(lambda q_, r_: q_ + r_[CT:], q, res)
        pw = 2
        while pw < CT:
            w_hl = each(lambda m: _split2(bdiag(m)), x)
            if 2 * pw >= CT:
                res = each(lambda p_, q_, w_: mm_hl(cat0(p_, q_), *w_), p, q, w_hl)
                q = each(lambda q_, r_: q_ + r_[CT:], q, res)
            else:
                res = each(lambda p_, x_, q_, w_: mm_hl(cat0(p_, x_, q_), *w_), p, x, q, w_hl)
                x = each(lambda r_: r_[CT:2 * CT], res)
                q = each(lambda q_, r_: q_ + r_[2 * CT:], q, res)
            p = each(lambda p_, r_: p_ + r_[:CT], p, res)
            pw *= 2
        tq = each(cat0, p, q)
        res = each(lambda m, k_: _dot(m.astype(BF16), bdiag(k_).astype(BF16)), tq, a_ak)
        ty = each(lambda r_, k_: cat0(r_[:CT], r_[CT:] + k_), res, a_rk)
        o1 = each(lambda m, y_: apply(m, blockexp(y_)), tq, at)
        o2 = each(lambda m, y_: apply(m, blockexp(y_)), ty, vv)
        ect = each(lambda c_: jnp.exp(c_[CT - 1:CT, :]), cm)
        for u, rw in enumerate(rows):
            ah_ref[rw, :] = o1[u][:CT]
            rh_ref[rw, :] = rt[u] + o1[u][CT:]
            ul_ref[rw, :] = o2[u][:CT]
            yl_ref[rw, :] = o2[u][CT:]
            bh_ref[rw, :] = bt[u] * ect[u]
            kh_ref[rw, :] = kt[u] * ect[u]
            vo_ref[rw, :] = vv[u]
            dc_ref[rw, :] = jnp.broadcast_to(ect[u], (CT, 1024))
        return carry

    lax.fori_loop(0, NCH // U, chunks, 0)


def _rwkv_a_call(proj1, st_rkv, st_wa, wts, *, N, L, RB, CT, U):
    short = L == CT
    rows = N * L
    assert rows % RB == 0 and (RB // CT) % U == 0

    def col(cb, width=MIX_W):
        return pl.BlockSpec((RB, width), lambda i: (i, cb))

    def halo(cb, width=MIX_W):
        return pl.BlockSpec((8, width), lambda i: (jnp.maximum(i * (RB // 8) - 1, 0), cb))

    if short:
        st_specs = [pl.BlockSpec((RB, 3072), lambda i: (i, 0)), pl.BlockSpec((RB, 128), lambda i: (i, 0))]
    else:
        st_specs = [pl.BlockSpec((N, 3072), lambda i: (0, 0)), pl.BlockSpec((N, 128), lambda i: (0, 0))]

    def full(shape):
        return pl.BlockSpec(shape, lambda i: (0,) * len(shape))

    o_spec = pl.BlockSpec((RB, MIX_W), lambda i: (i, 0))
    if short:
        sh_specs = (pl.BlockSpec((RB // CT, 3072), lambda i: (i, 0)), pl.BlockSpec((RB // CT, 128), lambda i: (i, 0)))
    else:
        sh_specs = (pl.BlockSpec((N, 3072), lambda i: (0, 0)), pl.BlockSpec((N, 128), lambda i: (0, 0)))
    outs = pl.pallas_call(
        functools.partial(_rwkv_a_kernel, RB=RB, CT=CT, N=N, L=L, U=U),
        out_shape=((jax.ShapeDtypeStruct((rows, MIX_W), F32),) * N_RA_OUT
                   + (jax.ShapeDtypeStruct((N, 3072), F32), jax.ShapeDtypeStruct((N, 128), F32))),
        grid=(rows // RB,),
        in_specs=([col(P1_R // 1024), col(P1_RK // 1024), col(P1_RV // 1024), col(P1_WA // 128, 128),
                   halo(P1_R // 1024), halo(P1_RK // 1024), halo(P1_RV // 1024), halo(P1_WA // 128, 128)]
                  + st_specs
                  + [full((1, 3072)), full((1, 128)), full((1, 1024)), full((128, 1024)), full((1, 1024)),
                     full((128, 1024)), full((1, 1024)), full((1, 1024)), full((1, 1024))]),
        out_specs=(o_spec,) * N_RA_OUT + sh_specs,
        scratch_shapes=[pltpu.VMEM((RB, MIX_W), F32)] * 6,
        compiler_params=_cp(("arbitrary",)),
    )(*([proj1] * 8 + [st_rkv, st_wa] + list(wts)))
    return outs[:N_RA_OUT], jnp.concatenate(outs[N_RA_OUT:], axis=1)


def _rwkv_b_kernel(ah_ref, rh_ref, bh_ref, kh_ref, v_ref, ul_ref, yl_ref, dc_ref, bo_ref, z_ref, lg_ref, lb_ref,
                   s0_ref, y_ref, so_ref, sbd_ref, yb_ref, *, NBLK, TLB, CT, T):
    t = pl.program_id(1)
    bd_mask = (_shr(_iota((256, 256), 0), 6) == _shr(_iota((256, 256), 1), 6)).astype(F32)
    e_tile = (_iota((64, 256), 0) == jnp.bitwise_and(_iota((64, 256), 1), 63)).astype(BF16)
    e_fold = (jnp.bitwise_and(_iota((256, 64), 0), 63) == _iota((256, 64), 1)).astype(BF16)

    @pl.when(t == 0)
    def _():
        for nb in range(NBLK):
            for j in range(4):
                hi, lo = _split2(s0_ref[nb, 256 * j:256 * (j + 1), :])
                full = _dot(jnp.concatenate([hi, lo], axis=0), e_tile)
                full = full[:256] + full[256:]
                sbd_ref[4 * nb + j] = full * bd_mask

    chains = [(nb, j, slice(256 * j, 256 * (j + 1))) for nb in range(NBLK) for j in range(4)]
    for c in range(TLB // CT):
        rows = slice(c * CT, (c + 1) * CT)
        sbs = [sbd_ref[4 * nb + j] for nb, j, cs in chains]
        outs = [_dot_nt(jnp.concatenate([ah_ref[nb, rows, cs], rh_ref[nb, rows, cs]], axis=0).astype(BF16),
                        sb.astype(BF16))
                for (nb, j, cs), sb in zip(chains, sbs)]
        upds = [_dot_tn(jnp.concatenate([o[:CT] + ul_ref[nb, rows, cs], v_ref[nb, rows, cs]],
                                        axis=0).astype(BF16),
                        jnp.concatenate([bh_ref[nb, rows, cs], kh_ref[nb, rows, cs]], axis=0).astype(BF16))
                for (nb, j, cs), o in zip(chains, outs)]
        for (nb, j, cs), sb, o, upd in zip(chains, sbs, outs, upds):
            yb_ref[nb, rows, cs] = o[CT:] + yl_ref[nb, rows, cs]
            sbd_ref[4 * nb + j] = sb * dc_ref[nb, c * CT:c * CT + 1, cs] + upd * bd_mask

    @pl.when(t == T - 1)
    def _():
        for nb in range(NBLK):
            outs = []
            for j in range(4):
                hi, lo = _split2(sbd_ref[4 * nb + j])
                both = _dot(jnp.concatenate([hi, lo], axis=0), e_fold)
                outs.append(both[:256] + both[256:])
            so_ref[nb] = jnp.concatenate(outs, axis=0)

    cat = lambda ref: jnp.concatenate([ref[nb] for nb in range(NBLK)], axis=0)
    y = cat(yb_ref)
    mu = _segsum(y, R_K) * (1.0 / R_K)
    yc = y - mu
    var = _segsum(yc * yc, R_K) * (1.0 / R_K)
    yn = yc * lax.rsqrt(var + R_LN_EPS) * lg_ref[...] + lb_ref[...] + cat(bo_ref)
    out = yn * _silu(cat(z_ref))
    for nb in range(NBLK):
        y_ref[nb] = out[nb * TLB:(nb + 1) * TLB, :]


def _rwkv_b_call(ra, proj1, ln_g, ln_b, s0, *, N, L, NBLK, TLB, CT):
    T = L // TLB
    blk = lambda cb: pl.BlockSpec((NBLK, TLB, MIX_W), lambda i, t: (i, t, cb))
    s_spec = pl.BlockSpec((NBLK, 1024, 64), lambda i, t: (i, 0, 0))
    ra3 = [a.reshape(N, L, MIX_W) for a in ra]
    y, s_new = pl.pallas_call(
        functools.partial(_rwkv_b_kernel, NBLK=NBLK, TLB=TLB, CT=CT, T=T),
        out_shape=(jax.ShapeDtypeStruct((N, L, MIX_W), F32),
                   jax.ShapeDtypeStruct((N, 1024, 64), F32)),
        grid=(N // NBLK, T),
        in_specs=([blk(0)] * N_RA_OUT
                  + [blk(P1_ZD // 1024),
                     pl.BlockSpec((1, MIX_W), lambda i, t: (0, 0)),
                     pl.BlockSpec((1, MIX_W), lambda i, t: (0, 0)),
                     s_spec]),
        out_specs=(blk(0), s_spec),
        scratch_shapes=[pltpu.VMEM((4 * NBLK, 256, 256), F32),
                        pltpu.VMEM((NBLK, TLB, MIX_W), F32)],
        compiler_params=_cp(("arbitrary", "arbitrary")),
    )(*(ra3 + [proj1.reshape(N, L, P1_N), ln_g, ln_b, s0]))
    return y.reshape(N * L, MIX_W), s_new


def _s5_weights(lam_re, lam_im, log_dt, b_re, b_im, c_re, c_im):
    dt = jnp.exp(log_dt)[:, None]
    mag = jnp.exp(lam_re * dt)
    ar = mag * jnp.cos(lam_im * dt)
    ai = mag * jnp.sin(lam_im * dt)
    den = lam_re * lam_re + lam_im * lam_im
    qr = ((ar - 1.0) * lam_re + ai * lam_im) / den
    qi = (ai * lam_re - (ar - 1.0) * lam_im) / den
    bbr = qr[..., None] * b_re - qi[..., None] * b_im
    bbi = qr[..., None] * b_im + qi[..., None] * b_re
    eye = jnp.eye(16, dtype=F32)

    def in_blocks(bb):
        bb = bb.reshape(4, 16, S5_STATE, S5_GROUP)
        return jnp.einsum('jgph,gk->jghkp', bb, eye).reshape(4, 256, 1024)

    def out_blocks(cc):
        cc = cc.reshape(4, 16, S5_GROUP, S5_STATE)
        return jnp.einsum('jghp,gk->jgpkh', cc, eye).reshape(4, 1024, 256)

    wb = jnp.concatenate([in_blocks(bbr), in_blocks(bbi)], axis=2).astype(BF16)
    wc = jnp.concatenate([out_blocks(c_re), out_blocks(-c_im)], axis=1).astype(BF16)
    pr, pi = jnp.ones_like(ar), jnp.zeros_like(ai)
    lag = []
    for _ in range(8):
        cpr = c_re * pr[:, None, :] - c_im * pi[:, None, :]
        cpi = c_re * pi[:, None, :] + c_im * pr[:, None, :]
        lag.append(jnp.einsum('gop,gph->goh', cpr, bbr, precision=lax.Precision.HIGHEST)
                   - jnp.einsum('gop,gph->goh', cpi, bbi, precision=lax.Precision.HIGHEST))
        pr, pi = pr * ar - pi * ai, pr * ai + pi * ar
    kd = jnp.stack(lag).reshape(8, 4, 16, S5_GROUP, S5_GROUP)
    wk = jnp.einsum('djgoh,gk->jdghko', kd, eye).reshape(4, 8 * 256, 256).astype(BF16)
    return ar.reshape(1, 4096), ai.reshape(1, 4096), wb, wk, wc


CFG = {
    "P": dict(N=BATCH, L=P_LEN, tm_mm=1376, tm=688, tm_ln=344,
              conv=dict(NB=1, TL=344), s5=dict(NB=1, TL=344), mlstm=dict(TL=344),
              ra=dict(RB=192, CT=16, U=6), rb=dict(NBLK=4, TLB=48, CT=16)),
    "S": dict(N=DEC_BATCH, L=DEC_SEQ, tm_mm=1024, tm=512, tm_ln=256,
              conv=dict(NB=16, TL=8), s5=dict(NB=32, TL=8), mlstm=dict(TL=8),
              ra=dict(RB=256, CT=8, U=8), rb=dict(NBLK=8, TLB=8, CT=8)),
}


def _trunk(x, st, w, cfg):
    n, l = cfg["N"], cfg["L"]
    proj0 = _matmul(x, w["w_in0"], cfg["tm_mm"], 512)
    act, conv_new = _conv_call(proj0, st["conv"], w["conv_w"], w["conv_b"], w["a_ln_g"], w["a_ln_b"],
                               N=n, L=l, **cfg["conv"])
    mix_a = _pw_gate(act, w["pw"], proj0, tm=cfg["tm"])
    yb, xr, xi = _s5_call(proj0, w["s5_wb"], w["s5_wk"], w["s5_wc"], w["s5_d"], w["s5_ar"], w["s5_ai"],
                          st["ssm_re"].reshape(n, 1, 4096), st["ssm_im"].reshape(n, 1, 4096),
                          N=n, L=l, **cfg["s5"])
    mix_b = _glu_gate(yb, w["glu_w"], w["glu_b"], proj0, tm=cfg["tm"])
    x1 = _out_ln(x, mix_a, mix_b, w["w_out0"], w["ln_g0"], w["ln_b0"], tm=cfg["tm_ln"])

    proj1 = _matmul(x1, w["w_in1"], cfg["tm_mm"], 512)
    m0 = jnp.pad(jnp.broadcast_to(st["m"][:, :, None], (n, M_HEADS, 128)), ((0, 0), (0, 4), (0, 0)))
    mix_c, c_new, n_new, m_new = _mlstm_call(proj1, w["gate_b"], w["hn_g"], st["c"], st["n"], m0,
                                             N=n, L=l, **cfg["mlstm"])
    sh = st["shift"]
    if l == cfg["ra"]["CT"]:
        sh = jnp.repeat(sh, l, axis=0)
    ra, shift_new = _rwkv_a_call(proj1, sh[:, :3072], sh[:, 3072:], w["rwkv"], N=n, L=l, **cfg["ra"])
    mix_d, s_new = _rwkv_b_call(ra, proj1, w["r_ln_g"], w["r_ln_b"], st["s"].reshape(n, 1024, 64),
                                N=n, L=l, **cfg["rb"])
    y = _out_ln(x1, mix_c, mix_d, w["w_out1"], w["ln_g1"], w["ln_b1"], tm=cfg["tm_ln"])

    states = (conv_new[None],
              xr.reshape(n, S5_GROUPS, S5_STATE)[None],
              xi.reshape(n, S5_GROUPS, S5_STATE)[None],
              c_new[None], n_new[None], m_new[:, :M_HEADS, 0][None],
              s_new.reshape(n, R_HEADS, R_K, R_K)[None],
              shift_new[None])
    return y, states


def kernel(x_prompt, x_sample, state_conv, state_ssm_re, state_ssm_im, state_mlstm_c, state_mlstm_n, state_mlstm_m, state_rwkv_s, state_rwkv_shift, meta_tokens, ev_w_in, a_conv_w, a_conv_b, a_ln_g, a_ln_b, a_pw, s5_lambda_re, s5_lambda_im, s5_log_dt, s5_b_re, s5_b_im, s5_c_re, s5_c_im, s5_d, s5_glu_w, s5_glu_b, ev_w_out, ev_ln_g, ev_ln_b, od_w_in, m_ig_b, m_fg_b, m_hn_g, r_mu, r_w0, r_w2, r_a0, r_a2, r_kk, r_ka, r_rk, r_ln_g, r_ln_b, od_w_out, od_ln_g, od_ln_b):
    nb = x_prompt.shape[0]
    row = lambda vec: vec.reshape(1, -1)
    zeros = lambda *s: jnp.zeros(s, F32)

    ar, ai, wb, wk, wc = _s5_weights(s5_lambda_re[0], s5_lambda_im[0], s5_log_dt[0], s5_b_re[0], s5_b_im[0],
                                     s5_c_re[0], s5_c_im[0])
    w1 = od_w_in[0]
    w_in1 = jnp.concatenate([w1[:, 0:4096], w1[:, 4104:5128], w1[:, 5128:8200], w1[:, 8328:9352],
                             w1[:, 8200:8328], w1[:, 4096:4104],
                             jnp.zeros((D_MODEL, P1_N - 9352), F32)], axis=1).astype(BF16)
    mu = r_mu[0]
    w = dict(
        w_in0=ev_w_in[0].astype(BF16), conv_w=a_conv_w[0], conv_b=row(a_conv_b[0]),
        a_ln_g=row(a_ln_g[0]), a_ln_b=row(a_ln_b[0]), pw=a_pw[0].astype(BF16),
        s5_wb=wb, s5_wk=wk, s5_wc=wc, s5_d=row(s5_d[0]), s5_ar=ar, s5_ai=ai,
        glu_w=s5_glu_w[0].astype(BF16), glu_b=row(s5_glu_b[0]),
        w_out0=ev_w_out[0].astype(BF16), ln_g0=row(ev_ln_g[0]), ln_b0=row(ev_ln_b[0]),
        w_in1=w_in1,
        gate_b=jnp.concatenate([m_ig_b[0], m_fg_b[0], jnp.zeros((120,), F32)]).reshape(1, 128),
        hn_g=row(m_hn_g[0]),
        rwkv=[row(mu[:3072]), row(mu[3072:]), row(r_w0[0]),
              jnp.concatenate([r_w2[0], jnp.zeros((64, MIX_W), F32)], axis=0).astype(BF16),
              row(r_a0[0]),
              jnp.concatenate([jnp.zeros((64, MIX_W), F32), r_a2[0]], axis=0).astype(BF16),
              row(r_kk[0]), row(r_ka[0]), row(r_rk[0])],
        r_ln_g=row(r_ln_g[0]), r_ln_b=row(r_ln_b[0]),
        w_out1=od_w_out[0].astype(BF16), ln_g1=row(od_ln_g[0]), ln_b1=row(od_ln_b[0]),
    )

    x_p = jnp.concatenate([jnp.broadcast_to(meta_tokens[None], (nb, N_META, D_MODEL)), x_prompt],
                          axis=1).reshape(nb * P_LEN, D_MODEL)
    st_p = dict(conv=zeros(nb, CONV_W - 1, MIX_W), ssm_re=zeros(nb, 4096), ssm_im=zeros(nb, 4096),
                c=zeros(nb, M_HEADS, M_D, M_D), n=zeros(nb, M_HEADS, M_D), m=zeros(nb, M_HEADS),
                s=zeros(nb, R_HEADS, R_K, R_K), shift=zeros(nb, 3200))
    y_p, states_p = _trunk(x_p, st_p, w, CFG["P"])

    st_s = dict(conv=state_conv[0], ssm_re=state_ssm_re[0], ssm_im=state_ssm_im[0],
                c=state_mlstm_c[0], n=state_mlstm_n[0], m=state_mlstm_m[0],
                s=state_rwkv_s[0], shift=state_rwkv_shift[0])
    y_s, states_s = _trunk(x_sample.reshape(DEC_BATCH * DEC_SEQ, D_MODEL), st_s, w, CFG["S"])

    y_prompt = y_p.reshape(nb, P_LEN, D_MODEL)[:, N_META:]
    y_sample = y_s.reshape(DEC_BATCH, DEC_SEQ, D_MODEL)
    return (y_prompt, y_sample) + states_p + states_s
```

```python
import functools
import math

import jax
import jax.numpy as jnp
from jax import lax
from jax.experimental import pallas as pl
from jax.experimental.pallas import tpu as pltpu

F32 = jnp.float32
BF16 = jnp.bfloat16

D_MODEL = 2048
MIX_W = 1024
N_META = 16
CONV_W = 31
S5_GROUP = 16
S5_GROUPS = 64
S5_STATE = 64
M_HEADS = 4
M_D = 256
R_HEADS = 16
R_K = 64
LN_EPS = 1e-5
R_LN_EPS = 64e-5
DEPTH = 2
ALPHA = (2 * DEPTH) ** 0.25

BATCH = 4
SEQ = 2048
P_LEN = N_META + SEQ
DEC_BATCH = 128
DEC_SEQ = 8

P1_Q, P1_K, P1_V, P1_O, P1_ZC = 0, 1024, 2048, 3072, 4096
P1_R, P1_RK, P1_RV, P1_ZD, P1_WA, P1_GATE = 5120, 6144, 7168, 8192, 9216, 9344
P1_N = 9728

VMEM_LIMIT = 48 * 1024 * 1024


def _cp(sem):
    return pltpu.CompilerParams(dimension_semantics=sem, vmem_limit_bytes=VMEM_LIMIT)


def _dot(a, b):
    return jnp.dot(a, b, preferred_element_type=F32)


def _dot_nt(a, b):
    return lax.dot_general(a, b, (((1,), (1,)), ((), ())), preferred_element_type=F32)


def _dot_tn(a, b):
    return lax.dot_general(a, b, (((0,), (0,)), ((), ())), preferred_element_type=F32)


def _split2(x):
    hi = x.astype(BF16)
    lo = (x - hi.astype(F32)).astype(BF16)
    return hi, lo


def _split3(x):
    hi = x.astype(BF16)
    r1 = x - hi.astype(F32)
    mid = r1.astype(BF16)
    lo = (r1 - mid.astype(F32)).astype(BF16)
    return hi, mid, lo


def _sigmoid(x):
    return jax.nn.sigmoid(x)


def _silu(x):
    return x * jax.nn.sigmoid(x)


def _softplus(x):
    return jnp.maximum(x, 0.0) + jnp.log(1.0 + jnp.exp(-jnp.abs(x)))


def _gelu_tanh(x):
    c = math.sqrt(2.0 / math.pi)
    return x * (0.5 * (1.0 + jnp.tanh(c * (x + 0.044715 * (x * x * x)))))


def _iota(shape, axis):
    return lax.broadcasted_iota(jnp.int32, shape, axis)


def _shr(x, k):
    return lax.shift_right_logical(x, jnp.int32(k))


def _log2(n):
    k = int(round(math.log2(n)))
    assert 1 << k == n
    return k


def _block_ones(n, seg, dtype):
    r = _shr(_iota((n, n), 0), _log2(seg))
    c = _shr(_iota((n, n), 1), _log2(seg))
    return (r == c).astype(dtype)


def _segsum(x, seg):
    g = _block_ones(256, seg, BF16)
    outs = []
    for j in range(x.shape[1] // 256):
        hi, lo = _split2(x[:, 256 * j:256 * (j + 1)])
        outs.append(_dot(hi, g) + _dot(lo, g))
    return jnp.concatenate(outs, axis=1)


def _row_cumsum(x, period):
    rows = x.shape[0]
    rid = _iota(x.shape, 0)
    if period < rows:
        rid = jnp.bitwise_and(rid, period - 1)
    d = 1
    while d < min(period, rows):
        x = x + jnp.where(rid >= d, pltpu.roll(x, d, 0), 0.0)
        d *= 2
    return x


def _row_cummax(x, period):
    rows = x.shape[0]
    rid = _iota(x.shape, 0)
    if period < rows:
        rid = jnp.bitwise_and(rid, period - 1)
    d = 1
    while d < min(period, rows):
        x = jnp.maximum(x, jnp.where(rid >= d, pltpu.roll(x, d, 0), -jnp.inf))
        d *= 2
    return x


def _mm_kernel(x_ref, w_ref, o_ref):
    o_ref[...] = _dot(x_ref[...].astype(BF16), w_ref[...])


def _matmul(x, w, tm, tn):
    r, k = x.shape
    n = w.shape[1]
    return pl.pallas_call(
        _mm_kernel,
        out_shape=jax.ShapeDtypeStruct((r, n), F32),
        grid=(pl.cdiv(r, tm), n // tn),
        in_specs=[pl.BlockSpec((tm, k), lambda i, j: (i, 0)),
                  pl.BlockSpec((k, tn), lambda i, j: (0, j))],
        out_specs=pl.BlockSpec((tm, tn), lambda i, j: (i, j)),
        compiler_params=_cp(("parallel", "arbitrary")),
    )(x, w)


def _pw_kernel(a_ref, w_ref, z_ref, o_ref):
    o_ref[...] = _dot(a_ref[...].astype(BF16), w_ref[...]) * _silu(z_ref[...])


def _pw_gate(act, pw, proj0, tm, tn=512):
    r = act.shape[0]
    zb = 2048 // tn
    return pl.pallas_call(
        _pw_kernel,
        out_shape=jax.ShapeDtypeStruct((r, MIX_W), F32),
        grid=(pl.cdiv(r, tm), MIX_W // tn),
        in_specs=[pl.BlockSpec((tm, MIX_W), lambda i, j: (i, 0)),
                  pl.BlockSpec((MIX_W, tn), lambda i, j: (0, j)),
                  pl.BlockSpec((tm, tn), lambda i, j: (i, zb + j))],
        out_specs=pl.BlockSpec((tm, tn), lambda i, j: (i, j)),
        compiler_params=_cp(("parallel", "arbitrary")),
    )(act, pw, proj0)


def _glu_kernel(y_ref, wv_ref, wg_ref, bv_ref, bg_ref, z_ref, o_ref):
    y = y_ref[...].astype(BF16)
    v = _dot(y, wv_ref[...]) + bv_ref[...]
    g = _dot(y, wg_ref[...]) + bg_ref[...]
    o_ref[...] = v * _sigmoid(g) * _silu(z_ref[...])


def _glu_gate(yb, glu_w, glu_b, proj0, tm, tn=512):
    r = yb.shape[0]
    nb = MIX_W // tn
    zb = 4096 // tn
    return pl.pallas_call(
        _glu_kernel,
        out_shape=jax.ShapeDtypeStruct((r, MIX_W), F32),
        grid=(pl.cdiv(r, tm), nb),
        in_specs=[pl.BlockSpec((tm, MIX_W), lambda i, j: (i, 0)),
                  pl.BlockSpec((MIX_W, tn), lambda i, j: (0, j)),
                  pl.BlockSpec((MIX_W, tn), lambda i, j: (0, nb + j)),
                  pl.BlockSpec((1, tn), lambda i, j: (0, j)),
                  pl.BlockSpec((1, tn), lambda i, j: (0, nb + j)),
                  pl.BlockSpec((tm, tn), lambda i, j: (i, zb + j))],
        out_specs=pl.BlockSpec((tm, tn), lambda i, j: (i, j)),
        compiler_params=_cp(("parallel", "arbitrary")),
    )(yb, glu_w, glu_w, glu_b, glu_b, proj0)


def _out_ln_kernel(x_ref, ma_ref, mb_ref, wa_ref, wb_ref, g_ref, b_ref, o_ref):
    out = _dot(ma_ref[...].astype(BF16), wa_ref[...]) + _dot(mb_ref[...].astype(BF16), wb_ref[...])
    y = ALPHA * x_ref[...] + out
    mu = jnp.mean(y, axis=-1, keepdims=True)
    yc = y - mu
    var = jnp.mean(yc * yc, axis=-1, keepdims=True)
    o_ref[...] = yc * lax.rsqrt(var + LN_EPS) * g_ref[...] + b_ref[...]


def _out_ln(x, mix_a, mix_b, w_out, ln_g, ln_b, tm):
    r = x.shape[0]
    return pl.pallas_call(
        _out_ln_kernel,
        out_shape=jax.ShapeDtypeStruct((r, D_MODEL), F32),
        grid=(pl.cdiv(r, tm),),
        in_specs=[pl.BlockSpec((tm, D_MODEL), lambda i: (i, 0)),
                  pl.BlockSpec((tm, MIX_W), lambda i: (i, 0)),
                  pl.BlockSpec((tm, MIX_W), lambda i: (i, 0)),
                  pl.BlockSpec((MIX_W, D_MODEL), lambda i: (0, 0)),
                  pl.BlockSpec((MIX_W, D_MODEL), lambda i: (1, 0)),
                  pl.BlockSpec((1, D_MODEL), lambda i: (0, 0)),
                  pl.BlockSpec((1, D_MODEL), lambda i: (0, 0))],
        out_specs=pl.BlockSpec((tm, D_MODEL), lambda i: (i, 0)),
        compiler_params=_cp(("parallel",)),
    )(x, mix_a, mix_b, w_out, w_out, ln_g, ln_b)


def _out_ln_prompt(x, mix_a, mix_b, w_out, ln_g, ln_b, tm=256):
    tiles = SEQ // tm

    def rows(width):
        return pl.BlockSpec((pl.Element(tm), pl.Element(width)),
                            lambda n, t: (pl.multiple_of(n * P_LEN + N_META + t * tm, 8), 0))

    return pl.pallas_call(
        _out_ln_kernel,
        out_shape=jax.ShapeDtypeStruct((BATCH * SEQ, D_MODEL), F32),
        grid=(BATCH, tiles),
        in_specs=[rows(D_MODEL), rows(MIX_W), rows(MIX_W),
                  pl.BlockSpec((MIX_W, D_MODEL), lambda n, t: (0, 0)),
                  pl.BlockSpec((MIX_W, D_MODEL), lambda n, t: (1, 0)),
                  pl.BlockSpec((1, D_MODEL), lambda n, t: (0, 0)),
                  pl.BlockSpec((1, D_MODEL), lambda n, t: (0, 0))],
        out_specs=pl.BlockSpec((tm, D_MODEL), lambda n, t: (n * tiles + t, 0)),
        compiler_params=_cp(("parallel", "arbitrary")),
    )(x, mix_a, mix_b, w_out, w_out, ln_g, ln_b).reshape(BATCH, SEQ, D_MODEL)


def _conv_kernel(u_ref, g_ref, st_ref, w_ref, cb_ref, lg_ref, lb_ref, act_ref, nst_ref, hp_ref, hs_ref, wb_ref,
                 *, NB, TL, T):
    t = pl.program_id(1)
    rc = 8
    for j in range(CONV_W):
        wb_ref[j] = jnp.broadcast_to(w_ref[j:j + 1, :], (8, MIX_W))
    for nb in range(NB):
        base = nb * TL

        @pl.when(t == 0)
        def _():
            hp_ref[nb, 0:2, :] = jnp.zeros((2, MIX_W), F32)
            hp_ref[nb, 2:32, :] = st_ref[nb]

        hp_ref[nb, TL + 32:TL + 40, :] = jnp.zeros((8, MIX_W), F32)
        hp_ref[nb, 32:32 + TL, :] = u_ref[base:base + TL, :] * _sigmoid(g_ref[base:base + TL, :])
        for b in range(8):
            hs_ref[b] = hp_ref[nb, b:b + TL + 32, :]

        def chunk(c, carry):
            r0 = pl.multiple_of(c * rc, rc)
            acc0 = wb_ref[0] * hs_ref[2, pl.ds(r0, rc), :]
            acc1 = wb_ref[1] * hs_ref[3, pl.ds(r0, rc), :]
            for j in range(2, CONV_W):
                o = j + 2
                term = wb_ref[j] * hs_ref[o % 8, pl.ds(r0 + 8 * (o // 8), rc), :]
                if j % 2 == 0:
                    acc0 = acc0 + term
                else:
                    acc1 = acc1 + term
            act_ref[pl.ds(base + r0, rc), :] = acc0 + acc1
            return carry

        lax.fori_loop(0, TL // rc, chunk, 0, unroll=2 if (TL // rc) % 2 == 0 else 1)

        @pl.when(t == T - 1)
        def _():
            nst_ref[nb] = hp_ref[nb, TL + 2:TL + 32, :]

        if T > 1:
            hp_ref[nb, 0:32, :] = hp_ref[nb, TL:TL + 32, :]

    y = act_ref[...] + cb_ref[...]
    mu = jnp.mean(y, axis=-1, keepdims=True)
    yc = y - mu
    var = jnp.mean(yc * yc, axis=-1, keepdims=True)
    act_ref[...] = _silu(yc * lax.rsqrt(var + LN_EPS) * lg_ref[...] + lb_ref[...])


def _conv_call(proj0, state, conv_w, conv_b, ln_g, ln_b, *, N, L, NB, TL):
    T = L // TL
    RB = NB * TL
    assert NB == 1 or T == 1
    return pl.pallas_call(
        functools.partial(_conv_kernel, NB=NB, TL=TL, T=T),
        out_shape=(jax.ShapeDtypeStruct((N * L, MIX_W), F32),
                   jax.ShapeDtypeStruct((N, CONV_W - 1, MIX_W), F32)),
        grid=(N // NB, T),
        in_specs=[pl.BlockSpec((RB, MIX_W), lambda i, t: (i * T + t, 0)),
                  pl.BlockSpec((RB, MIX_W), lambda i, t: (i * T + t, 1)),
                  pl.BlockSpec((NB, CONV_W - 1, MIX_W), lambda i, t: (i, 0, 0)),
                  pl.BlockSpec((CONV_W, MIX_W), lambda i, t: (0, 0)),
                  pl.BlockSpec((1, MIX_W), lambda i, t: (0, 0)),
                  pl.BlockSpec((1, MIX_W), lambda i, t: (0, 0)),
                  pl.BlockSpec((1, MIX_W), lambda i, t: (0, 0))],
        out_specs=(pl.BlockSpec((RB, MIX_W), lambda i, t: (i * T + t, 0)),
                   pl.BlockSpec((NB, CONV_W - 1, MIX_W), lambda i, t: (i, 0, 0))),
        scratch_shapes=[pltpu.VMEM((NB, TL + 40, MIX_W), F32),
                        pltpu.VMEM((8, TL + 32, MIX_W), F32),
                        pltpu.VMEM((CONV_W, 8, MIX_W), F32)],
        compiler_params=_cp(("arbitrary", "arbitrary")),
    )(proj0, proj0, state, conv_w, conv_b, ln_g, ln_b)


def _s5_kernel(u_ref, wb_ref, wk_ref, wc_ref, d_ref, ar_ref, ai_ref, x0r_ref, x0i_ref,
               y_ref, xfr_ref, xfi_ref, xs_ref, cr_ref, ci_ref, *, NB, TL, T):
    t = pl.program_id(2)
    RB = NB * TL
    GL = TL // 8
    u = u_ref[...]
    ub = u.astype(BF16)
    big = _dot(ub, wb_ref[0])
    xs_ref[0] = big[:, :1024]
    xs_ref[1] = big[:, 1024:]
    rid = jnp.bitwise_and(_iota((RB, 256), 0), 7)
    lags = [ub] + [jnp.where(rid >= d, pltpu.roll(u, d, 0), 0.0).astype(BF16) for d in range(1, 8)]
    y_loc = _dot(jnp.concatenate(lags, axis=1), wk_ref[0])
    ar = ar_ref[...]
    ai = ai_ref[...]

    def cmul(pr, pi, qr, qi):
        return pr * qr - pi * qi, pr * qi + pi * qr

    a1 = (ar, ai)
    a2 = cmul(*a1, *a1)
    a4 = cmul(*a2, *a2)
    a3 = cmul(*a2, *a1)
    a5 = cmul(*a4, *a1)
    a6 = cmul(*a4, *a2)
    a7 = cmul(*a6, *a1)
    a8 = cmul(*a4, *a4)
    a0 = (jnp.ones_like(ar), jnp.zeros_like(ai))
    r8 = _iota((8, 1024), 0)

    def table(powers):
        tr = jnp.zeros((8, 1024), F32)
        ti = jnp.zeros((8, 1024), F32)
        for k, (pr, pi) in enumerate(powers):
            tr = jnp.where(r8 == k, pr, tr)
            ti = jnp.where(r8 == k, pi, ti)
        return tr, ti

    pwr, pwi = table((a1, a2, a3, a4, a5, a6, a7, a8))
    qwr, qwi = table((a7, a6, a5, a4, a3, a2, a1, a0))
    a8r, a8i = a8

    first = t == 0

    def seq_body(nb, carry0):
        x0r = x0r_ref[nb]
        x0i = x0i_ref[nb]
        if T > 1:
            c_r = jnp.where(first, x0r, cr_ref[0:1, :])
            c_i = jnp.where(first, x0i, ci_ref[0:1, :])
        else:
            c_r, c_i = x0r, x0i

        def grp(g, c):
            c_r, c_i = c
            off = pl.multiple_of(nb * TL + g * 8, 8)
            vr = xs_ref[0, pl.ds(off, 8), :]
            vi = xs_ref[1, pl.ds(off, 8), :]
            er = jnp.sum(qwr * vr - qwi * vi, axis=0, keepdims=True)
            ei = jnp.sum(qwr * vi + qwi * vr, axis=0, keepdims=True)
            br = jnp.broadcast_to(c_r, (8, 1024))
            bi = jnp.broadcast_to(c_i, (8, 1024))
            xs_ref[0, pl.ds(off, 8), :] = pwr * br - pwi * bi
            xs_ref[1, pl.ds(off, 8), :] = pwr * bi + pwi * br
            return a8r * c_r - a8i * c_i + er, a8r * c_i + a8i * c_r + ei

        c_r, c_i = lax.fori_loop(0, GL, grp, (c_r, c_i))
        if T > 1:
            cr_ref[...] = jnp.broadcast_to(c_r, (8, 1024))
            ci_ref[...] = jnp.broadcast_to(c_i, (8, 1024))

        @pl.when(t == T - 1)
        def _():
            xfr_ref[nb] = c_r
            xfi_ref[nb] = c_i

        return carry0

    lax.fori_loop(0, NB, seq_body, 0)
    y = (_dot(xs_ref[0].astype(BF16), wc_ref[0, 0:1024, :])
         + _dot(xs_ref[1].astype(BF16), wc_ref[0, 1024:2048, :]))
    y_ref[...] = _gelu_tanh(y + y_loc + d_ref[...] * u)


def _s5_call(proj0, wb, wk, wc, dvec, ar, ai, x0r, x0i, *, N, L, NB, TL):
    T = L // TL
    RB = NB * TL
    assert NB == 1 or T == 1
    ub = 3072 // 256
    st = jax.ShapeDtypeStruct((N, 1, 4096), F32)
    return pl.pallas_call(
        functools.partial(_s5_kernel, NB=NB, TL=TL, T=T),
        out_shape=(jax.ShapeDtypeStruct((N * L, MIX_W), F32), st, st),
        grid=(N // NB, 4, T),
        in_specs=[pl.BlockSpec((RB, 256), lambda i, j, t: (i * T + t, ub + j)),
                  pl.BlockSpec((1, 256, 2048), lambda i, j, t: (j, 0, 0)),
                  pl.BlockSpec((1, 2048, 256), lambda i, j, t: (j, 0, 0)),
                  pl.BlockSpec((1, 2048, 256), lambda i, j, t: (j, 0, 0)),
                  pl.BlockSpec((1, 256), lambda i, j, t: (0, j)),
                  pl.BlockSpec((1, 1024), lambda i, j, t: (0, j)),
                  pl.BlockSpec((1, 1024), lambda i, j, t: (0, j)),
                  pl.BlockSpec((NB, 1, 1024), lambda i, j, t: (i, 0, j)),
                  pl.BlockSpec((NB, 1, 1024), lambda i, j, t: (i, 0, j))],
        out_specs=(pl.BlockSpec((RB, 256), lambda i, j, t: (i * T + t, j)),
                   pl.BlockSpec((NB, 1, 1024), lambda i, j, t: (i, 0, j)),
                   pl.BlockSpec((NB, 1, 1024), lambda i, j, t: (i, 0, j))),
        scratch_shapes=[pltpu.VMEM((2, RB, 1024), F32),
                        pltpu.VMEM((8, 1024), F32),
                        pltpu.VMEM((8, 1024), F32)],
        compiler_params=_cp(("arbitrary", "arbitrary", "arbitrary")),
    )(proj0, wb, wk, wc, dvec, ar, ai, x0r, x0i)


def _mlstm_kernel(q_ref, k_ref, v_ref, o_ref, z_ref, gt_ref, gb_ref, hg_ref, c0_ref, n0_ref, m0_ref,
                  y_ref, c_ref, n_ref, m_ref, cs_ref, ns_ref, ms_ref, *, TL, T):
    t = pl.program_id(1)

    @pl.when(t == 0)
    def _():
        cs_ref[...] = c0_ref[0]
        ns_ref[...] = n0_ref[0]
        ms_ref[...] = m0_ref[0]

    G = gt_ref[...] + gb_ref[...]
    B = _row_cumsum(-_softplus(-G), TL)
    Bs = pltpu.roll(B, 124, 1)
    A = G - Bs
    CM = _row_cummax(A, TL)
    ms = ms_ref[...]
    dg = _iota((8, 128), 0) == _iota((8, 128), 1)
    mrow = jnp.sum(jnp.where(dg, ms, 0.0), axis=0, keepdims=True)
    M = jnp.maximum(mrow, CM)
    MT = Bs + M
    sel = dg.astype(BF16)
    a_hi, a_mid, a_lo = _split3(A)
    Arow = _dot_nt(sel, a_hi) + _dot_nt(sel, a_mid) + _dot_nt(sel, a_lo)
    causal = _iota((TL, TL), 0) >= _iota((TL, TL), 1)
    H = range(M_HEADS)
    sl = [slice(M_D * h, M_D * (h + 1)) for h in H]
    q = [q_ref[:, sl[h]] * (M_D ** -0.5) for h in H]
    qb = [x.astype(BF16) for x in q]
    kf = [k_ref[:, sl[h]] for h in H]
    kb = [x.astype(BF16) for x in kf]
    vf = [v_ref[:, sl[h]] for h in H]
    c_old = [cs_ref[h] for h in H]
    n_old = [ns_ref[h:h + 1, :] for h in H]
    m_col = [M[:, h:h + 1] for h in H]
    mt_col = [MT[:, h:h + 1] for h in H]
    b_col = [Bs[:, h:h + 1] for h in H]
    m_prev = [mrow[:, h:h + 1] for h in H]
    dm = [jnp.exp(jnp.where(causal, Arow[h:h + 1, :] - m_col[h], -jnp.inf)) for h in H]
    s = [_dot_nt(qb[h], kb[h]) * dm[h] for h in H]
    inter = [jnp.exp(m_prev[h] - m_col[h]) for h in H]
    h_intra = [_dot(s[h].astype(BF16), vf[h].astype(BF16)) for h in H]
    h_inter = [_dot(qb[h], c_old[h].astype(BF16)) * inter[h] for h in H]
    n_all = [jnp.sum(s[h], axis=1, keepdims=True) + jnp.sum(q[h] * n_old[h], axis=1, keepdims=True) * inter[h]
             for h in H]
    hh = [(h_intra[h] + h_inter[h]) / jnp.maximum(jnp.abs(n_all[h]), jnp.exp(-mt_col[h])) for h in H]
    m_new = [mt_col[h][TL - 1:TL, :] for h in H]
    b_end = [b_col[h][TL - 1:TL, :] for h in H]
    dec = [jnp.exp(m_prev[h] + b_end[h] - m_new[h]) for h in H]
    w_s = [jnp.exp(b_end[h] - b_col[h] + G[:, h:h + 1] - m_new[h]) for h in H]
    c_new = [c_old[h] * dec[h] + _dot_tn(kb[h], (vf[h] * w_s[h]).astype(BF16)) for h in H]
    n_new = [n_old[h] * dec[h] + jnp.sum(kf[h] * w_s[h], axis=0, keepdims=True) for h in H]
    outs = []
    for h in H:
        mu = jnp.mean(hh[h], axis=-1, keepdims=True)
        hc = hh[h] - mu
        var = jnp.mean(hc * hc, axis=-1, keepdims=True)
        hn = hc * lax.rsqrt(var + LN_EPS) * hg_ref[:, sl[h]]
        outs.append(hn * _sigmoid(o_ref[:, sl[h]]) * _silu(z_ref[:, sl[h]]))
    for h in H:
        cs_ref[h] = c_new[h]
        ns_ref[h:h + 1, :] = n_new[h]
        ms_ref[h:h + 1, :] = jnp.broadcast_to(m_new[h], (1, 128))
        y_ref[:, sl[h]] = outs[h]

    @pl.when(t == T - 1)
    def _():
        c_ref[0] = cs_ref[...]
        n_ref[0] = ns_ref[...]
        m_ref[0] = ms_ref[...]


def _mlstm_call(proj1, gate_b, hn_g, c0, n0, m0, *, N, L, TL):
    T = L // TL

    def col(cb):
        return pl.BlockSpec((TL, MIX_W), lambda i, t: (i * T + t, cb))

    return pl.pallas_call(
        functools.partial(_mlstm_kernel, TL=TL, T=T),
        out_shape=(jax.ShapeDtypeStruct((N * L, MIX_W), F32),
                   jax.ShapeDtypeStruct((N, M_HEADS, M_D, M_D), F32),
                   jax.ShapeDtypeStruct((N, M_HEADS, M_D), F32),
                   jax.ShapeDtypeStruct((N, 8, 128), F32)),
        grid=(N, T),
        in_specs=[col(0), col(1), col(2), col(3), col(4),
                  pl.BlockSpec((TL, 128), lambda i, t: (i * T + t, P1_GATE // 128)),
                  pl.BlockSpec((1, 128), lambda i, t: (0, 0)),
                  pl.BlockSpec((1, MIX_W), lambda i, t: (0, 0)),
                  pl.BlockSpec((1, M_HEADS, M_D, M_D), lambda i, t: (i, 0, 0, 0)),
                  pl.BlockSpec((1, M_HEADS, M_D), lambda i, t: (i, 0, 0)),
                  pl.BlockSpec((1, 8, 128), lambda i, t: (i, 0, 0))],
        out_specs=(pl.BlockSpec((TL, MIX_W), lambda i, t: (i * T + t, 0)),
                   pl.BlockSpec((1, M_HEADS, M_D, M_D), lambda i, t: (i, 0, 0, 0)),
                   pl.BlockSpec((1, M_HEADS, M_D), lambda i, t: (i, 0, 0)),
                   pl.BlockSpec((1, 8, 128), lambda i, t: (i, 0, 0))),
        scratch_shapes=[pltpu.VMEM((M_HEADS, M_D, M_D), F32),
                        pltpu.VMEM((M_HEADS, M_D), F32),
                        pltpu.VMEM((8, 128), F32)],
        compiler_params=_cp(("arbitrary", "arbitrary")),
    )(proj1, proj1, proj1, proj1, proj1, proj1, gate_b, hn_g, c0, n0, m0)


N_RA_OUT = 9


def _rwkv_a_kernel(pr_ref, pk_ref, pv_ref, pwa_ref, hr_ref, hk_ref, hv_ref, hwa_ref, st_ref, stwa_ref,
                   mu_ref, muwa_ref, w0_ref, w2_ref, a0_ref, a2_ref, kkp_ref, ka_ref, rk_ref,
                   ah_ref, rh_ref, bh_ref, kh_ref, vo_ref, ul_ref, yl_ref, dc_ref, bo_ref, shr_ref, shw_ref,
                   s_at, s_rt, s_bt, s_kt, s_v, s_cum, *, RB, CT, N, L, U):
    HS = R_HEADS * CT
    HG = 128 // CT
    NG = R_HEADS // HG
    GW = HG * R_K
    NCH = RB // CT
    short = L == CT
    rid = _iota((RB, 1), 0)
    grow = pl.program_id(0) * RB + rid

    def shifted(p_ref, h_ref, s_ref, lo, hi, mu):
        p = p_ref[...]
        prev = pltpu.roll(p, 1, 0)
        if short:
            prev = jnp.where(jnp.bitwise_and(rid, CT - 1) == 0, s_ref[:, lo:hi], prev)
        else:
            prev = jnp.where(rid == 0, h_ref[7:8, :], prev)
            for n in range(N):
                prev = jnp.where(grow == n * L, s_ref[n:n + 1, lo:hi], prev)
        return p + (prev - p) * mu

    raw = ((pr_ref, shr_ref, 0), (pk_ref, shr_ref, 1024), (pv_ref, shr_ref, 2048), (pwa_ref, shw_ref, 0))
    if short:
        nseq = RB // CT
        sel = (_iota((nseq, RB), 1) == _iota((nseq, RB), 0) * CT + (CT - 1)).astype(BF16)
        for src, dst, lo in raw:
            hi, mid, low = _split3(src[...])
            dst[:, lo:lo + src.shape[1]] = _dot(sel, hi) + _dot(sel, mid) + _dot(sel, low)
    else:
        for n in range(N):
            tile, off = divmod(n * L + L - 1, RB)

            @pl.when(pl.program_id(0) == tile)
            def _():
                for src, dst, lo in raw:
                    dst[n:n + 1, lo:lo + src.shape[1]] = src[off:off + 1, :]

    r = shifted(pr_ref, hr_ref, st_ref, 0, 1024, mu_ref[:, 0:1024])
    k = shifted(pk_ref, hk_ref, st_ref, 1024, 2048, mu_ref[:, 1024:2048])
    v = shifted(pv_ref, hv_ref, st_ref, 2048, 3072, mu_ref[:, 2048:3072])
    wa = shifted(pwa_ref, hwa_ref, stwa_ref, 0, 128, muwa_ref[...])
    w = -_softplus(-(w0_ref[...] + _dot(jnp.tanh(wa).astype(BF16), w2_ref[...]))) - 0.5
    wlog = -jnp.exp(w)
    a = _sigmoid(a0_ref[...] + _dot(wa.astype(BF16), a2_ref[...]))
    kk = k * kkp_ref[...]
    kk = kk / jnp.maximum(jnp.sqrt(_segsum(kk * kk, R_K)), 1e-12)
    kmod = k * (1.0 + (a - 1.0) * ka_ref[...])
    bo_ref[...] = _segsum(r * kmod * rk_ref[...], R_K) * v
    cum = _row_cumsum(wlog, CT)
    einv = jnp.exp(-cum)
    s_at[...] = (-kk) * jnp.exp(cum - wlog)
    s_rt[...] = r * jnp.exp(cum)
    s_bt[...] = kk * a * einv
    s_kt[...] = kmod * einv
    s_v[...] = v
    s_cum[...] = cum

    be_mask = (_shr(_iota((128, GW), 0), _log2(CT)) == _shr(_iota((128, GW), 1), _log2(R_K))).astype(F32)
    bd_mask = (_shr(_iota((HS, HS), 0), _log2(CT)) == _shr(_iota((HS, HS), 1), _log2(CT))).astype(F32)
    tt = _iota((CT, HS), 0)
    ss = jnp.bitwise_and(_iota((CT, HS), 1), CT - 1)
    strict = tt > ss
    incl = tt >= ss
    eye_c = (tt == ss).astype(F32)
    cat0 = lambda *xs: jnp.concatenate(xs, axis=0)

    def blockexp(x):
        return [(jnp.concatenate([x[:, GW * g:GW * (g + 1)]] * HG, axis=0) * be_mask).astype(BF16)
                for g in range(NG)]

    def gram(lhs, be):
        lb = lhs.astype(BF16)
        return jnp.concatenate([_dot_nt(lb[:, GW * g:GW * (g + 1)], be[g]) for g in range(NG)], axis=1)

    def apply(cmp, be):
        cb = cmp.astype(BF16)
        return jnp.concatenate([_dot(cb[:, 128 * g:128 * (g + 1)], be[g]) for g in range(NG)], axis=1)

    def bdiag(x):
        return jnp.concatenate([x] * R_HEADS, axis=0) * bd_mask

    def mm_hl(stack, wh, wl):
        sh, sl = _split2(stack)
        n = stack.shape[0]
        full = _dot(cat0(sh, sl), wh)
        return full[:n] + full[n:] + _dot(sh, wl)

    def chunks(i, carry):
        rows = [pl.ds(pl.multiple_of((i * U + u) * CT, CT), CT) for u in range(U)]
        ld = lambda ref: [ref[rw, :] for rw in rows]
        at, rt, bt, kt, vv, cm = ld(s_at), ld(s_rt), ld(s_bt), ld(s_kt), ld(s_v), ld(s_cum)
        each = lambda f, *xs: [f(*a_) for a_ in zip(*xs)]
        ar_ = each(cat0, at, rt)
        gb = each(lambda l_, y_: gram(l_, blockexp(y_)), ar_, bt)
        gk = each(lambda l_, y_: gram(l_, blockexp(y_)), ar_, kt)
        a_ab = each(lambda m: jnp.where(strict, m[:CT], 0.0), gb)
        a_rb = each(lambda m: jnp.where(incl, m[CT:], 0.0), gb)
        a_ak = each(lambda m: jnp.where(strict, m[:CT], 0.0), gk)
        a_rk = each(lambda m: jnp.where(incl, m[CT:], 0.0), gk)
        p = each(lambda m: eye_c + m, a_ab)
        x = a_ab
        q = a_rb
        w_hl = each(lambda m: _split2(bdiag(m)), x)
        res = each(lambda x_, q_, w_: mm_hl(cat0(x_, q_), *w_), x, q, w_hl)
        x = each(lambda r_: r_[:CT], res)
        q = each(lambda q_, r_: q_ + r_[CT:], q, res)
        pw = 2
        while pw < CT:
            w_hl = each(lambda m: _split2(bdiag(m)), x)
            if 2 * pw >= CT:
                res = each(lambda p_, q_, w_: mm_hl(cat0(p_, q_), *w_), p, q, w_hl)
                q = each(lambda q_, r_: q_ + r_[CT:], q, res)
            else:
                res = each(lambda p_, x_, q_, w_: mm_hl(cat0(p_, x_, q_), *w_), p, x, q, w_hl)
                x = each(lambda r_: r_[CT:2 * CT], res)
                q = each(lambda q_, r_: q_ + r_[2 * CT:], q, res)
            p = each(lambda p_, r_: p_ + r_[:CT], p, res)
            pw *= 2
        tq = each(cat0, p, q)
        res = each(lambda m, k_: _dot(m.astype(BF16), bdiag(k_).astype(BF16)), tq, a_ak)
        ty = each(lambda r_, k_: cat0(r_[:CT], r_[CT:] + k_), res, a_rk)
        o1 = each(lambda m, y_: apply(m, blockexp(y_)), tq, at)
        o2 = each(lambda m, y_: apply(m, blockexp(y_)), ty, vv)
        ect = each(lambda c_: jnp.exp(c_[CT - 1:CT, :]), cm)
        for u, rw in enumerate(rows):
            ah_ref[rw, :] = o1[u][:CT]
            rh_ref[rw, :] = rt[u] + o1[u][CT:]
            ul_ref[rw, :] = o2[u][:CT]
            yl_ref[rw, :] = o2[u][CT:]
            bh_ref[rw, :] = bt[u] * ect[u]
            kh_ref[rw, :] = kt[u] * ect[u]
            vo_ref[rw, :] = vv[u]
            dc_ref[rw, :] = jnp.broadcast_to(ect[u], (CT, 1024))
        return carry

    lax.fori_loop(0, NCH // U, chunks, 0)


def _rwkv_a_call(proj1, st_rkv, st_wa, wts, *, N, L, RB, CT, U):
    short = L == CT
    rows = N * L
    assert rows % RB == 0 and (RB // CT) % U == 0

    def col(cb, width=MIX_W):
        return pl.BlockSpec((RB, width), lambda i: (i, cb))

    def halo(cb, width=MIX_W):
        return pl.BlockSpec((8, width), lambda i: (jnp.maximum(i * (RB // 8) - 1, 0), cb))

    if short:
        st_specs = [pl.BlockSpec((RB, 3072), lambda i: (i, 0)), pl.BlockSpec((RB, 128), lambda i: (i, 0))]
    else:
        st_specs = [pl.BlockSpec((N, 3072), lambda i: (0, 0)), pl.BlockSpec((N, 128), lambda i: (0, 0))]

    def full(shape):
        return pl.BlockSpec(shape, lambda i: (0,) * len(shape))

    o_spec = pl.BlockSpec((RB, MIX_W), lambda i: (i, 0))
    if short:
        sh_specs = (pl.BlockSpec((RB // CT, 3072), lambda i: (i, 0)), pl.BlockSpec((RB // CT, 128), lambda i: (i, 0)))
    else:
        sh_specs = (pl.BlockSpec((N, 3072), lambda i: (0, 0)), pl.BlockSpec((N, 128), lambda i: (0, 0)))
    outs = pl.pallas_call(
        functools.partial(_rwkv_a_kernel, RB=RB, CT=CT, N=N, L=L, U=U),
        out_shape=((jax.ShapeDtypeStruct((rows, MIX_W), F32),) * N_RA_OUT
                   + (jax.ShapeDtypeStruct((N, 3072), F32), jax.ShapeDtypeStruct((N, 128), F32))),
        grid=(rows // RB,),
        in_specs=([col(P1_R // 1024), col(P1_RK // 1024), col(P1_RV // 1024), col(P1_WA // 128, 128),
                   halo(P1_R // 1024), halo(P1_RK // 1024), halo(P1_RV // 1024), halo(P1_WA // 128, 128)]
                  + st_specs
                  + [full((1, 3072)), full((1, 128)), full((1, 1024)), full((128, 1024)), full((1, 1024)),
                     full((128, 1024)), full((1, 1024)), full((1, 1024)), full((1, 1024))]),
        out_specs=(o_spec,) * N_RA_OUT + sh_specs,
        scratch_shapes=[pltpu.VMEM((RB, MIX_W), F32)] * 6,
        compiler_params=_cp(("arbitrary",)),
    )(*([proj1] * 8 + [st_rkv, st_wa] + list(wts)))
    return outs[:N_RA_OUT], jnp.concatenate(outs[N_RA_OUT:], axis=1)


def _rwkv_b_kernel(ah_ref, rh_ref, bh_ref, kh_ref, v_ref, ul_ref, yl_ref, dc_ref, bo_ref, z_ref, lg_ref, lb_ref,
                   s0_ref, y_ref, so_ref, sbd_ref, yb_ref, *, NBLK, TLB, CT, T):
    t = pl.program_id(1)
    bd_mask = (_shr(_iota((256, 256), 0), 6) == _shr(_iota((256, 256), 1), 6)).astype(F32)
    e_tile = (_iota((64, 256), 0) == jnp.bitwise_and(_iota((64, 256), 1), 63)).astype(BF16)
    e_fold = (jnp.bitwise_and(_iota((256, 64), 0), 63) == _iota((256, 64), 1)).astype(BF16)

    @pl.when(t == 0)
    def _():
        for nb in range(NBLK):
            for j in range(4):
                hi, lo = _split2(s0_ref[nb, 256 * j:256 * (j + 1), :])
                full = _dot(jnp.concatenate([hi, lo], axis=0), e_tile)
                full = full[:256] + full[256:]
                sbd_ref[4 * nb + j] = full * bd_mask

    chains = [(nb, j, slice(256 * j, 256 * (j + 1))) for nb in range(NBLK) for j in range(4)]
    for c in range(TLB // CT):
        rows = slice(c * CT, (c + 1) * CT)
        sbs = [sbd_ref[4 * nb + j] for nb, j, cs in chains]
        outs = [_dot_nt(jnp.concatenate([ah_ref[nb, rows, cs], rh_ref[nb, rows, cs]], axis=0).astype(BF16),
                        sb.astype(BF16))
                for (nb, j, cs), sb in zip(chains, sbs)]
        upds = [_dot_tn(jnp.concatenate([o[:CT] + ul_ref[nb, rows, cs], v_ref[nb, rows, cs]],
                                        axis=0).astype(BF16),
                        jnp.concatenate([bh_ref[nb, rows, cs], kh_ref[nb, rows, cs]], axis=0).astype(BF16))
                for (nb, j, cs), o in zip(chains, outs)]
        for (nb, j, cs), sb, o, upd in zip(chains, sbs, outs, upds):
            yb_ref[nb, rows, cs] = o[CT:] + yl_ref[nb, rows, cs]
            sbd_ref[4 * nb + j] = sb * dc_ref[nb, c * CT:c * CT + 1, cs] + upd * bd_mask

    @pl.when(t == T - 1)
    def _():
        for nb in range(NBLK):
            outs = []
            for j in range(4):
                hi, lo = _split2(sbd_ref[4 * nb + j])
                both = _dot(jnp.concatenate([hi, lo], axis=0), e_fold)
                outs.append(both[:256] + both[256:])
            so_ref[nb] = jnp.concatenate(outs, axis=0)

    cat = lambda ref: jnp.concatenate([ref[nb] for nb in range(NBLK)], axis=0)
    y = cat(yb_ref)
    mu = _segsum(y, R_K) * (1.0 / R_K)
    yc = y - mu
    var = _segsum(yc * yc, R_K) * (1.0 / R_K)
    yn = yc * lax.rsqrt(var + R_LN_EPS) * lg_ref[...] + lb_ref[...] + cat(bo_ref)
    out = yn * _silu(cat(z_ref))
    for nb in range(NBLK):
        y_ref[nb] = out[nb * TLB:(nb + 1) * TLB, :]


def _rwkv_b_call(ra, proj1, ln_g, ln_b, s0, *, N, L, NBLK, TLB, CT):
    T = L // TLB
    blk = lambda cb: pl.BlockSpec((NBLK, TLB, MIX_W), lambda i, t: (i, t, cb))
    s_spec = pl.BlockSpec((NBLK, 1024, 64), lambda i, t: (i, 0, 0))
    ra3 = [a.reshape(N, L, MIX_W) for a in ra]
    y, s_new = pl.pallas_call(
        functools.partial(_rwkv_b_kernel, NBLK=NBLK, TLB=TLB, CT=CT, T=T),
        out_shape=(jax.ShapeDtypeStruct((N, L, MIX_W), F32),
                   jax.ShapeDtypeStruct((N, 1024, 64), F32)),
        grid=(N // NBLK, T),
        in_specs=([blk(0)] * N_RA_OUT
                  + [blk(P1_ZD // 1024),
                     pl.BlockSpec((1, MIX_W), lambda i, t: (0, 0)),
                     pl.BlockSpec((1, MIX_W), lambda i, t: (0, 0)),
                     s_spec]),
        out_specs=(blk(0), s_spec),
        scratch_shapes=[pltpu.VMEM((4 * NBLK, 256, 256), F32),
                        pltpu.VMEM((NBLK, TLB, MIX_W), F32)],
        compiler_params=_cp(("arbitrary", "arbitrary")),
    )(*(ra3 + [proj1.reshape(N, L, P1_N), ln_g, ln_b, s0]))
    return y.reshape(N * L, MIX_W), s_new


def _regroup_w1_kernel(w_ref, o_ref):
    rows = w_ref.shape[0]
    o_ref[:, 0:4096] = w_ref[:, 0:4096].astype(BF16)
    o_ref[:, 4096:8192] = w_ref[:, 4104:8200].astype(BF16)
    o_ref[:, 8192:9216] = w_ref[:, 8328:9352].astype(BF16)
    o_ref[:, 9216:9344] = w_ref[:, 8200:8328].astype(BF16)
    gates = w_ref[:, 4096:4224]
    o_ref[:, 9344:9472] = jnp.where(_iota((rows, 128), 1) < 8, gates, 0.0).astype(BF16)
    o_ref[:, 9472:P1_N] = jnp.zeros((rows, P1_N - 9472), BF16)


def _regroup_w1(w1, tr=128):
    k, n = w1.shape
    return pl.pallas_call(
        _regroup_w1_kernel,
        out_shape=jax.ShapeDtypeStruct((k, P1_N), BF16),
        grid=(k // tr,),
        in_specs=[pl.BlockSpec((tr, n), lambda i: (i, 0))],
        out_specs=pl.BlockSpec((tr, P1_N), lambda i: (i, 0)),
        compiler_params=_cp(("parallel",)),
    )(w1)


def _s5_weights(lam_re, lam_im, log_dt, b_re, b_im, c_re, c_im):
    dt = jnp.exp(log_dt)[:, None]
    mag = jnp.exp(lam_re * dt)
    ar = mag * jnp.cos(lam_im * dt)
    ai = mag * jnp.sin(lam_im * dt)
    den = lam_re * lam_re + lam_im * lam_im
    qr = ((ar - 1.0) * lam_re + ai * lam_im) / den
    qi = (ai * lam_re - (ar - 1.0) * lam_im) / den
    bbr = qr[..., None] * b_re - qi[..., None] * b_im
    bbi = qr[..., None] * b_im + qi[..., None] * b_re
    eye = jnp.eye(16, dtype=F32)

    def in_blocks(bb):
        bb = bb.reshape(4, 16, S5_STATE, S5_GROUP)
        return jnp.einsum('jgph,gk->jghkp', bb, eye).reshape(4, 256, 1024)

    def out_blocks(cc):
        cc = cc.reshape(4, 16, S5_GROUP, S5_STATE)
        return jnp.einsum('jghp,gk->jgpkh', cc, eye).reshape(4, 1024, 256)

    wb = jnp.concatenate([in_blocks(bbr), in_blocks(bbi)], axis=2).astype(BF16)
    wc = jnp.concatenate([out_blocks(c_re), out_blocks(-c_im)], axis=1).astype(BF16)
    pr, pi = jnp.ones_like(ar), jnp.zeros_like(ai)
    lag = []
    for _ in range(8):
        cpr = c_re * pr[:, None, :] - c_im * pi[:, None, :]
        cpi = c_re * pi[:, None, :] + c_im * pr[:, None, :]
        lag.append(jnp.einsum('gop,gph->goh', cpr, bbr, precision=lax.Precision.HIGHEST)
                   - jnp.einsum('gop,gph->goh', cpi, bbi, precision=lax.Precision.HIGHEST))
        pr, pi = pr * ar - pi * ai, pr * ai + pi * ar
    kd = jnp.stack(lag).reshape(8, 4, 16, S5_GROUP, S5_GROUP)
    wk = jnp.einsum('djgoh,gk->jdghko', kd, eye).reshape(4, 8 * 256, 256).astype(BF16)
    return ar.reshape(1, 4096), ai.reshape(1, 4096), wb, wk, wc


CFG = {
    "P": dict(N=BATCH, L=P_LEN, tm_mm=1376, tm=688, tm_ln=344, drop_meta=True,
              conv=dict(NB=1, TL=344), s5=dict(NB=1, TL=688), mlstm=dict(TL=344),
              ra=dict(RB=192, CT=16, U=6), rb=dict(NBLK=4, TLB=48, CT=16)),
    "S": dict(N=DEC_BATCH, L=DEC_SEQ, tm_mm=1024, tm=512, tm_ln=256, drop_meta=False,
              conv=dict(NB=16, TL=8), s5=dict(NB=32, TL=8), mlstm=dict(TL=8),
              ra=dict(RB=256, CT=8, U=8), rb=dict(NBLK=8, TLB=8, CT=8)),
}


def _trunk(x, st, w, cfg):
    n, l = cfg["N"], cfg["L"]
    proj0 = _matmul(x, w["w_in0"], cfg["tm_mm"], 512)
    act, conv_new = _conv_call(proj0, st["conv"], w["conv_w"], w["conv_b"], w["a_ln_g"], w["a_ln_b"],
                               N=n, L=l, **cfg["conv"])
    mix_a = _pw_gate(act, w["pw"], proj0, tm=cfg["tm"])
    yb, xr, xi = _s5_call(proj0, w["s5_wb"], w["s5_wk"], w["s5_wc"], w["s5_d"], w["s5_ar"], w["s5_ai"],
                          st["ssm_re"].reshape(n, 1, 4096), st["ssm_im"].reshape(n, 1, 4096),
                          N=n, L=l, **cfg["s5"])
    mix_b = _glu_gate(yb, w["glu_w"], w["glu_b"], proj0, tm=cfg["tm"])
    x1 = _out_ln(x, mix_a, mix_b, w["w_out0"], w["ln_g0"], w["ln_b0"], tm=cfg["tm_ln"])

    proj1 = _matmul(x1, w["w_in1"], cfg["tm_mm"], 512)
    m0 = jnp.pad(jnp.broadcast_to(st["m"][:, :, None], (n, M_HEADS, 128)), ((0, 0), (0, 4), (0, 0)))
    mix_c, c_new, n_new, m_new = _mlstm_call(proj1, w["gate_b"], w["hn_g"], st["c"], st["n"], m0,
                                             N=n, L=l, **cfg["mlstm"])
    sh = st["shift"]
    if l == cfg["ra"]["CT"]:
        sh = jnp.repeat(sh, l, axis=0)
    ra, shift_new = _rwkv_a_call(proj1, sh[:, :3072], sh[:, 3072:], w["rwkv"], N=n, L=l, **cfg["ra"])
    mix_d, s_new = _rwkv_b_call(ra, proj1, w["r_ln_g"], w["r_ln_b"], st["s"].reshape(n, 1024, 64),
                                N=n, L=l, **cfg["rb"])
    final_ln = _out_ln_prompt if cfg["drop_meta"] else functools.partial(_out_ln, tm=cfg["tm_ln"])
    y = final_ln(x1, mix_c, mix_d, w["w_out1"], w["ln_g1"], w["ln_b1"])

    states = (conv_new[None],
              xr.reshape(n, S5_GROUPS, S5_STATE)[None],
              xi.reshape(n, S5_GROUPS, S5_STATE)[None],
              c_new[None], n_new[None], m_new[:, :M_HEADS, 0][None],
              s_new.reshape(n, R_HEADS, R_K, R_K)[None],
              shift_new[None])
    return y, states


def kernel(x_prompt, x_sample, state_conv, state_ssm_re, state_ssm_im, state_mlstm_c, state_mlstm_n, state_mlstm_m, state_rwkv_s, state_rwkv_shift, meta_tokens, ev_w_in, a_conv_w, a_conv_b, a_ln_g, a_ln_b, a_pw, s5_lambda_re, s5_lambda_im, s5_log_dt, s5_b_re, s5_b_im, s5_c_re, s5_c_im, s5_d, s5_glu_w, s5_glu_b, ev_w_out, ev_ln_g, ev_ln_b, od_w_in, m_ig_b, m_fg_b, m_hn_g, r_mu, r_w0, r_w2, r_a0, r_a2, r_kk, r_ka, r_rk, r_ln_g, r_ln_b, od_w_out, od_ln_g, od_ln_b):
    nb = x_prompt.shape[0]
    row = lambda vec: vec.reshape(1, -1)
    zeros = lambda *s: jnp.zeros(s, F32)

    ar, ai, wb, wk, wc = _s5_weights(s5_lambda_re[0], s5_lambda_im[0], s5_log_dt[0], s5_b_re[0], s5_b_im[0],
                                     s5_c_re[0], s5_c_im[0])
    w_in1 = _regroup_w1(od_w_in[0])
    mu = r_mu[0]
    w = dict(
        w_in0=ev_w_in[0].astype(BF16), conv_w=a_conv_w[0], conv_b=row(a_conv_b[0]),
        a_ln_g=row(a_ln_g[0]), a_ln_b=row(a_ln_b[0]), pw=a_pw[0].astype(BF16),
        s5_wb=wb, s5_wk=wk, s5_wc=wc, s5_d=row(s5_d[0]), s5_ar=ar, s5_ai=ai,
        glu_w=s5_glu_w[0].astype(BF16), glu_b=row(s5_glu_b[0]),
        w_out0=ev_w_out[0].astype(BF16), ln_g0=row(ev_ln_g[0]), ln_b0=row(ev_ln_b[0]),
        w_in1=w_in1,
        gate_b=jnp.concatenate([m_ig_b[0], m_fg_b[0], jnp.zeros((120,), F32)]).reshape(1, 128),
        hn_g=row(m_hn_g[0]),
        rwkv=[row(mu[:3072]), row(mu[3072:]), row(r_w0[0]),
              jnp.concatenate([r_w2[0], jnp.zeros((64, MIX_W), F32)], axis=0).astype(BF16),
              row(r_a0[0]),
              jnp.concatenate([jnp.zeros((64, MIX_W), F32), r_a2[0]], axis=0).astype(BF16),
              row(r_kk[0]), row(r_ka[0]), row(r_rk[0])],
        r_ln_g=row(r_ln_g[0]), r_ln_b=row(r_ln_b[0]),
        w_out1=od_w_out[0].astype(BF16), ln_g1=row(od_ln_g[0]), ln_b1=row(od_ln_b[0]),
    )

    x_p = jnp.concatenate([jnp.broadcast_to(meta_tokens[None], (nb, N_META, D_MODEL)), x_prompt],
                          axis=1).reshape(nb * P_LEN, D_MODEL)
    st_p = dict(conv=zeros(nb, CONV_W - 1, MIX_W), ssm_re=zeros(nb, 4096), ssm_im=zeros(nb, 4096),
                c=zeros(nb, M_HEADS, M_D, M_D), n=zeros(nb, M_HEADS, M_D), m=zeros(nb, M_HEADS),
                s=zeros(nb, R_HEADS, R_K, R_K), shift=zeros(nb, 3200))
    y_p, states_p = _trunk(x_p, st_p, w, CFG["P"])

    st_s = dict(conv=state_conv[0], ssm_re=state_ssm_re[0], ssm_im=state_ssm_im[0],
                c=state_mlstm_c[0], n=state_mlstm_n[0], m=state_mlstm_m[0],
                s=state_rwkv_s[0], shift=state_rwkv_shift[0])
    y_s, states_s = _trunk(x_sample.reshape(DEC_BATCH * DEC_SEQ, D_MODEL), st_s, w, CFG["S"])

    y_prompt = y_p
    y_sample = y_s.reshape(DEC_BATCH, DEC_SEQ, D_MODEL)
    return (y_prompt, y_sample) + states_p + states_s
```

```python
import functools
import math

import jax
import jax.numpy as jnp
from jax import lax
from jax.experimental import pallas as pl
from jax.experimental.pallas import tpu as pltpu

F32 = jnp.float32
BF16 = jnp.bfloat16

D_MODEL = 2048
MIX_W = 1024
N_META = 16
CONV_W = 31
S5_GROUP = 16
S5_GROUPS = 64
S5_STATE = 64
M_HEADS = 4
M_D = 256
R_HEADS = 16
R_K = 64
LN_EPS = 1e-5
R_LN_EPS = 64e-5
DEPTH = 2
ALPHA = (2 * DEPTH) ** 0.25

BATCH = 4
SEQ = 2048
P_LEN = N_META + SEQ
DEC_BATCH = 128
DEC_SEQ = 8

P1_Q, P1_K, P1_V, P1_O, P1_ZC = 0, 1024, 2048, 3072, 4096
P1_R, P1_RK, P1_RV, P1_ZD, P1_WA, P1_GATE = 5120, 6144, 7168, 8192, 9216, 9344
P1_N = 9728

VMEM_LIMIT = 48 * 1024 * 1024


def _cp(sem):
    return pltpu.CompilerParams(dimension_semantics=sem, vmem_limit_bytes=VMEM_LIMIT)


def _dot(a, b):
    return jnp.dot(a, b, preferred_element_type=F32)


def _dot_nt(a, b):
    return lax.dot_general(a, b, (((1,), (1,)), ((), ())), preferred_element_type=F32)


def _dot_tn(a, b):
    return lax.dot_general(a, b, (((0,), (0,)), ((), ())), preferred_element_type=F32)


def _split2(x):
    hi = x.astype(BF16)
    lo = (x - hi.astype(F32)).astype(BF16)
    return hi, lo


def _split3(x):
    hi = x.astype(BF16)
    r1 = x - hi.astype(F32)
    mid = r1.astype(BF16)
    lo = (r1 - mid.astype(F32)).astype(BF16)
    return hi, mid, lo


def _sigmoid(x):
    return jax.nn.sigmoid(x)


def _silu(x):
    return x * jax.nn.sigmoid(x)


def _softplus(x):
    return jnp.maximum(x, 0.0) + jnp.log(1.0 + jnp.exp(-jnp.abs(x)))


def _gelu_tanh(x):
    c = math.sqrt(2.0 / math.pi)
    return x * (0.5 * (1.0 + jnp.tanh(c * (x + 0.044715 * (x * x * x)))))


def _iota(shape, axis):
    return lax.broadcasted_iota(jnp.int32, shape, axis)


def _shr(x, k):
    return lax.shift_right_logical(x, jnp.int32(k))


def _log2(n):
    k = int(round(math.log2(n)))
    assert 1 << k == n
    return k


def _block_ones(n, seg, dtype):
    r = _shr(_iota((n, n), 0), _log2(seg))
    c = _shr(_iota((n, n), 1), _log2(seg))
    return (r == c).astype(dtype)


def _segsum(x, seg):
    g = _block_ones(256, seg, BF16)
    outs = []
    for j in range(x.shape[1] // 256):
        hi, lo = _split2(x[:, 256 * j:256 * (j + 1)])
        outs.append(_dot(hi, g) + _dot(lo, g))
    return jnp.concatenate(outs, axis=1)


def _row_cumsum(x, period):
    rows = x.shape[0]
    rid = _iota(x.shape, 0)
    if period < rows:
        rid = jnp.bitwise_and(rid, period - 1)
    d = 1
    while d < min(period, rows):
        x = x + jnp.where(rid >= d, pltpu.roll(x, d, 0), 0.0)
        d *= 2
    return x


def _row_cummax(x, period):
    rows = x.shape[0]
    rid = _iota(x.shape, 0)
    if period < rows:
        rid = jnp.bitwise_and(rid, period - 1)
    d = 1
    while d < min(period, rows):
        x = jnp.maximum(x, jnp.where(rid >= d, pltpu.roll(x, d, 0), -jnp.inf))
        d *= 2
    return x


def _mm_kernel(x_ref, w_ref, o_ref):
    o_ref[...] = _dot(x_ref[...].astype(BF16), w_ref[...])


def _matmul(x, w, tm, tn):
    r, k = x.shape
    n = w.shape[1]
    return pl.pallas_call(
        _mm_kernel,
        out_shape=jax.ShapeDtypeStruct((r, n), F32),
        grid=(pl.cdiv(r, tm), n // tn),
        in_specs=[pl.BlockSpec((tm, k), lambda i, j: (i, 0)),
                  pl.BlockSpec((k, tn), lambda i, j: (0, j))],
        out_specs=pl.BlockSpec((tm, tn), lambda i, j: (i, j)),
        compiler_params=_cp(("parallel", "arbitrary")),
    )(x, w)


def _mm_nt_kernel(x_ref, w_ref, o_ref):
    o_ref[...] = _dot_nt(x_ref[...].astype(BF16), w_ref[...])


def _matmul_nt(x, w_t, tm, tn):
    r, k = x.shape
    n = w_t.shape[0]
    return pl.pallas_call(
        _mm_nt_kernel,
        out_shape=jax.ShapeDtypeStruct((r, n), F32),
        grid=(pl.cdiv(r, tm), n // tn),
        in_specs=[pl.BlockSpec((tm, k), lambda i, j: (i, 0)),
                  pl.BlockSpec((tn, k), lambda i, j: (j, 0))],
        out_specs=pl.BlockSpec((tm, tn), lambda i, j: (i, j)),
        compiler_params=_cp(("parallel", "arbitrary")),
    )(x, w_t)


def _pw_kernel(a_ref, w_ref, z_ref, o_ref):
    o_ref[...] = _dot(a_ref[...].astype(BF16), w_ref[...]) * _silu(z_ref[...])


def _pw_gate(act, pw, proj0, tm, tn=512):
    r = act.shape[0]
    zb = 2048 // tn
    return pl.pallas_call(
        _pw_kernel,
        out_shape=jax.ShapeDtypeStruct((r, MIX_W), F32),
        grid=(pl.cdiv(r, tm), MIX_W // tn),
        in_specs=[pl.BlockSpec((tm, MIX_W), lambda i, j: (i, 0)),
                  pl.BlockSpec((MIX_W, tn), lambda i, j: (0, j)),
                  pl.BlockSpec((tm, tn), lambda i, j: (i, zb + j))],
        out_specs=pl.BlockSpec((tm, tn), lambda i, j: (i, j)),
        compiler_params=_cp(("parallel", "arbitrary")),
    )(act, pw, proj0)


def _glu_kernel(y_ref, wv_ref, wg_ref, bv_ref, bg_ref, z_ref, o_ref):
    y = y_ref[...].astype(BF16)
    v = _dot(y, wv_ref[...]) + bv_ref[...]
    g = _dot(y, wg_ref[...]) + bg_ref[...]
    o_ref[...] = v * _sigmoid(g) * _silu(z_ref[...])


def _glu_gate(yb, glu_w, glu_b, proj0, tm, tn=512):
    r = yb.shape[0]
    nb = MIX_W // tn
    zb = 4096 // tn
    return pl.pallas_call(
        _glu_kernel,
        out_shape=jax.ShapeDtypeStruct((r, MIX_W), F32),
        grid=(pl.cdiv(r, tm), nb),
        in_specs=[pl.BlockSpec((tm, MIX_W), lambda i, j: (i, 0)),
                  pl.BlockSpec((MIX_W, tn), lambda i, j: (0, j)),
                  pl.BlockSpec((MIX_W, tn), lambda i, j: (0, nb + j)),
                  pl.BlockSpec((1, tn), lambda i, j: (0, j)),
                  pl.BlockSpec((1, tn), lambda i, j: (0, nb + j)),
                  pl.BlockSpec((tm, tn), lambda i, j: (i, zb + j))],
        out_specs=pl.BlockSpec((tm, tn), lambda i, j: (i, j)),
        compiler_params=_cp(("parallel", "arbitrary")),
    )(yb, glu_w, glu_w, glu_b, glu_b, proj0)


def _out_ln_kernel(x_ref, ma_ref, mb_ref, wa_ref, wb_ref, g_ref, b_ref, o_ref):
    out = _dot(ma_ref[...].astype(BF16), wa_ref[...]) + _dot(mb_ref[...].astype(BF16), wb_ref[...])
    y = ALPHA * x_ref[...] + out
    mu = jnp.mean(y, axis=-1, keepdims=True)
    yc = y - mu
    var = jnp.mean(yc * yc, axis=-1, keepdims=True)
    o_ref[...] = yc * lax.rsqrt(var + LN_EPS) * g_ref[...] + b_ref[...]


def _out_ln(x, mix_a, mix_b, w_out, ln_g, ln_b, tm):
    r = x.shape[0]
    return pl.pallas_call(
        _out_ln_kernel,
        out_shape=jax.ShapeDtypeStruct((r, D_MODEL), F32),
        grid=(pl.cdiv(r, tm),),
        in_specs=[pl.BlockSpec((tm, D_MODEL), lambda i: (i, 0)),
                  pl.BlockSpec((tm, MIX_W), lambda i: (i, 0)),
                  pl.BlockSpec((tm, MIX_W), lambda i: (i, 0)),
                  pl.BlockSpec((MIX_W, D_MODEL), lambda i: (0, 0)),
                  pl.BlockSpec((MIX_W, D_MODEL), lambda i: (1, 0)),
                  pl.BlockSpec((1, D_MODEL), lambda i: (0, 0)),
                  pl.BlockSpec((1, D_MODEL), lambda i: (0, 0))],
        out_specs=pl.BlockSpec((tm, D_MODEL), lambda i: (i, 0)),
        compiler_params=_cp(("parallel",)),
    )(x, mix_a, mix_b, w_out, w_out, ln_g, ln_b)


def _out_ln_prompt(x, mix_a, mix_b, w_out, ln_g, ln_b, tm=256):
    tiles = SEQ // tm

    def rows(width):
        return pl.BlockSpec((pl.Element(tm), pl.Element(width)),
                            lambda n, t: (pl.multiple_of(n * P_LEN + N_META + t * tm, 8), 0))

    return pl.pallas_call(
        _out_ln_kernel,
        out_shape=jax.ShapeDtypeStruct((BATCH * SEQ, D_MODEL), F32),
        grid=(BATCH, tiles),
        in_specs=[rows(D_MODEL), rows(MIX_W), rows(MIX_W),
                  pl.BlockSpec((MIX_W, D_MODEL), lambda n, t: (0, 0)),
                  pl.BlockSpec((MIX_W, D_MODEL), lambda n, t: (1, 0)),
                  pl.BlockSpec((1, D_MODEL), lambda n, t: (0, 0)),
                  pl.BlockSpec((1, D_MODEL), lambda n, t: (0, 0))],
        out_specs=pl.BlockSpec((tm, D_MODEL), lambda n, t: (n * tiles + t, 0)),
        compiler_params=_cp(("parallel", "arbitrary")),
    )(x, mix_a, mix_b, w_out, w_out, ln_g, ln_b).reshape(BATCH, SEQ, D_MODEL)


def _conv_kernel(u_ref, g_ref, st_ref, w_ref, cb_ref, lg_ref, lb_ref, act_ref, nst_ref, hp_ref, hs_ref, wb_ref,
                 *, NB, TL, T):
    t = pl.program_id(1)
    for j in range(CONV_W):
        wb_ref[j] = jnp.broadcast_to(w_ref[j:j + 1, :], (8, MIX_W))
    for nb in range(NB):
        base = nb * TL

        @pl.when(t == 0)
        def _():
            hp_ref[nb, 0:2, :] = jnp.zeros((2, MIX_W), F32)
            hp_ref[nb, 2:32, :] = st_ref[nb]

        hp_ref[nb, TL + 32:TL + 40, :] = jnp.zeros((8, MIX_W), F32)
        hp_ref[nb, 32:32 + TL, :] = u_ref[base:base + TL, :] * _sigmoid(g_ref[base:base + TL, :])
        for b in range(8):
            hs_ref[b] = hp_ref[nb, b:b + TL + 32, :]

        def taps(r0, groups):
            acc = [None] * groups
            for j in range(CONV_W):
                o = j + 2
                wj = wb_ref[j]
                for g in range(groups):
                    term = wj * hs_ref[o % 8, pl.ds(r0 + 8 * (o // 8 + g), 8), :]
                    acc[g] = term if acc[g] is None else acc[g] + term
            for g in range(groups):
                act_ref[pl.ds(base + r0 + 8 * g, 8), :] = acc[g]

        def chunk(c, carry):
            taps(pl.multiple_of(c * 16, 16), 2)
            return carry

        lax.fori_loop(0, TL // 16, chunk, 0)
        if TL % 16:
            taps(TL - 8, 1)

        @pl.when(t == T - 1)
        def _():
            nst_ref[nb] = hp_ref[nb, TL + 2:TL + 32, :]

        if T > 1:
            hp_ref[nb, 0:32, :] = hp_ref[nb, TL:TL + 32, :]

    y = act_ref[...] + cb_ref[...]
    mu = jnp.mean(y, axis=-1, keepdims=True)
    yc = y - mu
    var = jnp.mean(yc * yc, axis=-1, keepdims=True)
    act_ref[...] = _silu(yc * lax.rsqrt(var + LN_EPS) * lg_ref[...] + lb_ref[...])


def _conv_call(proj0, state, conv_w, conv_b, ln_g, ln_b, *, N, L, NB, TL):
    T = L // TL
    RB = NB * TL
    assert NB == 1 or T == 1
    return pl.pallas_call(
        functools.partial(_conv_kernel, NB=NB, TL=TL, T=T),
        out_shape=(jax.ShapeDtypeStruct((N * L, MIX_W), F32),
                   jax.ShapeDtypeStruct((N, CONV_W - 1, MIX_W), F32)),
        grid=(N // NB, T),
        in_specs=[pl.BlockSpec((RB, MIX_W), lambda i, t: (i * T + t, 0)),
                  pl.BlockSpec((RB, MIX_W), lambda i, t: (i * T + t, 1)),
                  pl.BlockSpec((NB, CONV_W - 1, MIX_W), lambda i, t: (i, 0, 0)),
                  pl.BlockSpec((CONV_W, MIX_W), lambda i, t: (0, 0)),
                  pl.BlockSpec((1, MIX_W), lambda i, t: (0, 0)),
                  pl.BlockSpec((1, MIX_W), lambda i, t: (0, 0)),
                  pl.BlockSpec((1, MIX_W), lambda i, t: (0, 0))],
        out_specs=(pl.BlockSpec((RB, MIX_W), lambda i, t: (i * T + t, 0)),
                   pl.BlockSpec((NB, CONV_W - 1, MIX_W), lambda i, t: (i, 0, 0))),
        scratch_shapes=[pltpu.VMEM((NB, TL + 40, MIX_W), F32),
                        pltpu.VMEM((8, TL + 32, MIX_W), F32),
                        pltpu.VMEM((CONV_W, 8, MIX_W), F32)],
        compiler_params=_cp(("arbitrary", "arbitrary")),
    )(proj0, proj0, state, conv_w, conv_b, ln_g, ln_b)


def _s5_kernel(u_ref, wb_ref, wk_ref, wc_ref, d_ref, ar_ref, ai_ref, x0r_ref, x0i_ref,
               y_ref, xfr_ref, xfi_ref, xs_ref, cr_ref, ci_ref, *, NB, TL, T):
    t = pl.program_id(2)
    RB = NB * TL
    GL = TL // 8
    u = u_ref[...]
    ub = u.astype(BF16)
    big = _dot(ub, wb_ref[0])
    xs_ref[0] = big[:, :1024]
    xs_ref[1] = big[:, 1024:]
    rid = jnp.bitwise_and(_iota((RB, 256), 0), 7)
    lags = [ub] + [jnp.where(rid >= d, pltpu.roll(u, d, 0), 0.0).astype(BF16) for d in range(1, 8)]
    y_loc = _dot(jnp.concatenate(lags, axis=1), wk_ref[0])
    ar = ar_ref[...]
    ai = ai_ref[...]

    def cmul(pr, pi, qr, qi):
        return pr * qr - pi * qi, pr * qi + pi * qr

    a1 = (ar, ai)
    a2 = cmul(*a1, *a1)
    a4 = cmul(*a2, *a2)
    a3 = cmul(*a2, *a1)
    a5 = cmul(*a4, *a1)
    a6 = cmul(*a4, *a2)
    a7 = cmul(*a6, *a1)
    a8 = cmul(*a4, *a4)
    a0 = (jnp.ones_like(ar), jnp.zeros_like(ai))
    r8 = _iota((8, 1024), 0)

    def table(powers):
        tr = jnp.zeros((8, 1024), F32)
        ti = jnp.zeros((8, 1024), F32)
        for k, (pr, pi) in enumerate(powers):
            tr = jnp.where(r8 == k, pr, tr)
            ti = jnp.where(r8 == k, pi, ti)
        return tr, ti

    pwr, pwi = table((a1, a2, a3, a4, a5, a6, a7, a8))
    qwr, qwi = table((a7, a6, a5, a4, a3, a2, a1, a0))
    a8r, a8i = a8

    first = t == 0

    def seq_body(nb, carry0):
        x0r = x0r_ref[nb]
        x0i = x0i_ref[nb]
        if T > 1:
            c_r = jnp.where(first, x0r, cr_ref[0:1, :])
            c_i = jnp.where(first, x0i, ci_ref[0:1, :])
        else:
            c_r, c_i = x0r, x0i

        def grp(g, c):
            c_r, c_i = c
            off = pl.multiple_of(nb * TL + g * 8, 8)
            vr = xs_ref[0, pl.ds(off, 8), :]
            vi = xs_ref[1, pl.ds(off, 8), :]
            er = jnp.sum(qwr * vr - qwi * vi, axis=0, keepdims=True)
            ei = jnp.sum(qwr * vi + qwi * vr, axis=0, keepdims=True)
            br = jnp.broadcast_to(c_r, (8, 1024))
            bi = jnp.broadcast_to(c_i, (8, 1024))
            xs_ref[0, pl.ds(off, 8), :] = pwr * br - pwi * bi
            xs_ref[1, pl.ds(off, 8), :] = pwr * bi + pwi * br
            return a8r * c_r - a8i * c_i + er, a8r * c_i + a8i * c_r + ei

        c_r, c_i = lax.fori_loop(0, GL, grp, (c_r, c_i))
        if T > 1:
            cr_ref[...] = jnp.broadcast_to(c_r, (8, 1024))
            ci_ref[...] = jnp.broadcast_to(c_i, (8, 1024))

        @pl.when(t == T - 1)
        def _():
            xfr_ref[nb] = c_r
            xfi_ref[nb] = c_i

        return carry0

    lax.fori_loop(0, NB, seq_body, 0)
    y = (_dot(xs_ref[0].astype(BF16), wc_ref[0, 0:1024, :])
         + _dot(xs_ref[1].astype(BF16), wc_ref[0, 1024:2048, :]))
    y_ref[...] = _gelu_tanh(y + y_loc + d_ref[...] * u)


def _s5_call(proj0, wb, wk, wc, dvec, ar, ai, x0r, x0i, *, N, L, NB, TL):
    T = L // TL
    RB = NB * TL
    assert NB == 1 or T == 1
    ub = 3072 // 256
    st = jax.ShapeDtypeStruct((N, 1, 4096), F32)
    return pl.pallas_call(
        functools.partial(_s5_kernel, NB=NB, TL=TL, T=T),
        out_shape=(jax.ShapeDtypeStruct((N * L, MIX_W), F32), st, st),
        grid=(N // NB, 4, T),
        in_specs=[pl.BlockSpec((RB, 256), lambda i, j, t: (i * T + t, ub + j)),
                  pl.BlockSpec((1, 256, 2048), lambda i, j, t: (j, 0, 0)),
                  pl.BlockSpec((1, 2048, 256), lambda i, j, t: (j, 0, 0)),
                  pl.BlockSpec((1, 2048, 256), lambda i, j, t: (j, 0, 0)),
                  pl.BlockSpec((1, 256), lambda i, j, t: (0, j)),
                  pl.BlockSpec((1, 1024), lambda i, j, t: (0, j)),
                  pl.BlockSpec((1, 1024), lambda i, j, t: (0, j)),
                  pl.BlockSpec((NB, 1, 1024), lambda i, j, t: (i, 0, j)),
                  pl.BlockSpec((NB, 1, 1024), lambda i, j, t: (i, 0, j))],
        out_specs=(pl.BlockSpec((RB, 256), lambda i, j, t: (i * T + t, j)),
                   pl.BlockSpec((NB, 1, 1024), lambda i, j, t: (i, 0, j)),
                   pl.BlockSpec((NB, 1, 1024), lambda i, j, t: (i, 0, j))),
        scratch_shapes=[pltpu.VMEM((2, RB, 1024), F32),
                        pltpu.VMEM((8, 1024), F32),
                        pltpu.VMEM((8, 1024), F32)],
        compiler_params=_cp(("arbitrary", "arbitrary", "arbitrary")),
    )(proj0, wb, wk, wc, dvec, ar, ai, x0r, x0i)


def _mlstm_kernel(q_ref, k_ref, v_ref, o_ref, z_ref, gt_ref, gb_ref, hg_ref, c0_ref, n0_ref, m0_ref,
                  y_ref, c_ref, n_ref, m_ref, cs_ref, ns_ref, ms_ref, *, NB, TL, T):
    for nb in range(NB):
        rows = lambda ref: ref.at[pl.ds(nb * TL, TL)]
        one = lambda ref: ref.at[pl.ds(nb, 1)]
        _mlstm_seq(rows(q_ref), rows(k_ref), rows(v_ref), rows(o_ref), rows(z_ref), rows(gt_ref), gb_ref, hg_ref,
                   one(c0_ref), one(n0_ref), one(m0_ref), rows(y_ref), one(c_ref), one(n_ref), one(m_ref),
                   cs_ref.at[nb], ns_ref.at[nb], ms_ref.at[nb], TL=TL, T=T)


def _mlstm_seq(q_ref, k_ref, v_ref, o_ref, z_ref, gt_ref, gb_ref, hg_ref, c0_ref, n0_ref, m0_ref,
               y_ref, c_ref, n_ref, m_ref, cs_ref, ns_ref, ms_ref, *, TL, T):
    t = pl.program_id(1)

    @pl.when(t == 0)
    def _():
        cs_ref[...] = c0_ref[0]
        ns_ref[...] = n0_ref[0]
        ms_ref[...] = m0_ref[0]

    G = gt_ref[...] + gb_ref[...]
    B = _row_cumsum(-_softplus(-G), TL)
    Bs = pltpu.roll(B, 124, 1)
    A = G - Bs
    CM = _row_cummax(A, TL)
    ms = ms_ref[...]
    dg = _iota((8, 128), 0) == _iota((8, 128), 1)
    mrow = jnp.sum(jnp.where(dg, ms, 0.0), axis=0, keepdims=True)
    M = jnp.maximum(mrow, CM)
    MT = Bs + M
    sel = dg.astype(BF16)
    a_hi, a_mid, a_lo = _split3(A)
    Arow = _dot_nt(sel, a_hi) + _dot_nt(sel, a_mid) + _dot_nt(sel, a_lo)
    causal = _iota((TL, TL), 0) >= _iota((TL, TL), 1)
    H = range(M_HEADS)
    sl = [slice(M_D * h, M_D * (h + 1)) for h in H]
    q = [q_ref[:, sl[h]] * (M_D ** -0.5) for h in H]
    qb = [x.astype(BF16) for x in q]
    kf = [k_ref[:, sl[h]] for h in H]
    kb = [x.astype(BF16) for x in kf]
    vf = [v_ref[:, sl[h]] for h in H]
    c_old = [cs_ref[h] for h in H]
    n_old = [ns_ref[h:h + 1, :] for h in H]
    m_col = [M[:, h:h + 1] for h in H]
    mt_col = [MT[:, h:h + 1] for h in H]
    b_col = [Bs[:, h:h + 1] for h in H]
    m_prev = [mrow[:, h:h + 1] for h in H]
    dm = [jnp.exp(jnp.where(causal, Arow[h:h + 1, :] - m_col[h], -jnp.inf)) for h in H]
    s = [_dot_nt(qb[h], kb[h]) * dm[h] for h in H]
    inter = [jnp.exp(m_prev[h] - m_col[h]) for h in H]
    h_intra = [_dot(s[h].astype(BF16), vf[h].astype(BF16)) for h in H]
    h_inter = [_dot(qb[h], c_old[h].astype(BF16)) * inter[h] for h in H]
    n_all = [jnp.sum(s[h], axis=1, keepdims=True) + jnp.sum(q[h] * n_old[h], axis=1, keepdims=True) * inter[h]
             for h in H]
    hh = [(h_intra[h] + h_inter[h]) / jnp.maximum(jnp.abs(n_all[h]), jnp.exp(-mt_col[h])) for h in H]
    m_new = [mt_col[h][TL - 1:TL, :] for h in H]
    b_end = [b_col[h][TL - 1:TL, :] for h in H]
    dec = [jnp.exp(m_prev[h] + b_end[h] - m_new[h]) for h in H]
    w_s = [jnp.exp(b_end[h] - b_col[h] + G[:, h:h + 1] - m_new[h]) for h in H]
    c_new = [c_old[h] * dec[h] + _dot_tn(kb[h], (vf[h] * w_s[h]).astype(BF16)) for h in H]
    n_new = [n_old[h] * dec[h] + jnp.sum(kf[h] * w_s[h], axis=0, keepdims=True) for h in H]
    outs = []
    for h in H:
        mu = jnp.mean(hh[h], axis=-1, keepdims=True)
        hc = hh[h] - mu
        var = jnp.mean(hc * hc, axis=-1, keepdims=True)
        hn = hc * lax.rsqrt(var + LN_EPS) * hg_ref[:, sl[h]]
        outs.append(hn * _sigmoid(o_ref[:, sl[h]]) * _silu(z_ref[:, sl[h]]))
    for h in H:
        cs_ref[h] = c_new[h]
        ns_ref[h:h + 1, :] = n_new[h]
        ms_ref[h:h + 1, :] = jnp.broadcast_to(m_new[h], (1, 128))
        y_ref[:, sl[h]] = outs[h]

    @pl.when(t == T - 1)
    def _():
        c_ref[0] = cs_ref[...]
        n_ref[0] = ns_ref[...]
        m_ref[0] = ms_ref[...]


def _mlstm_call(proj1, gate_b, hn_g, c0, n0, m0, *, N, L, NB, TL):
    T = L // TL
    RB = NB * TL
    assert NB == 1 or T == 1

    def col(cb):
        return pl.BlockSpec((RB, MIX_W), lambda i, t: (i * T + t, cb))

    st_specs = (pl.BlockSpec((NB, M_HEADS, M_D, M_D), lambda i, t: (i, 0, 0, 0)),
                pl.BlockSpec((NB, M_HEADS, M_D), lambda i, t: (i, 0, 0)),
                pl.BlockSpec((NB, 8, 128), lambda i, t: (i, 0, 0)))
    return pl.pallas_call(
        functools.partial(_mlstm_kernel, NB=NB, TL=TL, T=T),
        out_shape=(jax.ShapeDtypeStruct((N * L, MIX_W), F32),
                   jax.ShapeDtypeStruct((N, M_HEADS, M_D, M_D), F32),
                   jax.ShapeDtypeStruct((N, M_HEADS, M_D), F32),
                   jax.ShapeDtypeStruct((N, 8, 128), F32)),
        grid=(N // NB, T),
        in_specs=[col(0), col(1), col(2), col(3), col(4),
                  pl.BlockSpec((RB, 128), lambda i, t: (i * T + t, P1_GATE // 128)),
                  pl.BlockSpec((1, 128), lambda i, t: (0, 0)),
                  pl.BlockSpec((1, MIX_W), lambda i, t: (0, 0))] + list(st_specs),
        out_specs=(pl.BlockSpec((RB, MIX_W), lambda i, t: (i * T + t, 0)),) + st_specs,
        scratch_shapes=[pltpu.VMEM((NB, M_HEADS, M_D, M_D), F32),
                        pltpu.VMEM((NB, M_HEADS, M_D), F32),
                        pltpu.VMEM((NB, 8, 128), F32)],
        compiler_params=_cp(("arbitrary", "arbitrary")),
    )(proj1, proj1, proj1, proj1, proj1, proj1, gate_b, hn_g, c0, n0, m0)


N_RA_OUT = 9


def _rwkv_a_kernel(pr_ref, pk_ref, pv_ref, pwa_ref, hr_ref, hk_ref, hv_ref, hwa_ref, st_ref, stwa_ref,
                   mu_ref, muwa_ref, w0_ref, w2_ref, a0_ref, a2_ref, kkp_ref, ka_ref, rk_ref,
                   ah_ref, rh_ref, bh_ref, kh_ref, vo_ref, ul_ref, yl_ref, dc_ref, bo_ref, shr_ref, shw_ref,
                   s_at, s_rt, s_bt, s_kt, s_v, s_cum, *, RB, CT, N, L, U):
    HS = R_HEADS * CT
    HG = 128 // CT
    NG = R_HEADS // HG
    GW = HG * R_K
    NCH = RB // CT
    short = L == CT
    rid = _iota((RB, 1), 0)
    grow = pl.program_id(0) * RB + rid

    def shifted(p_ref, h_ref, s_ref, lo, hi, mu):
        p = p_ref[...]
        prev = pltpu.roll(p, 1, 0)
        if short:
            prev = jnp.where(jnp.bitwise_and(rid, CT - 1) == 0, s_ref[:, lo:hi], prev)
        else:
            prev = jnp.where(rid == 0, h_ref[7:8, :], prev)
            for n in range(N):
                prev = jnp.where(grow == n * L, s_ref[n:n + 1, lo:hi], prev)
        return p + (prev - p) * mu

    raw = ((pr_ref, shr_ref, 0), (pk_ref, shr_ref, 1024), (pv_ref, shr_ref, 2048), (pwa_ref, shw_ref, 0))
    if short:
        nseq = RB // CT
        sel = (_iota((nseq, RB), 1) == _iota((nseq, RB), 0) * CT + (CT - 1)).astype(BF16)
        for src, dst, lo in raw:
            hi, mid, low = _split3(src[...])
            dst[:, lo:lo + src.shape[1]] = _dot(sel, hi) + _dot(sel, mid) + _dot(sel, low)
    else:
        for n in range(N):
            tile, off = divmod(n * L + L - 1, RB)

            @pl.when(pl.program_id(0) == tile)
            def _():
                for src, dst, lo in raw:
                    dst[n:n + 1, lo:lo + src.shape[1]] = src[off:off + 1, :]

    r = shifted(pr_ref, hr_ref, st_ref, 0, 1024, mu_ref[:, 0:1024])
    k = shifted(pk_ref, hk_ref, st_ref, 1024, 2048, mu_ref[:, 1024:2048])
    v = shifted(pv_ref, hv_ref, st_ref, 2048, 3072, mu_ref[:, 2048:3072])
    wa = shifted(pwa_ref, hwa_ref, stwa_ref, 0, 128, muwa_ref[...])
    w = -_softplus(-(w0_ref[...] + _dot(jnp.tanh(wa).astype(BF16), w2_ref[...]))) - 0.5
    wlog = -jnp.exp(w)
    a = _sigmoid(a0_ref[...] + _dot(wa.astype(BF16), a2_ref[...]))
    kk = k * kkp_ref[...]
    kk = kk / jnp.maximum(jnp.sqrt(_segsum(kk * kk, R_K)), 1e-12)
    kmod = k * (1.0 + (a - 1.0) * ka_ref[...])
    bo_ref[...] = _segsum(r * kmod * rk_ref[...], R_K) * v
    cum = _row_cumsum(wlog, CT)
    einv = jnp.exp(-cum)
    s_at[...] = (-kk) * jnp.exp(cum - wlog)
    s_rt[...] = r * jnp.exp(cum)
    s_bt[...] = kk * a * einv
    s_kt[...] = kmod * einv
    s_v[...] = v
    s_cum[...] = cum

    be_mask = (_shr(_iota((128, GW), 0), _log2(CT)) == _shr(_iota((128, GW), 1), _log2(R_K))).astype(F32)
    bd_mask = (_shr(_iota((HS, HS), 0), _log2(CT)) == _shr(_iota((HS, HS), 1), _log2(CT))).astype(F32)
    tt = _iota((CT, HS), 0)
    ss = jnp.bitwise_and(_iota((CT, HS), 1), CT - 1)
    strict = tt > ss
    incl = tt >= ss
    eye_c = (tt == ss).astype(F32)
    cat0 = lambda *xs: jnp.concatenate(xs, axis=0)

    def blockexp(x):
        return [(jnp.concatenate([x[:, GW * g:GW * (g + 1)]] * HG, axis=0) * be_mask).astype(BF16)
                for g in range(NG)]

    def gram(lhs, be):
        lb = lhs.astype(BF16)
        return jnp.concatenate([_dot_nt(lb[:, GW * g:GW * (g + 1)], be[g]) for g in range(NG)], axis=1)

    def apply(cmp, be):
        cb = cmp.astype(BF16)
        return jnp.concatenate([_dot(cb[:, 128 * g:128 * (g + 1)], be[g]) for g in range(NG)], axis=1)

    def bdiag(x):
        return jnp.concatenate([x] * R_HEADS, axis=0) * bd_mask

    def mm_hl(stack, wh, wl):
        sh, sl = _split2(stack)
        n = stack.shape[0]
        full = _dot(cat0(sh, sl), wh)
        return full[:n] + full[n:] + _dot(sh, wl)

    def chunks(i, carry):
        rows = [pl.ds(pl.multiple_of((i * U + u) * CT, CT), CT) for u in range(U)]
        ld = lambda ref: [ref[rw, :] for rw in rows]
        at, rt, bt, kt, vv, cm = ld(s_at), ld(s_rt), ld(s_bt), ld(s_kt), ld(s_v), ld(s_cum)
        each = lambda f, *xs: [f(*a_) for a_ in zip(*xs)]
        ar_ = each(cat0, at, rt)
        gb = each(lambda l_, y_: gram(l_, blockexp(y_)), ar_, bt)
        gk = each(lambda l_, y_: gram(l_, blockexp(y_)), ar_, kt)
        a_ab = each(lambda m: jnp.where(strict, m[:CT], 0.0), gb)
        a_rb = each(lambda m: jnp.where(incl, m[CT:], 0.0), gb)
        a_ak = each(lambda m: jnp.where(strict, m[:CT], 0.0), gk)
        a_rk = each(lambda m: jnp.where(incl, m[CT:], 0.0), gk)
        p = each(lambda m: eye_c + m, a_ab)
        x = a_ab
        q = a_rb
        w_hl = each(lambda m: _split2(bdiag(m)), x)
        res = each(lambda x_, q_, w_: mm_hl(cat0(x_, q_), *w_), x, q, w_hl)
        x = each(lambda r_: r_[:CT], res)
        q = each(lambda q_, r_: q_ + r_[CT:], q, res)
        pw = 2
        while pw < CT:
            w_hl = each(lambda m: _split2(bdiag(m)), x)
            if 2 * pw >= CT:
                res = each(lambda p_, q_, w_: mm_hl(cat0(p_, q_), *w_), p, q, w_hl)
                q = each(lambda q_, r_: q_ + r_[CT:], q, res)
            else:
                res = each(lambda p_, x_, q_, w_: mm_hl(cat0(p_, x_, q_), *w_), p, x, q, w_hl)
                x = each(lambda r_: r_[CT:2 * CT], res)
                q = each(lambda q_, r_: q_ + r_[2 * CT:], q, res)
            p = each(lambda p_, r_: p_ + r_[:CT], p, res)
            pw *= 2
        tq = each(cat0, p, q)
        res = each(lambda m, k_: _dot(m.astype(BF16), bdiag(k_).astype(BF16)), tq, a_ak)
        ty = each(lambda r_, k_: cat0(r_[:CT], r_[CT:] + k_), res, a_rk)
        o1 = each(lambda m, y_: apply(m, blockexp(y_)), tq, at)
        o2 = each(lambda m, y_: apply(m, blockexp(y_)), ty, vv)
        ect = each(lambda c_: jnp.exp(c_[CT - 1:CT, :]), cm)
        for u, rw in enumerate(rows):
            ah_ref[rw, :] = o1[u][:CT]
            rh_ref[rw, :] = rt[u] + o1[u][CT:]
            ul_ref[rw, :] = o2[u][:CT]
            yl_ref[rw, :] = o2[u][CT:]
            bh_ref[rw, :] = bt[u] * ect[u]
            kh_ref[rw, :] = kt[u] * ect[u]
            vo_ref[rw, :] = vv[u]
            dc_ref[rw, :] = jnp.broadcast_to(ect[u], (CT, 1024))
        return carry

    lax.fori_loop(0, NCH // U, chunks, 0)


def _rwkv_a_call(proj1, st_rkv, st_wa, wts, *, N, L, RB, CT, U):
    short = L == CT
    rows = N * L
    assert rows % RB == 0 and (RB // CT) % U == 0

    def col(cb, width=MIX_W):
        return pl.BlockSpec((RB, width), lambda i: (i, cb))

    def halo(cb, width=MIX_W):
        return pl.BlockSpec((8, width), lambda i: (jnp.maximum(i * (RB // 8) - 1, 0), cb))

    if short:
        st_specs = [pl.BlockSpec((RB, 3072), lambda i: (i, 0)), pl.BlockSpec((RB, 128), lambda i: (i, 0))]
    else:
        st_specs = [pl.BlockSpec((N, 3072), lambda i: (0, 0)), pl.BlockSpec((N, 128), lambda i: (0, 0))]

    def full(shape):
        return pl.BlockSpec(shape, lambda i: (0,) * len(shape))

    o_spec = pl.BlockSpec((RB, MIX_W), lambda i: (i, 0))
    if short:
        sh_specs = (pl.BlockSpec((RB // CT, 3072), lambda i: (i, 0)), pl.BlockSpec((RB // CT, 128), lambda i: (i, 0)))
    else:
        sh_specs = (pl.BlockSpec((N, 3072), lambda i: (0, 0)), pl.BlockSpec((N, 128), lambda i: (0, 0)))
    outs = pl.pallas_call(
        functools.partial(_rwkv_a_kernel, RB=RB, CT=CT, N=N, L=L, U=U),
        out_shape=((jax.ShapeDtypeStruct((rows, MIX_W), F32),) * N_RA_OUT
                   + (jax.ShapeDtypeStruct((N, 3072), F32), jax.ShapeDtypeStruct((N, 128), F32))),
        grid=(rows // RB,),
        in_specs=([col(P1_R // 1024), col(P1_RK // 1024), col(P1_RV // 1024), col(P1_WA // 128, 128),
                   halo(P1_R // 1024), halo(P1_RK // 1024), halo(P1_RV // 1024), halo(P1_WA // 128, 128)]
                  + st_specs
                  + [full((1, 3072)), full((1, 128)), full((1, 1024)), full((128, 1024)), full((1, 1024)),
                     full((128, 1024)), full((1, 1024)), full((1, 1024)), full((1, 1024))]),
        out_specs=(o_spec,) * N_RA_OUT + sh_specs,
        scratch_shapes=[pltpu.VMEM((RB, MIX_W), F32)] * 6,
        compiler_params=_cp(("arbitrary",)),
    )(*([proj1] * 8 + [st_rkv, st_wa] + list(wts)))
    return outs[:N_RA_OUT], jnp.concatenate(outs[N_RA_OUT:], axis=1)


def _rwkv_b_kernel(ah_ref, rh_ref, bh_ref, kh_ref, v_ref, ul_ref, yl_ref, dc_ref, bo_ref, z_ref, lg_ref, lb_ref,
                   s0_ref, y_ref, so_ref, sbd_ref, yb_ref, *, NBLK, TLB, CT, T):
    t = pl.program_id(1)
    bd_mask = (_shr(_iota((256, 256), 0), 6) == _shr(_iota((256, 256), 1), 6)).astype(F32)
    e_tile = (_iota((64, 256), 0) == jnp.bitwise_and(_iota((64, 256), 1), 63)).astype(BF16)
    e_fold = (jnp.bitwise_and(_iota((256, 64), 0), 63) == _iota((256, 64), 1)).astype(BF16)

    @pl.when(t == 0)
    def _():
        for nb in range(NBLK):
            for j in range(4):
                hi, lo = _split2(s0_ref[nb, 256 * j:256 * (j + 1), :])
                full = _dot(jnp.concatenate([hi, lo], axis=0), e_tile)
                full = full[:256] + full[256:]
                sbd_ref[4 * nb + j] = full * bd_mask

    chains = [(nb, j, slice(256 * j, 256 * (j + 1))) for nb in range(NBLK) for j in range(4)]
    for c in range(TLB // CT):
        rows = slice(c * CT, (c + 1) * CT)
        sbs = [sbd_ref[4 * nb + j] for nb, j, cs in chains]
        outs = [_dot_nt(jnp.concatenate([ah_ref[nb, rows, cs], rh_ref[nb, rows, cs]], axis=0).astype(BF16),
                        sb.astype(BF16))
                for (nb, j, cs), sb in zip(chains, sbs)]
        upds = [_dot_tn(jnp.concatenate([o[:CT] + ul_ref[nb, rows, cs], v_ref[nb, rows, cs]],
                                        axis=0).astype(BF16),
                        jnp.concatenate([bh_ref[nb, rows, cs], kh_ref[nb, rows, cs]], axis=0).astype(BF16))
                for (nb, j, cs), o in zip(chains, outs)]
        for (nb, j, cs), sb, o, upd in zip(chains, sbs, outs, upds):
            yb_ref[nb, rows, cs] = o[CT:] + yl_ref[nb, rows, cs]
            sbd_ref[4 * nb + j] = sb * dc_ref[nb, c * CT:c * CT + 1, cs] + upd * bd_mask

    @pl.when(t == T - 1)
    def _():
        for nb in range(NBLK):
            outs = []
            for j in range(4):
                hi, lo = _split2(sbd_ref[4 * nb + j])
                both = _dot(jnp.concatenate([hi, lo], axis=0), e_fold)
                outs.append(both[:256] + both[256:])
            so_ref[nb] = jnp.concatenate(outs, axis=0)

    cat = lambda ref: jnp.concatenate([ref[nb] for nb in range(NBLK)], axis=0)
    y = cat(yb_ref)
    mu = _segsum(y, R_K) * (1.0 / R_K)
    yc = y - mu
    var = _segsum(yc * yc, R_K) * (1.0 / R_K)
    yn = yc * lax.rsqrt(var + R_LN_EPS) * lg_ref[...] + lb_ref[...] + cat(bo_ref)
    out = yn * _silu(cat(z_ref))
    for nb in range(NBLK):
        y_ref[nb] = out[nb * TLB:(nb + 1) * TLB, :]


def _rwkv_b_call(ra, proj1, ln_g, ln_b, s0, *, N, L, NBLK, TLB, CT):
    T = L // TLB
    blk = lambda cb: pl.BlockSpec((NBLK, TLB, MIX_W), lambda i, t: (i, t, cb))
    s_spec = pl.BlockSpec((NBLK, 1024, 64), lambda i, t: (i, 0, 0))
    ra3 = [a.reshape(N, L, MIX_W) for a in ra]
    y, s_new = pl.pallas_call(
        functools.partial(_rwkv_b_kernel, NBLK=NBLK, TLB=TLB, CT=CT, T=T),
        out_shape=(jax.ShapeDtypeStruct((N, L, MIX_W), F32),
                   jax.ShapeDtypeStruct((N, 1024, 64), F32)),
        grid=(N // NBLK, T),
        in_specs=([blk(0)] * N_RA_OUT
                  + [blk(P1_ZD // 1024),
                     pl.BlockSpec((1, MIX_W), lambda i, t: (0, 0)),
                     pl.BlockSpec((1, MIX_W), lambda i, t: (0, 0)),
                     s_spec]),
        out_specs=(blk(0), s_spec),
        scratch_shapes=[pltpu.VMEM((4 * NBLK, 256, 256), F32),
                        pltpu.VMEM((NBLK, TLB, MIX_W), F32)],
        compiler_params=_cp(("arbitrary", "arbitrary")),
    )(*(ra3 + [proj1.reshape(N, L, P1_N), ln_g, ln_b, s0]))
    return y.reshape(N * L, MIX_W), s_new


def _regroup_w1(w1):
    wt = w1.T
    return jnp.concatenate([wt[0:4096], wt[4104:8200], wt[8328:9352], wt[8200:8328], wt[4096:4104],
                            jnp.zeros((P1_N - 9352, D_MODEL), F32)], axis=0).astype(BF16)


def _s5_weights(lam_re, lam_im, log_dt, b_re, b_im, c_re, c_im):
    dt = jnp.exp(log_dt)[:, None]
    mag = jnp.exp(lam_re * dt)
    ar = mag * jnp.cos(lam_im * dt)
    ai = mag * jnp.sin(lam_im * dt)
    den = lam_re * lam_re + lam_im * lam_im
    qr = ((ar - 1.0) * lam_re + ai * lam_im) / den
    qi = (ai * lam_re - (ar - 1.0) * lam_im) / den
    bbr = qr[..., None] * b_re - qi[..., None] * b_im
    bbi = qr[..., None] * b_im + qi[..., None] * b_re
    eye = jnp.eye(16, dtype=F32)

    def in_blocks(bb):
        bb = bb.reshape(4, 16, S5_STATE, S5_GROUP)
        return jnp.einsum('jgph,gk->jghkp', bb, eye).reshape(4, 256, 1024)

    def out_blocks(cc):
        cc = cc.reshape(4, 16, S5_GROUP, S5_STATE)
        return jnp.einsum('jghp,gk->jgpkh', cc, eye).reshape(4, 1024, 256)

    wb = jnp.concatenate([in_blocks(bbr), in_blocks(bbi)], axis=2).astype(BF16)
    wc = jnp.concatenate([out_blocks(c_re), out_blocks(-c_im)], axis=1).astype(BF16)
    pr, pi = jnp.ones_like(ar), jnp.zeros_like(ai)
    lag = []
    for _ in range(8):
        cpr = c_re * pr[:, None, :] - c_im * pi[:, None, :]
        cpi = c_re * pi[:, None, :] + c_im * pr[:, None, :]
        lag.append(jnp.einsum('gop,gph->goh', cpr, bbr, precision=lax.Precision.HIGHEST)
                   - jnp.einsum('gop,gph->goh', cpi, bbi, precision=lax.Precision.HIGHEST))
        pr, pi = pr * ar - pi * ai, pr * ai + pi * ar
    kd = jnp.stack(lag).reshape(8, 4, 16, S5_GROUP, S5_GROUP)
    kd = kd.transpose(1, 0, 2, 4, 3).reshape(4, 8, 256, S5_GROUP)
    spread = (jnp.arange(256)[None, :] % S5_GROUP == jnp.arange(S5_GROUP)[:, None]).astype(F32)
    same_group = (jnp.arange(256)[:, None] // S5_GROUP == jnp.arange(256)[None, :] // S5_GROUP)
    wk = jnp.matmul(kd, spread, precision=lax.Precision.HIGHEST) * same_group
    wk = wk.reshape(4, 8 * 256, 256).astype(BF16)
    return ar.reshape(1, 4096), ai.reshape(1, 4096), wb, wk, wc


CFG = {
    "P": dict(N=BATCH, L=P_LEN, tm_mm=1376, tm=688, tm_ln=344, drop_meta=True,
              conv=dict(NB=1, TL=344), s5=dict(NB=1, TL=688), mlstm=dict(NB=1, TL=344),
              ra=dict(RB=192, CT=16, U=12), rb=dict(NBLK=4, TLB=48, CT=16)),
    "S": dict(N=DEC_BATCH, L=DEC_SEQ, tm_mm=1024, tm=512, tm_ln=256, drop_meta=False,
              conv=dict(NB=16, TL=8), s5=dict(NB=32, TL=8), mlstm=dict(NB=4, TL=8),
              ra=dict(RB=256, CT=8, U=8), rb=dict(NBLK=8, TLB=8, CT=8)),
}


def _trunk(x, st, w, cfg):
    n, l = cfg["N"], cfg["L"]
    proj0 = _matmul(x, w["w_in0"], cfg["tm_mm"], 512)
    act, conv_new = _conv_call(proj0, st["conv"], w["conv_w"], w["conv_b"], w["a_ln_g"], w["a_ln_b"],
                               N=n, L=l, **cfg["conv"])
    mix_a = _pw_gate(act, w["pw"], proj0, tm=cfg["tm"])
    yb, xr, xi = _s5_call(proj0, w["s5_wb"], w["s5_wk"], w["s5_wc"], w["s5_d"], w["s5_ar"], w["s5_ai"],
                          st["ssm_re"].reshape(n, 1, 4096), st["ssm_im"].reshape(n, 1, 4096),
                          N=n, L=l, **cfg["s5"])
    mix_b = _glu_gate(yb, w["glu_w"], w["glu_b"], proj0, tm=cfg["tm"])
    x1 = _out_ln(x, mix_a, mix_b, w["w_out0"], w["ln_g0"], w["ln_b0"], tm=cfg["tm_ln"])

    proj1 = _matmul_nt(x1, w["w_in1"], cfg["tm_mm"], 512)
    m0 = jnp.pad(jnp.broadcast_to(st["m"][:, :, None], (n, M_HEADS, 128)), ((0, 0), (0, 4), (0, 0)))
    mix_c, c_new, n_new, m_new = _mlstm_call(proj1, w["gate_b"], w["hn_g"], st["c"], st["n"], m0,
                                             N=n, L=l, **cfg["mlstm"])
    sh = st["shift"]
    if l == cfg["ra"]["CT"]:
        sh = jnp.repeat(sh, l, axis=0)
    ra, shift_new = _rwkv_a_call(proj1, sh[:, :3072], sh[:, 3072:], w["rwkv"], N=n, L=l, **cfg["ra"])
    mix_d, s_new = _rwkv_b_call(ra, proj1, w["r_ln_g"], w["r_ln_b"], st["s"].reshape(n, 1024, 64),
                                N=n, L=l, **cfg["rb"])
    final_ln = _out_ln_prompt if cfg["drop_meta"] else functools.partial(_out_ln, tm=cfg["tm_ln"])
    y = final_ln(x1, mix_c, mix_d, w["w_out1"], w["ln_g1"], w["ln_b1"])

    states = (conv_new[None],
              xr.reshape(n, S5_GROUPS, S5_STATE)[None],
              xi.reshape(n, S5_GROUPS, S5_STATE)[None],
              c_new[None], n_new[None], m_new[:, :M_HEADS, 0][None],
              s_new.reshape(n, R_HEADS, R_K, R_K)[None],
              shift_new[None])
    return y, states


def kernel(x_prompt, x_sample, state_conv, state_ssm_re, state_ssm_im, state_mlstm_c, state_mlstm_n, state_mlstm_m, state_rwkv_s, state_rwkv_shift, meta_tokens, ev_w_in, a_conv_w, a_conv_b, a_ln_g, a_ln_b, a_pw, s5_lambda_re, s5_lambda_im, s5_log_dt, s5_b_re, s5_b_im, s5_c_re, s5_c_im, s5_d, s5_glu_w, s5_glu_b, ev_w_out, ev_ln_g, ev_ln_b, od_w_in, m_ig_b, m_fg_b, m_hn_g, r_mu, r_w0, r_w2, r_a0, r_a2, r_kk, r_ka, r_rk, r_ln_g, r_ln_b, od_w_out, od_ln_g, od_ln_b):
    nb = x_prompt.shape[0]
    row = lambda vec: vec.reshape(1, -1)
    zeros = lambda *s: jnp.zeros(s, F32)

    ar, ai, wb, wk, wc = _s5_weights(s5_lambda_re[0], s5_lambda_im[0], s5_log_dt[0], s5_b_re[0], s5_b_im[0],
                                     s5_c_re[0], s5_c_im[0])
    w_in1 = _regroup_w1(od_w_in[0])
    mu = r_mu[0]
    w = dict(
        w_in0=ev_w_in[0].astype(BF16), conv_w=a_conv_w[0], conv_b=row(a_conv_b[0]),
        a_ln_g=row(a_ln_g[0]), a_ln_b=row(a_ln_b[0]), pw=a_pw[0].astype(BF16),
        s5_wb=wb, s5_wk=wk, s5_wc=wc, s5_d=row(s5_d[0]), s5_ar=ar, s5_ai=ai,
        glu_w=s5_glu_w[0].astype(BF16), glu_b=row(s5_glu_b[0]),
        w_out0=ev_w_out[0].astype(BF16), ln_g0=row(ev_ln_g[0]), ln_b0=row(ev_ln_b[0]),
        w_in1=w_in1,
        gate_b=jnp.concatenate([m_ig_b[0], m_fg_b[0], jnp.zeros((120,), F32)]).reshape(1, 128),
        hn_g=row(m_hn_g[0]),
        rwkv=[row(mu[:3072]), row(mu[3072:]), row(r_w0[0]),
              jnp.concatenate([r_w2[0], jnp.zeros((64, MIX_W), F32)], axis=0).astype(BF16),
              row(r_a0[0]),
              jnp.concatenate([jnp.zeros((64, MIX_W), F32), r_a2[0]], axis=0).astype(BF16),
              row(r_kk[0]), row(r_ka[0]), row(r_rk[0])],
        r_ln_g=row(r_ln_g[0]), r_ln_b=row(r_ln_b[0]),
        w_out1=od_w_out[0].astype(BF16), ln_g1=row(od_ln_g[0]), ln_b1=row(od_ln_b[0]),
    )

    x_p = jnp.concatenate([jnp.broadcast_to(meta_tokens[None], (nb, N_META, D_MODEL)), x_prompt],
                          axis=1).reshape(nb * P_LEN, D_MODEL)
    st_p = dict(conv=zeros(nb, CONV_W - 1, MIX_W), ssm_re=zeros(nb, 4096), ssm_im=zeros(nb, 4096),
                c=zeros(nb, M_HEADS, M_D, M_D), n=zeros(nb, M_HEADS, M_D), m=zeros(nb, M_HEADS),
                s=zeros(nb, R_HEADS, R_K, R_K), shift=zeros(nb, 3200))
    y_p, states_p = _trunk(x_p, st_p, w, CFG["P"])

    st_s = dict(conv=state_conv[0], ssm_re=state_ssm_re[0], ssm_im=state_ssm_im[0],
                c=state_mlstm_c[0], n=state_mlstm_n[0], m=state_mlstm_m[0],
                s=state_rwkv_s[0], shift=state_rwkv_shift[0])
    y_s, states_s = _trunk(x_sample.reshape(DEC_BATCH * DEC_SEQ, D_MODEL), st_s, w, CFG["S"])

    y_prompt = y_p
    y_sample = y_s.reshape(DEC_BATCH, DEC_SEQ, D_MODEL)
    return (y_prompt, y_sample) + states_p + states_s
```

```python
import functools
import math

import jax
import jax.numpy as jnp
from jax import lax
from jax.experimental import pallas as pl
from jax.experimental.pallas import tpu as pltpu

F32 = jnp.float32
BF16 = jnp.bfloat16

D_MODEL = 2048
MIX_W = 1024
N_META = 16
CONV_W = 31
S5_GROUP = 16
S5_GROUPS = 64
S5_STATE = 64
M_HEADS = 4
M_D = 256
R_HEADS = 16
R_K = 64
LN_EPS = 1e-5
R_LN_EPS = 64e-5
DEPTH = 2
ALPHA = (2 * DEPTH) ** 0.25

BATCH = 4
SEQ = 2048
P_LEN = N_META + SEQ
DEC_BATCH = 128
DEC_SEQ = 8

P1_Q, P1_K, P1_V, P1_O, P1_ZC = 0, 1024, 2048, 3072, 4096
P1_R, P1_RK, P1_RV, P1_ZD, P1_WA, P1_GATE = 5120, 6144, 7168, 8192, 9216, 9344
P1_N = 9728

VMEM_LIMIT = 48 * 1024 * 1024


def _cp(sem):
    return pltpu.CompilerParams(dimension_semantics=sem, vmem_limit_bytes=VMEM_LIMIT)


def _dot(a, b):
    return jnp.dot(a, b, preferred_element_type=F32)


def _dot_nt(a, b):
    return lax.dot_general(a, b, (((1,), (1,)), ((), ())), preferred_element_type=F32)


def _dot_tn(a, b):
    return lax.dot_general(a, b, (((0,), (0,)), ((), ())), preferred_element_type=F32)


def _split2(x):
    hi = x.astype(BF16)
    lo = (x - hi.astype(F32)).astype(BF16)
    return hi, lo


def _split3(x):
    hi = x.astype(BF16)
    r1 = x - hi.astype(F32)
    mid = r1.astype(BF16)
    lo = (r1 - mid.astype(F32)).astype(BF16)
    return hi, mid, lo


def _sigmoid(x):
    return jax.nn.sigmoid(x)


def _silu(x):
    return x * jax.nn.sigmoid(x)


def _softplus(x):
    return jnp.maximum(x, 0.0) + jnp.log(1.0 + jnp.exp(-jnp.abs(x)))


def _gelu_tanh(x):
    c = math.sqrt(2.0 / math.pi)
    return x * (0.5 * (1.0 + jnp.tanh(c * (x + 0.044715 * (x * x * x)))))


def _iota(shape, axis):
    return lax.broadcasted_iota(jnp.int32, shape, axis)


def _shr(x, k):
    return lax.shift_right_logical(x, jnp.int32(k))


def _log2(n):
    k = int(round(math.log2(n)))
    assert 1 << k == n
    return k


def _block_ones(n, seg, dtype):
    r = _shr(_iota((n, n), 0), _log2(seg))
    c = _shr(_iota((n, n), 1), _log2(seg))
    return (r == c).astype(dtype)


def _segsum(x, seg):
    g = _block_ones(256, seg, BF16)
    outs = []
    for j in range(x.shape[1] // 256):
        hi, lo = _split2(x[:, 256 * j:256 * (j + 1)])
        outs.append(_dot(hi, g) + _dot(lo, g))
    return jnp.concatenate(outs, axis=1)


def _row_cumsum(x, period):
    rows = x.shape[0]
    rid = _iota(x.shape, 0)
    if period < rows:
        rid = jnp.bitwise_and(rid, period - 1)
    d = 1
    while d < min(period, rows):
        x = x + jnp.where(rid >= d, pltpu.roll(x, d, 0), 0.0)
        d *= 2
    return x


def _row_cummax(x, period):
    rows = x.shape[0]
    rid = _iota(x.shape, 0)
    if period < rows:
        rid = jnp.bitwise_and(rid, period - 1)
    d = 1
    while d < min(period, rows):
        x = jnp.maximum(x, jnp.where(rid >= d, pltpu.roll(x, d, 0), -jnp.inf))
        d *= 2
    return x


def _mm_kernel(x_ref, w_ref, o_ref):
    o_ref[...] = _dot(x_ref[...].astype(BF16), w_ref[...])


def _matmul(x, w, tm, tn):
    r, k = x.shape
    n = w.shape[1]
    return pl.pallas_call(
        _mm_kernel,
        out_shape=jax.ShapeDtypeStruct((r, n), F32),
        grid=(pl.cdiv(r, tm), n // tn),
        in_specs=[pl.BlockSpec((tm, k), lambda i, j: (i, 0)),
                  pl.BlockSpec((k, tn), lambda i, j: (0, j))],
        out_specs=pl.BlockSpec((tm, tn), lambda i, j: (i, j)),
        compiler_params=_cp(("parallel", "arbitrary")),
    )(x, w)


def _mm_nt_kernel(x_ref, w_ref, o_ref):
    o_ref[...] = _dot_nt(x_ref[...].astype(BF16), w_ref[...])


def _matmul_nt(x, w_t, tm, tn):
    r, k = x.shape
    n = w_t.shape[0]
    return pl.pallas_call(
        _mm_nt_kernel,
        out_shape=jax.ShapeDtypeStruct((r, n), F32),
        grid=(pl.cdiv(r, tm), n // tn),
        in_specs=[pl.BlockSpec((tm, k), lambda i, j: (i, 0)),
                  pl.BlockSpec((tn, k), lambda i, j: (j, 0))],
        out_specs=pl.BlockSpec((tm, tn), lambda i, j: (i, j)),
        compiler_params=_cp(("parallel", "arbitrary")),
    )(x, w_t)


def _pw_kernel(a_ref, w_ref, z_ref, o_ref):
    o_ref[...] = _dot(a_ref[...].astype(BF16), w_ref[...]) * _silu(z_ref[...])


def _pw_gate(act, pw, proj0, tm, tn=512):
    r = act.shape[0]
    zb = 2048 // tn
    return pl.pallas_call(
        _pw_kernel,
        out_shape=jax.ShapeDtypeStruct((r, MIX_W), F32),
        grid=(pl.cdiv(r, tm), MIX_W // tn),
        in_specs=[pl.BlockSpec((tm, MIX_W), lambda i, j: (i, 0)),
                  pl.BlockSpec((MIX_W, tn), lambda i, j: (0, j)),
                  pl.BlockSpec((tm, tn), lambda i, j: (i, zb + j))],
        out_specs=pl.BlockSpec((tm, tn), lambda i, j: (i, j)),
        compiler_params=_cp(("parallel", "arbitrary")),
    )(act, pw, proj0)


def _glu_kernel(y_ref, wv_ref, wg_ref, bv_ref, bg_ref, z_ref, o_ref):
    y = y_ref[...].astype(BF16)
    v = _dot(y, wv_ref[...]) + bv_ref[...]
    g = _dot(y, wg_ref[...]) + bg_ref[...]
    o_ref[...] = v * _sigmoid(g) * _silu(z_ref[...])


def _glu_gate(yb, glu_w, glu_b, proj0, tm, tn=512):
    r = yb.shape[0]
    nb = MIX_W // tn
    zb = 4096 // tn
    return pl.pallas_call(
        _glu_kernel,
        out_shape=jax.ShapeDtypeStruct((r, MIX_W), F32),
        grid=(pl.cdiv(r, tm), nb),
        in_specs=[pl.BlockSpec((tm, MIX_W), lambda i, j: (i, 0)),
                  pl.BlockSpec((MIX_W, tn), lambda i, j: (0, j)),
                  pl.BlockSpec((MIX_W, tn), lambda i, j: (0, nb + j)),
                  pl.BlockSpec((1, tn), lambda i, j: (0, j)),
                  pl.BlockSpec((1, tn), lambda i, j: (0, nb + j)),
                  pl.BlockSpec((tm, tn), lambda i, j: (i, zb + j))],
        out_specs=pl.BlockSpec((tm, tn), lambda i, j: (i, j)),
        compiler_params=_cp(("parallel", "arbitrary")),
    )(yb, glu_w, glu_w, glu_b, glu_b, proj0)


def _out_ln_kernel(x_ref, ma_ref, mb_ref, wa_ref, wb_ref, g_ref, b_ref, o_ref):
    out = _dot(ma_ref[...].astype(BF16), wa_ref[...]) + _dot(mb_ref[...].astype(BF16), wb_ref[...])
    y = ALPHA * x_ref[...] + out
    mu = jnp.mean(y, axis=-1, keepdims=True)
    yc = y - mu
    var = jnp.mean(yc * yc, axis=-1, keepdims=True)
    o_ref[...] = yc * lax.rsqrt(var + LN_EPS) * g_ref[...] + b_ref[...]


def _out_ln(x, mix_a, mix_b, w_out, ln_g, ln_b, tm):
    r = x.shape[0]
    return pl.pallas_call(
        _out_ln_kernel,
        out_shape=jax.ShapeDtypeStruct((r, D_MODEL), F32),
        grid=(pl.cdiv(r, tm),),
        in_specs=[pl.BlockSpec((tm, D_MODEL), lambda i: (i, 0)),
                  pl.BlockSpec((tm, MIX_W), lambda i: (i, 0)),
                  pl.BlockSpec((tm, MIX_W), lambda i: (i, 0)),
                  pl.BlockSpec((MIX_W, D_MODEL), lambda i: (0, 0), pipeline_mode=pl.Buffered(1)),
                  pl.BlockSpec((MIX_W, D_MODEL), lambda i: (1, 0), pipeline_mode=pl.Buffered(1)),
                  pl.BlockSpec((1, D_MODEL), lambda i: (0, 0)),
                  pl.BlockSpec((1, D_MODEL), lambda i: (0, 0))],
        out_specs=pl.BlockSpec((tm, D_MODEL), lambda i: (i, 0)),
        compiler_params=_cp(("parallel",)),
    )(x, mix_a, mix_b, w_out, w_out, ln_g, ln_b)


def _out_ln_prompt(x, mix_a, mix_b, w_out, ln_g, ln_b, tm=512):
    tiles = SEQ // tm

    def rows(width):
        return pl.BlockSpec((pl.Element(tm), pl.Element(width)),
                            lambda n, t: (pl.multiple_of(n * P_LEN + N_META + t * tm, 8), 0))

    return pl.pallas_call(
        _out_ln_kernel,
        out_shape=jax.ShapeDtypeStruct((BATCH * SEQ, D_MODEL), F32),
        grid=(BATCH, tiles),
        in_specs=[rows(D_MODEL), rows(MIX_W), rows(MIX_W),
                  pl.BlockSpec((MIX_W, D_MODEL), lambda n, t: (0, 0), pipeline_mode=pl.Buffered(1)),
                  pl.BlockSpec((MIX_W, D_MODEL), lambda n, t: (1, 0), pipeline_mode=pl.Buffered(1)),
                  pl.BlockSpec((1, D_MODEL), lambda n, t: (0, 0)),
                  pl.BlockSpec((1, D_MODEL), lambda n, t: (0, 0))],
        out_specs=pl.BlockSpec((tm, D_MODEL), lambda n, t: (n * tiles + t, 0)),
        compiler_params=_cp(("parallel", "arbitrary")),
    )(x, mix_a, mix_b, w_out, w_out, ln_g, ln_b).reshape(BATCH, SEQ, D_MODEL)


def _conv_kernel(u_ref, g_ref, st_ref, w_ref, cb_ref, lg_ref, lb_ref, act_ref, nst_ref, hp_ref, hs_ref, wb_ref,
                 *, NB, TL, T):
    t = pl.program_id(1)
    for j in range(CONV_W):
        wb_ref[j] = jnp.broadcast_to(w_ref[j:j + 1, :], (8, MIX_W))
    for nb in range(NB):
        base = nb * TL

        @pl.when(t == 0)
        def _():
            hp_ref[nb, 0:2, :] = jnp.zeros((2, MIX_W), F32)
            hp_ref[nb, 2:32, :] = st_ref[nb]

        hp_ref[nb, TL + 32:TL + 40, :] = jnp.zeros((8, MIX_W), F32)
        hp_ref[nb, 32:32 + TL, :] = u_ref[base:base + TL, :] * _sigmoid(g_ref[base:base + TL, :])
        for b in range(8):
            hs_ref[b] = hp_ref[nb, b:b + TL + 32, :]

        def taps(r0, groups):
            acc = [None] * groups
            for j in range(CONV_W):
                o = j + 2
                wj = wb_ref[j]
                for g in range(groups):
                    term = wj * hs_ref[o % 8, pl.ds(r0 + 8 * (o // 8 + g), 8), :]
                    acc[g] = term if acc[g] is None else acc[g] + term
            for g in range(groups):
                act_ref[pl.ds(base + r0 + 8 * g, 8), :] = acc[g]

        def chunk(c, carry):
            taps(pl.multiple_of(c * 16, 16), 2)
            return carry

        lax.fori_loop(0, TL // 16, chunk, 0)
        if TL % 16:
            taps(TL - 8, 1)

        @pl.when(t == T - 1)
        def _():
            nst_ref[nb] = hp_ref[nb, TL + 2:TL + 32, :]

        if T > 1:
            hp_ref[nb, 0:32, :] = hp_ref[nb, TL:TL + 32, :]

    y = act_ref[...] + cb_ref[...]
    mu = jnp.mean(y, axis=-1, keepdims=True)
    yc = y - mu
    var = jnp.mean(yc * yc, axis=-1, keepdims=True)
    act_ref[...] = _silu(yc * lax.rsqrt(var + LN_EPS) * lg_ref[...] + lb_ref[...])


def _conv_call(proj0, state, conv_w, conv_b, ln_g, ln_b, *, N, L, NB, TL):
    T = L // TL
    RB = NB * TL
    assert NB == 1 or T == 1
    return pl.pallas_call(
        functools.partial(_conv_kernel, NB=NB, TL=TL, T=T),
        out_shape=(jax.ShapeDtypeStruct((N * L, MIX_W), F32),
                   jax.ShapeDtypeStruct((N, CONV_W - 1, MIX_W), F32)),
        grid=(N // NB, T),
        in_specs=[pl.BlockSpec((RB, MIX_W), lambda i, t: (i * T + t, 0)),
                  pl.BlockSpec((RB, MIX_W), lambda i, t: (i * T + t, 1)),
                  pl.BlockSpec((NB, CONV_W - 1, MIX_W), lambda i, t: (i, 0, 0)),
                  pl.BlockSpec((CONV_W, MIX_W), lambda i, t: (0, 0)),
                  pl.BlockSpec((1, MIX_W), lambda i, t: (0, 0)),
                  pl.BlockSpec((1, MIX_W), lambda i, t: (0, 0)),
                  pl.BlockSpec((1, MIX_W), lambda i, t: (0, 0))],
        out_specs=(pl.BlockSpec((RB, MIX_W), lambda i, t: (i * T + t, 0)),
                   pl.BlockSpec((NB, CONV_W - 1, MIX_W), lambda i, t: (i, 0, 0))),
        scratch_shapes=[pltpu.VMEM((NB, TL + 40, MIX_W), F32),
                        pltpu.VMEM((8, TL + 32, MIX_W), F32),
                        pltpu.VMEM((CONV_W, 8, MIX_W), F32)],
        compiler_params=_cp(("arbitrary", "arbitrary")),
    )(proj0, proj0, state, conv_w, conv_b, ln_g, ln_b)


def _s5_kernel(u_ref, wb_ref, wk_ref, wc_ref, d_ref, ar_ref, ai_ref, x0r_ref, x0i_ref,
               y_ref, xfr_ref, xfi_ref, xs_ref, cr_ref, ci_ref, *, NB, TL, T):
    t = pl.program_id(2)
    RB = NB * TL
    GL = TL // 8
    u = u_ref[...]
    ub = u.astype(BF16)
    big = _dot(ub, wb_ref[0])
    xs_ref[0] = big[:, :1024]
    xs_ref[1] = big[:, 1024:]
    rid = jnp.bitwise_and(_iota((RB, 256), 0), 7)
    lags = [ub] + [jnp.where(rid >= d, pltpu.roll(u, d, 0), 0.0).astype(BF16) for d in range(1, 8)]
    y_loc = _dot(jnp.concatenate(lags, axis=1), wk_ref[0])
    ar = ar_ref[...]
    ai = ai_ref[...]

    def cmul(pr, pi, qr, qi):
        return pr * qr - pi * qi, pr * qi + pi * qr

    a1 = (ar, ai)
    a2 = cmul(*a1, *a1)
    a4 = cmul(*a2, *a2)
    a3 = cmul(*a2, *a1)
    a5 = cmul(*a4, *a1)
    a6 = cmul(*a4, *a2)
    a7 = cmul(*a6, *a1)
    a8 = cmul(*a4, *a4)
    a0 = (jnp.ones_like(ar), jnp.zeros_like(ai))
    r8 = _iota((8, 1024), 0)

    def table(powers):
        tr = jnp.zeros((8, 1024), F32)
        ti = jnp.zeros((8, 1024), F32)
        for k, (pr, pi) in enumerate(powers):
            tr = jnp.where(r8 == k, pr, tr)
            ti = jnp.where(r8 == k, pi, ti)
        return tr, ti

    pwr, pwi = table((a1, a2, a3, a4, a5, a6, a7, a8))
    qwr, qwi = table((a7, a6, a5, a4, a3, a2, a1, a0))
    a8r, a8i = a8

    first = t == 0

    def seq_body(nb, carry0):
        x0r = x0r_ref[nb]
        x0i = x0i_ref[nb]
        if T > 1:
            c_r = jnp.where(first, x0r, cr_ref[0:1, :])
            c_i = jnp.where(first, x0i, ci_ref[0:1, :])
        else:
            c_r, c_i = x0r, x0i

        def grp(g, c):
            c_r, c_i = c
            off = pl.multiple_of(nb * TL + g * 8, 8)
            vr = xs_ref[0, pl.ds(off, 8), :]
            vi = xs_ref[1, pl.ds(off, 8), :]
            er = jnp.sum(qwr * vr - qwi * vi, axis=0, keepdims=True)
            ei = jnp.sum(qwr * vi + qwi * vr, axis=0, keepdims=True)
            br = jnp.broadcast_to(c_r, (8, 1024))
            bi = jnp.broadcast_to(c_i, (8, 1024))
            xs_ref[0, pl.ds(off, 8), :] = pwr * br - pwi * bi
            xs_ref[1, pl.ds(off, 8), :] = pwr * bi + pwi * br
            return a8r * c_r - a8i * c_i + er, a8r * c_i + a8i * c_r + ei

        c_r, c_i = lax.fori_loop(0, GL, grp, (c_r, c_i))
        if T > 1:
            cr_ref[...] = jnp.broadcast_to(c_r, (8, 1024))
            ci_ref[...] = jnp.broadcast_to(c_i, (8, 1024))

        @pl.when(t == T - 1)
        def _():
            xfr_ref[nb] = c_r
            xfi_ref[nb] = c_i

        return carry0

    lax.fori_loop(0, NB, seq_body, 0)
    y = (_dot(xs_ref[0].astype(BF16), wc_ref[0, 0:1024, :])
         + _dot(xs_ref[1].astype(BF16), wc_ref[0, 1024:2048, :]))
    y_ref[...] = _gelu_tanh(y + y_loc + d_ref[...] * u)


def _s5_call(proj0, wb, wk, wc, dvec, ar, ai, x0r, x0i, *, N, L, NB, TL):
    T = L // TL
    RB = NB * TL
    assert NB == 1 or T == 1
    ub = 3072 // 256
    st = jax.ShapeDtypeStruct((N, 1, 4096), F32)
    return pl.pallas_call(
        functools.partial(_s5_kernel, NB=NB, TL=TL, T=T),
        out_shape=(jax.ShapeDtypeStruct((N * L, MIX_W), F32), st, st),
        grid=(N // NB, 4, T),
        in_specs=[pl.BlockSpec((RB, 256), lambda i, j, t: (i * T + t, ub + j)),
                  pl.BlockSpec((1, 256, 2048), lambda i, j, t: (j, 0, 0)),
                  pl.BlockSpec((1, 2048, 256), lambda i, j, t: (j, 0, 0)),
                  pl.BlockSpec((1, 2048, 256), lambda i, j, t: (j, 0, 0)),
                  pl.BlockSpec((1, 256), lambda i, j, t: (0, j)),
                  pl.BlockSpec((1, 1024), lambda i, j, t: (0, j)),
                  pl.BlockSpec((1, 1024), lambda i, j, t: (0, j)),
                  pl.BlockSpec((NB, 1, 1024), lambda i, j, t: (i, 0, j)),
                  pl.BlockSpec((NB, 1, 1024), lambda i, j, t: (i, 0, j))],
        out_specs=(pl.BlockSpec((RB, 256), lambda i, j, t: (i * T + t, j)),
                   pl.BlockSpec((NB, 1, 1024), lambda i, j, t: (i, 0, j)),
                   pl.BlockSpec((NB, 1, 1024), lambda i, j, t: (i, 0, j))),
        scratch_shapes=[pltpu.VMEM((2, RB, 1024), F32),
                        pltpu.VMEM((8, 1024), F32),
                        pltpu.VMEM((8, 1024), F32)],
        compiler_params=_cp(("arbitrary", "arbitrary", "arbitrary")),
    )(proj0, wb, wk, wc, dvec, ar, ai, x0r, x0i)


def _mlstm_kernel(q_ref, k_ref, v_ref, o_ref, z_ref, gt_ref, gb_ref, hg_ref, c0_ref, n0_ref, m0_ref,
                  y_ref, c_ref, n_ref, m_ref, cs_ref, ns_ref, ms_ref, *, NB, TL, T):
    for nb in range(NB):
        rows = lambda ref: ref.at[pl.ds(nb * TL, TL)]
        one = lambda ref: ref.at[pl.ds(nb, 1)]
        _mlstm_seq(rows(q_ref), rows(k_ref), rows(v_ref), rows(o_ref), rows(z_ref), rows(gt_ref), gb_ref, hg_ref,
                   one(c0_ref), one(n0_ref), one(m0_ref), rows(y_ref), one(c_ref), one(n_ref), one(m_ref),
                   cs_ref.at[nb], ns_ref.at[nb], ms_ref.at[nb], TL=TL, T=T)


def _mlstm_seq(q_ref, k_ref, v_ref, o_ref, z_ref, gt_ref, gb_ref, hg_ref, c0_ref, n0_ref, m0_ref,
               y_ref, c_ref, n_ref, m_ref, cs_ref, ns_ref, ms_ref, *, TL, T):
    t = pl.program_id(1)

    @pl.when(t == 0)
    def _():
        cs_ref[...] = c0_ref[0]
        ns_ref[...] = n0_ref[0]
        ms_ref[...] = m0_ref[0]

    G = gt_ref[...] + gb_ref[...]
    B = _row_cumsum(-_softplus(-G), TL)
    Bs = pltpu.roll(B, 124, 1)
    A = G - Bs
    CM = _row_cummax(A, TL)
    ms = ms_ref[...]
    dg = _iota((8, 128), 0) == _iota((8, 128), 1)
    mrow = jnp.sum(jnp.where(dg, ms, 0.0), axis=0, keepdims=True)
    M = jnp.maximum(mrow, CM)
    MT = Bs + M
    sel = dg.astype(BF16)
    a_hi, a_mid, a_lo = _split3(A)
    Arow = _dot_nt(sel, a_hi) + _dot_nt(sel, a_mid) + _dot_nt(sel, a_lo)
    causal = _iota((TL, TL), 0) >= _iota((TL, TL), 1)
    H = range(M_HEADS)
    sl = [slice(M_D * h, M_D * (h + 1)) for h in H]
    q = [q_ref[:, sl[h]] * (M_D ** -0.5) for h in H]
    qb = [x.astype(BF16) for x in q]
    kf = [k_ref[:, sl[h]] for h in H]
    kb = [x.astype(BF16) for x in kf]
    vf = [v_ref[:, sl[h]] for h in H]
    c_old = [cs_ref[h] for h in H]
    n_old = [ns_ref[h:h + 1, :] for h in H]
    m_col = [M[:, h:h + 1] for h in H]
    mt_col = [MT[:, h:h + 1] for h in H]
    b_col = [Bs[:, h:h + 1] for h in H]
    m_prev = [mrow[:, h:h + 1] for h in H]
    dm = [jnp.exp(jnp.where(causal, Arow[h:h + 1, :] - m_col[h], -jnp.inf)) for h in H]
    s = [_dot_nt(qb[h], kb[h]) * dm[h] for h in H]
    inter = [jnp.exp(m_prev[h] - m_col[h]) for h in H]
    h_intra = [_dot(s[h].astype(BF16), vf[h].astype(BF16)) for h in H]
    h_inter = [_dot(qb[h], c_old[h].astype(BF16)) * inter[h] for h in H]
    n_all = [jnp.sum(s[h], axis=1, keepdims=True) + jnp.sum(q[h] * n_old[h], axis=1, keepdims=True) * inter[h]
             for h in H]
    hh = [(h_intra[h] + h_inter[h]) / jnp.maximum(jnp.abs(n_all[h]), jnp.exp(-mt_col[h])) for h in H]
    m_new = [mt_col[h][TL - 1:TL, :] for h in H]
    b_end = [b_col[h][TL - 1:TL, :] for h in H]
    dec = [jnp.exp(m_prev[h] + b_end[h] - m_new[h]) for h in H]
    w_s = [jnp.exp(b_end[h] - b_col[h] + G[:, h:h + 1] - m_new[h]) for h in H]
    c_new = [c_old[h] * dec[h] + _dot_tn(kb[h], (vf[h] * w_s[h]).astype(BF16)) for h in H]
    n_new = [n_old[h] * dec[h] + jnp.sum(kf[h] * w_s[h], axis=0, keepdims=True) for h in H]
    outs = []
    for h in H:
        mu = jnp.mean(hh[h], axis=-1, keepdims=True)
        hc = hh[h] - mu
        var = jnp.mean(hc * hc, axis=-1, keepdims=True)
        hn = hc * lax.rsqrt(var + LN_EPS) * hg_ref[:, sl[h]]
        outs.append(hn * _sigmoid(o_ref[:, sl[h]]) * _silu(z_ref[:, sl[h]]))
    for h in H:
        cs_ref[h] = c_new[h]
        ns_ref[h:h + 1, :] = n_new[h]
        ms_ref[h:h + 1, :] = jnp.broadcast_to(m_new[h], (1, 128))
        y_ref[:, sl[h]] = outs[h]

    @pl.when(t == T - 1)
    def _():
        c_ref[0] = cs_ref[...]
        n_ref[0] = ns_ref[...]
        m_ref[0] = ms_ref[...]


def _mlstm_call(proj1, gate_b, hn_g, c0, n0, m0, *, N, L, NB, TL):
    T = L // TL
    RB = NB * TL
    assert NB == 1 or T == 1

    def col(cb):
        return pl.BlockSpec((RB, MIX_W), lambda i, t: (i * T + t, cb))

    st_specs = (pl.BlockSpec((NB, M_HEADS, M_D, M_D), lambda i, t: (i, 0, 0, 0)),
                pl.BlockSpec((NB, M_HEADS, M_D), lambda i, t: (i, 0, 0)),
                pl.BlockSpec((NB, 8, 128), lambda i, t: (i, 0, 0)))
    return pl.pallas_call(
        functools.partial(_mlstm_kernel, NB=NB, TL=TL, T=T),
        out_shape=(jax.ShapeDtypeStruct((N * L, MIX_W), F32),
                   jax.ShapeDtypeStruct((N, M_HEADS, M_D, M_D), F32),
                   jax.ShapeDtypeStruct((N, M_HEADS, M_D), F32),
                   jax.ShapeDtypeStruct((N, 8, 128), F32)),
        grid=(N // NB, T),
        in_specs=[col(0), col(1), col(2), col(3), col(4),
                  pl.BlockSpec((RB, 128), lambda i, t: (i * T + t, P1_GATE // 128)),
                  pl.BlockSpec((1, 128), lambda i, t: (0, 0)),
                  pl.BlockSpec((1, MIX_W), lambda i, t: (0, 0))] + list(st_specs),
        out_specs=(pl.BlockSpec((RB, MIX_W), lambda i, t: (i * T + t, 0)),) + st_specs,
        scratch_shapes=[pltpu.VMEM((NB, M_HEADS, M_D, M_D), F32),
                        pltpu.VMEM((NB, M_HEADS, M_D), F32),
                        pltpu.VMEM((NB, 8, 128), F32)],
        compiler_params=_cp(("arbitrary", "arbitrary")),
    )(proj1, proj1, proj1, proj1, proj1, proj1, gate_b, hn_g, c0, n0, m0)


N_RA_OUT = 9


def _rwkv_a_kernel(pr_ref, pk_ref, pv_ref, pwa_ref, hr_ref, hk_ref, hv_ref, hwa_ref, st_ref, stwa_ref,
                   mu_ref, muwa_ref, w0_ref, w2_ref, a0_ref, a2_ref, kkp_ref, ka_ref, rk_ref,
                   ah_ref, rh_ref, bh_ref, kh_ref, vo_ref, ul_ref, yl_ref, dc_ref, bo_ref, shr_ref, shw_ref,
                   s_at, s_rt, s_bt, s_kt, s_v, s_cum, *, RB, CT, N, L, U):
    HS = R_HEADS * CT
    HG = 128 // CT
    NG = R_HEADS // HG
    GW = HG * R_K
    NCH = RB // CT
    short = L == CT
    rid = _iota((RB, 1), 0)
    grow = pl.program_id(0) * RB + rid

    def shifted(p_ref, h_ref, s_ref, lo, hi, mu):
        p = p_ref[...]
        prev = pltpu.roll(p, 1, 0)
        if short:
            prev = jnp.where(jnp.bitwise_and(rid, CT - 1) == 0, s_ref[:, lo:hi], prev)
        else:
            prev = jnp.where(rid == 0, h_ref[7:8, :], prev)
            for n in range(N):
                prev = jnp.where(grow == n * L, s_ref[n:n + 1, lo:hi], prev)
        return p + (prev - p) * mu

    raw = ((pr_ref, shr_ref, 0), (pk_ref, shr_ref, 1024), (pv_ref, shr_ref, 2048), (pwa_ref, shw_ref, 0))
    if short:
        nseq = RB // CT
        sel = (_iota((nseq, RB), 1) == _iota((nseq, RB), 0) * CT + (CT - 1)).astype(BF16)
        for src, dst, lo in raw:
            hi, mid, low = _split3(src[...])
            dst[:, lo:lo + src.shape[1]] = _dot(sel, hi) + _dot(sel, mid) + _dot(sel, low)
    else:
        for n in range(N):
            tile, off = divmod(n * L + L - 1, RB)

            @pl.when(pl.program_id(0) == tile)
            def _():
                for src, dst, lo in raw:
                    dst[n:n + 1, lo:lo + src.shape[1]] = src[off:off + 1, :]

    r = shifted(pr_ref, hr_ref, st_ref, 0, 1024, mu_ref[:, 0:1024])
    k = shifted(pk_ref, hk_ref, st_ref, 1024, 2048, mu_ref[:, 1024:2048])
    v = shifted(pv_ref, hv_ref, st_ref, 2048, 3072, mu_ref[:, 2048:3072])
    wa = shifted(pwa_ref, hwa_ref, stwa_ref, 0, 128, muwa_ref[...])
    w = -_softplus(-(w0_ref[...] + _dot(jnp.tanh(wa).astype(BF16), w2_ref[...]))) - 0.5
    wlog = -jnp.exp(w)
    a = _sigmoid(a0_ref[...] + _dot(wa.astype(BF16), a2_ref[...]))
    kk = k * kkp_ref[...]
    kk = kk / jnp.maximum(jnp.sqrt(_segsum(kk * kk, R_K)), 1e-12)
    kmod = k * (1.0 + (a - 1.0) * ka_ref[...])
    bo_ref[...] = _segsum(r * kmod * rk_ref[...], R_K) * v
    cum = _row_cumsum(wlog, CT)
    einv = jnp.exp(-cum)
    s_at[...] = (-kk) * jnp.exp(cum - wlog)
    s_rt[...] = r * jnp.exp(cum)
    s_bt[...] = kk * a * einv
    s_kt[...] = kmod * einv
    s_v[...] = v
    s_cum[...] = cum

    mdt = BF16 if CT % 16 == 0 else F32
    be_mask = (_shr(_iota((128, GW), 0), _log2(CT)) == _shr(_iota((128, GW), 1), _log2(R_K))).astype(mdt)
    bd_mask = (_shr(_iota((HS, HS), 0), _log2(CT)) == _shr(_iota((HS, HS), 1), _log2(CT))).astype(mdt)
    tt = _iota((CT, HS), 0)
    ss = jnp.bitwise_and(_iota((CT, HS), 1), CT - 1)
    strict = tt > ss
    incl = tt >= ss
    eye_c = (tt == ss).astype(F32)
    cat0 = lambda *xs: jnp.concatenate(xs, axis=0)

    def blockexp(x):
        return [(jnp.concatenate([x[:, GW * g:GW * (g + 1)].astype(mdt)] * HG, axis=0) * be_mask).astype(BF16)
                for g in range(NG)]

    def gram(lhs, be):
        lb = lhs.astype(BF16)
        return jnp.concatenate([_dot_nt(lb[:, GW * g:GW * (g + 1)], be[g]) for g in range(NG)], axis=1)

    def apply(cmp, be):
        cb = cmp.astype(BF16)
        return jnp.concatenate([_dot(cb[:, 128 * g:128 * (g + 1)], be[g]) for g in range(NG)], axis=1)

    def bdiag(x):
        return (jnp.concatenate([x.astype(mdt)] * R_HEADS, axis=0) * bd_mask).astype(BF16)

    def bdiag_hl(x):
        if mdt == BF16:
            return tuple(bdiag(part) for part in _split2(x))
        return _split2(jnp.concatenate([x] * R_HEADS, axis=0) * bd_mask)

    def mm_hl(stack, wh, wl):
        sh, sl = _split2(stack)
        n = stack.shape[0]
        full = _dot(cat0(sh, sl), wh)
        return full[:n] + full[n:] + _dot(sh, wl)

    def chunks(i, carry):
        rows = [pl.ds(pl.multiple_of((i * U + u) * CT, CT), CT) for u in range(U)]
        ld = lambda ref: [ref[rw, :] for rw in rows]
        at, rt, bt, kt, vv, cm = ld(s_at), ld(s_rt), ld(s_bt), ld(s_kt), ld(s_v), ld(s_cum)
        each = lambda f, *xs: [f(*a_) for a_ in zip(*xs)]
        ar_ = each(cat0, at, rt)
        gb = each(lambda l_, y_: gram(l_, blockexp(y_)), ar_, bt)
        gk = each(lambda l_, y_: gram(l_, blockexp(y_)), ar_, kt)
        a_ab = each(lambda m: jnp.where(strict, m[:CT], 0.0), gb)
        a_rb = each(lambda m: jnp.where(incl, m[CT:], 0.0), gb)
        a_ak = each(lambda m: jnp.where(strict, m[:CT], 0.0), gk)
        a_rk = each(lambda m: jnp.where(incl, m[CT:], 0.0), gk)
        p = each(lambda m: eye_c + m, a_ab)
        x = a_ab
        q = a_rb
        w_hl = each(bdiag_hl, x)
        res = each(lambda x_, q_, w_: mm_hl(cat0(x_, q_), *w_), x, q, w_hl)
        x = each(lambda r_: r_[:CT], res)
        q = each(lambda q_, r_: q_ + r_[CT:], q, res)
        pw = 2
        while pw < CT:
            w_hl = each(bdiag_hl, x)
            if 2 * pw >= CT:
                res = each(lambda p_, q_, w_: mm_hl(cat0(p_, q_), *w_), p, q, w_hl)
                q = each(lambda q_, r_: q_ + r_[CT:], q, res)
            else:
                res = each(lambda p_, x_, q_, w_: mm_hl(cat0(p_, x_, q_), *w_), p, x, q, w_hl)
                x = each(lambda r_: r_[CT:2 * CT], res)
                q = each(lambda q_, r_: q_ + r_[2 * CT:], q, res)
            p = each(lambda p_, r_: p_ + r_[:CT], p, res)
            pw *= 2
        tq = each(cat0, p, q)
        res = each(lambda m, k_: _dot(m.astype(BF16), bdiag(k_)), tq, a_ak)
        ty = each(lambda r_, k_: cat0(r_[:CT], r_[CT:] + k_), res, a_rk)
        o1 = each(lambda m, y_: apply(m, blockexp(y_)), tq, at)
        o2 = each(lambda m, y_: apply(m, blockexp(y_)), ty, vv)
        ect = each(lambda c_: jnp.exp(c_[CT - 1:CT, :]), cm)
        for u, rw in enumerate(rows):
            ah_ref[rw, :] = o1[u][:CT]
            rh_ref[rw, :] = rt[u] + o1[u][CT:]
            ul_ref[rw, :] = o2[u][:CT]
            yl_ref[rw, :] = o2[u][CT:]
            bh_ref[rw, :] = bt[u] * ect[u]
            kh_ref[rw, :] = kt[u] * ect[u]
            vo_ref[rw, :] = vv[u]
            dc_ref[rw, :] = jnp.broadcast_to(ect[u], (CT, 1024))
        return carry

    lax.fori_loop(0, NCH // U, chunks, 0)


def _rwkv_a_call(proj1, st_rkv, st_wa, wts, *, N, L, RB, CT, U):
    short = L == CT
    rows = N * L
    assert rows % RB == 0 and (RB // CT) % U == 0

    def col(cb, width=MIX_W):
        return pl.BlockSpec((RB, width), lambda i: (i, cb))

    def halo(cb, width=MIX_W):
        return pl.BlockSpec((8, width), lambda i: (jnp.maximum(i * (RB // 8) - 1, 0), cb))

    if short:
        st_specs = [pl.BlockSpec((RB, 3072), lambda i: (i, 0)), pl.BlockSpec((RB, 128), lambda i: (i, 0))]
    else:
        st_specs = [pl.BlockSpec((N, 3072), lambda i: (0, 0)), pl.BlockSpec((N, 128), lambda i: (0, 0))]

    def full(shape):
        return pl.BlockSpec(shape, lambda i: (0,) * len(shape))

    o_spec = pl.BlockSpec((RB, MIX_W), lambda i: (i, 0))
    if short:
        sh_specs = (pl.BlockSpec((RB // CT, 3072), lambda i: (i, 0)), pl.BlockSpec((RB // CT, 128), lambda i: (i, 0)))
    else:
        sh_specs = (pl.BlockSpec((N, 3072), lambda i: (0, 0)), pl.BlockSpec((N, 128), lambda i: (0, 0)))
    outs = pl.pallas_call(
        functools.partial(_rwkv_a_kernel, RB=RB, CT=CT, N=N, L=L, U=U),
        out_shape=((jax.ShapeDtypeStruct((rows, MIX_W), F32),) * N_RA_OUT
                   + (jax.ShapeDtypeStruct((N, 3072), F32), jax.ShapeDtypeStruct((N, 128), F32))),
        grid=(rows // RB,),
        in_specs=([col(P1_R // 1024), col(P1_RK // 1024), col(P1_RV // 1024), col(P1_WA // 128, 128),
                   halo(P1_R // 1024), halo(P1_RK // 1024), halo(P1_RV // 1024), halo(P1_WA // 128, 128)]
                  + st_specs
                  + [full((1, 3072)), full((1, 128)), full((1, 1024)), full((128, 1024)), full((1, 1024)),
                     full((128, 1024)), full((1, 1024)), full((1, 1024)), full((1, 1024))]),
        out_specs=(o_spec,) * N_RA_OUT + sh_specs,
        scratch_shapes=[pltpu.VMEM((RB, MIX_W), F32)] * 6,
        compiler_params=_cp(("arbitrary",)),
    )(*([proj1] * 8 + [st_rkv, st_wa] + list(wts)))
    return outs[:N_RA_OUT], jnp.concatenate(outs[N_RA_OUT:], axis=1)


def _rwkv_b_kernel(ah_ref, rh_ref, bh_ref, kh_ref, v_ref, ul_ref, yl_ref, dc_ref, bo_ref, z_ref, lg_ref, lb_ref,
                   s0_ref, y_ref, so_ref, sbd_ref, yb_ref, *, NBLK, TLB, CT, T):
    t = pl.program_id(1)
    bd_mask = (_shr(_iota((256, 256), 0), 6) == _shr(_iota((256, 256), 1), 6)).astype(F32)
    e_tile = (_iota((64, 256), 0) == jnp.bitwise_and(_iota((64, 256), 1), 63)).astype(BF16)
    e_fold = (jnp.bitwise_and(_iota((256, 64), 0), 63) == _iota((256, 64), 1)).astype(BF16)

    @pl.when(t == 0)
    def _():
        for nb in range(NBLK):
            for j in range(4):
                hi, lo = _split2(s0_ref[nb, 256 * j:256 * (j + 1), :])
                full = _dot(jnp.concatenate([hi, lo], axis=0), e_tile)
                full = full[:256] + full[256:]
                sbd_ref[4 * nb + j] = full * bd_mask

    chains = [(nb, j, slice(256 * j, 256 * (j + 1))) for nb in range(NBLK) for j in range(4)]
    for c in range(TLB // CT):
        rows = slice(c * CT, (c + 1) * CT)
        sbs = [sbd_ref[4 * nb + j] for nb, j, cs in chains]
        outs = [_dot_nt(jnp.concatenate([ah_ref[nb, rows, cs], rh_ref[nb, rows, cs]], axis=0).astype(BF16),
                        sb.astype(BF16))
                for (nb, j, cs), sb in zip(chains, sbs)]
        upds = [_dot_tn(jnp.concatenate([o[:CT] + ul_ref[nb, rows, cs], v_ref[nb, rows, cs]],
                                        axis=0).astype(BF16),
                        jnp.concatenate([bh_ref[nb, rows, cs], kh_ref[nb, rows, cs]], axis=0).astype(BF16))
                for (nb, j, cs), o in zip(chains, outs)]
        for (nb, j, cs), sb, o, upd in zip(chains, sbs, outs, upds):
            yb_ref[nb, rows, cs] = o[CT:] + yl_ref[nb, rows, cs]
            sbd_ref[4 * nb + j] = sb * dc_ref[nb, c * CT:c * CT + 1, cs] + upd * bd_mask

    @pl.when(t == T - 1)
    def _():
        for nb in range(NBLK):
            outs = []
            for j in range(4):
                hi, lo = _split2(sbd_ref[4 * nb + j])
                both = _dot(jnp.concatenate([hi, lo], axis=0), e_fold)
                outs.append(both[:256] + both[256:])
            so_ref[nb] = jnp.concatenate(outs, axis=0)

    cat = lambda ref: jnp.concatenate([ref[nb] for nb in range(NBLK)], axis=0)
    y = cat(yb_ref)
    mu = _segsum(y, R_K) * (1.0 / R_K)
    yc = y - mu
    var = _segsum(yc * yc, R_K) * (1.0 / R_K)
    yn = yc * lax.rsqrt(var + R_LN_EPS) * lg_ref[...] + lb_ref[...] + cat(bo_ref)
    out = yn * _silu(cat(z_ref))
    for nb in range(NBLK):
        y_ref[nb] = out[nb * TLB:(nb + 1) * TLB, :]


def _rwkv_b_call(ra, proj1, ln_g, ln_b, s0, *, N, L, NBLK, TLB, CT):
    T = L // TLB
    blk = lambda cb: pl.BlockSpec((NBLK, TLB, MIX_W), lambda i, t: (i, t, cb))
    s_spec = pl.BlockSpec((NBLK, 1024, 64), lambda i, t: (i, 0, 0))
    ra3 = [a.reshape(N, L, MIX_W) for a in ra]
    y, s_new = pl.pallas_call(
        functools.partial(_rwkv_b_kernel, NBLK=NBLK, TLB=TLB, CT=CT, T=T),
        out_shape=(jax.ShapeDtypeStruct((N, L, MIX_W), F32),
                   jax.ShapeDtypeStruct((N, 1024, 64), F32)),
        grid=(N // NBLK, T),
        in_specs=([blk(0)] * N_RA_OUT
                  + [blk(P1_ZD // 1024),
                     pl.BlockSpec((1, MIX_W), lambda i, t: (0, 0)),
                     pl.BlockSpec((1, MIX_W), lambda i, t: (0, 0)),
                     s_spec]),
        out_specs=(blk(0), s_spec),
        scratch_shapes=[pltpu.VMEM((4 * NBLK, 256, 256), F32),
                        pltpu.VMEM((NBLK, TLB, MIX_W), F32)],
        compiler_params=_cp(("arbitrary", "arbitrary")),
    )(*(ra3 + [proj1.reshape(N, L, P1_N), ln_g, ln_b, s0]))
    return y.reshape(N * L, MIX_W), s_new


def _regroup_w1(w1):
    wt = w1.T
    return jnp.concatenate([wt[0:4096], wt[4104:8200], wt[8328:9352], wt[8200:8328], wt[4096:4104],
                            jnp.zeros((P1_N - 9352, D_MODEL), F32)], axis=0).astype(BF16)


def _s5_weights(lam_re, lam_im, log_dt, b_re, b_im, c_re, c_im):
    dt = jnp.exp(log_dt)[:, None]
    mag = jnp.exp(lam_re * dt)
    ar = mag * jnp.cos(lam_im * dt)
    ai = mag * jnp.sin(lam_im * dt)
    den = lam_re * lam_re + lam_im * lam_im
    qr = ((ar - 1.0) * lam_re + ai * lam_im) / den
    qi = (ai * lam_re - (ar - 1.0) * lam_im) / den
    bbr = qr[..., None] * b_re - qi[..., None] * b_im
    bbi = qr[..., None] * b_im + qi[..., None] * b_re
    hp = lax.Precision.HIGHEST
    spread_p = (jnp.arange(1024)[None, :] % S5_STATE == jnp.arange(S5_STATE)[:, None]).astype(F32)
    spread_h = (jnp.arange(256)[None, :] % S5_GROUP == jnp.arange(S5_GROUP)[:, None]).astype(F32)
    grp_in = (jnp.arange(256)[:, None] // S5_GROUP == jnp.arange(1024)[None, :] // S5_STATE)
    grp_out = (jnp.arange(1024)[:, None] // S5_STATE == jnp.arange(256)[None, :] // S5_GROUP)

    def in_blocks(bb):
        rows = bb.transpose(0, 2, 1).reshape(4, 256, S5_STATE)
        return jnp.matmul(rows, spread_p, precision=hp) * grp_in

    def out_blocks(cc):
        rows = cc.transpose(0, 2, 1).reshape(4, 1024, S5_GROUP)
        return jnp.matmul(rows, spread_h, precision=hp) * grp_out

    wb = jnp.concatenate([in_blocks(bbr), in_blocks(bbi)], axis=2).astype(BF16)
    wc = jnp.concatenate([out_blocks(c_re), out_blocks(-c_im)], axis=1).astype(BF16)
    pr, pi = jnp.ones_like(ar), jnp.zeros_like(ai)
    lag = []
    for _ in range(8):
        cpr = c_re * pr[:, None, :] - c_im * pi[:, None, :]
        cpi = c_re * pi[:, None, :] + c_im * pr[:, None, :]
        lag.append(jnp.einsum('gop,gph->goh', cpr, bbr, precision=lax.Precision.HIGHEST)
                   - jnp.einsum('gop,gph->goh', cpi, bbi, precision=lax.Precision.HIGHEST))
        pr, pi = pr * ar - pi * ai, pr * ai + pi * ar
    kd = jnp.stack(lag).reshape(8, 4, 16, S5_GROUP, S5_GROUP)
    kd = kd.transpose(1, 0, 2, 4, 3).reshape(4, 8, 256, S5_GROUP)
    spread = (jnp.arange(256)[None, :] % S5_GROUP == jnp.arange(S5_GROUP)[:, None]).astype(F32)
    same_group = (jnp.arange(256)[:, None] // S5_GROUP == jnp.arange(256)[None, :] // S5_GROUP)
    wk = jnp.matmul(kd, spread, precision=lax.Precision.HIGHEST) * same_group
    wk = wk.reshape(4, 8 * 256, 256).astype(BF16)
    return ar.reshape(1, 4096), ai.reshape(1, 4096), wb, wk, wc


CFG = {
    "P": dict(N=BATCH, L=P_LEN, tm_mm=1376, tm=688, tm_ln=688, drop_meta=True,
              conv=dict(NB=1, TL=344), s5=dict(NB=1, TL=688), mlstm=dict(NB=1, TL=344),
              ra=dict(RB=192, CT=16, U=12), rb=dict(NBLK=4, TLB=48, CT=16)),
    "S": dict(N=DEC_BATCH, L=DEC_SEQ, tm_mm=1024, tm=512, tm_ln=256, drop_meta=False,
              conv=dict(NB=16, TL=8), s5=dict(NB=32, TL=8), mlstm=dict(NB=4, TL=8),
              ra=dict(RB=256, CT=8, U=8), rb=dict(NBLK=8, TLB=8, CT=8)),
}


def _trunk(x, st, w, cfg):
    n, l = cfg["N"], cfg["L"]
    proj0 = _matmul(x, w["w_in0"], cfg["tm_mm"], 512)
    act, conv_new = _conv_call(proj0, st["conv"], w["conv_w"], w["conv_b"], w["a_ln_g"], w["a_ln_b"],
                               N=n, L=l, **cfg["conv"])
    mix_a = _pw_gate(act, w["pw"], proj0, tm=cfg["tm"])
    yb, xr, xi = _s5_call(proj0, w["s5_wb"], w["s5_wk"], w["s5_wc"], w["s5_d"], w["s5_ar"], w["s5_ai"],
                          st["ssm_re"].reshape(n, 1, 4096), st["ssm_im"].reshape(n, 1, 4096),
                          N=n, L=l, **cfg["s5"])
    mix_b = _glu_gate(yb, w["glu_w"], w["glu_b"], proj0, tm=cfg["tm"])
    x1 = _out_ln(x, mix_a, mix_b, w["w_out0"], w["ln_g0"], w["ln_b0"], tm=cfg["tm_ln"])

    proj1 = _matmul_nt(x1, w["w_in1"], cfg["tm_mm"], 512)
    m0 = jnp.pad(jnp.broadcast_to(st["m"][:, :, None], (n, M_HEADS, 128)), ((0, 0), (0, 4), (0, 0)))
    mix_c, c_new, n_new, m_new = _mlstm_call(proj1, w["gate_b"], w["hn_g"], st["c"], st["n"], m0,
                                             N=n, L=l, **cfg["mlstm"])
    sh = st["shift"]
    if l == cfg["ra"]["CT"]:
        sh = jnp.repeat(sh, l, axis=0)
    ra, shift_new = _rwkv_a_call(proj1, sh[:, :3072], sh[:, 3072:], w["rwkv"], N=n, L=l, **cfg["ra"])
    mix_d, s_new = _rwkv_b_call(ra, proj1, w["r_ln_g"], w["r_ln_b"], st["s"].reshape(n, 1024, 64),
                                N=n, L=l, **cfg["rb"])
    final_ln = _out_ln_prompt if cfg["drop_meta"] else functools.partial(_out_ln, tm=cfg["tm_ln"])
    y = final_ln(x1, mix_c, mix_d, w["w_out1"], w["ln_g1"], w["ln_b1"])

    states = (conv_new[None],
              xr.reshape(n, S5_GROUPS, S5_STATE)[None],
              xi.reshape(n, S5_GROUPS, S5_STATE)[None],
              c_new[None], n_new[None], m_new[:, :M_HEADS, 0][None],
              s_new.reshape(n, R_HEADS, R_K, R_K)[None],
              shift_new[None])
    return y, states


def kernel(x_prompt, x_sample, state_conv, state_ssm_re, state_ssm_im, state_mlstm_c, state_mlstm_n, state_mlstm_m, state_rwkv_s, state_rwkv_shift, meta_tokens, ev_w_in, a_conv_w, a_conv_b, a_ln_g, a_ln_b, a_pw, s5_lambda_re, s5_lambda_im, s5_log_dt, s5_b_re, s5_b_im, s5_c_re, s5_c_im, s5_d, s5_glu_w, s5_glu_b, ev_w_out, ev_ln_g, ev_ln_b, od_w_in, m_ig_b, m_fg_b, m_hn_g, r_mu, r_w0, r_w2, r_a0, r_a2, r_kk, r_ka, r_rk, r_ln_g, r_ln_b, od_w_out, od_ln_g, od_ln_b):
    nb = x_prompt.shape[0]
    row = lambda vec: vec.reshape(1, -1)
    zeros = lambda *s: jnp.zeros(s, F32)

    ar, ai, wb, wk, wc = _s5_weights(s5_lambda_re[0], s5_lambda_im[0], s5_log_dt[0], s5_b_re[0], s5_b_im[0],
                                     s5_c_re[0], s5_c_im[0])
    w_in1 = _regroup_w1(od_w_in[0])
    mu = r_mu[0]
    w = dict(
        w_in0=ev_w_in[0].astype(BF16), conv_w=a_conv_w[0], conv_b=row(a_conv_b[0]),
        a_ln_g=row(a_ln_g[0]), a_ln_b=row(a_ln_b[0]), pw=a_pw[0].astype(BF16),
        s5_wb=wb, s5_wk=wk, s5_wc=wc, s5_d=row(s5_d[0]), s5_ar=ar, s5_ai=ai,
        glu_w=s5_glu_w[0].astype(BF16), glu_b=row(s5_glu_b[0]),
        w_out0=ev_w_out[0].astype(BF16), ln_g0=row(ev_ln_g[0]), ln_b0=row(ev_ln_b[0]),
        w_in1=w_in1,
        gate_b=jnp.concatenate([m_ig_b[0], m_fg_b[0], jnp.zeros((120,), F32)]).reshape(1, 128),
        hn_g=row(m_hn_g[0]),
        rwkv=[row(mu[:3072]), row(mu[3072:]), row(r_w0[0]),
              jnp.concatenate([r_w2[0], jnp.zeros((64, MIX_W), F32)], axis=0).astype(BF16),
              row(r_a0[0]),
              jnp.concatenate([jnp.zeros((64, MIX_W), F32), r_a2[0]], axis=0).astype(BF16),
              row(r_kk[0]), row(r_ka[0]), row(r_rk[0])],
        r_ln_g=row(r_ln_g[0]), r_ln_b=row(r_ln_b[0]),
        w_out1=od_w_out[0].astype(BF16), ln_g1=row(od_ln_g[0]), ln_b1=row(od_ln_b[0]),
    )

    x_p = jnp.concatenate([jnp.broadcast_to(meta_tokens[None], (nb, N_META, D_MODEL)), x_prompt],
                          axis=1).reshape(nb * P_LEN, D_MODEL)
    st_p = dict(conv=zeros(nb, CONV_W - 1, MIX_W), ssm_re=zeros(nb, 4096), ssm_im=zeros(nb, 4096),
                c=zeros(nb, M_HEADS, M_D, M_D), n=zeros(nb, M_HEADS, M_D), m=zeros(nb, M_HEADS),
                s=zeros(nb, R_HEADS, R_K, R_K), shift=zeros(nb, 3200))
    y_p, states_p = _trunk(x_p, st_p, w, CFG["P"])

    st_s = dict(conv=state_conv[0], ssm_re=state_ssm_re[0], ssm_im=state_ssm_im[0],
                c=state_mlstm_c[0], n=state_mlstm_n[0], m=state_mlstm_m[0],
                s=state_rwkv_s[0], shift=state_rwkv_shift[0])
    y_s, states_s = _trunk(x_sample.reshape(DEC_BATCH * DEC_SEQ, D_MODEL), st_s, w, CFG["S"])

    y_prompt = y_p
    y_sample = y_s.reshape(DEC_BATCH, DEC_SEQ, D_MODEL)
    return (y_prompt, y_sample) + states_p + states_s
```

```python
import functools
import math

import jax
import jax.numpy as jnp
from jax import lax
from jax.experimental import pallas as pl
from jax.experimental.pallas import tpu as pltpu

F32 = jnp.float32
BF16 = jnp.bfloat16

D_MODEL = 2048
MIX_W = 1024
N_META = 16
CONV_W = 31
S5_GROUP = 16
S5_GROUPS = 64
S5_STATE = 64
M_HEADS = 4
M_D = 256
R_HEADS = 16
R_K = 64
LN_EPS = 1e-5
R_LN_EPS = 64e-5
DEPTH = 2
ALPHA = (2 * DEPTH) ** 0.25

BATCH = 4
SEQ = 2048
P_LEN = N_META + SEQ
DEC_BATCH = 128
DEC_SEQ = 8

P1_Q, P1_K, P1_V, P1_O, P1_ZC = 0, 1024, 2048, 3072, 4096
P1_R, P1_RK, P1_RV, P1_ZD, P1_WA, P1_GATE = 5120, 6144, 7168, 8192, 9216, 9344
P1_N = 9728

VMEM_LIMIT = 48 * 1024 * 1024


def _cp(sem):
    return pltpu.CompilerParams(dimension_semantics=sem, vmem_limit_bytes=VMEM_LIMIT)


def _dot(a, b):
    return jnp.dot(a, b, preferred_element_type=F32)


def _dot_nt(a, b):
    return lax.dot_general(a, b, (((1,), (1,)), ((), ())), preferred_element_type=F32)


def _dot_tn(a, b):
    return lax.dot_general(a, b, (((0,), (0,)), ((), ())), preferred_element_type=F32)


def _split2(x):
    hi = x.astype(BF16)
    lo = (x - hi.astype(F32)).astype(BF16)
    return hi, lo


def _split3(x):
    hi = x.astype(BF16)
    r1 = x - hi.astype(F32)
    mid = r1.astype(BF16)
    lo = (r1 - mid.astype(F32)).astype(BF16)
    return hi, mid, lo


def _sigmoid(x):
    return jax.nn.sigmoid(x)


def _silu(x):
    return x * jax.nn.sigmoid(x)


def _softplus(x):
    return jnp.maximum(x, 0.0) + jnp.log(1.0 + jnp.exp(-jnp.abs(x)))


def _gelu_tanh(x):
    c = math.sqrt(2.0 / math.pi)
    return x * (0.5 * (1.0 + jnp.tanh(c * (x + 0.044715 * (x * x * x)))))


def _iota(shape, axis):
    return lax.broadcasted_iota(jnp.int32, shape, axis)


def _shr(x, k):
    return lax.shift_right_logical(x, jnp.int32(k))


def _log2(n):
    k = int(round(math.log2(n)))
    assert 1 << k == n
    return k


def _block_ones(n, seg, dtype):
    r = _shr(_iota((n, n), 0), _log2(seg))
    c = _shr(_iota((n, n), 1), _log2(seg))
    return (r == c).astype(dtype)


def _segsum(x, seg):
    g = _block_ones(256, seg, BF16)
    outs = []
    for j in range(x.shape[1] // 256):
        hi, lo = _split2(x[:, 256 * j:256 * (j + 1)])
        outs.append(_dot(hi, g) + _dot(lo, g))
    return jnp.concatenate(outs, axis=1)


def _row_cumsum(x, period):
    rows = x.shape[0]
    rid = _iota(x.shape, 0)
    if period < rows:
        rid = jnp.bitwise_and(rid, period - 1)
    d = 1
    while d < min(period, rows):
        x = x + jnp.where(rid >= d, pltpu.roll(x, d, 0), 0.0)
        d *= 2
    return x


def _row_cummax(x, period):
    rows = x.shape[0]
    rid = _iota(x.shape, 0)
    if period < rows:
        rid = jnp.bitwise_and(rid, period - 1)
    d = 1
    while d < min(period, rows):
        x = jnp.maximum(x, jnp.where(rid >= d, pltpu.roll(x, d, 0), -jnp.inf))
        d *= 2
    return x


def _mm_kernel(x_ref, w_ref, o_ref):
    o_ref[...] = _dot(x_ref[...].astype(BF16), w_ref[...])


def _matmul(x, w, tm, tn):
    r, k = x.shape
    n = w.shape[1]
    return pl.pallas_call(
        _mm_kernel,
        out_shape=jax.ShapeDtypeStruct((r, n), F32),
        grid=(pl.cdiv(r, tm), n // tn),
        in_specs=[pl.BlockSpec((tm, k), lambda i, j: (i, 0)),
                  pl.BlockSpec((k, tn), lambda i, j: (0, j))],
        out_specs=pl.BlockSpec((tm, tn), lambda i, j: (i, j)),
        compiler_params=_cp(("parallel", "arbitrary")),
    )(x, w)


def _mm_nt_kernel(x_ref, w_ref, o_ref):
    o_ref[...] = _dot_nt(x_ref[...].astype(BF16), w_ref[...])


def _matmul_nt(x, w_t, tm, tn):
    r, k = x.shape
    n = w_t.shape[0]
    return pl.pallas_call(
        _mm_nt_kernel,
        out_shape=jax.ShapeDtypeStruct((r, n), F32),
        grid=(pl.cdiv(r, tm), n // tn),
        in_specs=[pl.BlockSpec((tm, k), lambda i, j: (i, 0)),
                  pl.BlockSpec((tn, k), lambda i, j: (j, 0))],
        out_specs=pl.BlockSpec((tm, tn), lambda i, j: (i, j)),
        compiler_params=_cp(("parallel", "arbitrary")),
    )(x, w_t)


def _pw_kernel(a_ref, w_ref, z_ref, o_ref):
    o_ref[...] = _dot(a_ref[...].astype(BF16), w_ref[...]) * _silu(z_ref[...])


def _pw_gate(act, pw, proj0, tm, tn=512):
    r = act.shape[0]
    zb = 2048 // tn
    return pl.pallas_call(
        _pw_kernel,
        out_shape=jax.ShapeDtypeStruct((r, MIX_W), F32),
        grid=(pl.cdiv(r, tm), MIX_W // tn),
        in_specs=[pl.BlockSpec((tm, MIX_W), lambda i, j: (i, 0)),
                  pl.BlockSpec((MIX_W, tn), lambda i, j: (0, j)),
                  pl.BlockSpec((tm, tn), lambda i, j: (i, zb + j))],
        out_specs=pl.BlockSpec((tm, tn), lambda i, j: (i, j)),
        compiler_params=_cp(("parallel", "arbitrary")),
    )(act, pw, proj0)


def _glu_kernel(y_ref, wv_ref, wg_ref, bv_ref, bg_ref, z_ref, o_ref):
    y = y_ref[...].astype(BF16)
    v = _dot(y, wv_ref[...]) + bv_ref[...]
    g = _dot(y, wg_ref[...]) + bg_ref[...]
    o_ref[...] = v * _sigmoid(g) * _silu(z_ref[...])


def _glu_gate(yb, glu_w, glu_b, proj0, tm, tn=512):
    r = yb.shape[0]
    nb = MIX_W // tn
    zb = 4096 // tn
    return pl.pallas_call(
        _glu_kernel,
        out_shape=jax.ShapeDtypeStruct((r, MIX_W), F32),
        grid=(pl.cdiv(r, tm), nb),
        in_specs=[pl.BlockSpec((tm, MIX_W), lambda i, j: (i, 0)),
                  pl.BlockSpec((MIX_W, tn), lambda i, j: (0, j)),
                  pl.BlockSpec((MIX_W, tn), lambda i, j: (0, nb + j)),
                  pl.BlockSpec((1, tn), lambda i, j: (0, j)),
                  pl.BlockSpec((1, tn), lambda i, j: (0, nb + j)),
                  pl.BlockSpec((tm, tn), lambda i, j: (i, zb + j))],
        out_specs=pl.BlockSpec((tm, tn), lambda i, j: (i, j)),
        compiler_params=_cp(("parallel", "arbitrary")),
    )(yb, glu_w, glu_w, glu_b, glu_b, proj0)


def _out_ln_kernel(x_ref, ma_ref, mb_ref, wa_ref, wb_ref, g_ref, b_ref, o_ref):
    out = _dot(ma_ref[...].astype(BF16), wa_ref[...]) + _dot(mb_ref[...].astype(BF16), wb_ref[...])
    y = ALPHA * x_ref[...] + out
    mu = jnp.mean(y, axis=-1, keepdims=True)
    yc = y - mu
    var = jnp.mean(yc * yc, axis=-1, keepdims=True)
    o_ref[...] = yc * lax.rsqrt(var + LN_EPS) * g_ref[...] + b_ref[...]


def _out_ln(x, mix_a, mix_b, w_out, ln_g, ln_b, tm):
    r = x.shape[0]
    return pl.pallas_call(
        _out_ln_kernel,
        out_shape=jax.ShapeDtypeStruct((r, D_MODEL), F32),
        grid=(pl.cdiv(r, tm),),
        in_specs=[pl.BlockSpec((tm, D_MODEL), lambda i: (i, 0)),
                  pl.BlockSpec((tm, MIX_W), lambda i: (i, 0)),
                  pl.BlockSpec((tm, MIX_W), lambda i: (i, 0)),
                  pl.BlockSpec((MIX_W, D_MODEL), lambda i: (0, 0), pipeline_mode=pl.Buffered(1)),
                  pl.BlockSpec((MIX_W, D_MODEL), lambda i: (1, 0), pipeline_mode=pl.Buffered(1)),
                  pl.BlockSpec((1, D_MODEL), lambda i: (0, 0)),
                  pl.BlockSpec((1, D_MODEL), lambda i: (0, 0))],
        out_specs=pl.BlockSpec((tm, D_MODEL), lambda i: (i, 0)),
        compiler_params=_cp(("parallel",)),
    )(x, mix_a, mix_b, w_out, w_out, ln_g, ln_b)


def _out_ln_prompt(x, mix_a, mix_b, w_out, ln_g, ln_b, tm=512):
    tiles = SEQ // tm

    def rows(width):
        return pl.BlockSpec((pl.Element(tm), pl.Element(width)),
                            lambda n, t: (pl.multiple_of(n * P_LEN + N_META + t * tm, 8), 0))

    return pl.pallas_call(
        _out_ln_kernel,
        out_shape=jax.ShapeDtypeStruct((BATCH * SEQ, D_MODEL), F32),
        grid=(BATCH, tiles),
        in_specs=[rows(D_MODEL), rows(MIX_W), rows(MIX_W),
                  pl.BlockSpec((MIX_W, D_MODEL), lambda n, t: (0, 0), pipeline_mode=pl.Buffered(1)),
                  pl.BlockSpec((MIX_W, D_MODEL), lambda n, t: (1, 0), pipeline_mode=pl.Buffered(1)),
                  pl.BlockSpec((1, D_MODEL), lambda n, t: (0, 0)),
                  pl.BlockSpec((1, D_MODEL), lambda n, t: (0, 0))],
        out_specs=pl.BlockSpec((tm, D_MODEL), lambda n, t: (n * tiles + t, 0)),
        compiler_params=_cp(("parallel", "arbitrary")),
    )(x, mix_a, mix_b, w_out, w_out, ln_g, ln_b).reshape(BATCH, SEQ, D_MODEL)


def _conv_kernel(u_ref, g_ref, st_ref, w_ref, cb_ref, lg_ref, lb_ref, act_ref, nst_ref, hp_ref, hs_ref, wb_ref,
                 *, NB, TL, T):
    t = pl.program_id(1)
    for j in range(CONV_W):
        wb_ref[j] = jnp.broadcast_to(w_ref[j:j + 1, :], (8, MIX_W))
    for nb in range(NB):
        base = nb * TL

        @pl.when(t == 0)
        def _():
            hp_ref[nb, 0:2, :] = jnp.zeros((2, MIX_W), F32)
            hp_ref[nb, 2:32, :] = st_ref[nb]

        hp_ref[nb, TL + 32:TL + 40, :] = jnp.zeros((8, MIX_W), F32)
        hp_ref[nb, 32:32 + TL, :] = u_ref[base:base + TL, :] * _sigmoid(g_ref[base:base + TL, :])
        for b in range(8):
            hs_ref[b] = hp_ref[nb, b:b + TL + 32, :]

        def taps(r0, groups):
            acc = [None] * groups
            for j in range(CONV_W):
                o = j + 2
                wj = wb_ref[j]
                for g in range(groups):
                    term = wj * hs_ref[o % 8, pl.ds(r0 + 8 * (o // 8 + g), 8), :]
                    acc[g] = term if acc[g] is None else acc[g] + term
            for g in range(groups):
                act_ref[pl.ds(base + r0 + 8 * g, 8), :] = acc[g]

        def chunk(c, carry):
            taps(pl.multiple_of(c * 16, 16), 2)
            return carry

        lax.fori_loop(0, TL // 16, chunk, 0)
        if TL % 16:
            taps(TL - 8, 1)

        @pl.when(t == T - 1)
        def _():
            nst_ref[nb] = hp_ref[nb, TL + 2:TL + 32, :]

        if T > 1:
            hp_ref[nb, 0:32, :] = hp_ref[nb, TL:TL + 32, :]

    y = act_ref[...] + cb_ref[...]
    mu = jnp.mean(y, axis=-1, keepdims=True)
    yc = y - mu
    var = jnp.mean(yc * yc, axis=-1, keepdims=True)
    act_ref[...] = _silu(yc * lax.rsqrt(var + LN_EPS) * lg_ref[...] + lb_ref[...])


def _conv_call(proj0, state, conv_w, conv_b, ln_g, ln_b, *, N, L, NB, TL):
    T = L // TL
    RB = NB * TL
    assert NB == 1 or T == 1
    return pl.pallas_call(
        functools.partial(_conv_kernel, NB=NB, TL=TL, T=T),
        out_shape=(jax.ShapeDtypeStruct((N * L, MIX_W), F32),
                   jax.ShapeDtypeStruct((N, CONV_W - 1, MIX_W), F32)),
        grid=(N // NB, T),
        in_specs=[pl.BlockSpec((RB, MIX_W), lambda i, t: (i * T + t, 0)),
                  pl.BlockSpec((RB, MIX_W), lambda i, t: (i * T + t, 1)),
                  pl.BlockSpec((NB, CONV_W - 1, MIX_W), lambda i, t: (i, 0, 0)),
                  pl.BlockSpec((CONV_W, MIX_W), lambda i, t: (0, 0)),
                  pl.BlockSpec((1, MIX_W), lambda i, t: (0, 0)),
                  pl.BlockSpec((1, MIX_W), lambda i, t: (0, 0)),
                  pl.BlockSpec((1, MIX_W), lambda i, t: (0, 0))],
        out_specs=(pl.BlockSpec((RB, MIX_W), lambda i, t: (i * T + t, 0)),
                   pl.BlockSpec((NB, CONV_W - 1, MIX_W), lambda i, t: (i, 0, 0))),
        scratch_shapes=[pltpu.VMEM((NB, TL + 40, MIX_W), F32),
                        pltpu.VMEM((8, TL + 32, MIX_W), F32),
                        pltpu.VMEM((CONV_W, 8, MIX_W), F32)],
        compiler_params=_cp(("arbitrary", "arbitrary")),
    )(proj0, proj0, state, conv_w, conv_b, ln_g, ln_b)


def _s5_kernel(u_ref, wb_ref, wk_ref, wc_ref, d_ref, ar_ref, ai_ref, x0r_ref, x0i_ref,
               y_ref, xfr_ref, xfi_ref, xs_ref, cr_ref, ci_ref, *, NB, TL, T):
    t = pl.program_id(2)
    RB = NB * TL
    GL = TL // 8
    u = u_ref[...]
    ub = u.astype(BF16)
    big = _dot(ub, wb_ref[0])
    xs_ref[0] = big[:, :1024]
    xs_ref[1] = big[:, 1024:]
    rid = jnp.bitwise_and(_iota((RB, 256), 0), 7)
    lags = [ub] + [jnp.where(rid >= d, pltpu.roll(u, d, 0), 0.0).astype(BF16) for d in range(1, 8)]
    y_loc = _dot(jnp.concatenate(lags, axis=1), wk_ref[0])
    ar = ar_ref[...]
    ai = ai_ref[...]

    def cmul(pr, pi, qr, qi):
        return pr * qr - pi * qi, pr * qi + pi * qr

    a1 = (ar, ai)
    a2 = cmul(*a1, *a1)
    a4 = cmul(*a2, *a2)
    a3 = cmul(*a2, *a1)
    a5 = cmul(*a4, *a1)
    a6 = cmul(*a4, *a2)
    a7 = cmul(*a6, *a1)
    a8 = cmul(*a4, *a4)
    a0 = (jnp.ones_like(ar), jnp.zeros_like(ai))
    r8 = _iota((8, 1024), 0)

    def table(powers):
        tr = jnp.zeros((8, 1024), F32)
        ti = jnp.zeros((8, 1024), F32)
        for k, (pr, pi) in enumerate(powers):
            tr = jnp.where(r8 == k, pr, tr)
            ti = jnp.where(r8 == k, pi, ti)
        return tr, ti

    pwr, pwi = table((a1, a2, a3, a4, a5, a6, a7, a8))
    qwr, qwi = table((a7, a6, a5, a4, a3, a2, a1, a0))
    a8r, a8i = a8

    first = t == 0

    def seq_body(nb, carry0):
        x0r = x0r_ref[nb]
        x0i = x0i_ref[nb]
        if T > 1:
            c_r = jnp.where(first, x0r, cr_ref[0:1, :])
            c_i = jnp.where(first, x0i, ci_ref[0:1, :])
        else:
            c_r, c_i = x0r, x0i

        def grp(g, c):
            c_r, c_i = c
            off = pl.multiple_of(nb * TL + g * 8, 8)
            vr = xs_ref[0, pl.ds(off, 8), :]
            vi = xs_ref[1, pl.ds(off, 8), :]
            er = jnp.sum(qwr * vr - qwi * vi, axis=0, keepdims=True)
            ei = jnp.sum(qwr * vi + qwi * vr, axis=0, keepdims=True)
            br = jnp.broadcast_to(c_r, (8, 1024))
            bi = jnp.broadcast_to(c_i, (8, 1024))
            xs_ref[0, pl.ds(off, 8), :] = pwr * br - pwi * bi
            xs_ref[1, pl.ds(off, 8), :] = pwr * bi + pwi * br
            return a8r * c_r - a8i * c_i + er, a8r * c_i + a8i * c_r + ei

        c_r, c_i = lax.fori_loop(0, GL, grp, (c_r, c_i))
        if T > 1:
            cr_ref[...] = jnp.broadcast_to(c_r, (8, 1024))
            ci_ref[...] = jnp.broadcast_to(c_i, (8, 1024))

        @pl.when(t == T - 1)
        def _():
            xfr_ref[nb] = c_r
            xfi_ref[nb] = c_i

        return carry0

    lax.fori_loop(0, NB, seq_body, 0)
    y = (_dot(xs_ref[0].astype(BF16), wc_ref[0, 0:1024, :])
         + _dot(xs_ref[1].astype(BF16), wc_ref[0, 1024:2048, :]))
    y_ref[...] = _gelu_tanh(y + y_loc + d_ref[...] * u)


def _s5_call(proj0, wb, wk, wc, dvec, ar, ai, x0r, x0i, *, N, L, NB, TL):
    T = L // TL
    RB = NB * TL
    assert NB == 1 or T == 1
    ub = 3072 // 256
    st = jax.ShapeDtypeStruct((N, 1, 4096), F32)
    return pl.pallas_call(
        functools.partial(_s5_kernel, NB=NB, TL=TL, T=T),
        out_shape=(jax.ShapeDtypeStruct((N * L, MIX_W), F32), st, st),
        grid=(N // NB, 4, T),
        in_specs=[pl.BlockSpec((RB, 256), lambda i, j, t: (i * T + t, ub + j)),
                  pl.BlockSpec((1, 256, 2048), lambda i, j, t: (j, 0, 0)),
                  pl.BlockSpec((1, 2048, 256), lambda i, j, t: (j, 0, 0)),
                  pl.BlockSpec((1, 2048, 256), lambda i, j, t: (j, 0, 0)),
                  pl.BlockSpec((1, 256), lambda i, j, t: (0, j)),
                  pl.BlockSpec((1, 1024), lambda i, j, t: (0, j)),
                  pl.BlockSpec((1, 1024), lambda i, j, t: (0, j)),
                  pl.BlockSpec((NB, 1, 1024), lambda i, j, t: (i, 0, j)),
                  pl.BlockSpec((NB, 1, 1024), lambda i, j, t: (i, 0, j))],
        out_specs=(pl.BlockSpec((RB, 256), lambda i, j, t: (i * T + t, j)),
                   pl.BlockSpec((NB, 1, 1024), lambda i, j, t: (i, 0, j)),
                   pl.BlockSpec((NB, 1, 1024), lambda i, j, t: (i, 0, j))),
        scratch_shapes=[pltpu.VMEM((2, RB, 1024), F32),
                        pltpu.VMEM((8, 1024), F32),
                        pltpu.VMEM((8, 1024), F32)],
        compiler_params=_cp(("arbitrary", "arbitrary", "arbitrary")),
    )(proj0, wb, wk, wc, dvec, ar, ai, x0r, x0i)


def _mlstm_kernel(q_ref, k_ref, v_ref, o_ref, z_ref, gt_ref, gb_ref, hg_ref, c0_ref, n0_ref, m0_ref,
                  y_ref, c_ref, n_ref, m_ref, cs_ref, ns_ref, ms_ref, *, NB, TL, T):
    for nb in range(NB):
        rows = lambda ref: ref.at[pl.ds(nb * TL, TL)]
        one = lambda ref: ref.at[pl.ds(nb, 1)]
        _mlstm_seq(rows(q_ref), rows(k_ref), rows(v_ref), rows(o_ref), rows(z_ref), rows(gt_ref), gb_ref, hg_ref,
                   one(c0_ref), one(n0_ref), one(m0_ref), rows(y_ref), one(c_ref), one(n_ref), one(m_ref),
                   cs_ref.at[nb], ns_ref.at[nb], ms_ref.at[nb], TL=TL, T=T)


def _mlstm_seq(q_ref, k_ref, v_ref, o_ref, z_ref, gt_ref, gb_ref, hg_ref, c0_ref, n0_ref, m0_ref,
               y_ref, c_ref, n_ref, m_ref, cs_ref, ns_ref, ms_ref, *, TL, T):
    t = pl.program_id(1)

    @pl.when(t == 0)
    def _():
        cs_ref[...] = c0_ref[0]
        ns_ref[...] = n0_ref[0]
        ms_ref[...] = m0_ref[0]

    G = gt_ref[...] + gb_ref[...]
    B = _row_cumsum(-_softplus(-G), TL)
    Bs = pltpu.roll(B, 124, 1)
    A = G - Bs
    CM = _row_cummax(A, TL)
    ms = ms_ref[...]
    dg = _iota((8, 128), 0) == _iota((8, 128), 1)
    mrow = jnp.sum(jnp.where(dg, ms, 0.0), axis=0, keepdims=True)
    M = jnp.maximum(mrow, CM)
    MT = Bs + M
    sel = dg.astype(BF16)
    a_hi, a_mid, a_lo = _split3(A)
    Arow = _dot_nt(sel, a_hi) + _dot_nt(sel, a_mid) + _dot_nt(sel, a_lo)
    causal = _iota((TL, TL), 0) >= _iota((TL, TL), 1)
    H = range(M_HEADS)
    sl = [slice(M_D * h, M_D * (h + 1)) for h in H]
    q = [q_ref[:, sl[h]] * (M_D ** -0.5) for h in H]
    qb = [x.astype(BF16) for x in q]
    kf = [k_ref[:, sl[h]] for h in H]
    kb = [x.astype(BF16) for x in kf]
    vf = [v_ref[:, sl[h]] for h in H]
    c_old = [cs_ref[h] for h in H]
    n_old = [ns_ref[h:h + 1, :] for h in H]
    m_col = [M[:, h:h + 1] for h in H]
    mt_col = [MT[:, h:h + 1] for h in H]
    b_col = [Bs[:, h:h + 1] for h in H]
    m_prev = [mrow[:, h:h + 1] for h in H]
    dm = [jnp.exp(jnp.where(causal, Arow[h:h + 1, :] - m_col[h], -jnp.inf)) for h in H]
    s = [_dot_nt(qb[h], kb[h]) * dm[h] for h in H]
    inter = [jnp.exp(m_prev[h] - m_col[h]) for h in H]
    h_intra = [_dot(s[h].astype(BF16), vf[h].astype(BF16)) for h in H]
    h_inter = [_dot(qb[h], c_old[h].astype(BF16)) * inter[h] for h in H]
    n_all = [jnp.sum(s[h], axis=1, keepdims=True) + jnp.sum(q[h] * n_old[h], axis=1, keepdims=True) * inter[h]
             for h in H]
    hh = [(h_intra[h] + h_inter[h]) / jnp.maximum(jnp.abs(n_all[h]), jnp.exp(-mt_col[h])) for h in H]
    m_new = [mt_col[h][TL - 1:TL, :] for h in H]
    b_end = [b_col[h][TL - 1:TL, :] for h in H]
    dec = [jnp.exp(m_prev[h] + b_end[h] - m_new[h]) for h in H]
    w_s = [jnp.exp(b_end[h] - b_col[h] + G[:, h:h + 1] - m_new[h]) for h in H]
    c_new = [c_old[h] * dec[h] + _dot_tn(kb[h], (vf[h] * w_s[h]).astype(BF16)) for h in H]
    n_new = [n_old[h] * dec[h] + jnp.sum(kf[h] * w_s[h], axis=0, keepdims=True) for h in H]
    outs = []
    for h in H:
        mu = jnp.mean(hh[h], axis=-1, keepdims=True)
        hc = hh[h] - mu
        var = jnp.mean(hc * hc, axis=-1, keepdims=True)
        hn = hc * lax.rsqrt(var + LN_EPS) * hg_ref[:, sl[h]]
        outs.append(hn * _sigmoid(o_ref[:, sl[h]]) * _silu(z_ref[:, sl[h]]))
    for h in H:
        cs_ref[h] = c_new[h]
        ns_ref[h:h + 1, :] = n_new[h]
        ms_ref[h:h + 1, :] = jnp.broadcast_to(m_new[h], (1, 128))
        y_ref[:, sl[h]] = outs[h]

    @pl.when(t == T - 1)
    def _():
        c_ref[0] = cs_ref[...]
        n_ref[0] = ns_ref[...]
        m_ref[0] = ms_ref[...]


def _mlstm_call(proj1, gate_b, hn_g, c0, n0, m0, *, N, L, NB, TL):
    T = L // TL
    RB = NB * TL
    assert NB == 1 or T == 1

    def col(cb):
        return pl.BlockSpec((RB, MIX_W), lambda i, t: (i * T + t, cb))

    st_specs = (pl.BlockSpec((NB, M_HEADS, M_D, M_D), lambda i, t: (i, 0, 0, 0)),
                pl.BlockSpec((NB, M_HEADS, M_D), lambda i, t: (i, 0, 0)),
                pl.BlockSpec((NB, 8, 128), lambda i, t: (i, 0, 0)))
    return pl.pallas_call(
        functools.partial(_mlstm_kernel, NB=NB, TL=TL, T=T),
        out_shape=(jax.ShapeDtypeStruct((N * L, MIX_W), F32),
                   jax.ShapeDtypeStruct((N, M_HEADS, M_D, M_D), F32),
                   jax.ShapeDtypeStruct((N, M_HEADS, M_D), F32),
                   jax.ShapeDtypeStruct((N, 8, 128), F32)),
        grid=(N // NB, T),
        in_specs=[col(0), col(1), col(2), col(3), col(4),
                  pl.BlockSpec((RB, 128), lambda i, t: (i * T + t, P1_GATE // 128)),
                  pl.BlockSpec((1, 128), lambda i, t: (0, 0)),
                  pl.BlockSpec((1, MIX_W), lambda i, t: (0, 0))] + list(st_specs),
        out_specs=(pl.BlockSpec((RB, MIX_W), lambda i, t: (i * T + t, 0)),) + st_specs,
        scratch_shapes=[pltpu.VMEM((NB, M_HEADS, M_D, M_D), F32),
                        pltpu.VMEM((NB, M_HEADS, M_D), F32),
                        pltpu.VMEM((NB, 8, 128), F32)],
        compiler_params=_cp(("arbitrary", "arbitrary")),
    )(proj1, proj1, proj1, proj1, proj1, proj1, gate_b, hn_g, c0, n0, m0)


N_RA_OUT = 9


def _rwkv_a_kernel(pr_ref, pk_ref, pv_ref, pwa_ref, hr_ref, hk_ref, hv_ref, hwa_ref, st_ref, stwa_ref,
                   mu_ref, muwa_ref, w0_ref, w2_ref, a0_ref, a2_ref, kkp_ref, ka_ref, rk_ref,
                   ah_ref, rh_ref, bh_ref, kh_ref, vo_ref, ul_ref, yl_ref, dc_ref, bo_ref, shr_ref, shw_ref,
                   s_at, s_rt, s_bt, s_kt, s_v, s_cum, *, RB, CT, N, L, U):
    HS = R_HEADS * CT
    HG = 128 // CT
    NG = R_HEADS // HG
    GW = HG * R_K
    NCH = RB // CT
    short = L == CT
    rid = _iota((RB, 1), 0)
    grow = pl.program_id(0) * RB + rid

    def shifted(p_ref, h_ref, s_ref, lo, hi, mu):
        p = p_ref[...]
        prev = pltpu.roll(p, 1, 0)
        if short:
            prev = jnp.where(jnp.bitwise_and(rid, CT - 1) == 0, s_ref[:, lo:hi], prev)
        else:
            prev = jnp.where(rid == 0, h_ref[7:8, :], prev)
            for n in range(N):
                prev = jnp.where(grow == n * L, s_ref[n:n + 1, lo:hi], prev)
        return p + (prev - p) * mu

    raw = ((pr_ref, shr_ref, 0), (pk_ref, shr_ref, 1024), (pv_ref, shr_ref, 2048), (pwa_ref, shw_ref, 0))
    if short:
        nseq = RB // CT
        sel = (_iota((nseq, RB), 1) == _iota((nseq, RB), 0) * CT + (CT - 1)).astype(BF16)
        for src, dst, lo in raw:
            hi, mid, low = _split3(src[...])
            dst[:, lo:lo + src.shape[1]] = _dot(sel, hi) + _dot(sel, mid) + _dot(sel, low)
    else:
        for n in range(N):
            tile, off = divmod(n * L + L - 1, RB)

            @pl.when(pl.program_id(0) == tile)
            def _():
                for src, dst, lo in raw:
                    dst[n:n + 1, lo:lo + src.shape[1]] = src[off:off + 1, :]

    r = shifted(pr_ref, hr_ref, st_ref, 0, 1024, mu_ref[:, 0:1024])
    k = shifted(pk_ref, hk_ref, st_ref, 1024, 2048, mu_ref[:, 1024:2048])
    v = shifted(pv_ref, hv_ref, st_ref, 2048, 3072, mu_ref[:, 2048:3072])
    wa = shifted(pwa_ref, hwa_ref, stwa_ref, 0, 128, muwa_ref[...])
    w = -_softplus(-(w0_ref[...] + _dot(jnp.tanh(wa).astype(BF16), w2_ref[...]))) - 0.5
    wlog = -jnp.exp(w)
    a = _sigmoid(a0_ref[...] + _dot(wa.astype(BF16), a2_ref[...]))
    kk = k * kkp_ref[...]
    kk = kk / jnp.maximum(jnp.sqrt(_segsum(kk * kk, R_K)), 1e-12)
    kmod = k * (1.0 + (a - 1.0) * ka_ref[...])
    bo_ref[...] = _segsum(r * kmod * rk_ref[...], R_K) * v
    cum = _row_cumsum(wlog, CT)
    einv = jnp.exp(-cum)
    s_at[...] = (-kk) * jnp.exp(cum - wlog)
    s_rt[...] = r * jnp.exp(cum)
    s_bt[...] = kk * a * einv
    s_kt[...] = kmod * einv
    s_v[...] = v
    s_cum[...] = cum

    mdt = BF16 if CT % 16 == 0 else F32
    be_mask = (_shr(_iota((128, GW), 0), _log2(CT)) == _shr(_iota((128, GW), 1), _log2(R_K))).astype(mdt)
    bd_mask = (_shr(_iota((HS, HS), 0), _log2(CT)) == _shr(_iota((HS, HS), 1), _log2(CT))).astype(mdt)
    tt = _iota((CT, HS), 0)
    ss = jnp.bitwise_and(_iota((CT, HS), 1), CT - 1)
    strict = tt > ss
    incl = tt >= ss
    eye_c = (tt == ss).astype(F32)
    cat0 = lambda *xs: jnp.concatenate(xs, axis=0)

    def blockexp(x):
        return [(jnp.concatenate([x[:, GW * g:GW * (g + 1)].astype(mdt)] * HG, axis=0) * be_mask).astype(BF16)
                for g in range(NG)]

    def gram(lhs, be):
        lb = lhs.astype(BF16)
        return jnp.concatenate([_dot_nt(lb[:, GW * g:GW * (g + 1)], be[g]) for g in range(NG)], axis=1)

    def apply(cmp, be):
        cb = cmp.astype(BF16)
        return jnp.concatenate([_dot(cb[:, 128 * g:128 * (g + 1)], be[g]) for g in range(NG)], axis=1)

    def bdiag(x):
        return (jnp.concatenate([x.astype(mdt)] * R_HEADS, axis=0) * bd_mask).astype(BF16)

    def bdiag_hl(x):
        if mdt == BF16:
            return tuple(bdiag(part) for part in _split2(x))
        return _split2(jnp.concatenate([x] * R_HEADS, axis=0) * bd_mask)

    def mm_hl(stack, wh, wl):
        sh, sl = _split2(stack)
        n = stack.shape[0]
        full = _dot(cat0(sh, sl), wh)
        return full[:n] + full[n:] + _dot(sh, wl)

    def chunks(i, carry):
        rows = [pl.ds(pl.multiple_of((i * U + u) * CT, CT), CT) for u in range(U)]
        ld = lambda ref: [ref[rw, :] for rw in rows]
        at, rt, bt, kt, vv, cm = ld(s_at), ld(s_rt), ld(s_bt), ld(s_kt), ld(s_v), ld(s_cum)
        each = lambda f, *xs: [f(*a_) for a_ in zip(*xs)]
        ar_ = each(cat0, at, rt)
        gb = each(lambda l_, y_: gram(l_, blockexp(y_)), ar_, bt)
        gk = each(lambda l_, y_: gram(l_, blockexp(y_)), ar_, kt)
        a_ab = each(lambda m: jnp.where(strict, m[:CT], 0.0), gb)
        a_rb = each(lambda m: jnp.where(incl, m[CT:], 0.0), gb)
        a_ak = each(lambda m: jnp.where(strict, m[:CT], 0.0), gk)
        a_rk = each(lambda m: jnp.where(incl, m[CT:], 0.0), gk)
        p = each(lambda m: eye_c + m, a_ab)
        x = a_ab
        q = a_rb
        w_hl = each(bdiag_hl, x)
        res = each(lambda x_, q_, w_: mm_hl(cat0(x_, q_), *w_), x, q, w_hl)
        x = each(lambda r_: r_[:CT], res)
        q = each(lambda q_, r_: q_ + r_[CT:], q, res)
        pw = 2
        while pw < CT:
            w_hl = each(bdiag_hl, x)
            if 2 * pw >= CT:
                res = each(lambda p_, q_, w_: mm_hl(cat0(p_, q_), *w_), p, q, w_hl)
                q = each(lambda q_, r_: q_ + r_[CT:], q, res)
            else:
                res = each(lambda p_, x_, q_, w_: mm_hl(cat0(p_, x_, q_), *w_), p, x, q, w_hl)
                x = each(lambda r_: r_[CT:2 * CT], res)
                q = each(lambda q_, r_: q_ + r_[2 * CT:], q, res)
            p = each(lambda p_, r_: p_ + r_[:CT], p, res)
            pw *= 2
        tq = each(cat0, p, q)
        res = each(lambda m, k_: _dot(m.astype(BF16), bdiag(k_)), tq, a_ak)
        ty = each(lambda r_, k_: cat0(r_[:CT], r_[CT:] + k_), res, a_rk)
        o1 = each(lambda m, y_: apply(m, blockexp(y_)), tq, at)
        o2 = each(lambda m, y_: apply(m, blockexp(y_)), ty, vv)
        ect = each(lambda c_: jnp.exp(c_[CT - 1:CT, :]), cm)
        for u, rw in enumerate(rows):
            ah_ref[rw, :] = o1[u][:CT]
            rh_ref[rw, :] = rt[u] + o1[u][CT:]
            ul_ref[rw, :] = o2[u][:CT]
            yl_ref[rw, :] = o2[u][CT:]
            bh_ref[rw, :] = bt[u] * ect[u]
            kh_ref[rw, :] = kt[u] * ect[u]
            vo_ref[rw, :] = vv[u]
            dc_ref[rw, :] = jnp.broadcast_to(ect[u], (CT, 1024))
        return carry

    lax.fori_loop(0, NCH // U, chunks, 0)


def _rwkv_a_call(proj1, st_rkv, st_wa, wts, *, N, L, RB, CT, U):
    short = L == CT
    rows = N * L
    assert rows % RB == 0 and (RB // CT) % U == 0

    def col(cb, width=MIX_W):
        return pl.BlockSpec((RB, width), lambda i: (i, cb))

    def halo(cb, width=MIX_W):
        return pl.BlockSpec((8, width), lambda i: (jnp.maximum(i * (RB // 8) - 1, 0), cb))

    if short:
        st_specs = [pl.BlockSpec((RB, 3072), lambda i: (i, 0)), pl.BlockSpec((RB, 128), lambda i: (i, 0))]
    else:
        st_specs = [pl.BlockSpec((N, 3072), lambda i: (0, 0)), pl.BlockSpec((N, 128), lambda i: (0, 0))]

    def full(shape):
        return pl.BlockSpec(shape, lambda i: (0,) * len(shape))

    o_spec = pl.BlockSpec((RB, MIX_W), lambda i: (i, 0))
    if short:
        sh_specs = (pl.BlockSpec((RB // CT, 3072), lambda i: (i, 0)), pl.BlockSpec((RB // CT, 128), lambda i: (i, 0)))
    else:
        sh_specs = (pl.BlockSpec((N, 3072), lambda i: (0, 0)), pl.BlockSpec((N, 128), lambda i: (0, 0)))
    outs = pl.pallas_call(
        functools.partial(_rwkv_a_kernel, RB=RB, CT=CT, N=N, L=L, U=U),
        out_shape=((jax.ShapeDtypeStruct((rows, MIX_W), F32),) * N_RA_OUT
                   + (jax.ShapeDtypeStruct((N, 3072), F32), jax.ShapeDtypeStruct((N, 128), F32))),
        grid=(rows // RB,),
        in_specs=([col(P1_R // 1024), col(P1_RK // 1024), col(P1_RV // 1024), col(P1_WA // 128, 128),
                   halo(P1_R // 1024), halo(P1_RK // 1024), halo(P1_RV // 1024), halo(P1_WA // 128, 128)]
                  + st_specs
                  + [full((1, 3072)), full((1, 128)), full((1, 1024)), full((128, 1024)), full((1, 1024)),
                     full((128, 1024)), full((1, 1024)), full((1, 1024)), full((1, 1024))]),
        out_specs=(o_spec,) * N_RA_OUT + sh_specs,
        scratch_shapes=[pltpu.VMEM((RB, MIX_W), F32)] * 6,
        compiler_params=_cp(("arbitrary",)),
    )(*([proj1] * 8 + [st_rkv, st_wa] + list(wts)))
    return outs[:N_RA_OUT], jnp.concatenate(outs[N_RA_OUT:], axis=1)


def _rwkv_b_kernel(ah_ref, rh_ref, bh_ref, kh_ref, v_ref, ul_ref, yl_ref, dc_ref, bo_ref, z_ref, lg_ref, lb_ref,
                   s0_ref, y_ref, so_ref, sbd_ref, yb_ref, *, NBLK, TLB, CT, T):
    t = pl.program_id(1)
    bd_mask = (_shr(_iota((256, 256), 0), 6) == _shr(_iota((256, 256), 1), 6)).astype(F32)

    @pl.when(t == 0)
    def _():
        for nb in range(NBLK):
            for j in range(4):
                x = s0_ref[nb, 256 * j:256 * (j + 1), :]
                x = jnp.concatenate([x, x], axis=1)
                sbd_ref[4 * nb + j] = jnp.concatenate([x, x], axis=1) * bd_mask

    chains = [(nb, j, slice(256 * j, 256 * (j + 1))) for nb in range(NBLK) for j in range(4)]
    for c in range(TLB // CT):
        rows = slice(c * CT, (c + 1) * CT)
        sbs = [sbd_ref[4 * nb + j] for nb, j, cs in chains]
        outs = [_dot_nt(jnp.concatenate([ah_ref[nb, rows, cs], rh_ref[nb, rows, cs]], axis=0).astype(BF16),
                        sb.astype(BF16))
                for (nb, j, cs), sb in zip(chains, sbs)]
        upds = [_dot_tn(jnp.concatenate([o[:CT] + ul_ref[nb, rows, cs], v_ref[nb, rows, cs]],
                                        axis=0).astype(BF16),
                        jnp.concatenate([bh_ref[nb, rows, cs], kh_ref[nb, rows, cs]], axis=0).astype(BF16))
                for (nb, j, cs), o in zip(chains, outs)]
        for (nb, j, cs), sb, o, upd in zip(chains, sbs, outs, upds):
            yb_ref[nb, rows, cs] = o[CT:] + yl_ref[nb, rows, cs]
            sbd_ref[4 * nb + j] = sb * dc_ref[nb, c * CT:c * CT + 1, cs] + upd * bd_mask

    @pl.when(t == T - 1)
    def _():
        for nb in range(NBLK):
            outs = []
            for j in range(4):
                sb = sbd_ref[4 * nb + j]
                half = sb[:, :128] + sb[:, 128:]
                outs.append(half[:, :64] + half[:, 64:])
            so_ref[nb] = jnp.concatenate(outs, axis=0)

    cat = lambda ref: jnp.concatenate([ref[nb] for nb in range(NBLK)], axis=0)
    y = cat(yb_ref)
    mu = _segsum(y, R_K) * (1.0 / R_K)
    yc = y - mu
    var = _segsum(yc * yc, R_K) * (1.0 / R_K)
    yn = yc * lax.rsqrt(var + R_LN_EPS) * lg_ref[...] + lb_ref[...] + cat(bo_ref)
    out = yn * _silu(cat(z_ref))
    for nb in range(NBLK):
        y_ref[nb] = out[nb * TLB:(nb + 1) * TLB, :]


def _rwkv_b_call(ra, proj1, ln_g, ln_b, s0, *, N, L, NBLK, TLB, CT):
    T = L // TLB
    blk = lambda cb: pl.BlockSpec((NBLK, TLB, MIX_W), lambda i, t: (i, t, cb))
    s_spec = pl.BlockSpec((NBLK, 1024, 64), lambda i, t: (i, 0, 0))
    ra3 = [a.reshape(N, L, MIX_W) for a in ra]
    y, s_new = pl.pallas_call(
        functools.partial(_rwkv_b_kernel, NBLK=NBLK, TLB=TLB, CT=CT, T=T),
        out_shape=(jax.ShapeDtypeStruct((N, L, MIX_W), F32),
                   jax.ShapeDtypeStruct((N, 1024, 64), F32)),
        grid=(N // NBLK, T),
        in_specs=([blk(0)] * N_RA_OUT
                  + [blk(P1_ZD // 1024),
                     pl.BlockSpec((1, MIX_W), lambda i, t: (0, 0)),
                     pl.BlockSpec((1, MIX_W), lambda i, t: (0, 0)),
                     s_spec]),
        out_specs=(blk(0), s_spec),
        scratch_shapes=[pltpu.VMEM((4 * NBLK, 256, 256), F32),
                        pltpu.VMEM((NBLK, TLB, MIX_W), F32)],
        compiler_params=_cp(("arbitrary", "arbitrary")),
    )(*(ra3 + [proj1.reshape(N, L, P1_N), ln_g, ln_b, s0]))
    return y.reshape(N * L, MIX_W), s_new


def _regroup_w1(w1):
    wt = w1.T
    return jnp.concatenate([wt[0:4096], wt[4104:8200], wt[8328:9352], wt[8200:8328], wt[4096:4104],
                            jnp.zeros((P1_N - 9352, D_MODEL), F32)], axis=0).astype(BF16)


def _s5_weights(lam_re, lam_im, log_dt, b_re, b_im, c_re, c_im):
    dt = jnp.exp(log_dt)[:, None]
    mag = jnp.exp(lam_re * dt)
    ar = mag * jnp.cos(lam_im * dt)
    ai = mag * jnp.sin(lam_im * dt)
    den = lam_re * lam_re + lam_im * lam_im
    qr = ((ar - 1.0) * lam_re + ai * lam_im) / den
    qi = (ai * lam_re - (ar - 1.0) * lam_im) / den
    bbr = qr[..., None] * b_re - qi[..., None] * b_im
    bbi = qr[..., None] * b_im + qi[..., None] * b_re
    hp = lax.Precision.HIGHEST
    spread_p = (jnp.arange(1024)[None, :] % S5_STATE == jnp.arange(S5_STATE)[:, None]).astype(F32)
    spread_h = (jnp.arange(256)[None, :] % S5_GROUP == jnp.arange(S5_GROUP)[:, None]).astype(F32)
    grp_in = (jnp.arange(256)[:, None] // S5_GROUP == jnp.arange(1024)[None, :] // S5_STATE)
    grp_out = (jnp.arange(1024)[:, None] // S5_STATE == jnp.arange(256)[None, :] // S5_GROUP)

    def in_blocks(bb):
        rows = bb.transpose(0, 2, 1).reshape(4, 256, S5_STATE)
        return jnp.matmul(rows, spread_p, precision=hp) * grp_in

    def out_blocks(cc):
        rows = cc.transpose(0, 2, 1).reshape(4, 1024, S5_GROUP)
        return jnp.matmul(rows, spread_h, precision=hp) * grp_out

    wb = jnp.concatenate([in_blocks(bbr), in_blocks(bbi)], axis=2).astype(BF16)
    wc = jnp.concatenate([out_blocks(c_re), out_blocks(-c_im)], axis=1).astype(BF16)
    pr, pi = jnp.ones_like(ar), jnp.zeros_like(ai)
    lag = []
    for _ in range(8):
        cpr = c_re * pr[:, None, :] - c_im * pi[:, None, :]
        cpi = c_re * pi[:, None, :] + c_im * pr[:, None, :]
        lag.append(jnp.einsum('gop,gph->goh', cpr, bbr, precision=lax.Precision.HIGHEST)
                   - jnp.einsum('gop,gph->goh', cpi, bbi, precision=lax.Precision.HIGHEST))
        pr, pi = pr * ar - pi * ai, pr * ai + pi * ar
    kd = jnp.stack(lag).reshape(8, 4, 16, S5_GROUP, S5_GROUP)
    kd = kd.transpose(1, 0, 2, 4, 3).reshape(4, 8, 256, S5_GROUP)
    spread = (jnp.arange(256)[None, :] % S5_GROUP == jnp.arange(S5_GROUP)[:, None]).astype(F32)
    same_group = (jnp.arange(256)[:, None] // S5_GROUP == jnp.arange(256)[None, :] // S5_GROUP)
    wk = jnp.matmul(kd, spread, precision=lax.Precision.HIGHEST) * same_group
    wk = wk.reshape(4, 8 * 256, 256).astype(BF16)
    return ar.reshape(1, 4096), ai.reshape(1, 4096), wb, wk, wc


CFG = {
    "P": dict(N=BATCH, L=P_LEN, tm_mm=1376, tm=688, tm_ln=688, drop_meta=True,
              conv=dict(NB=1, TL=344), s5=dict(NB=1, TL=688), mlstm=dict(NB=1, TL=344),
              ra=dict(RB=192, CT=16, U=12), rb=dict(NBLK=4, TLB=48, CT=16)),
    "S": dict(N=DEC_BATCH, L=DEC_SEQ, tm_mm=1024, tm=512, tm_ln=256, drop_meta=False,
              conv=dict(NB=16, TL=8), s5=dict(NB=32, TL=8), mlstm=dict(NB=4, TL=8),
              ra=dict(RB=256, CT=8, U=8), rb=dict(NBLK=8, TLB=8, CT=8)),
}


def _trunk(x, st, w, cfg):
    n, l = cfg["N"], cfg["L"]
    proj0 = _matmul(x, w["w_in0"], cfg["tm_mm"], 1024)
    act, conv_new = _conv_call(proj0, st["conv"], w["conv_w"], w["conv_b"], w["a_ln_g"], w["a_ln_b"],
                               N=n, L=l, **cfg["conv"])
    mix_a = _pw_gate(act, w["pw"], proj0, tm=cfg["tm"])
    yb, xr, xi = _s5_call(proj0, w["s5_wb"], w["s5_wk"], w["s5_wc"], w["s5_d"], w["s5_ar"], w["s5_ai"],
                          st["ssm_re"].reshape(n, 1, 4096), st["ssm_im"].reshape(n, 1, 4096),
                          N=n, L=l, **cfg["s5"])
    mix_b = _glu_gate(yb, w["glu_w"], w["glu_b"], proj0, tm=cfg["tm"])
    x1 = _out_ln(x, mix_a, mix_b, w["w_out0"], w["ln_g0"], w["ln_b0"], tm=cfg["tm_ln"])

    proj1 = _matmul_nt(x1, w["w_in1"], cfg["tm_mm"], 512)
    m0 = jnp.pad(jnp.broadcast_to(st["m"][:, :, None], (n, M_HEADS, 128)), ((0, 0), (0, 4), (0, 0)))
    mix_c, c_new, n_new, m_new = _mlstm_call(proj1, w["gate_b"], w["hn_g"], st["c"], st["n"], m0,
                                             N=n, L=l, **cfg["mlstm"])
    sh = st["shift"]
    if l == cfg["ra"]["CT"]:
        sh = jnp.repeat(sh, l, axis=0)
    ra, shift_new = _rwkv_a_call(proj1, sh[:, :3072], sh[:, 3072:], w["rwkv"], N=n, L=l, **cfg["ra"])
    mix_d, s_new = _rwkv_b_call(ra, proj1, w["r_ln_g"], w["r_ln_b"], st["s"].reshape(n, 1024, 64),
                                N=n, L=l, **cfg["rb"])
    final_ln = _out_ln_prompt if cfg["drop_meta"] else functools.partial(_out_ln, tm=cfg["tm_ln"])
    y = final_ln(x1, mix_c, mix_d, w["w_out1"], w["ln_g1"], w["ln_b1"])

    states = (conv_new[None],
              xr.reshape(n, S5_GROUPS, S5_STATE)[None],
              xi.reshape(n, S5_GROUPS, S5_STATE)[None],
              c_new[None], n_new[None], m_new[:, :M_HEADS, 0][None],
              s_new.reshape(n, R_HEADS, R_K, R_K)[None],
              shift_new[None])
    return y, states


def kernel(x_prompt, x_sample, state_conv, state_ssm_re, state_ssm_im, state_mlstm_c, state_mlstm_n, state_mlstm_m, state_rwkv_s, state_rwkv_shift, meta_tokens, ev_w_in, a_conv_w, a_conv_b, a_ln_g, a_ln_b, a_pw, s5_lambda_re, s5_lambda_im, s5_log_dt, s5_b_re, s5_b_im, s5_c_re, s5_c_im, s5_d, s5_glu_w, s5_glu_b, ev_w_out, ev_ln_g, ev_ln_b, od_w_in, m_ig_b, m_fg_b, m_hn_g, r_mu, r_w0, r_w2, r_a0, r_a2, r_kk, r_ka, r_rk, r_ln_g, r_ln_b, od_w_out, od_ln_g, od_ln_b):
    nb = x_prompt.shape[0]
    row = lambda vec: vec.reshape(1, -1)
    zeros = lambda *s: jnp.zeros(s, F32)

    ar, ai, wb, wk, wc = _s5_weights(s5_lambda_re[0], s5_lambda_im[0], s5_log_dt[0], s5_b_re[0], s5_b_im[0],
                                     s5_c_re[0], s5_c_im[0])
    w_in1 = _regroup_w1(od_w_in[0])
    mu = r_mu[0]
    w = dict(
        w_in0=ev_w_in[0].astype(BF16), conv_w=a_conv_w[0], conv_b=row(a_conv_b[0]),
        a_ln_g=row(a_ln_g[0]), a_ln_b=row(a_ln_b[0]), pw=a_pw[0].astype(BF16),
        s5_wb=wb, s5_wk=wk, s5_wc=wc, s5_d=row(s5_d[0]), s5_ar=ar, s5_ai=ai,
        glu_w=s5_glu_w[0].astype(BF16), glu_b=row(s5_glu_b[0]),
        w_out0=ev_w_out[0].astype(BF16), ln_g0=row(ev_ln_g[0]), ln_b0=row(ev_ln_b[0]),
        w_in1=w_in1,
        gate_b=jnp.concatenate([m_ig_b[0], m_fg_b[0], jnp.zeros((120,), F32)]).reshape(1, 128),
        hn_g=row(m_hn_g[0]),
        rwkv=[row(mu[:3072]), row(mu[3072:]), row(r_w0[0]),
              jnp.concatenate([r_w2[0], jnp.zeros((64, MIX_W), F32)], axis=0).astype(BF16),
              row(r_a0[0]),
              jnp.concatenate([jnp.zeros((64, MIX_W), F32), r_a2[0]], axis=0).astype(BF16),
              row(r_kk[0]), row(r_ka[0]), row(r_rk[0])],
        r_ln_g=row(r_ln_g[0]), r_ln_b=row(r_ln_b[0]),
        w_out1=od_w_out[0].astype(BF16), ln_g1=row(od_ln_g[0]), ln_b1=row(od_ln_b[0]),
    )

    x_p = jnp.concatenate([jnp.broadcast_to(meta_tokens[None], (nb, N_META, D_MODEL)), x_prompt],
                          axis=1).reshape(nb * P_LEN, D_MODEL)
    st_p = dict(conv=zeros(nb, CONV_W - 1, MIX_W), ssm_re=zeros(nb, 4096), ssm_im=zeros(nb, 4096),
                c=zeros(nb, M_HEADS, M_D, M_D), n=zeros(nb, M_HEADS, M_D), m=zeros(nb, M_HEADS),
                s=zeros(nb, R_HEADS, R_K, R_K), shift=zeros(nb, 3200))
    y_p, states_p = _trunk(x_p, st_p, w, CFG["P"])

    st_s = dict(conv=state_conv[0], ssm_re=state_ssm_re[0], ssm_im=state_ssm_im[0],
                c=state_mlstm_c[0], n=state_mlstm_n[0], m=state_mlstm_m[0],
                s=state_rwkv_s[0], shift=state_rwkv_shift[0])
    y_s, states_s = _trunk(x_sample.reshape(DEC_BATCH * DEC_SEQ, D_MODEL), st_s, w, CFG["S"])

    y_prompt = y_p
    y_sample = y_s.reshape(DEC_BATCH, DEC_SEQ, D_MODEL)
    return (y_prompt, y_sample) + states_p + states_s
```

```python
import functools
import math

import jax
import jax.numpy as jnp
from jax import lax
from jax.experimental import pallas as pl
from jax.experimental.pallas import tpu as pltpu

F32 = jnp.float32
BF16 = jnp.bfloat16

D_MODEL = 2048
MIX_W = 1024
N_META = 16
CONV_W = 31
S5_GROUP = 16
S5_GROUPS = 64
S5_STATE = 64
M_HEADS = 4
M_D = 256
R_HEADS = 16
R_K = 64
LN_EPS = 1e-5
R_LN_EPS = 64e-5
DEPTH = 2
ALPHA = (2 * DEPTH) ** 0.25

BATCH = 4
SEQ = 2048
P_LEN = N_META + SEQ
DEC_BATCH = 128
DEC_SEQ = 8

P1_Q, P1_K, P1_V, P1_O, P1_ZC = 0, 1024, 2048, 3072, 4096
P1_R, P1_RK, P1_RV, P1_ZD, P1_WA, P1_GATE = 5120, 6144, 7168, 8192, 9216, 9344
P1_N = 9728

VMEM_LIMIT = 48 * 1024 * 1024


def _cp(sem):
    return pltpu.CompilerParams(dimension_semantics=sem, vmem_limit_bytes=VMEM_LIMIT)


def _dot(a, b):
    return jnp.dot(a, b, preferred_element_type=F32)


def _dot_nt(a, b):
    return lax.dot_general(a, b, (((1,), (1,)), ((), ())), preferred_element_type=F32)


def _dot_tn(a, b):
    return lax.dot_general(a, b, (((0,), (0,)), ((), ())), preferred_element_type=F32)


def _split2(x):
    hi = x.astype(BF16)
    lo = (x - hi.astype(F32)).astype(BF16)
    return hi, lo


def _split3(x):
    hi = x.astype(BF16)
    r1 = x - hi.astype(F32)
    mid = r1.astype(BF16)
    lo = (r1 - mid.astype(F32)).astype(BF16)
    return hi, mid, lo


def _sigmoid(x):
    return jax.nn.sigmoid(x)


def _silu(x):
    return x * jax.nn.sigmoid(x)


def _softplus(x):
    return jnp.maximum(x, 0.0) + jnp.log(1.0 + jnp.exp(-jnp.abs(x)))


def _gelu_tanh(x):
    c = math.sqrt(2.0 / math.pi)
    return x * (0.5 * (1.0 + jnp.tanh(c * (x + 0.044715 * (x * x * x)))))


def _iota(shape, axis):
    return lax.broadcasted_iota(jnp.int32, shape, axis)


def _shr(x, k):
    return lax.shift_right_logical(x, jnp.int32(k))


def _log2(n):
    k = int(round(math.log2(n)))
    assert 1 << k == n
    return k


def _block_ones(n, seg, dtype):
    r = _shr(_iota((n, n), 0), _log2(seg))
    c = _shr(_iota((n, n), 1), _log2(seg))
    return (r == c).astype(dtype)


def _segsum(x, seg):
    g = _block_ones(256, seg, BF16)
    outs = []
    for j in range(x.shape[1] // 256):
        hi, lo = _split2(x[:, 256 * j:256 * (j + 1)])
        outs.append(_dot(hi, g) + _dot(lo, g))
    return jnp.concatenate(outs, axis=1)


def _row_cumsum(x, period):
    rows = x.shape[0]
    rid = _iota(x.shape, 0)
    if period < rows:
        rid = jnp.bitwise_and(rid, period - 1)
    d = 1
    while d < min(period, rows):
        x = x + jnp.where(rid >= d, pltpu.roll(x, d, 0), 0.0)
        d *= 2
    return x


def _row_cummax(x, period):
    rows = x.shape[0]
    rid = _iota(x.shape, 0)
    if period < rows:
        rid = jnp.bitwise_and(rid, period - 1)
    d = 1
    while d < min(period, rows):
        x = jnp.maximum(x, jnp.where(rid >= d, pltpu.roll(x, d, 0), -jnp.inf))
        d *= 2
    return x


def _mm_kernel(x_ref, w_ref, o_ref):
    o_ref[...] = _dot(x_ref[...].astype(BF16), w_ref[...])


def _matmul(x, w, tm, tn):
    r, k = x.shape
    n = w.shape[1]
    return pl.pallas_call(
        _mm_kernel,
        out_shape=jax.ShapeDtypeStruct((r, n), F32),
        grid=(pl.cdiv(r, tm), n // tn),
        in_specs=[pl.BlockSpec((tm, k), lambda i, j: (i, 0)),
                  pl.BlockSpec((k, tn), lambda i, j: (0, j))],
        out_specs=pl.BlockSpec((tm, tn), lambda i, j: (i, j)),
        compiler_params=_cp(("parallel", "arbitrary")),
    )(x, w)


def _mm_nt_kernel(x_ref, w_ref, o_ref):
    o_ref[...] = _dot_nt(x_ref[...].astype(BF16), w_ref[...])


def _matmul_nt(x, w_t, tm, tn):
    r, k = x.shape
    n = w_t.shape[0]
    return pl.pallas_call(
        _mm_nt_kernel,
        out_shape=jax.ShapeDtypeStruct((r, n), F32),
        grid=(pl.cdiv(r, tm), n // tn),
        in_specs=[pl.BlockSpec((tm, k), lambda i, j: (i, 0)),
                  pl.BlockSpec((tn, k), lambda i, j: (j, 0))],
        out_specs=pl.BlockSpec((tm, tn), lambda i, j: (i, j)),
        compiler_params=_cp(("parallel", "arbitrary")),
    )(x, w_t)


def _pw_kernel(a_ref, w_ref, z_ref, o_ref):
    o_ref[...] = _dot(a_ref[...].astype(BF16), w_ref[...]) * _silu(z_ref[...])


def _pw_gate(act, pw, proj0, tm, tn=512):
    r = act.shape[0]
    zb = 2048 // tn
    return pl.pallas_call(
        _pw_kernel,
        out_shape=jax.ShapeDtypeStruct((r, MIX_W), F32),
        grid=(pl.cdiv(r, tm), MIX_W // tn),
        in_specs=[pl.BlockSpec((tm, MIX_W), lambda i, j: (i, 0)),
                  pl.BlockSpec((MIX_W, tn), lambda i, j: (0, j)),
                  pl.BlockSpec((tm, tn), lambda i, j: (i, zb + j))],
        out_specs=pl.BlockSpec((tm, tn), lambda i, j: (i, j)),
        compiler_params=_cp(("parallel", "arbitrary")),
    )(act, pw, proj0)


def _glu_kernel(y_ref, wv_ref, wg_ref, bv_ref, bg_ref, z_ref, o_ref):
    y = y_ref[...].astype(BF16)
    v = _dot(y, wv_ref[...]) + bv_ref[...]
    g = _dot(y, wg_ref[...]) + bg_ref[...]
    o_ref[...] = v * _sigmoid(g) * _silu(z_ref[...])


def _glu_gate(yb, glu_w, glu_b, proj0, tm, tn=512):
    r = yb.shape[0]
    nb = MIX_W // tn
    zb = 4096 // tn
    return pl.pallas_call(
        _glu_kernel,
        out_shape=jax.ShapeDtypeStruct((r, MIX_W), F32),
        grid=(pl.cdiv(r, tm), nb),
        in_specs=[pl.BlockSpec((tm, MIX_W), lambda i, j: (i, 0)),
                  pl.BlockSpec((MIX_W, tn), lambda i, j: (0, j)),
                  pl.BlockSpec((MIX_W, tn), lambda i, j: (0, nb + j)),
                  pl.BlockSpec((1, tn), lambda i, j: (0, j)),
                  pl.BlockSpec((1, tn), lambda i, j: (0, nb + j)),
                  pl.BlockSpec((tm, tn), lambda i, j: (i, zb + j))],
        out_specs=pl.BlockSpec((tm, tn), lambda i, j: (i, j)),
        compiler_params=_cp(("parallel", "arbitrary")),
    )(yb, glu_w, glu_w, glu_b, glu_b, proj0)


def _out_ln_kernel(x_ref, ma_ref, mb_ref, wa_ref, wb_ref, g_ref, b_ref, o_ref):
    out = _dot(ma_ref[...].astype(BF16), wa_ref[...]) + _dot(mb_ref[...].astype(BF16), wb_ref[...])
    y = ALPHA * x_ref[...] + out
    mu = jnp.mean(y, axis=-1, keepdims=True)
    yc = y - mu
    var = jnp.mean(yc * yc, axis=-1, keepdims=True)
    o_ref[...] = yc * lax.rsqrt(var + LN_EPS) * g_ref[...] + b_ref[...]


def _out_ln(x, mix_a, mix_b, w_out, ln_g, ln_b, tm):
    r = x.shape[0]
    return pl.pallas_call(
        _out_ln_kernel,
        out_shape=jax.ShapeDtypeStruct((r, D_MODEL), F32),
        grid=(pl.cdiv(r, tm),),
        in_specs=[pl.BlockSpec((tm, D_MODEL), lambda i: (i, 0)),
                  pl.BlockSpec((tm, MIX_W), lambda i: (i, 0)),
                  pl.BlockSpec((tm, MIX_W), lambda i: (i, 0)),
                  pl.BlockSpec((MIX_W, D_MODEL), lambda i: (0, 0), pipeline_mode=pl.Buffered(1)),
                  pl.BlockSpec((MIX_W, D_MODEL), lambda i: (1, 0), pipeline_mode=pl.Buffered(1)),
                  pl.BlockSpec((1, D_MODEL), lambda i: (0, 0)),
                  pl.BlockSpec((1, D_MODEL), lambda i: (0, 0))],
        out_specs=pl.BlockSpec((tm, D_MODEL), lambda i: (i, 0)),
        compiler_params=_cp(("parallel",)),
    )(x, mix_a, mix_b, w_out, w_out, ln_g, ln_b)


def _out_ln_prompt(x, mix_a, mix_b, w_out, ln_g, ln_b, tm=512):
    tiles = SEQ // tm

    def rows(width):
        return pl.BlockSpec((pl.Element(tm), pl.Element(width)),
                            lambda n, t: (pl.multiple_of(n * P_LEN + N_META + t * tm, 8), 0))

    return pl.pallas_call(
        _out_ln_kernel,
        out_shape=jax.ShapeDtypeStruct((BATCH * SEQ, D_MODEL), F32),
        grid=(BATCH, tiles),
        in_specs=[rows(D_MODEL), rows(MIX_W), rows(MIX_W),
                  pl.BlockSpec((MIX_W, D_MODEL), lambda n, t: (0, 0), pipeline_mode=pl.Buffered(1)),
                  pl.BlockSpec((MIX_W, D_MODEL), lambda n, t: (1, 0), pipeline_mode=pl.Buffered(1)),
                  pl.BlockSpec((1, D_MODEL), lambda n, t: (0, 0)),
                  pl.BlockSpec((1, D_MODEL), lambda n, t: (0, 0))],
        out_specs=pl.BlockSpec((tm, D_MODEL), lambda n, t: (n * tiles + t, 0)),
        compiler_params=_cp(("parallel", "arbitrary")),
    )(x, mix_a, mix_b, w_out, w_out, ln_g, ln_b).reshape(BATCH, SEQ, D_MODEL)


def _conv_kernel(u_ref, g_ref, st_ref, w_ref, cb_ref, lg_ref, lb_ref, act_ref, nst_ref, hp_ref, hs_ref, wb_ref,
                 *, NB, TL, T):
    t = pl.program_id(1)
    for j in range(CONV_W):
        wb_ref[j] = jnp.broadcast_to(w_ref[j:j + 1, :], (8, MIX_W))
    for nb in range(NB):
        base = nb * TL

        @pl.when(t == 0)
        def _():
            hp_ref[nb, 0:2, :] = jnp.zeros((2, MIX_W), F32)
            hp_ref[nb, 2:32, :] = st_ref[nb]

        hp_ref[nb, TL + 32:TL + 40, :] = jnp.zeros((8, MIX_W), F32)
        hp_ref[nb, 32:32 + TL, :] = u_ref[base:base + TL, :] * _sigmoid(g_ref[base:base + TL, :])
        for b in range(1, 8):
            hs_ref[b - 1] = hp_ref[nb, b:b + TL + 32, :]

        def taps(r0, groups):
            acc = [None] * groups
            for j in range(CONV_W):
                o = j + 2
                wj = wb_ref[j]
                for g in range(groups):
                    rows = pl.ds(r0 + 8 * (o // 8 + g), 8)
                    term = wj * (hs_ref[o % 8 - 1, rows, :] if o % 8 else hp_ref[nb, rows, :])
                    acc[g] = term if acc[g] is None else acc[g] + term
            for g in range(groups):
                act_ref[pl.ds(base + r0 + 8 * g, 8), :] = acc[g]

        def chunk(c, carry):
            taps(pl.multiple_of(c * 16, 16), 2)
            return carry

        lax.fori_loop(0, TL // 16, chunk, 0)
        if TL % 16:
            taps(TL - 8, 1)

        @pl.when(t == T - 1)
        def _():
            nst_ref[nb] = hp_ref[nb, TL + 2:TL + 32, :]

        if T > 1:
            hp_ref[nb, 0:32, :] = hp_ref[nb, TL:TL + 32, :]

    y = act_ref[...] + cb_ref[...]
    mu = jnp.mean(y, axis=-1, keepdims=True)
    yc = y - mu
    var = jnp.mean(yc * yc, axis=-1, keepdims=True)
    act_ref[...] = _silu(yc * lax.rsqrt(var + LN_EPS) * lg_ref[...] + lb_ref[...])


def _conv_call(proj0, state, conv_w, conv_b, ln_g, ln_b, *, N, L, NB, TL):
    T = L // TL
    RB = NB * TL
    assert NB == 1 or T == 1
    return pl.pallas_call(
        functools.partial(_conv_kernel, NB=NB, TL=TL, T=T),
        out_shape=(jax.ShapeDtypeStruct((N * L, MIX_W), F32),
                   jax.ShapeDtypeStruct((N, CONV_W - 1, MIX_W), F32)),
        grid=(N // NB, T),
        in_specs=[pl.BlockSpec((RB, MIX_W), lambda i, t: (i * T + t, 0)),
                  pl.BlockSpec((RB, MIX_W), lambda i, t: (i * T + t, 1)),
                  pl.BlockSpec((NB, CONV_W - 1, MIX_W), lambda i, t: (i, 0, 0)),
                  pl.BlockSpec((CONV_W, MIX_W), lambda i, t: (0, 0)),
                  pl.BlockSpec((1, MIX_W), lambda i, t: (0, 0)),
                  pl.BlockSpec((1, MIX_W), lambda i, t: (0, 0)),
                  pl.BlockSpec((1, MIX_W), lambda i, t: (0, 0))],
        out_specs=(pl.BlockSpec((RB, MIX_W), lambda i, t: (i * T + t, 0)),
                   pl.BlockSpec((NB, CONV_W - 1, MIX_W), lambda i, t: (i, 0, 0))),
        scratch_shapes=[pltpu.VMEM((NB, TL + 40, MIX_W), F32),
                        pltpu.VMEM((7, TL + 32, MIX_W), F32),
                        pltpu.VMEM((CONV_W, 8, MIX_W), F32)],
        compiler_params=_cp(("arbitrary", "arbitrary")),
    )(proj0, proj0, state, conv_w, conv_b, ln_g, ln_b)


def _s5_kernel(u_ref, wb_ref, wk_ref, wc_ref, d_ref, ar_ref, ai_ref, x0r_ref, x0i_ref,
               y_ref, xfr_ref, xfi_ref, xs_ref, cr_ref, ci_ref, *, NB, TL, T):
    t = pl.program_id(2)
    RB = NB * TL
    GL = TL // 8
    u = u_ref[...]
    ub = u.astype(BF16)
    big = _dot(ub, wb_ref[0])
    xs_ref[0] = big[:, :1024]
    xs_ref[1] = big[:, 1024:]
    rid = jnp.bitwise_and(_iota((RB, 256), 0), 7)
    lags = [ub] + [jnp.where(rid >= d, pltpu.roll(u, d, 0), 0.0).astype(BF16) for d in range(1, 8)]
    y_loc = _dot(jnp.concatenate(lags, axis=1), wk_ref[0])
    ar = ar_ref[...]
    ai = ai_ref[...]

    def cmul(pr, pi, qr, qi):
        return pr * qr - pi * qi, pr * qi + pi * qr

    a1 = (ar, ai)
    a2 = cmul(*a1, *a1)
    a4 = cmul(*a2, *a2)
    a3 = cmul(*a2, *a1)
    a5 = cmul(*a4, *a1)
    a6 = cmul(*a4, *a2)
    a7 = cmul(*a6, *a1)
    a8 = cmul(*a4, *a4)
    a0 = (jnp.ones_like(ar), jnp.zeros_like(ai))
    r8 = _iota((8, 1024), 0)

    def table(powers):
        tr = jnp.zeros((8, 1024), F32)
        ti = jnp.zeros((8, 1024), F32)
        for k, (pr, pi) in enumerate(powers):
            tr = jnp.where(r8 == k, pr, tr)
            ti = jnp.where(r8 == k, pi, ti)
        return tr, ti

    pwr, pwi = table((a1, a2, a3, a4, a5, a6, a7, a8))
    qwr, qwi = table((a7, a6, a5, a4, a3, a2, a1, a0))
    a8r, a8i = a8

    first = t == 0

    def seq_body(nb, carry0):
        x0r = x0r_ref[nb]
        x0i = x0i_ref[nb]
        if T > 1:
            c_r = jnp.where(first, x0r, cr_ref[0:1, :])
            c_i = jnp.where(first, x0i, ci_ref[0:1, :])
        else:
            c_r, c_i = x0r, x0i

        def grp(g, c):
            c_r, c_i = c
            off = pl.multiple_of(nb * TL + g * 8, 8)
            vr = xs_ref[0, pl.ds(off, 8), :]
            vi = xs_ref[1, pl.ds(off, 8), :]
            er = jnp.sum(qwr * vr - qwi * vi, axis=0, keepdims=True)
            ei = jnp.sum(qwr * vi + qwi * vr, axis=0, keepdims=True)
            br = jnp.broadcast_to(c_r, (8, 1024))
            bi = jnp.broadcast_to(c_i, (8, 1024))
            xs_ref[0, pl.ds(off, 8), :] = pwr * br - pwi * bi
            xs_ref[1, pl.ds(off, 8), :] = pwr * bi + pwi * br
            return a8r * c_r - a8i * c_i + er, a8r * c_i + a8i * c_r + ei

        c_r, c_i = lax.fori_loop(0, GL, grp, (c_r, c_i))
        if T > 1:
            cr_ref[...] = jnp.broadcast_to(c_r, (8, 1024))
            ci_ref[...] = jnp.broadcast_to(c_i, (8, 1024))

        @pl.when(t == T - 1)
        def _():
            xfr_ref[nb] = c_r
            xfi_ref[nb] = c_i

        return carry0

    lax.fori_loop(0, NB, seq_body, 0)
    y = (_dot(xs_ref[0].astype(BF16), wc_ref[0, 0:1024, :])
         + _dot(xs_ref[1].astype(BF16), wc_ref[0, 1024:2048, :]))
    y_ref[...] = _gelu_tanh(y + y_loc + d_ref[...] * u)


def _s5_call(proj0, wb, wk, wc, dvec, ar, ai, x0r, x0i, *, N, L, NB, TL):
    T = L // TL
    RB = NB * TL
    assert NB == 1 or T == 1
    ub = 3072 // 256
    st = jax.ShapeDtypeStruct((N, 1, 4096), F32)
    return pl.pallas_call(
        functools.partial(_s5_kernel, NB=NB, TL=TL, T=T),
        out_shape=(jax.ShapeDtypeStruct((N * L, MIX_W), F32), st, st),
        grid=(N // NB, 4, T),
        in_specs=[pl.BlockSpec((RB, 256), lambda i, j, t: (i * T + t, ub + j)),
                  pl.BlockSpec((1, 256, 2048), lambda i, j, t: (j, 0, 0)),
                  pl.BlockSpec((1, 2048, 256), lambda i, j, t: (j, 0, 0)),
                  pl.BlockSpec((1, 2048, 256), lambda i, j, t: (j, 0, 0)),
                  pl.BlockSpec((1, 256), lambda i, j, t: (0, j)),
                  pl.BlockSpec((1, 1024), lambda i, j, t: (0, j)),
                  pl.BlockSpec((1, 1024), lambda i, j, t: (0, j)),
                  pl.BlockSpec((NB, 1, 1024), lambda i, j, t: (i, 0, j)),
                  pl.BlockSpec((NB, 1, 1024), lambda i, j, t: (i, 0, j))],
        out_specs=(pl.BlockSpec((RB, 256), lambda i, j, t: (i * T + t, j)),
                   pl.BlockSpec((NB, 1, 1024), lambda i, j, t: (i, 0, j)),
                   pl.BlockSpec((NB, 1, 1024), lambda i, j, t: (i, 0, j))),
        scratch_shapes=[pltpu.VMEM((2, RB, 1024), F32),
                        pltpu.VMEM((8, 1024), F32),
                        pltpu.VMEM((8, 1024), F32)],
        compiler_params=_cp(("arbitrary", "arbitrary", "arbitrary")),
    )(proj0, wb, wk, wc, dvec, ar, ai, x0r, x0i)


def _mlstm_kernel(q_ref, k_ref, v_ref, o_ref, z_ref, gt_ref, gb_ref, hg_ref, c0_ref, n0_ref, m0_ref,
                  y_ref, c_ref, n_ref, m_ref, cs_ref, ns_ref, ms_ref, *, NB, TL, T):
    for nb in range(NB):
        rows = lambda ref: ref.at[pl.ds(nb * TL, TL)]
        one = lambda ref: ref.at[pl.ds(nb, 1)]
        _mlstm_seq(rows(q_ref), rows(k_ref), rows(v_ref), rows(o_ref), rows(z_ref), rows(gt_ref), gb_ref, hg_ref,
                   one(c0_ref), one(n0_ref), one(m0_ref), rows(y_ref), one(c_ref), one(n_ref), one(m_ref),
                   cs_ref.at[nb], ns_ref.at[nb], ms_ref.at[nb], TL=TL, T=T)


def _mlstm_seq(q_ref, k_ref, v_ref, o_ref, z_ref, gt_ref, gb_ref, hg_ref, c0_ref, n0_ref, m0_ref,
               y_ref, c_ref, n_ref, m_ref, cs_ref, ns_ref, ms_ref, *, TL, T):
    t = pl.program_id(1)

    @pl.when(t == 0)
    def _():
        cs_ref[...] = c0_ref[0]
        ns_ref[...] = n0_ref[0]
        ms_ref[...] = m0_ref[0]

    G = gt_ref[...] + gb_ref[...]
    B = _row_cumsum(-_softplus(-G), TL)
    Bs = pltpu.roll(B, 124, 1)
    A = G - Bs
    CM = _row_cummax(A, TL)
    ms = ms_ref[...]
    dg = _iota((8, 128), 0) == _iota((8, 128), 1)
    mrow = jnp.sum(jnp.where(dg, ms, 0.0), axis=0, keepdims=True)
    M = jnp.maximum(mrow, CM)
    MT = Bs + M
    sel = dg.astype(BF16)
    a_hi, a_mid, a_lo = _split3(A)
    Arow = _dot_nt(sel, a_hi) + _dot_nt(sel, a_mid) + _dot_nt(sel, a_lo)
    causal = _iota((TL, TL), 0) >= _iota((TL, TL), 1)
    H = range(M_HEADS)
    sl = [slice(M_D * h, M_D * (h + 1)) for h in H]
    q = [q_ref[:, sl[h]] * (M_D ** -0.5) for h in H]
    qb = [x.astype(BF16) for x in q]
    kf = [k_ref[:, sl[h]] for h in H]
    kb = [x.astype(BF16) for x in kf]
    vf = [v_ref[:, sl[h]] for h in H]
    c_old = [cs_ref[h] for h in H]
    n_old = [ns_ref[h:h + 1, :] for h in H]
    m_col = [M[:, h:h + 1] for h in H]
    mt_col = [MT[:, h:h + 1] for h in H]
    b_col = [Bs[:, h:h + 1] for h in H]
    m_prev = [mrow[:, h:h + 1] for h in H]
    dm = [jnp.exp(jnp.where(causal, Arow[h:h + 1, :] - m_col[h], -jnp.inf)) for h in H]
    s = [_dot_nt(qb[h], kb[h]) * dm[h] for h in H]
    inter = [jnp.exp(m_prev[h] - m_col[h]) for h in H]
    h_intra = [_dot(s[h].astype(BF16), vf[h].astype(BF16)) for h in H]
    h_inter = [_dot(qb[h], c_old[h].astype(BF16)) * inter[h] for h in H]
    n_all = [jnp.sum(s[h], axis=1, keepdims=True) + jnp.sum(q[h] * n_old[h], axis=1, keepdims=True) * inter[h]
             for h in H]
    hh = [(h_intra[h] + h_inter[h]) / jnp.maximum(jnp.abs(n_all[h]), jnp.exp(-mt_col[h])) for h in H]
    m_new = [mt_col[h][TL - 1:TL, :] for h in H]
    b_end = [b_col[h][TL - 1:TL, :] for h in H]
    dec = [jnp.exp(m_prev[h] + b_end[h] - m_new[h]) for h in H]
    w_s = [jnp.exp(b_end[h] - b_col[h] + G[:, h:h + 1] - m_new[h]) for h in H]
    c_new = [c_old[h] * dec[h] + _dot_tn(kb[h], (vf[h] * w_s[h]).astype(BF16)) for h in H]
    n_new = [n_old[h] * dec[h] + jnp.sum(kf[h] * w_s[h], axis=0, keepdims=True) for h in H]
    outs = []
    for h in H:
        mu = jnp.mean(hh[h], axis=-1, keepdims=True)
        hc = hh[h] - mu
        var = jnp.mean(hc * hc, axis=-1, keepdims=True)
        hn = hc * lax.rsqrt(var + LN_EPS) * hg_ref[:, sl[h]]
        outs.append(hn * _sigmoid(o_ref[:, sl[h]]) * _silu(z_ref[:, sl[h]]))
    for h in H:
        cs_ref[h] = c_new[h]
        ns_ref[h:h + 1, :] = n_new[h]
        ms_ref[h:h + 1, :] = jnp.broadcast_to(m_new[h], (1, 128))
        y_ref[:, sl[h]] = outs[h]

    @pl.when(t == T - 1)
    def _():
        c_ref[0] = cs_ref[...]
        n_ref[0] = ns_ref[...]
        m_ref[0] = ms_ref[...]


def _mlstm_call(proj1, gate_b, hn_g, c0, n0, m0, *, N, L, NB, TL):
    T = L // TL
    RB = NB * TL
    assert NB == 1 or T == 1

    def col(cb):
        return pl.BlockSpec((RB, MIX_W), lambda i, t: (i * T + t, cb))

    st_specs = (pl.BlockSpec((NB, M_HEADS, M_D, M_D), lambda i, t: (i, 0, 0, 0)),
                pl.BlockSpec((NB, M_HEADS, M_D), lambda i, t: (i, 0, 0)),
                pl.BlockSpec((NB, 8, 128), lambda i, t: (i, 0, 0)))
    return pl.pallas_call(
        functools.partial(_mlstm_kernel, NB=NB, TL=TL, T=T),
        out_shape=(jax.ShapeDtypeStruct((N * L, MIX_W), F32),
                   jax.ShapeDtypeStruct((N, M_HEADS, M_D, M_D), F32),
                   jax.ShapeDtypeStruct((N, M_HEADS, M_D), F32),
                   jax.ShapeDtypeStruct((N, 8, 128), F32)),
        grid=(N // NB, T),
        in_specs=[col(0), col(1), col(2), col(3), col(4),
                  pl.BlockSpec((RB, 128), lambda i, t: (i * T + t, P1_GATE // 128)),
                  pl.BlockSpec((1, 128), lambda i, t: (0, 0)),
                  pl.BlockSpec((1, MIX_W), lambda i, t: (0, 0))] + list(st_specs),
        out_specs=(pl.BlockSpec((RB, MIX_W), lambda i, t: (i * T + t, 0)),) + st_specs,
        scratch_shapes=[pltpu.VMEM((NB, M_HEADS, M_D, M_D), F32),
                        pltpu.VMEM((NB, M_HEADS, M_D), F32),
                        pltpu.VMEM((NB, 8, 128), F32)],
        compiler_params=_cp(("arbitrary", "arbitrary")),
    )(proj1, proj1, proj1, proj1, proj1, proj1, gate_b, hn_g, c0, n0, m0)


N_RA_OUT = 9


def _rwkv_a_kernel(pr_ref, pk_ref, pv_ref, pwa_ref, hr_ref, hk_ref, hv_ref, hwa_ref, st_ref, stwa_ref,
                   mu_ref, muwa_ref, w0_ref, w2_ref, a0_ref, a2_ref, kkp_ref, ka_ref, rk_ref,
                   ah_ref, rh_ref, bh_ref, kh_ref, vo_ref, ul_ref, yl_ref, dc_ref, bo_ref, shr_ref, shw_ref,
                   s_at, s_rt, s_bt, s_kt, s_v, s_cum, *, RB, CT, N, L, U):
    HS = R_HEADS * CT
    HG = 128 // CT
    NG = R_HEADS // HG
    GW = HG * R_K
    NCH = RB // CT
    short = L == CT
    rid = _iota((RB, 1), 0)
    grow = pl.program_id(0) * RB + rid

    if short:
        spread = (_shr(_iota((RB, RB // CT), 0), _log2(CT)) == _iota((RB, RB // CT), 1)).astype(BF16)

    def shifted(p_ref, h_ref, s_ref, lo, hi, mu):
        p = p_ref[...]
        prev = pltpu.roll(p, 1, 0)
        if short:
            hi3, mid3, lo3 = _split3(s_ref[:, lo:hi])
            first_prev = _dot(spread, hi3) + _dot(spread, mid3) + _dot(spread, lo3)
            prev = jnp.where(jnp.bitwise_and(rid, CT - 1) == 0, first_prev, prev)
        else:
            prev = jnp.where(rid == 0, h_ref[7:8, :], prev)
            for n in range(N):
                prev = jnp.where(grow == n * L, s_ref[n:n + 1, lo:hi], prev)
        return p + (prev - p) * mu

    raw = ((pr_ref, shr_ref, 0), (pk_ref, shr_ref, 1024), (pv_ref, shr_ref, 2048), (pwa_ref, shw_ref, 0))
    if short:
        nseq = RB // CT
        sel = (_iota((nseq, RB), 1) == _iota((nseq, RB), 0) * CT + (CT - 1)).astype(BF16)
        for src, dst, lo in raw:
            hi, mid, low = _split3(src[...])
            dst[:, lo:lo + src.shape[1]] = _dot(sel, hi) + _dot(sel, mid) + _dot(sel, low)
    else:
        for n in range(N):
            tile, off = divmod(n * L + L - 1, RB)

            @pl.when(pl.program_id(0) == tile)
            def _():
                for src, dst, lo in raw:
                    dst[n:n + 1, lo:lo + src.shape[1]] = src[off:off + 1, :]

    r = shifted(pr_ref, hr_ref, st_ref, 0, 1024, mu_ref[:, 0:1024])
    k = shifted(pk_ref, hk_ref, st_ref, 1024, 2048, mu_ref[:, 1024:2048])
    v = shifted(pv_ref, hv_ref, st_ref, 2048, 3072, mu_ref[:, 2048:3072])
    wa = shifted(pwa_ref, hwa_ref, stwa_ref, 0, 128, muwa_ref[...])
    w = -_softplus(-(w0_ref[...] + _dot(jnp.tanh(wa).astype(BF16), w2_ref[...]))) - 0.5
    wlog = -jnp.exp(w)
    a = _sigmoid(a0_ref[...] + _dot(wa.astype(BF16), a2_ref[...]))
    kk = k * kkp_ref[...]
    kk = kk / jnp.maximum(jnp.sqrt(_segsum(kk * kk, R_K)), 1e-12)
    kmod = k * (1.0 + (a - 1.0) * ka_ref[...])
    bo_ref[...] = _segsum(r * kmod * rk_ref[...], R_K) * v
    cum = _row_cumsum(wlog, CT)
    einv = jnp.exp(-cum)
    s_at[...] = (-kk) * jnp.exp(cum - wlog)
    s_rt[...] = r * jnp.exp(cum)
    s_bt[...] = kk * a * einv
    s_kt[...] = kmod * einv
    s_v[...] = v
    s_cum[...] = cum

    mdt = BF16 if CT % 16 == 0 else F32
    be_mask = (_shr(_iota((128, GW), 0), _log2(CT)) == _shr(_iota((128, GW), 1), _log2(R_K))).astype(mdt)
    bd_mask = (_shr(_iota((HS, HS), 0), _log2(CT)) == _shr(_iota((HS, HS), 1), _log2(CT))).astype(mdt)
    tt = _iota((CT, HS), 0)
    ss = jnp.bitwise_and(_iota((CT, HS), 1), CT - 1)
    strict = tt > ss
    incl = tt >= ss
    eye_c = (tt == ss).astype(F32)
    cat0 = lambda *xs: jnp.concatenate(xs, axis=0)

    def blockexp(x):
        return [(jnp.concatenate([x[:, GW * g:GW * (g + 1)].astype(mdt)] * HG, axis=0) * be_mask).astype(BF16)
                for g in range(NG)]

    def gram(lhs, be):
        lb = lhs.astype(BF16)
        return jnp.concatenate([_dot_nt(lb[:, GW * g:GW * (g + 1)], be[g]) for g in range(NG)], axis=1)

    def apply(cmp, be):
        cb = cmp.astype(BF16)
        return jnp.concatenate([_dot(cb[:, 128 * g:128 * (g + 1)], be[g]) for g in range(NG)], axis=1)

    def bdiag(x):
        return (jnp.concatenate([x.astype(mdt)] * R_HEADS, axis=0) * bd_mask).astype(BF16)

    def bdiag_hl(x):
        if mdt == BF16:
            return tuple(bdiag(part) for part in _split2(x))
        return _split2(jnp.concatenate([x] * R_HEADS, axis=0) * bd_mask)

    def mm_hl(stack, wh, wl):
        sh, sl = _split2(stack)
        n = stack.shape[0]
        full = _dot(cat0(sh, sl), wh)
        return full[:n] + full[n:] + _dot(sh, wl)

    def chunks(i, carry):
        rows = [pl.ds(pl.multiple_of((i * U + u) * CT, CT), CT) for u in range(U)]
        ld = lambda ref: [ref[rw, :] for rw in rows]
        at, rt, bt, kt, vv, cm = ld(s_at), ld(s_rt), ld(s_bt), ld(s_kt), ld(s_v), ld(s_cum)
        each = lambda f, *xs: [f(*a_) for a_ in zip(*xs)]
        ar_ = each(cat0, at, rt)
        gb = each(lambda l_, y_: gram(l_, blockexp(y_)), ar_, bt)
        gk = each(lambda l_, y_: gram(l_, blockexp(y_)), ar_, kt)
        a_ab = each(lambda m: jnp.where(strict, m[:CT], 0.0), gb)
        a_rb = each(lambda m: jnp.where(incl, m[CT:], 0.0), gb)
        a_ak = each(lambda m: jnp.where(strict, m[:CT], 0.0), gk)
        a_rk = each(lambda m: jnp.where(incl, m[CT:], 0.0), gk)
        p = each(lambda m: eye_c + m, a_ab)
        x = a_ab
        q = a_rb
        w_hl = each(bdiag_hl, x)
        res = each(lambda x_, q_, w_: mm_hl(cat0(x_, q_), *w_), x, q, w_hl)
        x = each(lambda r_: r_[:CT], res)
        q = each(lambda q_, r_: q_ + r_[CT:], q, res)
        pw = 2
        while pw < CT:
            w_hl = each(bdiag_hl, x)
            if 2 * pw >= CT:
                res = each(lambda p_, q_, w_: mm_hl(cat0(p_, q_), *w_), p, q, w_hl)
                q = each(lambda q_, r_: q_ + r_[CT:], q, res)
            else:
                res = each(lambda p_, x_, q_, w_: mm_hl(cat0(p_, x_, q_), *w_), p, x, q, w_hl)
                x = each(lambda r_: r_[CT:2 * CT], res)
                q = each(lambda q_, r_: q_ + r_[2 * CT:], q, res)
            p = each(lambda p_, r_: p_ + r_[:CT], p, res)
            pw *= 2
        tq = each(cat0, p, q)
        res = each(lambda m, k_: _dot(m.astype(BF16), bdiag(k_)), tq, a_ak)
        ty = each(lambda r_, k_: cat0(r_[:CT], r_[CT:] + k_), res, a_rk)
        o1 = each(lambda m, y_: apply(m, blockexp(y_)), tq, at)
        o2 = each(lambda m, y_: apply(m, blockexp(y_)), ty, vv)
        ect = each(lambda c_: jnp.exp(c_[CT - 1:CT, :]), cm)
        for u, rw in enumerate(rows):
            ah_ref[rw, :] = o1[u][:CT]
            rh_ref[rw, :] = rt[u] + o1[u][CT:]
            ul_ref[rw, :] = o2[u][:CT]
            yl_ref[rw, :] = o2[u][CT:]
            bh_ref[rw, :] = bt[u] * ect[u]
            kh_ref[rw, :] = kt[u] * ect[u]
            vo_ref[rw, :] = vv[u]
            dc_ref[rw, :] = jnp.broadcast_to(ect[u], (CT, 1024))
        return carry

    lax.fori_loop(0, NCH // U, chunks, 0)


def _rwkv_a_call(proj1, st_rkv, st_wa, wts, *, N, L, RB, CT, U):
    short = L == CT
    rows = N * L
    assert rows % RB == 0 and (RB // CT) % U == 0

    def col(cb, width=MIX_W):
        return pl.BlockSpec((RB, width), lambda i: (i, cb))

    def halo(cb, width=MIX_W):
        return pl.BlockSpec((8, width), lambda i: (jnp.maximum(i * (RB // 8) - 1, 0), cb))

    if short:
        st_specs = [pl.BlockSpec((RB // CT, 3072), lambda i: (i, 0)), pl.BlockSpec((RB // CT, 128), lambda i: (i, 0))]
    else:
        st_specs = [pl.BlockSpec((N, 3072), lambda i: (0, 0)), pl.BlockSpec((N, 128), lambda i: (0, 0))]

    def full(shape):
        return pl.BlockSpec(shape, lambda i: (0,) * len(shape))

    o_spec = pl.BlockSpec((RB, MIX_W), lambda i: (i, 0))
    if short:
        sh_specs = (pl.BlockSpec((RB // CT, 3072), lambda i: (i, 0)), pl.BlockSpec((RB // CT, 128), lambda i: (i, 0)))
    else:
        sh_specs = (pl.BlockSpec((N, 3072), lambda i: (0, 0)), pl.BlockSpec((N, 128), lambda i: (0, 0)))
    outs = pl.pallas_call(
        functools.partial(_rwkv_a_kernel, RB=RB, CT=CT, N=N, L=L, U=U),
        out_shape=((jax.ShapeDtypeStruct((rows, MIX_W), F32),) * N_RA_OUT
                   + (jax.ShapeDtypeStruct((N, 3072), F32), jax.ShapeDtypeStruct((N, 128), F32))),
        grid=(rows // RB,),
        in_specs=([col(P1_R // 1024), col(P1_RK // 1024), col(P1_RV // 1024), col(P1_WA // 128, 128),
                   halo(P1_R // 1024), halo(P1_RK // 1024), halo(P1_RV // 1024), halo(P1_WA // 128, 128)]
                  + st_specs
                  + [full((1, 3072)), full((1, 128)), full((1, 1024)), full((128, 1024)), full((1, 1024)),
                     full((128, 1024)), full((1, 1024)), full((1, 1024)), full((1, 1024))]),
        out_specs=(o_spec,) * N_RA_OUT + sh_specs,
        scratch_shapes=[pltpu.VMEM((RB, MIX_W), F32)] * 6,
        compiler_params=_cp(("arbitrary",)),
    )(*([proj1] * 8 + [st_rkv, st_wa] + list(wts)))
    return outs[:N_RA_OUT], jnp.concatenate(outs[N_RA_OUT:], axis=1)


def _rwkv_b_kernel(ah_ref, rh_ref, bh_ref, kh_ref, v_ref, ul_ref, yl_ref, dc_ref, bo_ref, z_ref, lg_ref, lb_ref,
                   s0_ref, y_ref, so_ref, sbd_ref, yb_ref, *, NBLK, TLB, CT, T):
    t = pl.program_id(1)
    bd_mask = (_shr(_iota((256, 256), 0), 6) == _shr(_iota((256, 256), 1), 6)).astype(F32)

    @pl.when(t == 0)
    def _():
        for nb in range(NBLK):
            for j in range(4):
                x = s0_ref[nb, 256 * j:256 * (j + 1), :]
                x = jnp.concatenate([x, x], axis=1)
                sbd_ref[4 * nb + j] = jnp.concatenate([x, x], axis=1) * bd_mask

    chains = [(nb, j, slice(256 * j, 256 * (j + 1))) for nb in range(NBLK) for j in range(4)]
    for c in range(TLB // CT):
        rows = slice(c * CT, (c + 1) * CT)
        sbs = [sbd_ref[4 * nb + j] for nb, j, cs in chains]
        outs = [_dot_nt(jnp.concatenate([ah_ref[nb, rows, cs], rh_ref[nb, rows, cs]], axis=0).astype(BF16),
                        sb.astype(BF16))
                for (nb, j, cs), sb in zip(chains, sbs)]
        upds = [_dot_tn(jnp.concatenate([o[:CT] + ul_ref[nb, rows, cs], v_ref[nb, rows, cs]],
                                        axis=0).astype(BF16),
                        jnp.concatenate([bh_ref[nb, rows, cs], kh_ref[nb, rows, cs]], axis=0).astype(BF16))
                for (nb, j, cs), o in zip(chains, outs)]
        for (nb, j, cs), sb, o, upd in zip(chains, sbs, outs, upds):
            yb_ref[nb, rows, cs] = o[CT:] + yl_ref[nb, rows, cs]
            sbd_ref[4 * nb + j] = sb * dc_ref[nb, c * CT:c * CT + 1, cs] + upd * bd_mask

    @pl.when(t == T - 1)
    def _():
        for nb in range(NBLK):
            outs = []
            for j in range(4):
                sb = sbd_ref[4 * nb + j]
                half = sb[:, :128] + sb[:, 128:]
                outs.append(half[:, :64] + half[:, 64:])
            so_ref[nb] = jnp.concatenate(outs, axis=0)

    cat = lambda ref: jnp.concatenate([ref[nb] for nb in range(NBLK)], axis=0)
    y = cat(yb_ref)
    mu = _segsum(y, R_K) * (1.0 / R_K)
    yc = y - mu
    var = _segsum(yc * yc, R_K) * (1.0 / R_K)
    yn = yc * lax.rsqrt(var + R_LN_EPS) * lg_ref[...] + lb_ref[...] + cat(bo_ref)
    out = yn * _silu(cat(z_ref))
    for nb in range(NBLK):
        y_ref[nb] = out[nb * TLB:(nb + 1) * TLB, :]


def _rwkv_b_call(ra, proj1, ln_g, ln_b, s0, *, N, L, NBLK, TLB, CT):
    T = L // TLB
    blk = lambda cb: pl.BlockSpec((NBLK, TLB, MIX_W), lambda i, t: (i, t, cb))
    s_spec = pl.BlockSpec((NBLK, 1024, 64), lambda i, t: (i, 0, 0))
    ra3 = [a.reshape(N, L, MIX_W) for a in ra]
    y, s_new = pl.pallas_call(
        functools.partial(_rwkv_b_kernel, NBLK=NBLK, TLB=TLB, CT=CT, T=T),
        out_shape=(jax.ShapeDtypeStruct((N, L, MIX_W), F32),
                   jax.ShapeDtypeStruct((N, 1024, 64), F32)),
        grid=(N // NBLK, T),
        in_specs=([blk(0)] * N_RA_OUT
                  + [blk(P1_ZD // 1024),
                     pl.BlockSpec((1, MIX_W), lambda i, t: (0, 0)),
                     pl.BlockSpec((1, MIX_W), lambda i, t: (0, 0)),
                     s_spec]),
        out_specs=(blk(0), s_spec),
        scratch_shapes=[pltpu.VMEM((4 * NBLK, 256, 256), F32),
                        pltpu.VMEM((NBLK, TLB, MIX_W), F32)],
        compiler_params=_cp(("arbitrary", "arbitrary")),
    )(*(ra3 + [proj1.reshape(N, L, P1_N), ln_g, ln_b, s0]))
    return y.reshape(N * L, MIX_W), s_new


def _regroup_w1(w1):
    wt = w1.T
    return jnp.concatenate([wt[0:4096], wt[4104:8200], wt[8328:9352], wt[8200:8328], wt[4096:4104],
                            jnp.zeros((P1_N - 9352, D_MODEL), F32)], axis=0).astype(BF16)


def _s5_weights(lam_re, lam_im, log_dt, b_re, b_im, c_re, c_im):
    dt = jnp.exp(log_dt)[:, None]
    mag = jnp.exp(lam_re * dt)
    ar = mag * jnp.cos(lam_im * dt)
    ai = mag * jnp.sin(lam_im * dt)
    den = lam_re * lam_re + lam_im * lam_im
    qr = ((ar - 1.0) * lam_re + ai * lam_im) / den
    qi = (ai * lam_re - (ar - 1.0) * lam_im) / den
    bbr = qr[..., None] * b_re - qi[..., None] * b_im
    bbi = qr[..., None] * b_im + qi[..., None] * b_re
    hp = lax.Precision.HIGHEST
    spread_p = (jnp.arange(1024)[None, :] % S5_STATE == jnp.arange(S5_STATE)[:, None]).astype(F32)
    spread_h = (jnp.arange(256)[None, :] % S5_GROUP == jnp.arange(S5_GROUP)[:, None]).astype(F32)
    grp_in = (jnp.arange(256)[:, None] // S5_GROUP == jnp.arange(1024)[None, :] // S5_STATE)
    grp_out = (jnp.arange(1024)[:, None] // S5_STATE == jnp.arange(256)[None, :] // S5_GROUP)

    def in_blocks(bb):
        rows = bb.transpose(0, 2, 1).reshape(4, 256, S5_STATE)
        return jnp.matmul(rows, spread_p, precision=hp) * grp_in

    def out_blocks(cc):
        rows = cc.transpose(0, 2, 1).reshape(4, 1024, S5_GROUP)
        return jnp.matmul(rows, spread_h, precision=hp) * grp_out

    wb = jnp.concatenate([in_blocks(bbr), in_blocks(bbi)], axis=2).astype(BF16)
    wc = jnp.concatenate([out_blocks(c_re), out_blocks(-c_im)], axis=1).astype(BF16)
    pr, pi = jnp.ones_like(ar), jnp.zeros_like(ai)
    lag = []
    for _ in range(8):
        cpr = c_re * pr[:, None, :] - c_im * pi[:, None, :]
        cpi = c_re * pi[:, None, :] + c_im * pr[:, None, :]
        lag.append(jnp.einsum('gop,gph->goh', cpr, bbr, precision=lax.Precision.HIGHEST)
                   - jnp.einsum('gop,gph->goh', cpi, bbi, precision=lax.Precision.HIGHEST))
        pr, pi = pr * ar - pi * ai, pr * ai + pi * ar
    kd = jnp.stack(lag).reshape(8, 4, 16, S5_GROUP, S5_GROUP)
    kd = kd.transpose(1, 0, 2, 4, 3).reshape(4, 8, 256, S5_GROUP)
    spread = (jnp.arange(256)[None, :] % S5_GROUP == jnp.arange(S5_GROUP)[:, None]).astype(F32)
    same_group = (jnp.arange(256)[:, None] // S5_GROUP == jnp.arange(256)[None, :] // S5_GROUP)
    wk = jnp.matmul(kd, spread, precision=lax.Precision.HIGHEST) * same_group
    wk = wk.reshape(4, 8 * 256, 256).astype(BF16)
    return ar.reshape(1, 4096), ai.reshape(1, 4096), wb, wk, wc


CFG = {
    "P": dict(N=BATCH, L=P_LEN, tm_mm=1376, tm=688, tm_ln=688, drop_meta=True,
              conv=dict(NB=1, TL=344), s5=dict(NB=1, TL=688), mlstm=dict(NB=1, TL=344),
              ra=dict(RB=192, CT=16, U=12), rb=dict(NBLK=4, TLB=48, CT=16)),
    "S": dict(N=DEC_BATCH, L=DEC_SEQ, tm_mm=1024, tm=512, tm_ln=256, drop_meta=False,
              conv=dict(NB=16, TL=8), s5=dict(NB=32, TL=8), mlstm=dict(NB=4, TL=8),
              ra=dict(RB=256, CT=8, U=8), rb=dict(NBLK=8, TLB=8, CT=8)),
}


def _trunk(x, st, w, cfg):
    n, l = cfg["N"], cfg["L"]
    proj0 = _matmul(x, w["w_in0"], cfg["tm_mm"], 1024)
    act, conv_new = _conv_call(proj0, st["conv"], w["conv_w"], w["conv_b"], w["a_ln_g"], w["a_ln_b"],
                               N=n, L=l, **cfg["conv"])
    mix_a = _pw_gate(act, w["pw"], proj0, tm=cfg["tm_mm"])
    yb, xr, xi = _s5_call(proj0, w["s5_wb"], w["s5_wk"], w["s5_wc"], w["s5_d"], w["s5_ar"], w["s5_ai"],
                          st["ssm_re"].reshape(n, 1, 4096), st["ssm_im"].reshape(n, 1, 4096),
                          N=n, L=l, **cfg["s5"])
    mix_b = _glu_gate(yb, w["glu_w"], w["glu_b"], proj0, tm=cfg["tm_mm"])
    x1 = _out_ln(x, mix_a, mix_b, w["w_out0"], w["ln_g0"], w["ln_b0"], tm=cfg["tm_ln"])

    proj1 = _matmul_nt(x1, w["w_in1"], cfg["tm_mm"], 512)
    m0 = jnp.pad(jnp.broadcast_to(st["m"][:, :, None], (n, M_HEADS, 128)), ((0, 0), (0, 4), (0, 0)))
    mix_c, c_new, n_new, m_new = _mlstm_call(proj1, w["gate_b"], w["hn_g"], st["c"], st["n"], m0,
                                             N=n, L=l, **cfg["mlstm"])
    sh = st["shift"]
    ra, shift_new = _rwkv_a_call(proj1, sh[:, :3072], sh[:, 3072:], w["rwkv"], N=n, L=l, **cfg["ra"])
    mix_d, s_new = _rwkv_b_call(ra, proj1, w["r_ln_g"], w["r_ln_b"], st["s"].reshape(n, 1024, 64),
                                N=n, L=l, **cfg["rb"])
    final_ln = _out_ln_prompt if cfg["drop_meta"] else functools.partial(_out_ln, tm=cfg["tm_ln"])
    y = final_ln(x1, mix_c, mix_d, w["w_out1"], w["ln_g1"], w["ln_b1"])

    states = (conv_new[None],
              xr.reshape(n, S5_GROUPS, S5_STATE)[None],
              xi.reshape(n, S5_GROUPS, S5_STATE)[None],
              c_new[None], n_new[None], m_new[:, :M_HEADS, 0][None],
              s_new.reshape(n, R_HEADS, R_K, R_K)[None],
              shift_new[None])
    return y, states


def kernel(x_prompt, x_sample, state_conv, state_ssm_re, state_ssm_im, state_mlstm_c, state_mlstm_n, state_mlstm_m, state_rwkv_s, state_rwkv_shift, meta_tokens, ev_w_in, a_conv_w, a_conv_b, a_ln_g, a_ln_b, a_pw, s5_lambda_re, s5_lambda_im, s5_log_dt, s5_b_re, s5_b_im, s5_c_re, s5_c_im, s5_d, s5_glu_w, s5_glu_b, ev_w_out, ev_ln_g, ev_ln_b, od_w_in, m_ig_b, m_fg_b, m_hn_g, r_mu, r_w0, r_w2, r_a0, r_a2, r_kk, r_ka, r_rk, r_ln_g, r_ln_b, od_w_out, od_ln_g, od_ln_b):
    nb = x_prompt.shape[0]
    row = lambda vec: vec.reshape(1, -1)
    zeros = lambda *s: jnp.zeros(s, F32)

    ar, ai, wb, wk, wc = _s5_weights(s5_lambda_re[0], s5_lambda_im[0], s5_log_dt[0], s5_b_re[0], s5_b_im[0],
                                     s5_c_re[0], s5_c_im[0])
    w_in1 = _regroup_w1(od_w_in[0])
    mu = r_mu[0]
    w = dict(
        w_in0=ev_w_in[0].astype(BF16), conv_w=a_conv_w[0], conv_b=row(a_conv_b[0]),
        a_ln_g=row(a_ln_g[0]), a_ln_b=row(a_ln_b[0]), pw=a_pw[0].astype(BF16),
        s5_wb=wb, s5_wk=wk, s5_wc=wc, s5_d=row(s5_d[0]), s5_ar=ar, s5_ai=ai,
        glu_w=s5_glu_w[0].astype(BF16), glu_b=row(s5_glu_b[0]),
        w_out0=ev_w_out[0].astype(BF16), ln_g0=row(ev_ln_g[0]), ln_b0=row(ev_ln_b[0]),
        w_in1=w_in1,
        gate_b=jnp.concatenate([m_ig_b[0], m_fg_b[0], jnp.zeros((120,), F32)]).reshape(1, 128),
        hn_g=row(m_hn_g[0]),
        rwkv=[row(mu[:3072]), row(mu[3072:]), row(r_w0[0]),
              jnp.concatenate([r_w2[0], jnp.zeros((64, MIX_W), F32)], axis=0).astype(BF16),
              row(r_a0[0]),
              jnp.concatenate([jnp.zeros((64, MIX_W), F32), r_a2[0]], axis=0).astype(BF16),
              row(r_kk[0]), row(r_ka[0]), row(r_rk[0])],
        r_ln_g=row(r_ln_g[0]), r_ln_b=row(r_ln_b[0]),
        w_out1=od_w_out[0].astype(BF16), ln_g1=row(od_ln_g[0]), ln_b1=row(od_ln_b[0]),
    )

    x_p = jnp.concatenate([jnp.broadcast_to(meta_tokens[None], (nb, N_META, D_MODEL)), x_prompt],
                          axis=1).reshape(nb * P_LEN, D_MODEL)
    st_p = dict(conv=zeros(nb, CONV_W - 1, MIX_W), ssm_re=zeros(nb, 4096), ssm_im=zeros(nb, 4096),
                c=zeros(nb, M_HEADS, M_D, M_D), n=zeros(nb, M_HEADS, M_D), m=zeros(nb, M_HEADS),
                s=zeros(nb, R_HEADS, R_K, R_K), shift=zeros(nb, 3200))
    y_p, states_p = _trunk(x_p, st_p, w, CFG["P"])

    st_s = dict(conv=state_conv[0], ssm_re=state_ssm_re[0], ssm_im=state_ssm_im[0],
                c=state_mlstm_c[0], n=state_mlstm_n[0], m=state_mlstm_m[0],
                s=state_rwkv_s[0], shift=state_rwkv_shift[0])
    y_s, states_s = _trunk(x_sample.reshape(DEC_BATCH * DEC_SEQ, D_MODEL), st_s, w, CFG["S"])

    y_prompt = y_p
    y_sample = y_s.reshape(DEC_BATCH, DEC_SEQ, D_MODEL)
    return (y_prompt, y_sample) + states_p + states_s
```

```python
import functools
import math

import jax
import jax.numpy as jnp
from jax import lax
from jax.experimental import pallas as pl
from jax.experimental.pallas import tpu as pltpu

F32 = jnp.float32
BF16 = jnp.bfloat16

D_MODEL = 2048
MIX_W = 1024
N_META = 16
CONV_W = 31
S5_GROUP = 16
S5_GROUPS = 64
S5_STATE = 64
M_HEADS = 4
M_D = 256
R_HEADS = 16
R_K = 64
LN_EPS = 1e-5
R_LN_EPS = 64e-5
DEPTH = 2
ALPHA = (2 * DEPTH) ** 0.25

BATCH = 4
SEQ = 2048
P_LEN = N_META + SEQ
DEC_BATCH = 128
DEC_SEQ = 8

P1_Q, P1_K, P1_V, P1_O, P1_ZC = 0, 1024, 2048, 3072, 4096
P1_R, P1_RK, P1_RV, P1_ZD, P1_WA, P1_GATE = 5120, 6144, 7168, 8192, 9216, 9344
P1_N = 9728

VMEM_LIMIT = 48 * 1024 * 1024


def _cp(sem):
    return pltpu.CompilerParams(dimension_semantics=sem, vmem_limit_bytes=VMEM_LIMIT)


def _dot(a, b):
    return jnp.dot(a, b, preferred_element_type=F32)


def _dot_nt(a, b):
    return lax.dot_general(a, b, (((1,), (1,)), ((), ())), preferred_element_type=F32)


def _dot_tn(a, b):
    return lax.dot_general(a, b, (((0,), (0,)), ((), ())), preferred_element_type=F32)


def _split2(x):
    hi = x.astype(BF16)
    lo = (x - hi.astype(F32)).astype(BF16)
    return hi, lo


def _split3(x):
    hi = x.astype(BF16)
    r1 = x - hi.astype(F32)
    mid = r1.astype(BF16)
    lo = (r1 - mid.astype(F32)).astype(BF16)
    return hi, mid, lo


def _sigmoid(x):
    return jax.nn.sigmoid(x)


def _silu(x):
    return x * jax.nn.sigmoid(x)


def _softplus(x):
    return jnp.maximum(x, 0.0) + jnp.log(1.0 + jnp.exp(-jnp.abs(x)))


def _gelu_tanh(x):
    c = math.sqrt(2.0 / math.pi)
    return x * (0.5 * (1.0 + jnp.tanh(c * (x + 0.044715 * (x * x * x)))))


def _iota(shape, axis):
    return lax.broadcasted_iota(jnp.int32, shape, axis)


def _shr(x, k):
    return lax.shift_right_logical(x, jnp.int32(k))


def _log2(n):
    k = int(round(math.log2(n)))
    assert 1 << k == n
    return k


def _block_ones(n, seg, dtype):
    r = _shr(_iota((n, n), 0), _log2(seg))
    c = _shr(_iota((n, n), 1), _log2(seg))
    return (r == c).astype(dtype)


def _segsum(x, seg):
    g = _block_ones(256, seg, BF16)
    outs = []
    for j in range(x.shape[1] // 256):
        hi, lo = _split2(x[:, 256 * j:256 * (j + 1)])
        outs.append(_dot(hi, g) + _dot(lo, g))
    return jnp.concatenate(outs, axis=1)


def _row_cumsum(x, period):
    rows = x.shape[0]
    rid = _iota(x.shape, 0)
    if period < rows:
        rid = jnp.bitwise_and(rid, period - 1)
    d = 1
    while d < min(period, rows):
        x = x + jnp.where(rid >= d, pltpu.roll(x, d, 0), 0.0)
        d *= 2
    return x


def _row_cummax(x, period):
    rows = x.shape[0]
    rid = _iota(x.shape, 0)
    if period < rows:
        rid = jnp.bitwise_and(rid, period - 1)
    d = 1
    while d < min(period, rows):
        x = jnp.maximum(x, jnp.where(rid >= d, pltpu.roll(x, d, 0), -jnp.inf))
        d *= 2
    return x


def _mm_kernel(x_ref, w_ref, o_ref):
    o_ref[...] = _dot(x_ref[...].astype(BF16), w_ref[...])


def _matmul(x, w, tm, tn):
    r, k = x.shape
    n = w.shape[1]
    return pl.pallas_call(
        _mm_kernel,
        out_shape=jax.ShapeDtypeStruct((r, n), F32),
        grid=(pl.cdiv(r, tm), n // tn),
        in_specs=[pl.BlockSpec((tm, k), lambda i, j: (i, 0)),
                  pl.BlockSpec((k, tn), lambda i, j: (0, j))],
        out_specs=pl.BlockSpec((tm, tn), lambda i, j: (i, j)),
        compiler_params=_cp(("parallel", "arbitrary")),
    )(x, w)


def _mm_nt_kernel(x_ref, w_ref, o_ref):
    o_ref[...] = _dot_nt(x_ref[...].astype(BF16), w_ref[...])


def _matmul_nt(x, w_t, tm, tn):
    r, k = x.shape
    n = w_t.shape[0]
    return pl.pallas_call(
        _mm_nt_kernel,
        out_shape=jax.ShapeDtypeStruct((r, n), F32),
        grid=(pl.cdiv(r, tm), n // tn),
        in_specs=[pl.BlockSpec((tm, k), lambda i, j: (i, 0)),
                  pl.BlockSpec((tn, k), lambda i, j: (j, 0))],
        out_specs=pl.BlockSpec((tm, tn), lambda i, j: (i, j)),
        compiler_params=_cp(("parallel", "arbitrary")),
    )(x, w_t)


def _pw_kernel(a_ref, w_ref, z_ref, o_ref):
    o_ref[...] = _dot(a_ref[...].astype(BF16), w_ref[...]) * _silu(z_ref[...])


def _pw_gate(act, pw, proj0, tm, tn=512):
    r = act.shape[0]
    zb = 2048 // tn
    return pl.pallas_call(
        _pw_kernel,
        out_shape=jax.ShapeDtypeStruct((r, MIX_W), F32),
        grid=(pl.cdiv(r, tm), MIX_W // tn),
        in_specs=[pl.BlockSpec((tm, MIX_W), lambda i, j: (i, 0)),
                  pl.BlockSpec((MIX_W, tn), lambda i, j: (0, j)),
                  pl.BlockSpec((tm, tn), lambda i, j: (i, zb + j))],
        out_specs=pl.BlockSpec((tm, tn), lambda i, j: (i, j)),
        compiler_params=_cp(("parallel", "arbitrary")),
    )(act, pw, proj0)


def _glu_kernel(y_ref, wv_ref, wg_ref, bv_ref, bg_ref, z_ref, o_ref):
    y = y_ref[...].astype(BF16)
    v = _dot(y, wv_ref[...]) + bv_ref[...]
    g = _dot(y, wg_ref[...]) + bg_ref[...]
    o_ref[...] = v * _sigmoid(g) * _silu(z_ref[...])


def _glu_gate(yb, glu_w, glu_b, proj0, tm, tn=512):
    r = yb.shape[0]
    nb = MIX_W // tn
    zb = 4096 // tn
    return pl.pallas_call(
        _glu_kernel,
        out_shape=jax.ShapeDtypeStruct((r, MIX_W), F32),
        grid=(pl.cdiv(r, tm), nb),
        in_specs=[pl.BlockSpec((tm, MIX_W), lambda i, j: (i, 0)),
                  pl.BlockSpec((MIX_W, tn), lambda i, j: (0, j)),
                  pl.BlockSpec((MIX_W, tn), lambda i, j: (0, nb + j)),
                  pl.BlockSpec((1, tn), lambda i, j: (0, j)),
                  pl.BlockSpec((1, tn), lambda i, j: (0, nb + j)),
                  pl.BlockSpec((tm, tn), lambda i, j: (i, zb + j))],
        out_specs=pl.BlockSpec((tm, tn), lambda i, j: (i, j)),
        compiler_params=_cp(("parallel", "arbitrary")),
    )(yb, glu_w, glu_w, glu_b, glu_b, proj0)


def _out_ln_kernel(x_ref, ma_ref, mb_ref, wa_ref, wb_ref, g_ref, b_ref, o_ref):
    out = _dot(ma_ref[...].astype(BF16), wa_ref[...]) + _dot(mb_ref[...].astype(BF16), wb_ref[...])
    y = ALPHA * x_ref[...] + out
    mu = jnp.mean(y, axis=-1, keepdims=True)
    yc = y - mu
    var = jnp.mean(yc * yc, axis=-1, keepdims=True)
    o_ref[...] = yc * lax.rsqrt(var + LN_EPS) * g_ref[...] + b_ref[...]


def _out_ln(x, mix_a, mix_b, w_out, ln_g, ln_b, tm):
    r = x.shape[0]
    return pl.pallas_call(
        _out_ln_kernel,
        out_shape=jax.ShapeDtypeStruct((r, D_MODEL), F32),
        grid=(pl.cdiv(r, tm),),
        in_specs=[pl.BlockSpec((tm, D_MODEL), lambda i: (i, 0)),
                  pl.BlockSpec((tm, MIX_W), lambda i: (i, 0)),
                  pl.BlockSpec((tm, MIX_W), lambda i: (i, 0)),
                  pl.BlockSpec((MIX_W, D_MODEL), lambda i: (0, 0), pipeline_mode=pl.Buffered(1)),
                  pl.BlockSpec((MIX_W, D_MODEL), lambda i: (1, 0), pipeline_mode=pl.Buffered(1)),
                  pl.BlockSpec((1, D_MODEL), lambda i: (0, 0)),
                  pl.BlockSpec((1, D_MODEL), lambda i: (0, 0))],
        out_specs=pl.BlockSpec((tm, D_MODEL), lambda i: (i, 0)),
        compiler_params=_cp(("parallel",)),
    )(x, mix_a, mix_b, w_out, w_out, ln_g, ln_b)


def _out_ln_prompt(x, mix_a, mix_b, w_out, ln_g, ln_b, tm=512):
    tiles = SEQ // tm

    def rows(width):
        return pl.BlockSpec((pl.Element(tm), pl.Element(width)),
                            lambda n, t: (pl.multiple_of(n * P_LEN + N_META + t * tm, 8), 0))

    return pl.pallas_call(
        _out_ln_kernel,
        out_shape=jax.ShapeDtypeStruct((BATCH * SEQ, D_MODEL), F32),
        grid=(BATCH, tiles),
        in_specs=[rows(D_MODEL), rows(MIX_W), rows(MIX_W),
                  pl.BlockSpec((MIX_W, D_MODEL), lambda n, t: (0, 0), pipeline_mode=pl.Buffered(1)),
                  pl.BlockSpec((MIX_W, D_MODEL), lambda n, t: (1, 0), pipeline_mode=pl.Buffered(1)),
                  pl.BlockSpec((1, D_MODEL), lambda n, t: (0, 0)),
                  pl.BlockSpec((1, D_MODEL), lambda n, t: (0, 0))],
        out_specs=pl.BlockSpec((tm, D_MODEL), lambda n, t: (n * tiles + t, 0)),
        compiler_params=_cp(("parallel", "arbitrary")),
    )(x, mix_a, mix_b, w_out, w_out, ln_g, ln_b).reshape(BATCH, SEQ, D_MODEL)


def _conv_kernel(u_ref, g_ref, st_ref, w_ref, cb_ref, lg_ref, lb_ref, act_ref, nst_ref, hp_ref, hs_ref, wb_ref,
                 *, NB, TL, T):
    t = pl.program_id(1)
    for j in range(CONV_W):
        wb_ref[j] = jnp.broadcast_to(w_ref[j:j + 1, :], (8, MIX_W))
    for nb in range(NB):
        base = nb * TL

        @pl.when(t == 0)
        def _():
            hp_ref[nb, 0:2, :] = jnp.zeros((2, MIX_W), F32)
            hp_ref[nb, 2:32, :] = st_ref[nb]

        hp_ref[nb, TL + 32:TL + 40, :] = jnp.zeros((8, MIX_W), F32)
        hp_ref[nb, 32:32 + TL, :] = u_ref[base:base + TL, :] * _sigmoid(g_ref[base:base + TL, :])
        for b in range(8):
            hs_ref[b] = hp_ref[nb, b:b + TL + 32, :]

        def taps(r0, groups):
            acc = [None] * groups
            for j in range(CONV_W):
                o = j + 2
                wj = wb_ref[j]
                for g in range(groups):
                    term = wj * hs_ref[o % 8, pl.ds(r0 + 8 * (o // 8 + g), 8), :]
                    acc[g] = term if acc[g] is None else acc[g] + term
            for g in range(groups):
                act_ref[pl.ds(base + r0 + 8 * g, 8), :] = acc[g]

        def chunk(c, carry):
            taps(pl.multiple_of(c * 16, 16), 2)
            return carry

        lax.fori_loop(0, TL // 16, chunk, 0)
        if TL % 16:
            taps(TL - 8, 1)

        @pl.when(t == T - 1)
        def _():
            nst_ref[nb] = hp_ref[nb, TL + 2:TL + 32, :]

        if T > 1:
            hp_ref[nb, 0:32, :] = hp_ref[nb, TL:TL + 32, :]

    y = act_ref[...] + cb_ref[...]
    mu = jnp.mean(y, axis=-1, keepdims=True)
    yc = y - mu
    var = jnp.mean(yc * yc, axis=-1, keepdims=True)
    act_ref[...] = _silu(yc * lax.rsqrt(var + LN_EPS) * lg_ref[...] + lb_ref[...])


def _conv_call(proj0, state, conv_w, conv_b, ln_g, ln_b, *, N, L, NB, TL):
    T = L // TL
    RB = NB * TL
    assert NB == 1 or T == 1
    return pl.pallas_call(
        functools.partial(_conv_kernel, NB=NB, TL=TL, T=T),
        out_shape=(jax.ShapeDtypeStruct((N * L, MIX_W), F32),
                   jax.ShapeDtypeStruct((N, CONV_W - 1, MIX_W), F32)),
        grid=(N // NB, T),
        in_specs=[pl.BlockSpec((RB, MIX_W), lambda i, t: (i * T + t, 0)),
                  pl.BlockSpec((RB, MIX_W), lambda i, t: (i * T + t, 1)),
                  pl.BlockSpec((NB, CONV_W - 1, MIX_W), lambda i, t: (i, 0, 0)),
                  pl.BlockSpec((CONV_W, MIX_W), lambda i, t: (0, 0)),
                  pl.BlockSpec((1, MIX_W), lambda i, t: (0, 0)),
                  pl.BlockSpec((1, MIX_W), lambda i, t: (0, 0)),
                  pl.BlockSpec((1, MIX_W), lambda i, t: (0, 0))],
        out_specs=(pl.BlockSpec((RB, MIX_W), lambda i, t: (i * T + t, 0)),
                   pl.BlockSpec((NB, CONV_W - 1, MIX_W), lambda i, t: (i, 0, 0))),
        scratch_shapes=[pltpu.VMEM((NB, TL + 40, MIX_W), F32),
                        pltpu.VMEM((8, TL + 32, MIX_W), F32),
                        pltpu.VMEM((CONV_W, 8, MIX_W), F32)],
        compiler_params=_cp(("arbitrary", "arbitrary")),
    )(proj0, proj0, state, conv_w, conv_b, ln_g, ln_b)


def _s5_kernel(u_ref, wb_ref, wk_ref, wc_ref, d_ref, ar_ref, ai_ref, x0r_ref, x0i_ref,
               y_ref, xfr_ref, xfi_ref, xs_ref, cr_ref, ci_ref, *, NB, TL, T):
    t = pl.program_id(2)
    RB = NB * TL
    GL = TL // 8
    u = u_ref[...]
    ub = u.astype(BF16)
    big = _dot(ub, wb_ref[0])
    xs_ref[0] = big[:, :1024]
    xs_ref[1] = big[:, 1024:]
    rid = jnp.bitwise_and(_iota((RB, 256), 0), 7)
    lags = [ub] + [jnp.where(rid >= d, pltpu.roll(u, d, 0), 0.0).astype(BF16) for d in range(1, 8)]
    y_loc = _dot(jnp.concatenate(lags, axis=1), wk_ref[0])
    ar = ar_ref[...]
    ai = ai_ref[...]

    def cmul(pr, pi, qr, qi):
        return pr * qr - pi * qi, pr * qi + pi * qr

    a1 = (ar, ai)
    a2 = cmul(*a1, *a1)
    a4 = cmul(*a2, *a2)
    a3 = cmul(*a2, *a1)
    a5 = cmul(*a4, *a1)
    a6 = cmul(*a4, *a2)
    a7 = cmul(*a6, *a1)
    a8 = cmul(*a4, *a4)
    a0 = (jnp.ones_like(ar), jnp.zeros_like(ai))
    r8 = _iota((8, 1024), 0)

    def table(powers):
        tr = jnp.zeros((8, 1024), F32)
        ti = jnp.zeros((8, 1024), F32)
        for k, (pr, pi) in enumerate(powers):
            tr = jnp.where(r8 == k, pr, tr)
            ti = jnp.where(r8 == k, pi, ti)
        return tr, ti

    pwr, pwi = table((a1, a2, a3, a4, a5, a6, a7, a8))
    qwr, qwi = table((a7, a6, a5, a4, a3, a2, a1, a0))
    a8r, a8i = a8

    first = t == 0

    def seq_body(nb, carry0):
        x0r = x0r_ref[nb]
        x0i = x0i_ref[nb]
        if T > 1:
            c_r = jnp.where(first, x0r, cr_ref[0:1, :])
            c_i = jnp.where(first, x0i, ci_ref[0:1, :])
        else:
            c_r, c_i = x0r, x0i

        def grp(g, c):
            c_r, c_i = c
            off = pl.multiple_of(nb * TL + g * 8, 8)
            vr = xs_ref[0, pl.ds(off, 8), :]
            vi = xs_ref[1, pl.ds(off, 8), :]
            er = jnp.sum(qwr * vr - qwi * vi, axis=0, keepdims=True)
            ei = jnp.sum(qwr * vi + qwi * vr, axis=0, keepdims=True)
            br = jnp.broadcast_to(c_r, (8, 1024))
            bi = jnp.broadcast_to(c_i, (8, 1024))
            xs_ref[0, pl.ds(off, 8), :] = pwr * br - pwi * bi
            xs_ref[1, pl.ds(off, 8), :] = pwr * bi + pwi * br
            return a8r * c_r - a8i * c_i + er, a8r * c_i + a8i * c_r + ei

        c_r, c_i = lax.fori_loop(0, GL, grp, (c_r, c_i))
        if T > 1:
            cr_ref[...] = jnp.broadcast_to(c_r, (8, 1024))
            ci_ref[...] = jnp.broadcast_to(c_i, (8, 1024))

        @pl.when(t == T - 1)
        def _():
            xfr_ref[nb] = c_r
            xfi_ref[nb] = c_i

        return carry0

    lax.fori_loop(0, NB, seq_body, 0)
    y = (_dot(xs_ref[0].astype(BF16), wc_ref[0, 0:1024, :])
         + _dot(xs_ref[1].astype(BF16), wc_ref[0, 1024:2048, :]))
    y_ref[...] = _gelu_tanh(y + y_loc + d_ref[...] * u)


def _s5_call(proj0, wb, wk, wc, dvec, ar, ai, x0r, x0i, *, N, L, NB, TL):
    T = L // TL
    RB = NB * TL
    assert NB == 1 or T == 1
    ub = 3072 // 256
    st = jax.ShapeDtypeStruct((N, 1, 4096), F32)
    return pl.pallas_call(
        functools.partial(_s5_kernel, NB=NB, TL=TL, T=T),
        out_shape=(jax.ShapeDtypeStruct((N * L, MIX_W), F32), st, st),
        grid=(N // NB, 4, T),
        in_specs=[pl.BlockSpec((RB, 256), lambda i, j, t: (i * T + t, ub + j)),
                  pl.BlockSpec((1, 256, 2048), lambda i, j, t: (j, 0, 0)),
                  pl.BlockSpec((1, 2048, 256), lambda i, j, t: (j, 0, 0)),
                  pl.BlockSpec((1, 2048, 256), lambda i, j, t: (j, 0, 0)),
                  pl.BlockSpec((1, 256), lambda i, j, t: (0, j)),
                  pl.BlockSpec((1, 1024), lambda i, j, t: (0, j)),
                  pl.BlockSpec((1, 1024), lambda i, j, t: (0, j)),
                  pl.BlockSpec((NB, 1, 1024), lambda i, j, t: (i, 0, j)),
                  pl.BlockSpec((NB, 1, 1024), lambda i, j, t: (i, 0, j))],
        out_specs=(pl.BlockSpec((RB, 256), lambda i, j, t: (i * T + t, j)),
                   pl.BlockSpec((NB, 1, 1024), lambda i, j, t: (i, 0, j)),
                   pl.BlockSpec((NB, 1, 1024), lambda i, j, t: (i, 0, j))),
        scratch_shapes=[pltpu.VMEM((2, RB, 1024), F32),
                        pltpu.VMEM((8, 1024), F32),
                        pltpu.VMEM((8, 1024), F32)],
        compiler_params=_cp(("arbitrary", "arbitrary", "arbitrary")),
    )(proj0, wb, wk, wc, dvec, ar, ai, x0r, x0i)


def _mlstm_kernel(q_ref, k_ref, v_ref, o_ref, z_ref, gt_ref, gb_ref, hg_ref, c0_ref, n0_ref, m0_ref,
                  y_ref, c_ref, n_ref, m_ref, cs_ref, ns_ref, ms_ref, *, NB, TL, T):
    for nb in range(NB):
        rows = lambda ref: ref.at[pl.ds(nb * TL, TL)]
        one = lambda ref: ref.at[pl.ds(nb, 1)]
        _mlstm_seq(rows(q_ref), rows(k_ref), rows(v_ref), rows(o_ref), rows(z_ref), rows(gt_ref), gb_ref, hg_ref,
                   one(c0_ref), one(n0_ref), one(m0_ref), rows(y_ref), one(c_ref), one(n_ref), one(m_ref),
                   cs_ref.at[nb], ns_ref.at[nb], ms_ref.at[nb], TL=TL, T=T)


def _mlstm_seq(q_ref, k_ref, v_ref, o_ref, z_ref, gt_ref, gb_ref, hg_ref, c0_ref, n0_ref, m0_ref,
               y_ref, c_ref, n_ref, m_ref, cs_ref, ns_ref, ms_ref, *, TL, T):
    t = pl.program_id(1)

    @pl.when(t == 0)
    def _():
        cs_ref[...] = c0_ref[0]
        ns_ref[...] = n0_ref[0]
        ms_ref[...] = m0_ref[0]

    G = gt_ref[...] + gb_ref[...]
    B = _row_cumsum(-_softplus(-G), TL)
    Bs = pltpu.roll(B, 124, 1)
    A = G - Bs
    CM = _row_cummax(A, TL)
    ms = ms_ref[...]
    dg = _iota((8, 128), 0) == _iota((8, 128), 1)
    mrow = jnp.sum(jnp.where(dg, ms, 0.0), axis=0, keepdims=True)
    M = jnp.maximum(mrow, CM)
    MT = Bs + M
    sel = dg.astype(BF16)
    a_hi, a_mid, a_lo = _split3(A)
    Arow = _dot_nt(sel, a_hi) + _dot_nt(sel, a_mid) + _dot_nt(sel, a_lo)
    causal = _iota((TL, TL), 0) >= _iota((TL, TL), 1)
    H = range(M_HEADS)
    sl = [slice(M_D * h, M_D * (h + 1)) for h in H]
    q = [q_ref[:, sl[h]] * (M_D ** -0.5) for h in H]
    qb = [x.astype(BF16) for x in q]
    kf = [k_ref[:, sl[h]] for h in H]
    kb = [x.astype(BF16) for x in kf]
    vf = [v_ref[:, sl[h]] for h in H]
    c_old = [cs_ref[h] for h in H]
    n_old = [ns_ref[h:h + 1, :] for h in H]
    m_col = [M[:, h:h + 1] for h in H]
    mt_col = [MT[:, h:h + 1] for h in H]
    b_col = [Bs[:, h:h + 1] for h in H]
    m_prev = [mrow[:, h:h + 1] for h in H]
    dm = [jnp.exp(jnp.where(causal, Arow[h:h + 1, :] - m_col[h], -jnp.inf)) for h in H]
    s = [_dot_nt(qb[h], kb[h]) * dm[h] for h in H]
    inter = [jnp.exp(m_prev[h] - m_col[h]) for h in H]
    h_intra = [_dot(s[h].astype(BF16), vf[h].astype(BF16)) for h in H]
    h_inter = [_dot(qb[h], c_old[h].astype(BF16)) * inter[h] for h in H]
    n_all = [jnp.sum(s[h], axis=1, keepdims=True) + jnp.sum(q[h] * n_old[h], axis=1, keepdims=True) * inter[h]
             for h in H]
    hh = [(h_intra[h] + h_inter[h]) / jnp.maximum(jnp.abs(n_all[h]), jnp.exp(-mt_col[h])) for h in H]
    m_new = [mt_col[h][TL - 1:TL, :] for h in H]
    b_end = [b_col[h][TL - 1:TL, :] for h in H]
    dec = [jnp.exp(m_prev[h] + b_end[h] - m_new[h]) for h in H]
    w_s = [jnp.exp(b_end[h] - b_col[h] + G[:, h:h + 1] - m_new[h]) for h in H]
    c_new = [c_old[h] * dec[h] + _dot_tn(kb[h], (vf[h] * w_s[h]).astype(BF16)) for h in H]
    n_new = [n_old[h] * dec[h] + jnp.sum(kf[h] * w_s[h], axis=0, keepdims=True) for h in H]
    outs = []
    for h in H:
        mu = jnp.mean(hh[h], axis=-1, keepdims=True)
        hc = hh[h] - mu
        var = jnp.mean(hc * hc, axis=-1, keepdims=True)
        hn = hc * lax.rsqrt(var + LN_EPS) * hg_ref[:, sl[h]]
        outs.append(hn * _sigmoid(o_ref[:, sl[h]]) * _silu(z_ref[:, sl[h]]))
    for h in H:
        cs_ref[h] = c_new[h]
        ns_ref[h:h + 1, :] = n_new[h]
        ms_ref[h:h + 1, :] = jnp.broadcast_to(m_new[h], (1, 128))
        y_ref[:, sl[h]] = outs[h]

    @pl.when(t == T - 1)
    def _():
        c_ref[0] = cs_ref[...]
        n_ref[0] = ns_ref[...]
        m_ref[0] = ms_ref[...]


def _mlstm_call(proj1, gate_b, hn_g, c0, n0, m0, *, N, L, NB, TL):
    T = L // TL
    RB = NB * TL
    assert NB == 1 or T == 1

    def col(cb):
        return pl.BlockSpec((RB, MIX_W), lambda i, t: (i * T + t, cb))

    st_specs = (pl.BlockSpec((NB, M_HEADS, M_D, M_D), lambda i, t: (i, 0, 0, 0)),
                pl.BlockSpec((NB, M_HEADS, M_D), lambda i, t: (i, 0, 0)),
                pl.BlockSpec((NB, 8, 128), lambda i, t: (i, 0, 0)))
    return pl.pallas_call(
        functools.partial(_mlstm_kernel, NB=NB, TL=TL, T=T),
        out_shape=(jax.ShapeDtypeStruct((N * L, MIX_W), F32),
                   jax.ShapeDtypeStruct((N, M_HEADS, M_D, M_D), F32),
                   jax.ShapeDtypeStruct((N, M_HEADS, M_D), F32),
                   jax.ShapeDtypeStruct((N, 8, 128), F32)),
        grid=(N // NB, T),
        in_specs=[col(0), col(1), col(2), col(3), col(4),
                  pl.BlockSpec((RB, 128), lambda i, t: (i * T + t, P1_GATE // 128)),
                  pl.BlockSpec((1, 128), lambda i, t: (0, 0)),
                  pl.BlockSpec((1, MIX_W), lambda i, t: (0, 0))] + list(st_specs),
        out_specs=(pl.BlockSpec((RB, MIX_W), lambda i, t: (i * T + t, 0)),) + st_specs,
        scratch_shapes=[pltpu.VMEM((NB, M_HEADS, M_D, M_D), F32),
                        pltpu.VMEM((NB, M_HEADS, M_D), F32),
                        pltpu.VMEM((NB, 8, 128), F32)],
        compiler_params=_cp(("arbitrary", "arbitrary")),
    )(proj1, proj1, proj1, proj1, proj1, proj1, gate_b, hn_g, c0, n0, m0)


N_RA_OUT = 9


def _rwkv_a_kernel(pr_ref, pk_ref, pv_ref, pwa_ref, hr_ref, hk_ref, hv_ref, hwa_ref, st_ref, stwa_ref,
                   mu_ref, muwa_ref, w0_ref, w2_ref, a0_ref, a2_ref, kkp_ref, ka_ref, rk_ref,
                   ah_ref, rh_ref, bh_ref, kh_ref, vo_ref, ul_ref, yl_ref, dc_ref, bo_ref, shr_ref, shw_ref,
                   s_at, s_rt, s_bt, s_kt, s_v, s_cum, *, RB, CT, N, L, U):
    HS = R_HEADS * CT
    HG = 128 // CT
    NG = R_HEADS // HG
    GW = HG * R_K
    NCH = RB // CT
    short = L == CT
    rid = _iota((RB, 1), 0)
    grow = pl.program_id(0) * RB + rid

    if short:
        spread = (_shr(_iota((RB, RB // CT), 0), _log2(CT)) == _iota((RB, RB // CT), 1)).astype(BF16)

    def shifted(p_ref, h_ref, s_ref, lo, hi, mu):
        p = p_ref[...]
        prev = pltpu.roll(p, 1, 0)
        if short:
            hi3, mid3, lo3 = _split3(s_ref[:, lo:hi])
            first_prev = _dot(spread, hi3) + _dot(spread, mid3) + _dot(spread, lo3)
            prev = jnp.where(jnp.bitwise_and(rid, CT - 1) == 0, first_prev, prev)
        else:
            prev = jnp.where(rid == 0, h_ref[7:8, :], prev)
            for n in range(N):
                prev = jnp.where(grow == n * L, s_ref[n:n + 1, lo:hi], prev)
        return p + (prev - p) * mu

    raw = ((pr_ref, shr_ref, 0), (pk_ref, shr_ref, 1024), (pv_ref, shr_ref, 2048), (pwa_ref, shw_ref, 0))
    if short:
        nseq = RB // CT
        sel = (_iota((nseq, RB), 1) == _iota((nseq, RB), 0) * CT + (CT - 1)).astype(BF16)
        for src, dst, lo in raw:
            hi, mid, low = _split3(src[...])
            dst[:, lo:lo + src.shape[1]] = _dot(sel, hi) + _dot(sel, mid) + _dot(sel, low)
    else:
        for n in range(N):
            tile, off = divmod(n * L + L - 1, RB)

            @pl.when(pl.program_id(0) == tile)
            def _():
                for src, dst, lo in raw:
                    dst[n:n + 1, lo:lo + src.shape[1]] = src[off:off + 1, :]

    r = shifted(pr_ref, hr_ref, st_ref, 0, 1024, mu_ref[:, 0:1024])
    k = shifted(pk_ref, hk_ref, st_ref, 1024, 2048, mu_ref[:, 1024:2048])
    v = shifted(pv_ref, hv_ref, st_ref, 2048, 3072, mu_ref[:, 2048:3072])
    wa = shifted(pwa_ref, hwa_ref, stwa_ref, 0, 128, muwa_ref[...])
    w = -_softplus(-(w0_ref[...] + _dot(jnp.tanh(wa).astype(BF16), w2_ref[...]))) - 0.5
    wlog = -jnp.exp(w)
    a = _sigmoid(a0_ref[...] + _dot(wa.astype(BF16), a2_ref[...]))
    kk = k * kkp_ref[...]
    kk = kk / jnp.maximum(jnp.sqrt(_segsum(kk * kk, R_K)), 1e-12)
    kmod = k * (1.0 + (a - 1.0) * ka_ref[...])
    bo_ref[...] = _segsum(r * kmod * rk_ref[...], R_K) * v
    cum = _row_cumsum(wlog, CT)
    einv = jnp.exp(-cum)
    s_at[...] = (-kk) * jnp.exp(cum - wlog)
    s_rt[...] = r * jnp.exp(cum)
    s_bt[...] = kk * a * einv
    s_kt[...] = kmod * einv
    s_v[...] = v
    s_cum[...] = cum

    mdt = BF16 if CT % 16 == 0 else F32
    be_mask = (_shr(_iota((128, GW), 0), _log2(CT)) == _shr(_iota((128, GW), 1), _log2(R_K))).astype(mdt)
    bd_mask = (_shr(_iota((HS, HS), 0), _log2(CT)) == _shr(_iota((HS, HS), 1), _log2(CT))).astype(mdt)
    tt = _iota((CT, HS), 0)
    ss = jnp.bitwise_and(_iota((CT, HS), 1), CT - 1)
    strict = tt > ss
    incl = tt >= ss
    eye_c = (tt == ss).astype(F32)
    cat0 = lambda *xs: jnp.concatenate(xs, axis=0)

    def blockexp(x):
        return [(jnp.concatenate([x[:, GW * g:GW * (g + 1)].astype(mdt)] * HG, axis=0) * be_mask).astype(BF16)
                for g in range(NG)]

    def gram(lhs, be):
        lb = lhs.astype(BF16)
        return jnp.concatenate([_dot_nt(lb[:, GW * g:GW * (g + 1)], be[g]) for g in range(NG)], axis=1)

    def apply(cmp, be):
        cb = cmp.astype(BF16)
        return jnp.concatenate([_dot(cb[:, 128 * g:128 * (g + 1)], be[g]) for g in range(NG)], axis=1)

    def bdiag(x):
        return (jnp.concatenate([x.astype(mdt)] * R_HEADS, axis=0) * bd_mask).astype(BF16)

    def bdiag_hl(x):
        if mdt == BF16:
            return tuple(bdiag(part) for part in _split2(x))
        return _split2(jnp.concatenate([x] * R_HEADS, axis=0) * bd_mask)

    def mm_hl(stack, wh, wl):
        sh, sl = _split2(stack)
        n = stack.shape[0]
        full = _dot(cat0(sh, sl), wh)
        return full[:n] + full[n:] + _dot(sh, wl)

    def chunks(i, carry):
        rows = [pl.ds(pl.multiple_of((i * U + u) * CT, CT), CT) for u in range(U)]
        ld = lambda ref: [ref[rw, :] for rw in rows]
        at, rt, bt, kt, vv, cm = ld(s_at), ld(s_rt), ld(s_bt), ld(s_kt), ld(s_v), ld(s_cum)
        each = lambda f, *xs: [f(*a_) for a_ in zip(*xs)]
        ar_ = each(cat0, at, rt)
        gb = each(lambda l_, y_: gram(l_, blockexp(y_)), ar_, bt)
        gk = each(lambda l_, y_: gram(l_, blockexp(y_)), ar_, kt)
        a_ab = each(lambda m: jnp.where(strict, m[:CT], 0.0), gb)
        a_rb = each(lambda m: jnp.where(incl, m[CT:], 0.0), gb)
        a_ak = each(lambda m: jnp.where(strict, m[:CT], 0.0), gk)
        a_rk = each(lambda m: jnp.where(incl, m[CT:], 0.0), gk)
        p = each(lambda m: eye_c + m, a_ab)
        x = a_ab
        q = a_rb
        w_hl = each(bdiag_hl, x)
        res = each(lambda x_, q_, w_: mm_hl(cat0(x_, q_), *w_), x, q, w_hl)
        x = each(lambda r_: r_[:CT], res)
        q = each(lambda q_, r_: q_ + r_[CT:], q, res)
        pw = 2
        while pw < CT:
            w_hl = each(bdiag_hl, x)
            if 2 * pw >= CT:
                res = each(lambda p_, q_, w_: mm_hl(cat0(p_, q_), *w_), p, q, w_hl)
                q = each(lambda q_, r_: q_ + r_[CT:], q, res)
            else:
                res = each(lambda p_, x_, q_, w_: mm_hl(cat0(p_, x_, q_), *w_), p, x, q, w_hl)
                x = each(lambda r_: r_[CT:2 * CT], res)
                q = each(lambda q_, r_: q_ + r_[2 * CT:], q, res)
            p = each(lambda p_, r_: p_ + r_[:CT], p, res)
            pw *= 2
        tq = each(cat0, p, q)
        res = each(lambda m, k_: _dot(m.astype(BF16), bdiag(k_)), tq, a_ak)
        ty = each(lambda r_, k_: cat0(r_[:CT], r_[CT:] + k_), res, a_rk)
        o1 = each(lambda m, y_: apply(m, blockexp(y_)), tq, at)
        o2 = each(lambda m, y_: apply(m, blockexp(y_)), ty, vv)
        ect = each(lambda c_: jnp.exp(c_[CT - 1:CT, :]), cm)
        for u, rw in enumerate(rows):
            ah_ref[rw, :] = o1[u][:CT]
            rh_ref[rw, :] = rt[u] + o1[u][CT:]
            ul_ref[rw, :] = o2[u][:CT]
            yl_ref[rw, :] = o2[u][CT:]
            bh_ref[rw, :] = bt[u] * ect[u]
            kh_ref[rw, :] = kt[u] * ect[u]
            vo_ref[rw, :] = vv[u]
            dc_ref[rw, :] = jnp.broadcast_to(ect[u], (CT, 1024))
        return carry

    lax.fori_loop(0, NCH // U, chunks, 0)


def _rwkv_a_call(proj1, st_rkv, st_wa, wts, *, N, L, RB, CT, U):
    short = L == CT
    rows = N * L
    assert rows % RB == 0 and (RB // CT) % U == 0

    def col(cb, width=MIX_W):
        return pl.BlockSpec((RB, width), lambda i: (i, cb))

    def halo(cb, width=MIX_W):
        return pl.BlockSpec((8, width), lambda i: (jnp.maximum(i * (RB // 8) - 1, 0), cb))

    if short:
        st_specs = [pl.BlockSpec((RB // CT, 3072), lambda i: (i, 0)), pl.BlockSpec((RB // CT, 128), lambda i: (i, 0))]
    else:
        st_specs = [pl.BlockSpec((N, 3072), lambda i: (0, 0)), pl.BlockSpec((N, 128), lambda i: (0, 0))]

    def full(shape):
        return pl.BlockSpec(shape, lambda i: (0,) * len(shape))

    o_spec = pl.BlockSpec((RB, MIX_W), lambda i: (i, 0))
    if short:
        sh_specs = (pl.BlockSpec((RB // CT, 3072), lambda i: (i, 0)), pl.BlockSpec((RB // CT, 128), lambda i: (i, 0)))
    else:
        sh_specs = (pl.BlockSpec((N, 3072), lambda i: (0, 0)), pl.BlockSpec((N, 128), lambda i: (0, 0)))
    outs = pl.pallas_call(
        functools.partial(_rwkv_a_kernel, RB=RB, CT=CT, N=N, L=L, U=U),
        out_shape=((jax.ShapeDtypeStruct((rows, MIX_W), F32),) * N_RA_OUT
                   + (jax.ShapeDtypeStruct((N, 3072), F32), jax.ShapeDtypeStruct((N, 128), F32))),
        grid=(rows // RB,),
        in_specs=([col(P1_R // 1024), col(P1_RK // 1024), col(P1_RV // 1024), col(P1_WA // 128, 128),
                   halo(P1_R // 1024), halo(P1_RK // 1024), halo(P1_RV // 1024), halo(P1_WA // 128, 128)]
                  + st_specs
                  + [full((1, 3072)), full((1, 128)), full((1, 1024)), full((128, 1024)), full((1, 1024)),
                     full((128, 1024)), full((1, 1024)), full((1, 1024)), full((1, 1024))]),
        out_specs=(o_spec,) * N_RA_OUT + sh_specs,
        scratch_shapes=[pltpu.VMEM((RB, MIX_W), F32)] * 6,
        compiler_params=_cp(("arbitrary",)),
    )(*([proj1] * 8 + [st_rkv, st_wa] + list(wts)))
    return outs[:N_RA_OUT], jnp.concatenate(outs[N_RA_OUT:], axis=1)


def _rwkv_b_kernel(ah_ref, rh_ref, bh_ref, kh_ref, v_ref, ul_ref, yl_ref, dc_ref, bo_ref, z_ref, lg_ref, lb_ref,
                   s0_ref, y_ref, so_ref, sbd_ref, yb_ref, *, NBLK, TLB, CT, T):
    t = pl.program_id(1)
    bd_mask = (_shr(_iota((256, 256), 0), 6) == _shr(_iota((256, 256), 1), 6)).astype(F32)

    @pl.when(t == 0)
    def _():
        for nb in range(NBLK):
            for j in range(4):
                x = s0_ref[nb, 256 * j:256 * (j + 1), :]
                x = jnp.concatenate([x, x], axis=1)
                sbd_ref[4 * nb + j] = jnp.concatenate([x, x], axis=1) * bd_mask

    chains = [(nb, j, slice(256 * j, 256 * (j + 1))) for nb in range(NBLK) for j in range(4)]
    for c in range(TLB // CT):
        rows = slice(c * CT, (c + 1) * CT)
        sbs = [sbd_ref[4 * nb + j] for nb, j, cs in chains]
        outs = [_dot_nt(jnp.concatenate([ah_ref[nb, rows, cs], rh_ref[nb, rows, cs]], axis=0).astype(BF16),
                        sb.astype(BF16))
                for (nb, j, cs), sb in zip(chains, sbs)]
        upds = [_dot_tn(jnp.concatenate([o[:CT] + ul_ref[nb, rows, cs], v_ref[nb, rows, cs]],
                                        axis=0).astype(BF16),
                        jnp.concatenate([bh_ref[nb, rows, cs], kh_ref[nb, rows, cs]], axis=0).astype(BF16))
                for (nb, j, cs), o in zip(chains, outs)]
        for (nb, j, cs), sb, o, upd in zip(chains, sbs, outs, upds):
            yb_ref[nb, rows, cs] = o[CT:] + yl_ref[nb, rows, cs]
            sbd_ref[4 * nb + j] = sb * dc_ref[nb, c * CT:c * CT + 1, cs] + upd * bd_mask

    @pl.when(t == T - 1)
    def _():
        for nb in range(NBLK):
            outs = []
            for j in range(4):
                sb = sbd_ref[4 * nb + j]
                half = sb[:, :128] + sb[:, 128:]
                outs.append(half[:, :64] + half[:, 64:])
            so_ref[nb] = jnp.concatenate(outs, axis=0)

    cat = lambda ref: jnp.concatenate([ref[nb] for nb in range(NBLK)], axis=0)
    y = cat(yb_ref)
    mu = _segsum(y, R_K) * (1.0 / R_K)
    yc = y - mu
    var = _segsum(yc * yc, R_K) * (1.0 / R_K)
    yn = yc * lax.rsqrt(var + R_LN_EPS) * lg_ref[...] + lb_ref[...] + cat(bo_ref)
    out = yn * _silu(cat(z_ref))
    for nb in range(NBLK):
        y_ref[nb] = out[nb * TLB:(nb + 1) * TLB, :]


def _rwkv_b_call(ra, proj1, ln_g, ln_b, s0, *, N, L, NBLK, TLB, CT):
    T = L // TLB
    blk = lambda cb: pl.BlockSpec((NBLK, TLB, MIX_W), lambda i, t: (i, t, cb))
    s_spec = pl.BlockSpec((NBLK, 1024, 64), lambda i, t: (i, 0, 0))
    ra3 = [a.reshape(N, L, MIX_W) for a in ra]
    y, s_new = pl.pallas_call(
        functools.partial(_rwkv_b_kernel, NBLK=NBLK, TLB=TLB, CT=CT, T=T),
        out_shape=(jax.ShapeDtypeStruct((N, L, MIX_W), F32),
                   jax.ShapeDtypeStruct((N, 1024, 64), F32)),
        grid=(N // NBLK, T),
        in_specs=([blk(0)] * N_RA_OUT
                  + [blk(P1_ZD // 1024),
                     pl.BlockSpec((1, MIX_W), lambda i, t: (0, 0)),
                     pl.BlockSpec((1, MIX_W), lambda i, t: (0, 0)),
                     s_spec]),
        out_specs=(blk(0), s_spec),
        scratch_shapes=[pltpu.VMEM((4 * NBLK, 256, 256), F32),
                        pltpu.VMEM((NBLK, TLB, MIX_W), F32)],
        compiler_params=_cp(("arbitrary", "arbitrary")),
    )(*(ra3 + [proj1.reshape(N, L, P1_N), ln_g, ln_b, s0]))
    return y.reshape(N * L, MIX_W), s_new


def _regroup_w1(w1):
    wt = w1.T
    return jnp.concatenate([wt[0:4096], wt[4104:8200], wt[8328:9352], wt[8200:8328], wt[4096:4104],
                            jnp.zeros((P1_N - 9352, D_MODEL), F32)], axis=0).astype(BF16)


def _s5_weights(lam_re, lam_im, log_dt, b_re, b_im, c_re, c_im):
    dt = jnp.exp(log_dt)[:, None]
    mag = jnp.exp(lam_re * dt)
    ar = mag * jnp.cos(lam_im * dt)
    ai = mag * jnp.sin(lam_im * dt)
    den = lam_re * lam_re + lam_im * lam_im
    qr = ((ar - 1.0) * lam_re + ai * lam_im) / den
    qi = (ai * lam_re - (ar - 1.0) * lam_im) / den
    bbr = qr[..., None] * b_re - qi[..., None] * b_im
    bbi = qr[..., None] * b_im + qi[..., None] * b_re
    hp = lax.Precision.HIGHEST
    spread_p = (jnp.arange(1024)[None, :] % S5_STATE == jnp.arange(S5_STATE)[:, None]).astype(F32)
    spread_h = (jnp.arange(256)[None, :] % S5_GROUP == jnp.arange(S5_GROUP)[:, None]).astype(F32)
    grp_in = (jnp.arange(256)[:, None] // S5_GROUP == jnp.arange(1024)[None, :] // S5_STATE)
    grp_out = (jnp.arange(1024)[:, None] // S5_STATE == jnp.arange(256)[None, :] // S5_GROUP)

    def in_blocks(bb):
        rows = bb.transpose(0, 2, 1).reshape(4, 256, S5_STATE)
        return jnp.matmul(rows, spread_p, precision=hp) * grp_in

    def out_blocks(cc):
        rows = cc.transpose(0, 2, 1).reshape(4, 1024, S5_GROUP)
        return jnp.matmul(rows, spread_h, precision=hp) * grp_out

    wb = jnp.concatenate([in_blocks(bbr), in_blocks(bbi)], axis=2).astype(BF16)
    wc = jnp.concatenate([out_blocks(c_re), out_blocks(-c_im)], axis=1).astype(BF16)
    pr, pi = jnp.ones_like(ar), jnp.zeros_like(ai)
    lag = []
    for _ in range(8):
        cpr = c_re * pr[:, None, :] - c_im * pi[:, None, :]
        cpi = c_re * pi[:, None, :] + c_im * pr[:, None, :]
        lag.append(jnp.einsum('gop,gph->goh', cpr, bbr, precision=lax.Precision.HIGHEST)
                   - jnp.einsum('gop,gph->goh', cpi, bbi, precision=lax.Precision.HIGHEST))
        pr, pi = pr * ar - pi * ai, pr * ai + pi * ar
    kd = jnp.stack(lag).reshape(8, 4, 16, S5_GROUP, S5_GROUP)
    kd = kd.transpose(1, 0, 2, 4, 3).reshape(4, 8, 256, S5_GROUP)
    spread = (jnp.arange(256)[None, :] % S5_GROUP == jnp.arange(S5_GROUP)[:, None]).astype(F32)
    same_group = (jnp.arange(256)[:, None] // S5_GROUP == jnp.arange(256)[None, :] // S5_GROUP)
    wk = jnp.matmul(kd, spread, precision=lax.Precision.HIGHEST) * same_group
    wk = wk.reshape(4, 8 * 256, 256).astype(BF16)
    return ar.reshape(1, 4096), ai.reshape(1, 4096), wb, wk, wc


CFG = {
    "P": dict(N=BATCH, L=P_LEN, tm_mm=1376, tm=688, tm_ln=688, drop_meta=True,
              conv=dict(NB=1, TL=344), s5=dict(NB=1, TL=688), mlstm=dict(NB=1, TL=344),
              ra=dict(RB=192, CT=16, U=12), rb=dict(NBLK=4, TLB=48, CT=16)),
    "S": dict(N=DEC_BATCH, L=DEC_SEQ, tm_mm=1024, tm=512, tm_ln=256, drop_meta=False,
              conv=dict(NB=16, TL=8), s5=dict(NB=32, TL=8), mlstm=dict(NB=4, TL=8),
              ra=dict(RB=256, CT=8, U=8), rb=dict(NBLK=8, TLB=8, CT=8)),
}


def _trunk(x, st, w, cfg):
    n, l = cfg["N"], cfg["L"]
    proj0 = _matmul(x, w["w_in0"], cfg["tm_mm"], 1024)
    act, conv_new = _conv_call(proj0, st["conv"], w["conv_w"], w["conv_b"], w["a_ln_g"], w["a_ln_b"],
                               N=n, L=l, **cfg["conv"])
    mix_a = _pw_gate(act, w["pw"], proj0, tm=cfg["tm_mm"])
    yb, xr, xi = _s5_call(proj0, w["s5_wb"], w["s5_wk"], w["s5_wc"], w["s5_d"], w["s5_ar"], w["s5_ai"],
                          st["ssm_re"].reshape(n, 1, 4096), st["ssm_im"].reshape(n, 1, 4096),
                          N=n, L=l, **cfg["s5"])
    mix_b = _glu_gate(yb, w["glu_w"], w["glu_b"], proj0, tm=cfg["tm_mm"])
    x1 = _out_ln(x, mix_a, mix_b, w["w_out0"], w["ln_g0"], w["ln_b0"], tm=cfg["tm_ln"])

    proj1 = _matmul_nt(x1, w["w_in1"], cfg["tm_mm"], 512)
    m0 = jnp.pad(jnp.broadcast_to(st["m"][:, :, None], (n, M_HEADS, 128)), ((0, 0), (0, 4), (0, 0)))
    mix_c, c_new, n_new, m_new = _mlstm_call(proj1, w["gate_b"], w["hn_g"], st["c"], st["n"], m0,
                                             N=n, L=l, **cfg["mlstm"])
    sh = st["shift"]
    ra, shift_new = _rwkv_a_call(proj1, sh[:, :3072], sh[:, 3072:], w["rwkv"], N=n, L=l, **cfg["ra"])
    mix_d, s_new = _rwkv_b_call(ra, proj1, w["r_ln_g"], w["r_ln_b"], st["s"].reshape(n, 1024, 64),
                                N=n, L=l, **cfg["rb"])
    final_ln = _out_ln_prompt if cfg["drop_meta"] else functools.partial(_out_ln, tm=cfg["tm_ln"])
    y = final_ln(x1, mix_c, mix_d, w["w_out1"], w["ln_g1"], w["ln_b1"])

    states = (conv_new[None],
              xr.reshape(n, S5_GROUPS, S5_STATE)[None],
              xi.reshape(n, S5_GROUPS, S5_STATE)[None],
              c_new[None], n_new[None], m_new[:, :M_HEADS, 0][None],
              s_new.reshape(n, R_HEADS, R_K, R_K)[None],
              shift_new[None])
    return y, states


def kernel(x_prompt, x_sample, state_conv, state_ssm_re, state_ssm_im, state_mlstm_c, state_mlstm_n, state_mlstm_m, state_rwkv_s, state_rwkv_shift, meta_tokens, ev_w_in, a_conv_w, a_conv_b, a_ln_g, a_ln_b, a_pw, s5_lambda_re, s5_lambda_im, s5_log_dt, s5_b_re, s5_b_im, s5_c_re, s5_c_im, s5_d, s5_glu_w, s5_glu_b, ev_w_out, ev_ln_g, ev_ln_b, od_w_in, m_ig_b, m_fg_b, m_hn_g, r_mu, r_w0, r_w2, r_a0, r_a2, r_kk, r_ka, r_rk, r_ln_g, r_ln_b, od_w_out, od_ln_g, od_ln_b):
    nb = x_prompt.shape[0]
    row = lambda vec: vec.reshape(1, -1)
    zeros = lambda *s: jnp.zeros(s, F32)

    ar, ai, wb, wk, wc = _s5_weights(s5_lambda_re[0], s5_lambda_im[0], s5_log_dt[0], s5_b_re[0], s5_b_im[0],
                                     s5_c_re[0], s5_c_im[0])
    w_in1 = _regroup_w1(od_w_in[0])
    mu = r_mu[0]
    w = dict(
        w_in0=ev_w_in[0].astype(BF16), conv_w=a_conv_w[0], conv_b=row(a_conv_b[0]),
        a_ln_g=row(a_ln_g[0]), a_ln_b=row(a_ln_b[0]), pw=a_pw[0].astype(BF16),
        s5_wb=wb, s5_wk=wk, s5_wc=wc, s5_d=row(s5_d[0]), s5_ar=ar, s5_ai=ai,
        glu_w=s5_glu_w[0].astype(BF16), glu_b=row(s5_glu_b[0]),
        w_out0=ev_w_out[0].astype(BF16), ln_g0=row(ev_ln_g[0]), ln_b0=row(ev_ln_b[0]),
        w_in1=w_in1,
        gate_b=jnp.concatenate([m_ig_b[0], m_fg_b[0], jnp.zeros((120,), F32)]).reshape(1, 128),
        hn_g=row(m_hn_g[0]),
        rwkv=[row(mu[:3072]), row(mu[3072:]), row(r_w0[0]),
              jnp.concatenate([r_w2[0], jnp.zeros((64, MIX_W), F32)], axis=0).astype(BF16),
              row(r_a0[0]),
              jnp.concatenate([jnp.zeros((64, MIX_W), F32), r_a2[0]], axis=0).astype(BF16),
              row(r_kk[0]), row(r_ka[0]), row(r_rk[0])],
        r_ln_g=row(r_ln_g[0]), r_ln_b=row(r_ln_b[0]),
        w_out1=od_w_out[0].astype(BF16), ln_g1=row(od_ln_g[0]), ln_b1=row(od_ln_b[0]),
    )

    x_p = jnp.concatenate([jnp.broadcast_to(meta_tokens[None], (nb, N_META, D_MODEL)), x_prompt],
                          axis=1).reshape(nb * P_LEN, D_MODEL)
    st_p = dict(conv=zeros(nb, CONV_W - 1, MIX_W), ssm_re=zeros(nb, 4096), ssm_im=zeros(nb, 4096),
                c=zeros(nb, M_HEADS, M_D, M_D), n=zeros(nb, M_HEADS, M_D), m=zeros(nb, M_HEADS),
                s=zeros(nb, R_HEADS, R_K, R_K), shift=zeros(nb, 3200))
    y_p, states_p = _trunk(x_p, st_p, w, CFG["P"])

    st_s = dict(conv=state_conv[0], ssm_re=state_ssm_re[0], ssm_im=state_ssm_im[0],
                c=state_mlstm_c[0], n=state_mlstm_n[0], m=state_mlstm_m[0],
                s=state_rwkv_s[0], shift=state_rwkv_shift[0])
    y_s, states_s = _trunk(x_sample.reshape(DEC_BATCH * DEC_SEQ, D_MODEL), st_s, w, CFG["S"])

    y_prompt = y_p
    y_sample = y_s.reshape(DEC_BATCH, DEC_SEQ, D_MODEL)
    return (y_prompt, y_sample) + states_p + states_s
```

```python
import functools
import math

import jax
import jax.numpy as jnp
from jax import lax
from jax.experimental import pallas as pl
from jax.experimental.pallas import tpu as pltpu

F32 = jnp.float32
BF16 = jnp.bfloat16

D_MODEL = 2048
MIX_W = 1024
N_META = 16
CONV_W = 31
S5_GROUP = 16
S5_GROUPS = 64
S5_STATE = 64
M_HEADS = 4
M_D = 256
R_HEADS = 16
R_K = 64
LN_EPS = 1e-5
R_LN_EPS = 64e-5
DEPTH = 2
ALPHA = (2 * DEPTH) ** 0.25

BATCH = 4
SEQ = 2048
P_LEN = N_META + SEQ
DEC_BATCH = 128
DEC_SEQ = 8

P1_Q, P1_K, P1_V, P1_O, P1_ZC = 0, 1024, 2048, 3072, 4096
P1_R, P1_RK, P1_RV, P1_ZD, P1_WA, P1_GATE = 5120, 6144, 7168, 8192, 9216, 9344
P1_N = 9728

VMEM_LIMIT = 48 * 1024 * 1024


def _cp(sem):
    return pltpu.CompilerParams(dimension_semantics=sem, vmem_limit_bytes=VMEM_LIMIT)


def _dot(a, b):
    return jnp.dot(a, b, preferred_element_type=F32)


def _dot_nt(a, b):
    return lax.dot_general(a, b, (((1,), (1,)), ((), ())), preferred_element_type=F32)


def _dot_tn(a, b):
    return lax.dot_general(a, b, (((0,), (0,)), ((), ())), preferred_element_type=F32)


def _split2(x):
    hi = x.astype(BF16)
    lo = (x - hi.astype(F32)).astype(BF16)
    return hi, lo


def _split3(x):
    hi = x.astype(BF16)
    r1 = x - hi.astype(F32)
    mid = r1.astype(BF16)
    lo = (r1 - mid.astype(F32)).astype(BF16)
    return hi, mid, lo


def _sigmoid(x):
    return jax.nn.sigmoid(x)


def _silu(x):
    return x * jax.nn.sigmoid(x)


def _softplus(x):
    return jnp.maximum(x, 0.0) + jnp.log(1.0 + jnp.exp(-jnp.abs(x)))


def _gelu_tanh(x):
    c = math.sqrt(2.0 / math.pi)
    return x * (0.5 * (1.0 + jnp.tanh(c * (x + 0.044715 * (x * x * x)))))


def _iota(shape, axis):
    return lax.broadcasted_iota(jnp.int32, shape, axis)


def _shr(x, k):
    return lax.shift_right_logical(x, jnp.int32(k))


def _log2(n):
    k = int(round(math.log2(n)))
    assert 1 << k == n
    return k


def _block_ones(n, seg, dtype):
    r = _shr(_iota((n, n), 0), _log2(seg))
    c = _shr(_iota((n, n), 1), _log2(seg))
    return (r == c).astype(dtype)


def _segsum(x, seg):
    g = _block_ones(256, seg, BF16)
    outs = []
    for j in range(x.shape[1] // 256):
        hi, lo = _split2(x[:, 256 * j:256 * (j + 1)])
        outs.append(_dot(hi, g) + _dot(lo, g))
    return jnp.concatenate(outs, axis=1)


def _row_cumsum(x, period):
    rows = x.shape[0]
    rid = _iota(x.shape, 0)
    if period < rows:
        rid = jnp.bitwise_and(rid, period - 1)
    d = 1
    while d < min(period, rows):
        x = x + jnp.where(rid >= d, pltpu.roll(x, d, 0), 0.0)
        d *= 2
    return x


def _row_cummax(x, period):
    rows = x.shape[0]
    rid = _iota(x.shape, 0)
    if period < rows:
        rid = jnp.bitwise_and(rid, period - 1)
    d = 1
    while d < min(period, rows):
        x = jnp.maximum(x, jnp.where(rid >= d, pltpu.roll(x, d, 0), -jnp.inf))
        d *= 2
    return x


def _mm_kernel(x_ref, w_ref, o_ref):
    o_ref[...] = _dot(x_ref[...].astype(BF16), w_ref[...])


def _matmul(x, w, tm, tn):
    r, k = x.shape
    n = w.shape[1]
    return pl.pallas_call(
        _mm_kernel,
        out_shape=jax.ShapeDtypeStruct((r, n), F32),
        grid=(pl.cdiv(r, tm), n // tn),
        in_specs=[pl.BlockSpec((tm, k), lambda i, j: (i, 0)),
                  pl.BlockSpec((k, tn), lambda i, j: (0, j))],
        out_specs=pl.BlockSpec((tm, tn), lambda i, j: (i, j)),
        compiler_params=_cp(("parallel", "arbitrary")),
    )(x, w)


def _mm_nt_kernel(x_ref, w_ref, o_ref):
    o_ref[...] = _dot_nt(x_ref[...].astype(BF16), w_ref[...])


def _matmul_nt(x, w_t, tm, tn):
    r, k = x.shape
    n = w_t.shape[0]
    return pl.pallas_call(
        _mm_nt_kernel,
        out_shape=jax.ShapeDtypeStruct((r, n), F32),
        grid=(pl.cdiv(r, tm), n // tn),
        in_specs=[pl.BlockSpec((tm, k), lambda i, j: (i, 0)),
                  pl.BlockSpec((tn, k), lambda i, j: (j, 0))],
        out_specs=pl.BlockSpec((tm, tn), lambda i, j: (i, j)),
        compiler_params=_cp(("parallel", "arbitrary")),
    )(x, w_t)


def _pw_kernel(a_ref, w_ref, z_ref, o_ref):
    o_ref[...] = _dot(a_ref[...].astype(BF16), w_ref[...]) * _silu(z_ref[...])


def _pw_gate(act, pw, proj0, tm, tn=512):
    r = act.shape[0]
    zb = 2048 // tn
    return pl.pallas_call(
        _pw_kernel,
        out_shape=jax.ShapeDtypeStruct((r, MIX_W), F32),
        grid=(pl.cdiv(r, tm), MIX_W // tn),
        in_specs=[pl.BlockSpec((tm, MIX_W), lambda i, j: (i, 0)),
                  pl.BlockSpec((MIX_W, tn), lambda i, j: (0, j)),
                  pl.BlockSpec((tm, tn), lambda i, j: (i, zb + j))],
        out_specs=pl.BlockSpec((tm, tn), lambda i, j: (i, j)),
        compiler_params=_cp(("parallel", "arbitrary")),
    )(act, pw, proj0)


def _glu_kernel(y_ref, wv_ref, wg_ref, bv_ref, bg_ref, z_ref, o_ref):
    y = y_ref[...].astype(BF16)
    v = _dot(y, wv_ref[...]) + bv_ref[...]
    g = _dot(y, wg_ref[...]) + bg_ref[...]
    o_ref[...] = v * _sigmoid(g) * _silu(z_ref[...])


def _glu_gate(yb, glu_w, glu_b, proj0, tm, tn=512):
    r = yb.shape[0]
    nb = MIX_W // tn
    zb = 4096 // tn
    return pl.pallas_call(
        _glu_kernel,
        out_shape=jax.ShapeDtypeStruct((r, MIX_W), F32),
        grid=(pl.cdiv(r, tm), nb),
        in_specs=[pl.BlockSpec((tm, MIX_W), lambda i, j: (i, 0)),
                  pl.BlockSpec((MIX_W, tn), lambda i, j: (0, j)),
                  pl.BlockSpec((MIX_W, tn), lambda i, j: (0, nb + j)),
                  pl.BlockSpec((1, tn), lambda i, j: (0, j)),
                  pl.BlockSpec((1, tn), lambda i, j: (0, nb + j)),
                  pl.BlockSpec((tm, tn), lambda i, j: (i, zb + j))],
        out_specs=pl.BlockSpec((tm, tn), lambda i, j: (i, j)),
        compiler_params=_cp(("parallel", "arbitrary")),
    )(yb, glu_w, glu_w, glu_b, glu_b, proj0)


def _out_ln_kernel(x_ref, ma_ref, mb_ref, wa_ref, wb_ref, g_ref, b_ref, o_ref):
    out = _dot(ma_ref[...].astype(BF16), wa_ref[...]) + _dot(mb_ref[...].astype(BF16), wb_ref[...])
    y = ALPHA * x_ref[...] + out
    mu = jnp.mean(y, axis=-1, keepdims=True)
    yc = y - mu
    var = jnp.mean(yc * yc, axis=-1, keepdims=True)
    o_ref[...] = yc * lax.rsqrt(var + LN_EPS) * g_ref[...] + b_ref[...]


def _out_ln_bf16_kernel(x_ref, ma_ref, mb_ref, wa_ref, wb_ref, g_ref, b_ref, o_ref, ob_ref):
    _out_ln_kernel(x_ref, ma_ref, mb_ref, wa_ref, wb_ref, g_ref, b_ref, o_ref)
    ob_ref[...] = o_ref[...].astype(BF16)


def _out_ln(x, mix_a, mix_b, w_out, ln_g, ln_b, tm, with_bf16=False):
    r = x.shape[0]
    o_spec = pl.BlockSpec((tm, D_MODEL), lambda i: (i, 0))
    return pl.pallas_call(
        _out_ln_bf16_kernel if with_bf16 else _out_ln_kernel,
        out_shape=((jax.ShapeDtypeStruct((r, D_MODEL), F32), jax.ShapeDtypeStruct((r, D_MODEL), BF16))
                   if with_bf16 else jax.ShapeDtypeStruct((r, D_MODEL), F32)),
        grid=(pl.cdiv(r, tm),),
        in_specs=[pl.BlockSpec((tm, D_MODEL), lambda i: (i, 0)),
                  pl.BlockSpec((tm, MIX_W), lambda i: (i, 0)),
                  pl.BlockSpec((tm, MIX_W), lambda i: (i, 0)),
                  pl.BlockSpec((MIX_W, D_MODEL), lambda i: (0, 0), pipeline_mode=pl.Buffered(1)),
                  pl.BlockSpec((MIX_W, D_MODEL), lambda i: (1, 0), pipeline_mode=pl.Buffered(1)),
                  pl.BlockSpec((1, D_MODEL), lambda i: (0, 0)),
                  pl.BlockSpec((1, D_MODEL), lambda i: (0, 0))],
        out_specs=(o_spec, o_spec) if with_bf16 else o_spec,
        compiler_params=_cp(("parallel",)),
    )(x, mix_a, mix_b, w_out, w_out, ln_g, ln_b)


def _out_ln_prompt(x, mix_a, mix_b, w_out, ln_g, ln_b, tm=512):
    tiles = SEQ // tm

    def rows(width):
        return pl.BlockSpec((pl.Element(tm), pl.Element(width)),
                            lambda n, t: (pl.multiple_of(n * P_LEN + N_META + t * tm, 8), 0))

    return pl.pallas_call(
        _out_ln_kernel,
        out_shape=jax.ShapeDtypeStruct((BATCH * SEQ, D_MODEL), F32),
        grid=(BATCH, tiles),
        in_specs=[rows(D_MODEL), rows(MIX_W), rows(MIX_W),
                  pl.BlockSpec((MIX_W, D_MODEL), lambda n, t: (0, 0), pipeline_mode=pl.Buffered(1)),
                  pl.BlockSpec((MIX_W, D_MODEL), lambda n, t: (1, 0), pipeline_mode=pl.Buffered(1)),
                  pl.BlockSpec((1, D_MODEL), lambda n, t: (0, 0)),
                  pl.BlockSpec((1, D_MODEL), lambda n, t: (0, 0))],
        out_specs=pl.BlockSpec((tm, D_MODEL), lambda n, t: (n * tiles + t, 0)),
        compiler_params=_cp(("parallel", "arbitrary")),
    )(x, mix_a, mix_b, w_out, w_out, ln_g, ln_b).reshape(BATCH, SEQ, D_MODEL)


def _conv_kernel(u_ref, g_ref, st_ref, w_ref, cb_ref, lg_ref, lb_ref, act_ref, nst_ref, hp_ref, hs_ref, wb_ref,
                 *, NB, TL, T):
    t = pl.program_id(1)
    for j in range(CONV_W):
        wb_ref[j] = jnp.broadcast_to(w_ref[j:j + 1, :], (8, MIX_W))
    for nb in range(NB):
        base = nb * TL

        @pl.when(t == 0)
        def _():
            hp_ref[nb, 0:2, :] = jnp.zeros((2, MIX_W), F32)
            hp_ref[nb, 2:32, :] = st_ref[nb]

        hp_ref[nb, TL + 32:TL + 40, :] = jnp.zeros((8, MIX_W), F32)
        hp_ref[nb, 32:32 + TL, :] = u_ref[base:base + TL, :] * _sigmoid(g_ref[base:base + TL, :])
        for b in range(8):
            hs_ref[b] = hp_ref[nb, b:b + TL + 32, :]

        def taps(r0, groups):
            acc = [None] * groups
            for j in range(CONV_W):
                o = j + 2
                wj = wb_ref[j]
                for g in range(groups):
                    term = wj * hs_ref[o % 8, pl.ds(r0 + 8 * (o // 8 + g), 8), :]
                    acc[g] = term if acc[g] is None else acc[g] + term
            for g in range(groups):
                act_ref[pl.ds(base + r0 + 8 * g, 8), :] = acc[g]

        def chunk(c, carry):
            taps(pl.multiple_of(c * 16, 16), 2)
            return carry

        lax.fori_loop(0, TL // 16, chunk, 0)
        if TL % 16:
            taps(TL - 8, 1)

        @pl.when(t == T - 1)
        def _():
            nst_ref[nb] = hp_ref[nb, TL + 2:TL + 32, :]

        if T > 1:
            hp_ref[nb, 0:32, :] = hp_ref[nb, TL:TL + 32, :]

    y = act_ref[...] + cb_ref[...]
    mu = jnp.mean(y, axis=-1, keepdims=True)
    yc = y - mu
    var = jnp.mean(yc * yc, axis=-1, keepdims=True)
    act_ref[...] = _silu(yc * lax.rsqrt(var + LN_EPS) * lg_ref[...] + lb_ref[...])


def _conv_call(proj0, state, conv_w, conv_b, ln_g, ln_b, *, N, L, NB, TL):
    T = L // TL
    RB = NB * TL
    assert NB == 1 or T == 1
    return pl.pallas_call(
        functools.partial(_conv_kernel, NB=NB, TL=TL, T=T),
        out_shape=(jax.ShapeDtypeStruct((N * L, MIX_W), F32),
                   jax.ShapeDtypeStruct((N, CONV_W - 1, MIX_W), F32)),
        grid=(N // NB, T),
        in_specs=[pl.BlockSpec((RB, MIX_W), lambda i, t: (i * T + t, 0)),
                  pl.BlockSpec((RB, MIX_W), lambda i, t: (i * T + t, 1)),
                  pl.BlockSpec((NB, CONV_W - 1, MIX_W), lambda i, t: (i, 0, 0)),
                  pl.BlockSpec((CONV_W, MIX_W), lambda i, t: (0, 0)),
                  pl.BlockSpec((1, MIX_W), lambda i, t: (0, 0)),
                  pl.BlockSpec((1, MIX_W), lambda i, t: (0, 0)),
                  pl.BlockSpec((1, MIX_W), lambda i, t: (0, 0))],
        out_specs=(pl.BlockSpec((RB, MIX_W), lambda i, t: (i * T + t, 0)),
                   pl.BlockSpec((NB, CONV_W - 1, MIX_W), lambda i, t: (i, 0, 0))),
        scratch_shapes=[pltpu.VMEM((NB, TL + 40, MIX_W), F32),
                        pltpu.VMEM((8, TL + 32, MIX_W), F32),
                        pltpu.VMEM((CONV_W, 8, MIX_W), F32)],
        compiler_params=_cp(("arbitrary", "arbitrary")),
    )(proj0, proj0, state, conv_w, conv_b, ln_g, ln_b)


def _s5_kernel(u_ref, wb_ref, wk_ref, wc_ref, d_ref, ar_ref, ai_ref, x0r_ref, x0i_ref,
               y_ref, xfr_ref, xfi_ref, xs_ref, cr_ref, ci_ref, *, NB, TL, T):
    t = pl.program_id(2)
    RB = NB * TL
    GL = TL // 8
    u = u_ref[...]
    ub = u.astype(BF16)
    big = _dot(ub, wb_ref[0])
    xs_ref[0] = big[:, :1024]
    xs_ref[1] = big[:, 1024:]
    rid = jnp.bitwise_and(_iota((RB, 256), 0), 7)
    lags = [ub] + [jnp.where(rid >= d, pltpu.roll(u, d, 0), 0.0).astype(BF16) for d in range(1, 8)]
    y_loc = _dot(jnp.concatenate(lags, axis=1), wk_ref[0])
    ar = ar_ref[...]
    ai = ai_ref[...]

    def cmul(pr, pi, qr, qi):
        return pr * qr - pi * qi, pr * qi + pi * qr

    a1 = (ar, ai)
    a2 = cmul(*a1, *a1)
    a4 = cmul(*a2, *a2)
    a3 = cmul(*a2, *a1)
    a5 = cmul(*a4, *a1)
    a6 = cmul(*a4, *a2)
    a7 = cmul(*a6, *a1)
    a8 = cmul(*a4, *a4)
    a0 = (jnp.ones_like(ar), jnp.zeros_like(ai))
    r8 = _iota((8, 1024), 0)

    def table(powers):
        tr = jnp.zeros((8, 1024), F32)
        ti = jnp.zeros((8, 1024), F32)
        for k, (pr, pi) in enumerate(powers):
            tr = jnp.where(r8 == k, pr, tr)
            ti = jnp.where(r8 == k, pi, ti)
        return tr, ti

    pwr, pwi = table((a1, a2, a3, a4, a5, a6, a7, a8))
    qwr, qwi = table((a7, a6, a5, a4, a3, a2, a1, a0))
    a8r, a8i = a8

    first = t == 0

    def seq_body(nb, carry0):
        x0r = x0r_ref[nb]
        x0i = x0i_ref[nb]
        if T > 1:
            c_r = jnp.where(first, x0r, cr_ref[0:1, :])
            c_i = jnp.where(first, x0i, ci_ref[0:1, :])
        else:
            c_r, c_i = x0r, x0i

        def grp(g, c):
            c_r, c_i = c
            off = pl.multiple_of(nb * TL + g * 8, 8)
            vr = xs_ref[0, pl.ds(off, 8), :]
            vi = xs_ref[1, pl.ds(off, 8), :]
            er = jnp.sum(qwr * vr - qwi * vi, axis=0, keepdims=True)
            ei = jnp.sum(qwr * vi + qwi * vr, axis=0, keepdims=True)
            br = jnp.broadcast_to(c_r, (8, 1024))
            bi = jnp.broadcast_to(c_i, (8, 1024))
            xs_ref[0, pl.ds(off, 8), :] = pwr * br - pwi * bi
            xs_ref[1, pl.ds(off, 8), :] = pwr * bi + pwi * br
            return a8r * c_r - a8i * c_i + er, a8r * c_i + a8i * c_r + ei

        c_r, c_i = lax.fori_loop(0, GL, grp, (c_r, c_i))
        if T > 1:
            cr_ref[...] = jnp.broadcast_to(c_r, (8, 1024))
            ci_ref[...] = jnp.broadcast_to(c_i, (8, 1024))

        @pl.when(t == T - 1)
        def _():
            xfr_ref[nb] = c_r
            xfi_ref[nb] = c_i

        return carry0

    lax.fori_loop(0, NB, seq_body, 0)
    y = (_dot(xs_ref[0].astype(BF16), wc_ref[0, 0:1024, :])
         + _dot(xs_ref[1].astype(BF16), wc_ref[0, 1024:2048, :]))
    y_ref[...] = _gelu_tanh(y + y_loc + d_ref[...] * u)


def _s5_call(proj0, wb, wk, wc, dvec, ar, ai, x0r, x0i, *, N, L, NB, TL):
    T = L // TL
    RB = NB * TL
    assert NB == 1 or T == 1
    ub = 3072 // 256
    st = jax.ShapeDtypeStruct((N, 1, 4096), F32)
    return pl.pallas_call(
        functools.partial(_s5_kernel, NB=NB, TL=TL, T=T),
        out_shape=(jax.ShapeDtypeStruct((N * L, MIX_W), F32), st, st),
        grid=(N // NB, 4, T),
        in_specs=[pl.BlockSpec((RB, 256), lambda i, j, t: (i * T + t, ub + j)),
                  pl.BlockSpec((1, 256, 2048), lambda i, j, t: (j, 0, 0)),
                  pl.BlockSpec((1, 2048, 256), lambda i, j, t: (j, 0, 0)),
                  pl.BlockSpec((1, 2048, 256), lambda i, j, t: (j, 0, 0)),
                  pl.BlockSpec((1, 256), lambda i, j, t: (0, j)),
                  pl.BlockSpec((1, 1024), lambda i, j, t: (0, j)),
                  pl.BlockSpec((1, 1024), lambda i, j, t: (0, j)),
                  pl.BlockSpec((NB, 1, 1024), lambda i, j, t: (i, 0, j)),
                  pl.BlockSpec((NB, 1, 1024), lambda i, j, t: (i, 0, j))],
        out_specs=(pl.BlockSpec((RB, 256), lambda i, j, t: (i * T + t, j)),
                   pl.BlockSpec((NB, 1, 1024), lambda i, j, t: (i, 0, j)),
                   pl.BlockSpec((NB, 1, 1024), lambda i, j, t: (i, 0, j))),
        scratch_shapes=[pltpu.VMEM((2, RB, 1024), F32),
                        pltpu.VMEM((8, 1024), F32),
                        pltpu.VMEM((8, 1024), F32)],
        compiler_params=_cp(("arbitrary", "arbitrary", "arbitrary")),
    )(proj0, wb, wk, wc, dvec, ar, ai, x0r, x0i)


def _mlstm_kernel(q_ref, k_ref, v_ref, o_ref, z_ref, gt_ref, gb_ref, hg_ref, c0_ref, n0_ref, m0_ref,
                  y_ref, c_ref, n_ref, m_ref, cs_ref, ns_ref, ms_ref, *, NB, TL, T):
    for nb in range(NB):
        rows = lambda ref: ref.at[pl.ds(nb * TL, TL)]
        one = lambda ref: ref.at[pl.ds(nb, 1)]
        _mlstm_seq(rows(q_ref), rows(k_ref), rows(v_ref), rows(o_ref), rows(z_ref), rows(gt_ref), gb_ref, hg_ref,
                   one(c0_ref), one(n0_ref), one(m0_ref), rows(y_ref), one(c_ref), one(n_ref), one(m_ref),
                   cs_ref.at[nb], ns_ref.at[nb], ms_ref.at[nb], TL=TL, T=T)


def _mlstm_seq(q_ref, k_ref, v_ref, o_ref, z_ref, gt_ref, gb_ref, hg_ref, c0_ref, n0_ref, m0_ref,
               y_ref, c_ref, n_ref, m_ref, cs_ref, ns_ref, ms_ref, *, TL, T):
    t = pl.program_id(1)

    @pl.when(t == 0)
    def _():
        cs_ref[...] = c0_ref[0]
        ns_ref[...] = n0_ref[0]
        ms_ref[...] = m0_ref[0]

    G = gt_ref[...] + gb_ref[...]
    B = _row_cumsum(-_softplus(-G), TL)
    Bs = pltpu.roll(B, 124, 1)
    A = G - Bs
    CM = _row_cummax(A, TL)
    ms = ms_ref[...]
    dg = _iota((8, 128), 0) == _iota((8, 128), 1)
    mrow = jnp.sum(jnp.where(dg, ms, 0.0), axis=0, keepdims=True)
    M = jnp.maximum(mrow, CM)
    MT = Bs + M
    sel = dg.astype(BF16)
    a_hi, a_mid, a_lo = _split3(A)
    Arow = _dot_nt(sel, a_hi) + _dot_nt(sel, a_mid) + _dot_nt(sel, a_lo)
    causal = _iota((TL, TL), 0) >= _iota((TL, TL), 1)
    H = range(M_HEADS)
    sl = [slice(M_D * h, M_D * (h + 1)) for h in H]
    q = [q_ref[:, sl[h]] * (M_D ** -0.5) for h in H]
    qb = [x.astype(BF16) for x in q]
    kf = [k_ref[:, sl[h]] for h in H]
    kb = [x.astype(BF16) for x in kf]
    vf = [v_ref[:, sl[h]] for h in H]
    c_old = [cs_ref[h] for h in H]
    n_old = [ns_ref[h:h + 1, :] for h in H]
    m_col = [M[:, h:h + 1] for h in H]
    mt_col = [MT[:, h:h + 1] for h in H]
    b_col = [Bs[:, h:h + 1] for h in H]
    m_prev = [mrow[:, h:h + 1] for h in H]
    dm = [jnp.exp(jnp.where(causal, Arow[h:h + 1, :] - m_col[h], -jnp.inf)) for h in H]
    s = [_dot_nt(qb[h], kb[h]) * dm[h] for h in H]
    inter = [jnp.exp(m_prev[h] - m_col[h]) for h in H]
    h_intra = [_dot(s[h].astype(BF16), vf[h].astype(BF16)) for h in H]
    h_inter = [_dot(qb[h], c_old[h].astype(BF16)) * inter[h] for h in H]
    n_all = [jnp.sum(s[h], axis=1, keepdims=True) + jnp.sum(q[h] * n_old[h], axis=1, keepdims=True) * inter[h]
             for h in H]
    hh = [(h_intra[h] + h_inter[h]) / jnp.maximum(jnp.abs(n_all[h]), jnp.exp(-mt_col[h])) for h in H]
    m_new = [mt_col[h][TL - 1:TL, :] for h in H]
    b_end = [b_col[h][TL - 1:TL, :] for h in H]
    dec = [jnp.exp(m_prev[h] + b_end[h] - m_new[h]) for h in H]
    w_s = [jnp.exp(b_end[h] - b_col[h] + G[:, h:h + 1] - m_new[h]) for h in H]
    c_new = [c_old[h] * dec[h] + _dot_tn(kb[h], (vf[h] * w_s[h]).astype(BF16)) for h in H]
    n_new = [n_old[h] * dec[h] + jnp.sum(kf[h] * w_s[h], axis=0, keepdims=True) for h in H]
    outs = []
    for h in H:
        mu = jnp.mean(hh[h], axis=-1, keepdims=True)
        hc = hh[h] - mu
        var = jnp.mean(hc * hc, axis=-1, keepdims=True)
        hn = hc * lax.rsqrt(var + LN_EPS) * hg_ref[:, sl[h]]
        outs.append(hn * _sigmoid(o_ref[:, sl[h]]) * _silu(z_ref[:, sl[h]]))
    for h in H:
        cs_ref[h] = c_new[h]
        ns_ref[h:h + 1, :] = n_new[h]
        ms_ref[h:h + 1, :] = jnp.broadcast_to(m_new[h], (1, 128))
        y_ref[:, sl[h]] = outs[h]

    @pl.when(t == T - 1)
    def _():
        c_ref[0] = cs_ref[...]
        n_ref[0] = ns_ref[...]
        m_ref[0] = ms_ref[...]


def _mlstm_call(proj1, gate_b, hn_g, c0, n0, m0, *, N, L, NB, TL):
    T = L // TL
    RB = NB * TL
    assert NB == 1 or T == 1

    def col(cb):
        return pl.BlockSpec((RB, MIX_W), lambda i, t: (i * T + t, cb))

    st_specs = (pl.BlockSpec((NB, M_HEADS, M_D, M_D), lambda i, t: (i, 0, 0, 0)),
                pl.BlockSpec((NB, M_HEADS, M_D), lambda i, t: (i, 0, 0)),
                pl.BlockSpec((NB, 8, 128), lambda i, t: (i, 0, 0)))
    return pl.pallas_call(
        functools.partial(_mlstm_kernel, NB=NB, TL=TL, T=T),
        out_shape=(jax.ShapeDtypeStruct((N * L, MIX_W), F32),
                   jax.ShapeDtypeStruct((N, M_HEADS, M_D, M_D), F32),
                   jax.ShapeDtypeStruct((N, M_HEADS, M_D), F32),
                   jax.ShapeDtypeStruct((N, 8, 128), F32)),
        grid=(N // NB, T),
        in_specs=[col(0), col(1), col(2), col(3), col(4),
                  pl.BlockSpec((RB, 128), lambda i, t: (i * T + t, P1_GATE // 128)),
                  pl.BlockSpec((1, 128), lambda i, t: (0, 0)),
                  pl.BlockSpec((1, MIX_W), lambda i, t: (0, 0))] + list(st_specs),
        out_specs=(pl.BlockSpec((RB, MIX_W), lambda i, t: (i * T + t, 0)),) + st_specs,
        scratch_shapes=[pltpu.VMEM((NB, M_HEADS, M_D, M_D), F32),
                        pltpu.VMEM((NB, M_HEADS, M_D), F32),
                        pltpu.VMEM((NB, 8, 128), F32)],
        compiler_params=_cp(("arbitrary", "arbitrary")),
    )(proj1, proj1, proj1, proj1, proj1, proj1, gate_b, hn_g, c0, n0, m0)


N_RA_OUT = 9


def _rwkv_a_kernel(pr_ref, pk_ref, pv_ref, pwa_ref, hr_ref, hk_ref, hv_ref, hwa_ref, st_ref, stwa_ref,
                   mu_ref, muwa_ref, w0_ref, w2_ref, a0_ref, a2_ref, kkp_ref, ka_ref, rk_ref,
                   ah_ref, rh_ref, bh_ref, kh_ref, vo_ref, ul_ref, yl_ref, dc_ref, bo_ref, shr_ref, shw_ref,
                   s_at, s_rt, s_bt, s_kt, s_v, s_cum, *, RB, CT, N, L, U):
    HS = R_HEADS * CT
    HG = 128 // CT
    NG = R_HEADS // HG
    GW = HG * R_K
    NCH = RB // CT
    short = L == CT
    rid = _iota((RB, 1), 0)
    grow = pl.program_id(0) * RB + rid

    if short:
        spread = (_shr(_iota((RB, RB // CT), 0), _log2(CT)) == _iota((RB, RB // CT), 1)).astype(BF16)

    def shifted(p_ref, h_ref, s_ref, lo, hi, mu):
        p = p_ref[...]
        prev = pltpu.roll(p, 1, 0)
        if short:
            hi3, mid3, lo3 = _split3(s_ref[:, lo:hi])
            first_prev = _dot(spread, hi3) + _dot(spread, mid3) + _dot(spread, lo3)
            prev = jnp.where(jnp.bitwise_and(rid, CT - 1) == 0, first_prev, prev)
        else:
            prev = jnp.where(rid == 0, h_ref[7:8, :], prev)
            for n in range(N):
                prev = jnp.where(grow == n * L, s_ref[n:n + 1, lo:hi], prev)
        return p + (prev - p) * mu

    raw = ((pr_ref, shr_ref, 0), (pk_ref, shr_ref, 1024), (pv_ref, shr_ref, 2048), (pwa_ref, shw_ref, 0))
    if short:
        nseq = RB // CT
        sel = (_iota((nseq, RB), 1) == _iota((nseq, RB), 0) * CT + (CT - 1)).astype(BF16)
        for src, dst, lo in raw:
            hi, mid, low = _split3(src[...])
            dst[:, lo:lo + src.shape[1]] = _dot(sel, hi) + _dot(sel, mid) + _dot(sel, low)
    else:
        for n in range(N):
            tile, off = divmod(n * L + L - 1, RB)

            @pl.when(pl.program_id(0) == tile)
            def _():
                for src, dst, lo in raw:
                    dst[n:n + 1, lo:lo + src.shape[1]] = src[off:off + 1, :]

    r = shifted(pr_ref, hr_ref, st_ref, 0, 1024, mu_ref[:, 0:1024])
    k = shifted(pk_ref, hk_ref, st_ref, 1024, 2048, mu_ref[:, 1024:2048])
    v = shifted(pv_ref, hv_ref, st_ref, 2048, 3072, mu_ref[:, 2048:3072])
    wa = shifted(pwa_ref, hwa_ref, stwa_ref, 0, 128, muwa_ref[...])
    w = -_softplus(-(w0_ref[...] + _dot(jnp.tanh(wa).astype(BF16), w2_ref[...]))) - 0.5
    wlog = -jnp.exp(w)
    a = _sigmoid(a0_ref[...] + _dot(wa.astype(BF16), a2_ref[...]))
    kk = k * kkp_ref[...]
    kk = kk / jnp.maximum(jnp.sqrt(_segsum(kk * kk, R_K)), 1e-12)
    kmod = k * (1.0 + (a - 1.0) * ka_ref[...])
    bo_ref[...] = _segsum(r * kmod * rk_ref[...], R_K) * v
    cum = _row_cumsum(wlog, CT)
    einv = jnp.exp(-cum)
    s_at[...] = (-kk) * jnp.exp(cum - wlog)
    s_rt[...] = r * jnp.exp(cum)
    s_bt[...] = kk * a * einv
    s_kt[...] = kmod * einv
    s_v[...] = v
    s_cum[...] = cum

    mdt = BF16 if CT % 16 == 0 else F32
    be_mask = (_shr(_iota((128, GW), 0), _log2(CT)) == _shr(_iota((128, GW), 1), _log2(R_K))).astype(mdt)
    bd_mask = (_shr(_iota((HS, HS), 0), _log2(CT)) == _shr(_iota((HS, HS), 1), _log2(CT))).astype(mdt)
    tt = _iota((CT, HS), 0)
    ss = jnp.bitwise_and(_iota((CT, HS), 1), CT - 1)
    strict = tt > ss
    incl = tt >= ss
    eye_c = (tt == ss).astype(F32)
    cat0 = lambda *xs: jnp.concatenate(xs, axis=0)

    def blockexp(x):
        return [(jnp.concatenate([x[:, GW * g:GW * (g + 1)].astype(mdt)] * HG, axis=0) * be_mask).astype(BF16)
                for g in range(NG)]

    def gram(lhs, be):
        lb = lhs.astype(BF16)
        return jnp.concatenate([_dot_nt(lb[:, GW * g:GW * (g + 1)], be[g]) for g in range(NG)], axis=1)

    def apply(cmp, be):
        cb = cmp.astype(BF16)
        return jnp.concatenate([_dot(cb[:, 128 * g:128 * (g + 1)], be[g]) for g in range(NG)], axis=1)

    def bdiag(x):
        return (jnp.concatenate([x.astype(mdt)] * R_HEADS, axis=0) * bd_mask).astype(BF16)

    def bdiag_hl(x):
        if mdt == BF16:
            return tuple(bdiag(part) for part in _split2(x))
        return _split2(jnp.concatenate([x] * R_HEADS, axis=0) * bd_mask)

    def mm_hl(stack, wh, wl):
        sh, sl = _split2(stack)
        n = stack.shape[0]
        full = _dot(cat0(sh, sl), wh)
        return full[:n] + full[n:] + _dot(sh, wl)

    def chunks(i, carry):
        rows = [pl.ds(pl.multiple_of((i * U + u) * CT, CT), CT) for u in range(U)]
        ld = lambda ref: [ref[rw, :] for rw in rows]
        at, rt, bt, kt, vv, cm = ld(s_at), ld(s_rt), ld(s_bt), ld(s_kt), ld(s_v), ld(s_cum)
        each = lambda f, *xs: [f(*a_) for a_ in zip(*xs)]
        ar_ = each(cat0, at, rt)
        gb = each(lambda l_, y_: gram(l_, blockexp(y_)), ar_, bt)
        gk = each(lambda l_, y_: gram(l_, blockexp(y_)), ar_, kt)
        a_ab = each(lambda m: jnp.where(strict, m[:CT], 0.0), gb)
        a_rb = each(lambda m: jnp.where(incl, m[CT:], 0.0), gb)
        a_ak = each(lambda m: jnp.where(strict, m[:CT], 0.0), gk)
        a_rk = each(lambda m: jnp.where(incl, m[CT:], 0.0), gk)
        p = each(lambda m: eye_c + m, a_ab)
        x = a_ab
        q = a_rb
        w_hl = each(bdiag_hl, x)
        res = each(lambda x_, q_, w_: mm_hl(cat0(x_, q_), *w_), x, q, w_hl)
        x = each(lambda r_: r_[:CT], res)
        q = each(lambda q_, r_: q_ + r_[CT:], q, res)
        pw = 2
        while pw < CT:
            w_hl = each(bdiag_hl, x)
            if 2 * pw >= CT:
                res = each(lambda p_, q_, w_: mm_hl(cat0(p_, q_), *w_), p, q, w_hl)
                q = each(lambda q_, r_: q_ + r_[CT:], q, res)
            else:
                res = each(lambda p_, x_, q_, w_: mm_hl(cat0(p_, x_, q_), *w_), p, x, q, w_hl)
                x = each(lambda r_: r_[CT:2 * CT], res)
                q = each(lambda q_, r_: q_ + r_[2 * CT:], q, res)
            p = each(lambda p_, r_: p_ + r_[:CT], p, res)
            pw *= 2
        tq = each(cat0, p, q)
        res = each(lambda m, k_: _dot(m.astype(BF16), bdiag(k_)), tq, a_ak)
        ty = each(lambda r_, k_: cat0(r_[:CT], r_[CT:] + k_), res, a_rk)
        o1 = each(lambda m, y_: apply(m, blockexp(y_)), tq, at)
        o2 = each(lambda m, y_: apply(m, blockexp(y_)), ty, vv)
        ect = each(lambda c_: jnp.exp(c_[CT - 1:CT, :]), cm)
        for u, rw in enumerate(rows):
            ah_ref[rw, :] = o1[u][:CT]
            rh_ref[rw, :] = rt[u] + o1[u][CT:]
            ul_ref[rw, :] = o2[u][:CT]
            yl_ref[rw, :] = o2[u][CT:]
            bh_ref[rw, :] = bt[u] * ect[u]
            kh_ref[rw, :] = kt[u] * ect[u]
            vo_ref[rw, :] = vv[u]
            dc_ref[rw, :] = jnp.broadcast_to(ect[u], (CT, 1024))
        return carry

    lax.fori_loop(0, NCH // U, chunks, 0)


def _rwkv_a_call(proj1, st_rkv, st_wa, wts, *, N, L, RB, CT, U):
    short = L == CT
    rows = N * L
    assert rows % RB == 0 and (RB // CT) % U == 0

    def col(cb, width=MIX_W):
        return pl.BlockSpec((RB, width), lambda i: (i, cb))

    def halo(cb, width=MIX_W):
        return pl.BlockSpec((8, width), lambda i: (jnp.maximum(i * (RB // 8) - 1, 0), cb))

    if short:
        st_specs = [pl.BlockSpec((RB // CT, 3072), lambda i: (i, 0)), pl.BlockSpec((RB // CT, 128), lambda i: (i, 0))]
    else:
        st_specs = [pl.BlockSpec((N, 3072), lambda i: (0, 0)), pl.BlockSpec((N, 128), lambda i: (0, 0))]

    def full(shape):
        return pl.BlockSpec(shape, lambda i: (0,) * len(shape))

    o_spec = pl.BlockSpec((RB, MIX_W), lambda i: (i, 0))
    if short:
        sh_specs = (pl.BlockSpec((RB // CT, 3072), lambda i: (i, 0)), pl.BlockSpec((RB // CT, 128), lambda i: (i, 0)))
    else:
        sh_specs = (pl.BlockSpec((N, 3072), lambda i: (0, 0)), pl.BlockSpec((N, 128), lambda i: (0, 0)))
    outs = pl.pallas_call(
        functools.partial(_rwkv_a_kernel, RB=RB, CT=CT, N=N, L=L, U=U),
        out_shape=((jax.ShapeDtypeStruct((rows, MIX_W), F32),) * N_RA_OUT
                   + (jax.ShapeDtypeStruct((N, 3072), F32), jax.ShapeDtypeStruct((N, 128), F32))),
        grid=(rows // RB,),
        in_specs=([col(P1_R // 1024), col(P1_RK // 1024), col(P1_RV // 1024), col(P1_WA // 128, 128),
                   halo(P1_R // 1024), halo(P1_RK // 1024), halo(P1_RV // 1024), halo(P1_WA // 128, 128)]
                  + st_specs
                  + [full((1, 3072)), full((1, 128)), full((1, 1024)), full((128, 1024)), full((1, 1024)),
                     full((128, 1024)), full((1, 1024)), full((1, 1024)), full((1, 1024))]),
        out_specs=(o_spec,) * N_RA_OUT + sh_specs,
        scratch_shapes=[pltpu.VMEM((RB, MIX_W), F32)] * 6,
        compiler_params=_cp(("arbitrary",)),
    )(*([proj1] * 8 + [st_rkv, st_wa] + list(wts)))
    return outs[:N_RA_OUT], jnp.concatenate(outs[N_RA_OUT:], axis=1)


def _rwkv_b_kernel(ah_ref, rh_ref, bh_ref, kh_ref, v_ref, ul_ref, yl_ref, dc_ref, bo_ref, z_ref, lg_ref, lb_ref,
                   s0_ref, y_ref, so_ref, sbd_ref, yb_ref, *, NBLK, TLB, CT, T):
    t = pl.program_id(1)
    bd_mask = (_shr(_iota((256, 256), 0), 6) == _shr(_iota((256, 256), 1), 6)).astype(F32)

    @pl.when(t == 0)
    def _():
        for nb in range(NBLK):
            for j in range(4):
                x = s0_ref[nb, 256 * j:256 * (j + 1), :]
                x = jnp.concatenate([x, x], axis=1)
                sbd_ref[4 * nb + j] = jnp.concatenate([x, x], axis=1) * bd_mask

    chains = [(nb, j, slice(256 * j, 256 * (j + 1))) for nb in range(NBLK) for j in range(4)]
    for c in range(TLB // CT):
        rows = slice(c * CT, (c + 1) * CT)
        sbs = [sbd_ref[4 * nb + j] for nb, j, cs in chains]
        outs = [_dot_nt(jnp.concatenate([ah_ref[nb, rows, cs], rh_ref[nb, rows, cs]], axis=0).astype(BF16),
                        sb.astype(BF16))
                for (nb, j, cs), sb in zip(chains, sbs)]
        upds = [_dot_tn(jnp.concatenate([o[:CT] + ul_ref[nb, rows, cs], v_ref[nb, rows, cs]],
                                        axis=0).astype(BF16),
                        jnp.concatenate([bh_ref[nb, rows, cs], kh_ref[nb, rows, cs]], axis=0).astype(BF16))
                for (nb, j, cs), o in zip(chains, outs)]
        for (nb, j, cs), sb, o, upd in zip(chains, sbs, outs, upds):
            yb_ref[nb, rows, cs] = o[CT:] + yl_ref[nb, rows, cs]
            sbd_ref[4 * nb + j] = sb * dc_ref[nb, c * CT:c * CT + 1, cs] + upd * bd_mask

    @pl.when(t == T - 1)
    def _():
        for nb in range(NBLK):
            outs = []
            for j in range(4):
                sb = sbd_ref[4 * nb + j]
                half = sb[:, :128] + sb[:, 128:]
                outs.append(half[:, :64] + half[:, 64:])
            so_ref[nb] = jnp.concatenate(outs, axis=0)

    cat = lambda ref: jnp.concatenate([ref[nb] for nb in range(NBLK)], axis=0)
    y = cat(yb_ref)
    mu = _segsum(y, R_K) * (1.0 / R_K)
    yc = y - mu
    var = _segsum(yc * yc, R_K) * (1.0 / R_K)
    yn = yc * lax.rsqrt(var + R_LN_EPS) * lg_ref[...] + lb_ref[...] + cat(bo_ref)
    out = yn * _silu(cat(z_ref))
    for nb in range(NBLK):
        y_ref[nb] = out[nb * TLB:(nb + 1) * TLB, :]


def _rwkv_b_call(ra, proj1, ln_g, ln_b, s0, *, N, L, NBLK, TLB, CT):
    T = L // TLB
    blk = lambda cb: pl.BlockSpec((NBLK, TLB, MIX_W), lambda i, t: (i, t, cb))
    s_spec = pl.BlockSpec((NBLK, 1024, 64), lambda i, t: (i, 0, 0))
    ra3 = [a.reshape(N, L, MIX_W) for a in ra]
    y, s_new = pl.pallas_call(
        functools.partial(_rwkv_b_kernel, NBLK=NBLK, TLB=TLB, CT=CT, T=T),
        out_shape=(jax.ShapeDtypeStruct((N, L, MIX_W), F32),
                   jax.ShapeDtypeStruct((N, 1024, 64), F32)),
        grid=(N // NBLK, T),
        in_specs=([blk(0)] * N_RA_OUT
                  + [blk(P1_ZD // 1024),
                     pl.BlockSpec((1, MIX_W), lambda i, t: (0, 0)),
                     pl.BlockSpec((1, MIX_W), lambda i, t: (0, 0)),
                     s_spec]),
        out_specs=(blk(0), s_spec),
        scratch_shapes=[pltpu.VMEM((4 * NBLK, 256, 256), F32),
                        pltpu.VMEM((NBLK, TLB, MIX_W), F32)],
        compiler_params=_cp(("arbitrary", "arbitrary")),
    )(*(ra3 + [proj1.reshape(N, L, P1_N), ln_g, ln_b, s0]))
    return y.reshape(N * L, MIX_W), s_new


def _regroup_w1(w1):
    wt = w1.T
    return jnp.concatenate([wt[0:4096], wt[4104:8200], wt[8328:9352], wt[8200:8328], wt[4096:4104],
                            jnp.zeros((P1_N - 9352, D_MODEL), F32)], axis=0).astype(BF16)


def _s5_weights(lam_re, lam_im, log_dt, b_re, b_im, c_re, c_im):
    dt = jnp.exp(log_dt)[:, None]
    mag = jnp.exp(lam_re * dt)
    ar = mag * jnp.cos(lam_im * dt)
    ai = mag * jnp.sin(lam_im * dt)
    den = lam_re * lam_re + lam_im * lam_im
    qr = ((ar - 1.0) * lam_re + ai * lam_im) / den
    qi = (ai * lam_re - (ar - 1.0) * lam_im) / den
    bbr = qr[..., None] * b_re - qi[..., None] * b_im
    bbi = qr[..., None] * b_im + qi[..., None] * b_re
    hp = lax.Precision.HIGHEST
    spread_p = (jnp.arange(1024)[None, :] % S5_STATE == jnp.arange(S5_STATE)[:, None]).astype(F32)
    spread_h = (jnp.arange(256)[None, :] % S5_GROUP == jnp.arange(S5_GROUP)[:, None]).astype(F32)
    grp_in = (jnp.arange(256)[:, None] // S5_GROUP == jnp.arange(1024)[None, :] // S5_STATE)
    grp_out = (jnp.arange(1024)[:, None] // S5_STATE == jnp.arange(256)[None, :] // S5_GROUP)

    def in_blocks(bb):
        rows = bb.transpose(0, 2, 1).reshape(4, 256, S5_STATE)
        return jnp.matmul(rows, spread_p, precision=hp) * grp_in

    def out_blocks(cc):
        rows = cc.transpose(0, 2, 1).reshape(4, 1024, S5_GROUP)
        return jnp.matmul(rows, spread_h, precision=hp) * grp_out

    wb = jnp.concatenate([in_blocks(bbr), in_blocks(bbi)], axis=2).astype(BF16)
    wc = jnp.concatenate([out_blocks(c_re), out_blocks(-c_im)], axis=1).astype(BF16)
    pr, pi = jnp.ones_like(ar), jnp.zeros_like(ai)
    lag = []
    for _ in range(8):
        cpr = c_re * pr[:, None, :] - c_im * pi[:, None, :]
        cpi = c_re * pi[:, None, :] + c_im * pr[:, None, :]
        lag.append(jnp.einsum('gop,gph->goh', cpr, bbr, precision=lax.Precision.HIGHEST)
                   - jnp.einsum('gop,gph->goh', cpi, bbi, precision=lax.Precision.HIGHEST))
        pr, pi = pr * ar - pi * ai, pr * ai + pi * ar
    kd = jnp.stack(lag).reshape(8, 4, 16, S5_GROUP, S5_GROUP)
    kd = kd.transpose(1, 0, 2, 4, 3).reshape(4, 8, 256, S5_GROUP)
    spread = (jnp.arange(256)[None, :] % S5_GROUP == jnp.arange(S5_GROUP)[:, None]).astype(F32)
    same_group = (jnp.arange(256)[:, None] // S5_GROUP == jnp.arange(256)[None, :] // S5_GROUP)
    wk = jnp.matmul(kd, spread, precision=lax.Precision.HIGHEST) * same_group
    wk = wk.reshape(4, 8 * 256, 256).astype(BF16)
    return ar.reshape(1, 4096), ai.reshape(1, 4096), wb, wk, wc


CFG = {
    "P": dict(N=BATCH, L=P_LEN, tm_mm=1376, tm_mm1=2752, tm=688, tm_ln=512, drop_meta=True,
              conv=dict(NB=1, TL=344), s5=dict(NB=1, TL=688), mlstm=dict(NB=1, TL=344),
              ra=dict(RB=192, CT=16, U=12), rb=dict(NBLK=4, TLB=48, CT=16)),
    "S": dict(N=DEC_BATCH, L=DEC_SEQ, tm_mm=1024, tm_mm1=1024, tm=512, tm_ln=256, drop_meta=False,
              conv=dict(NB=16, TL=8), s5=dict(NB=32, TL=8), mlstm=dict(NB=4, TL=8),
              ra=dict(RB=256, CT=8, U=8), rb=dict(NBLK=8, TLB=8, CT=8)),
}


def _trunk(x, st, w, cfg):
    n, l = cfg["N"], cfg["L"]
    proj0 = _matmul(x, w["w_in0"], cfg["tm_mm"], 1024)
    act, conv_new = _conv_call(proj0, st["conv"], w["conv_w"], w["conv_b"], w["a_ln_g"], w["a_ln_b"],
                               N=n, L=l, **cfg["conv"])
    mix_a = _pw_gate(act, w["pw"], proj0, tm=cfg["tm_mm"])
    yb, xr, xi = _s5_call(proj0, w["s5_wb"], w["s5_wk"], w["s5_wc"], w["s5_d"], w["s5_ar"], w["s5_ai"],
                          st["ssm_re"].reshape(n, 1, 4096), st["ssm_im"].reshape(n, 1, 4096),
                          N=n, L=l, **cfg["s5"])
    mix_b = _glu_gate(yb, w["glu_w"], w["glu_b"], proj0, tm=cfg["tm_mm"])
    x1, x1_bf16 = _out_ln(x, mix_a, mix_b, w["w_out0"], w["ln_g0"], w["ln_b0"], tm=cfg["tm_ln"], with_bf16=True)

    proj1 = _matmul_nt(x1_bf16, w["w_in1"], cfg["tm_mm1"], 512)
    m0 = jnp.pad(jnp.broadcast_to(st["m"][:, :, None], (n, M_HEADS, 128)), ((0, 0), (0, 4), (0, 0)))
    mix_c, c_new, n_new, m_new = _mlstm_call(proj1, w["gate_b"], w["hn_g"], st["c"], st["n"], m0,
                                             N=n, L=l, **cfg["mlstm"])
    sh = st["shift"]
    ra, shift_new = _rwkv_a_call(proj1, sh[:, :3072], sh[:, 3072:], w["rwkv"], N=n, L=l, **cfg["ra"])
    mix_d, s_new = _rwkv_b_call(ra, proj1, w["r_ln_g"], w["r_ln_b"], st["s"].reshape(n, 1024, 64),
                                N=n, L=l, **cfg["rb"])
    final_ln = _out_ln_prompt if cfg["drop_meta"] else functools.partial(_out_ln, tm=cfg["tm_ln"])
    y = final_ln(x1, mix_c, mix_d, w["w_out1"], w["ln_g1"], w["ln_b1"])

    states = (conv_new[None],
              xr.reshape(n, S5_GROUPS, S5_STATE)[None],
              xi.reshape(n, S5_GROUPS, S5_STATE)[None],
              c_new[None], n_new[None], m_new[:, :M_HEADS, 0][None],
              s_new.reshape(n, R_HEADS, R_K, R_K)[None],
              shift_new[None])
    return y, states


def kernel(x_prompt, x_sample, state_conv, state_ssm_re, state_ssm_im, state_mlstm_c, state_mlstm_n, state_mlstm_m, state_rwkv_s, state_rwkv_shift, meta_tokens, ev_w_in, a_conv_w, a_conv_b, a_ln_g, a_ln_b, a_pw, s5_lambda_re, s5_lambda_im, s5_log_dt, s5_b_re, s5_b_im, s5_c_re, s5_c_im, s5_d, s5_glu_w, s5_glu_b, ev_w_out, ev_ln_g, ev_ln_b, od_w_in, m_ig_b, m_fg_b, m_hn_g, r_mu, r_w0, r_w2, r_a0, r_a2, r_kk, r_ka, r_rk, r_ln_g, r_ln_b, od_w_out, od_ln_g, od_ln_b):
    nb = x_prompt.shape[0]
    row = lambda vec: vec.reshape(1, -1)
    zeros = lambda *s: jnp.zeros(s, F32)

    ar, ai, wb, wk, wc = _s5_weights(s5_lambda_re[0], s5_lambda_im[0], s5_log_dt[0], s5_b_re[0], s5_b_im[0],
                                     s5_c_re[0], s5_c_im[0])
    w_in1 = _regroup_w1(od_w_in[0])
    mu = r_mu[0]
    w = dict(
        w_in0=ev_w_in[0].astype(BF16), conv_w=a_conv_w[0], conv_b=row(a_conv_b[0]),
        a_ln_g=row(a_ln_g[0]), a_ln_b=row(a_ln_b[0]), pw=a_pw[0].astype(BF16),
        s5_wb=wb, s5_wk=wk, s5_wc=wc, s5_d=row(s5_d[0]), s5_ar=ar, s5_ai=ai,
        glu_w=s5_glu_w[0].astype(BF16), glu_b=row(s5_glu_b[0]),
        w_out0=ev_w_out[0].astype(BF16), ln_g0=row(ev_ln_g[0]), ln_b0=row(ev_ln_b[0]),
        w_in1=w_in1,
        gate_b=jnp.concatenate([m_ig_b[0], m_fg_b[0], jnp.zeros((120,), F32)]).reshape(1, 128),
        hn_g=row(m_hn_g[0]),
        rwkv=[row(mu[:3072]), row(mu[3072:]), row(r_w0[0]),
              jnp.concatenate([r_w2[0], jnp.zeros((64, MIX_W), F32)], axis=0).astype(BF16),
              row(r_a0[0]),
              jnp.concatenate([jnp.zeros((64, MIX_W), F32), r_a2[0]], axis=0).astype(BF16),
              row(r_kk[0]), row(r_ka[0]), row(r_rk[0])],
        r_ln_g=row(r_ln_g[0]), r_ln_b=row(r_ln_b[0]),
        w_out1=od_w_out[0].astype(BF16), ln_g1=row(od_ln_g[0]), ln_b1=row(od_ln_b[0]),
    )

    x_p = jnp.concatenate([jnp.broadcast_to(meta_tokens[None], (nb, N_META, D_MODEL)), x_prompt],
                          axis=1).reshape(nb * P_LEN, D_MODEL)
    st_p = dict(conv=zeros(nb, CONV_W - 1, MIX_W), ssm_re=zeros(nb, 4096), ssm_im=zeros(nb, 4096),
                c=zeros(nb, M_HEADS, M_D, M_D), n=zeros(nb, M_HEADS, M_D), m=zeros(nb, M_HEADS),
                s=zeros(nb, R_HEADS, R_K, R_K), shift=zeros(nb, 3200))
    y_p, states_p = _trunk(x_p, st_p, w, CFG["P"])

    st_s = dict(conv=state_conv[0], ssm_re=state_ssm_re[0], ssm_im=state_ssm_im[0],
                c=state_mlstm_c[0], n=state_mlstm_n[0], m=state_mlstm_m[0],
                s=state_rwkv_s[0], shift=state_rwkv_shift[0])
    y_s, states_s = _trunk(x_sample.reshape(DEC_BATCH * DEC_SEQ, D_MODEL), st_s, w, CFG["S"])

    y_prompt = y_p
    y_sample = y_s.reshape(DEC_BATCH, DEC_SEQ, D_MODEL)
    return (y_prompt, y_sample) + states_p + states_s
```

```python
import functools
import math

import jax
import jax.numpy as jnp
from jax import lax
from jax.experimental import pallas as pl
from jax.experimental.pallas import tpu as pltpu

F32 = jnp.float32
BF16 = jnp.bfloat16

D_MODEL = 2048
MIX_W = 1024
N_META = 16
CONV_W = 31
S5_GROUP = 16
S5_GROUPS = 64
S5_STATE = 64
M_HEADS = 4
M_D = 256
R_HEADS = 16
R_K = 64
LN_EPS = 1e-5
R_LN_EPS = 64e-5
DEPTH = 2
ALPHA = (2 * DEPTH) ** 0.25

BATCH = 4
SEQ = 2048
P_LEN = N_META + SEQ
DEC_BATCH = 128
DEC_SEQ = 8

P1_Q, P1_K, P1_V, P1_O, P1_ZC = 0, 1024, 2048, 3072, 4096
P1_R, P1_RK, P1_RV, P1_ZD, P1_WA, P1_GATE = 5120, 6144, 7168, 8192, 9216, 9344
P1_N = 9728

VMEM_LIMIT = 48 * 1024 * 1024


def _cp(sem):
    return pltpu.CompilerParams(dimension_semantics=sem, vmem_limit_bytes=VMEM_LIMIT)


def _dot(a, b):
    return jnp.dot(a, b, preferred_element_type=F32)


def _dot_nt(a, b):
    return lax.dot_general(a, b, (((1,), (1,)), ((), ())), preferred_element_type=F32)


def _dot_tn(a, b):
    return lax.dot_general(a, b, (((0,), (0,)), ((), ())), preferred_element_type=F32)


def _split2(x):
    hi = x.astype(BF16)
    lo = (x - hi.astype(F32)).astype(BF16)
    return hi, lo


def _split3(x):
    hi = x.astype(BF16)
    r1 = x - hi.astype(F32)
    mid = r1.astype(BF16)
    lo = (r1 - mid.astype(F32)).astype(BF16)
    return hi, mid, lo


def _sigmoid(x):
    return jax.nn.sigmoid(x)


def _silu(x):
    return x * jax.nn.sigmoid(x)


def _softplus(x):
    return jnp.maximum(x, 0.0) + jnp.log(1.0 + jnp.exp(-jnp.abs(x)))


def _gelu_tanh(x):
    c = math.sqrt(2.0 / math.pi)
    return x * (0.5 * (1.0 + jnp.tanh(c * (x + 0.044715 * (x * x * x)))))


def _iota(shape, axis):
    return lax.broadcasted_iota(jnp.int32, shape, axis)


def _shr(x, k):
    return lax.shift_right_logical(x, jnp.int32(k))


def _log2(n):
    k = int(round(math.log2(n)))
    assert 1 << k == n
    return k


def _block_ones(n, seg, dtype):
    r = _shr(_iota((n, n), 0), _log2(seg))
    c = _shr(_iota((n, n), 1), _log2(seg))
    return (r == c).astype(dtype)


def _segsum(x, seg):
    g = _block_ones(256, seg, BF16)
    outs = []
    for j in range(x.shape[1] // 256):
        hi, lo = _split2(x[:, 256 * j:256 * (j + 1)])
        outs.append(_dot(hi, g) + _dot(lo, g))
    return jnp.concatenate(outs, axis=1)


def _row_cumsum(x, period):
    rows = x.shape[0]
    rid = _iota(x.shape, 0)
    if period < rows:
        rid = jnp.bitwise_and(rid, period - 1)
    d = 1
    while d < min(period, rows):
        x = x + jnp.where(rid >= d, pltpu.roll(x, d, 0), 0.0)
        d *= 2
    return x


def _row_cummax(x, period):
    rows = x.shape[0]
    rid = _iota(x.shape, 0)
    if period < rows:
        rid = jnp.bitwise_and(rid, period - 1)
    d = 1
    while d < min(period, rows):
        x = jnp.maximum(x, jnp.where(rid >= d, pltpu.roll(x, d, 0), -jnp.inf))
        d *= 2
    return x


def _mm_kernel(x_ref, w_ref, o_ref):
    o_ref[...] = _dot(x_ref[...].astype(BF16), w_ref[...])


def _matmul(x, w, tm, tn):
    r, k = x.shape
    n = w.shape[1]
    return pl.pallas_call(
        _mm_kernel,
        out_shape=jax.ShapeDtypeStruct((r, n), F32),
        grid=(pl.cdiv(r, tm), n // tn),
        in_specs=[pl.BlockSpec((tm, k), lambda i, j: (i, 0)),
                  pl.BlockSpec((k, tn), lambda i, j: (0, j))],
        out_specs=pl.BlockSpec((tm, tn), lambda i, j: (i, j)),
        compiler_params=_cp(("parallel", "arbitrary")),
    )(x, w)


def _mm_nt_kernel(x_ref, w_ref, o_ref):
    o_ref[...] = _dot_nt(x_ref[...].astype(BF16), w_ref[...])


def _matmul_nt(x, w_t, tm, tn):
    r, k = x.shape
    n = w_t.shape[0]
    return pl.pallas_call(
        _mm_nt_kernel,
        out_shape=jax.ShapeDtypeStruct((r, n), F32),
        grid=(pl.cdiv(r, tm), n // tn),
        in_specs=[pl.BlockSpec((tm, k), lambda i, j: (i, 0)),
                  pl.BlockSpec((tn, k), lambda i, j: (j, 0))],
        out_specs=pl.BlockSpec((tm, tn), lambda i, j: (i, j)),
        compiler_params=_cp(("parallel", "arbitrary")),
    )(x, w_t)


def _pw_kernel(a_ref, w_ref, z_ref, o_ref):
    o_ref[...] = _dot(a_ref[...].astype(BF16), w_ref[...]) * _silu(z_ref[...])


def _pw_gate(act, pw, proj0, tm, tn=512):
    r = act.shape[0]
    zb = 2048 // tn
    return pl.pallas_call(
        _pw_kernel,
        out_shape=jax.ShapeDtypeStruct((r, MIX_W), F32),
        grid=(pl.cdiv(r, tm), MIX_W // tn),
        in_specs=[pl.BlockSpec((tm, MIX_W), lambda i, j: (i, 0)),
                  pl.BlockSpec((MIX_W, tn), lambda i, j: (0, j)),
                  pl.BlockSpec((tm, tn), lambda i, j: (i, zb + j))],
        out_specs=pl.BlockSpec((tm, tn), lambda i, j: (i, j)),
        compiler_params=_cp(("parallel", "arbitrary")),
    )(act, pw, proj0)


def _glu_kernel(y_ref, wv_ref, wg_ref, bv_ref, bg_ref, z_ref, o_ref):
    y = y_ref[...].astype(BF16)
    v = _dot(y, wv_ref[...]) + bv_ref[...]
    g = _dot(y, wg_ref[...]) + bg_ref[...]
    o_ref[...] = v * _sigmoid(g) * _silu(z_ref[...])


def _glu_gate(yb, glu_w, glu_b, proj0, tm, tn=512):
    r = yb.shape[0]
    nb = MIX_W // tn
    zb = 4096 // tn
    return pl.pallas_call(
        _glu_kernel,
        out_shape=jax.ShapeDtypeStruct((r, MIX_W), F32),
        grid=(pl.cdiv(r, tm), nb),
        in_specs=[pl.BlockSpec((tm, MIX_W), lambda i, j: (i, 0)),
                  pl.BlockSpec((MIX_W, tn), lambda i, j: (0, j)),
                  pl.BlockSpec((MIX_W, tn), lambda i, j: (0, nb + j)),
                  pl.BlockSpec((1, tn), lambda i, j: (0, j)),
                  pl.BlockSpec((1, tn), lambda i, j: (0, nb + j)),
                  pl.BlockSpec((tm, tn), lambda i, j: (i, zb + j))],
        out_specs=pl.BlockSpec((tm, tn), lambda i, j: (i, j)),
        compiler_params=_cp(("parallel", "arbitrary")),
    )(yb, glu_w, glu_w, glu_b, glu_b, proj0)


def _out_ln_kernel(x_ref, ma_ref, mb_ref, wa_ref, wb_ref, g_ref, b_ref, o_ref):
    out = _dot(ma_ref[...].astype(BF16), wa_ref[...]) + _dot(mb_ref[...].astype(BF16), wb_ref[...])
    y = ALPHA * x_ref[...] + out
    mu = jnp.mean(y, axis=-1, keepdims=True)
    yc = y - mu
    var = jnp.mean(yc * yc, axis=-1, keepdims=True)
    o_ref[...] = yc * lax.rsqrt(var + LN_EPS) * g_ref[...] + b_ref[...]


def _out_ln_bf16_kernel(x_ref, ma_ref, mb_ref, wa_ref, wb_ref, g_ref, b_ref, o_ref, ob_ref):
    _out_ln_kernel(x_ref, ma_ref, mb_ref, wa_ref, wb_ref, g_ref, b_ref, o_ref)
    ob_ref[...] = o_ref[...].astype(BF16)


def _out_ln(x, mix_a, mix_b, w_out, ln_g, ln_b, tm, with_bf16=False):
    r = x.shape[0]
    o_spec = pl.BlockSpec((tm, D_MODEL), lambda i: (i, 0))
    return pl.pallas_call(
        _out_ln_bf16_kernel if with_bf16 else _out_ln_kernel,
        out_shape=((jax.ShapeDtypeStruct((r, D_MODEL), F32), jax.ShapeDtypeStruct((r, D_MODEL), BF16))
                   if with_bf16 else jax.ShapeDtypeStruct((r, D_MODEL), F32)),
        grid=(pl.cdiv(r, tm),),
        in_specs=[pl.BlockSpec((tm, D_MODEL), lambda i: (i, 0)),
                  pl.BlockSpec((tm, MIX_W), lambda i: (i, 0)),
                  pl.BlockSpec((tm, MIX_W), lambda i: (i, 0)),
                  pl.BlockSpec((MIX_W, D_MODEL), lambda i: (0, 0), pipeline_mode=pl.Buffered(1)),
                  pl.BlockSpec((MIX_W, D_MODEL), lambda i: (1, 0), pipeline_mode=pl.Buffered(1)),
                  pl.BlockSpec((1, D_MODEL), lambda i: (0, 0)),
                  pl.BlockSpec((1, D_MODEL), lambda i: (0, 0))],
        out_specs=(o_spec, o_spec) if with_bf16 else o_spec,
        compiler_params=_cp(("parallel",)),
    )(x, mix_a, mix_b, w_out, w_out, ln_g, ln_b)


def _out_ln_prompt(x, mix_a, mix_b, w_out, ln_g, ln_b, tm=512):
    tiles = SEQ // tm

    def rows(width):
        return pl.BlockSpec((pl.Element(tm), pl.Element(width)),
                            lambda n, t: (pl.multiple_of(n * P_LEN + N_META + t * tm, 8), 0))

    return pl.pallas_call(
        _out_ln_kernel,
        out_shape=jax.ShapeDtypeStruct((BATCH * SEQ, D_MODEL), F32),
        grid=(BATCH, tiles),
        in_specs=[rows(D_MODEL), rows(MIX_W), rows(MIX_W),
                  pl.BlockSpec((MIX_W, D_MODEL), lambda n, t: (0, 0), pipeline_mode=pl.Buffered(1)),
                  pl.BlockSpec((MIX_W, D_MODEL), lambda n, t: (1, 0), pipeline_mode=pl.Buffered(1)),
                  pl.BlockSpec((1, D_MODEL), lambda n, t: (0, 0)),
                  pl.BlockSpec((1, D_MODEL), lambda n, t: (0, 0))],
        out_specs=pl.BlockSpec((tm, D_MODEL), lambda n, t: (n * tiles + t, 0)),
        compiler_params=_cp(("parallel", "arbitrary")),
    )(x, mix_a, mix_b, w_out, w_out, ln_g, ln_b).reshape(BATCH, SEQ, D_MODEL)


def _conv_kernel(u_ref, g_ref, st_ref, w_ref, cb_ref, lg_ref, lb_ref, act_ref, nst_ref, hp_ref, hs_ref, wb_ref,
                 *, NB, TL, T):
    t = pl.program_id(1)
    for j in range(CONV_W):
        wb_ref[j] = jnp.broadcast_to(w_ref[j:j + 1, :], (8, MIX_W))
    for nb in range(NB):
        base = nb * TL

        @pl.when(t == 0)
        def _():
            hp_ref[nb, 0:2, :] = jnp.zeros((2, MIX_W), F32)
            hp_ref[nb, 2:32, :] = st_ref[nb]

        hp_ref[nb, TL + 32:TL + 40, :] = jnp.zeros((8, MIX_W), F32)
        hp_ref[nb, 32:32 + TL, :] = u_ref[base:base + TL, :] * _sigmoid(g_ref[base:base + TL, :])
        for b in range(8):
            hs_ref[b] = hp_ref[nb, b:b + TL + 32, :]

        def taps(r0, groups):
            acc = [None] * groups
            for j in range(CONV_W):
                o = j + 2
                wj = wb_ref[j]
                for g in range(groups):
                    term = wj * hs_ref[o % 8, pl.ds(r0 + 8 * (o // 8 + g), 8), :]
                    acc[g] = term if acc[g] is None else acc[g] + term
            for g in range(groups):
                act_ref[pl.ds(base + r0 + 8 * g, 8), :] = acc[g]

        def chunk(c, carry):
            taps(pl.multiple_of(c * 16, 16), 2)
            return carry

        lax.fori_loop(0, TL // 16, chunk, 0)
        if TL % 16:
            taps(TL - 8, 1)

        @pl.when(t == T - 1)
        def _():
            nst_ref[nb] = hp_ref[nb, TL + 2:TL + 32, :]

        if T > 1:
            hp_ref[nb, 0:32, :] = hp_ref[nb, TL:TL + 32, :]

    y = act_ref[...] + cb_ref[...]
    mu = jnp.mean(y, axis=-1, keepdims=True)
    yc = y - mu
    var = jnp.mean(yc * yc, axis=-1, keepdims=True)
    act_ref[...] = _silu(yc * lax.rsqrt(var + LN_EPS) * lg_ref[...] + lb_ref[...])


def _conv_call(proj0, state, conv_w, conv_b, ln_g, ln_b, *, N, L, NB, TL):
    T = L // TL
    RB = NB * TL
    assert NB == 1 or T == 1
    return pl.pallas_call(
        functools.partial(_conv_kernel, NB=NB, TL=TL, T=T),
        out_shape=(jax.ShapeDtypeStruct((N * L, MIX_W), F32),
                   jax.ShapeDtypeStruct((N, CONV_W - 1, MIX_W), F32)),
        grid=(N // NB, T),
        in_specs=[pl.BlockSpec((RB, MIX_W), lambda i, t: (i * T + t, 0)),
                  pl.BlockSpec((RB, MIX_W), lambda i, t: (i * T + t, 1)),
                  pl.BlockSpec((NB, CONV_W - 1, MIX_W), lambda i, t: (i, 0, 0)),
                  pl.BlockSpec((CONV_W, MIX_W), lambda i, t: (0, 0)),
                  pl.BlockSpec((1, MIX_W), lambda i, t: (0, 0)),
                  pl.BlockSpec((1, MIX_W), lambda i, t: (0, 0)),
                  pl.BlockSpec((1, MIX_W), lambda i, t: (0, 0))],
        out_specs=(pl.BlockSpec((RB, MIX_W), lambda i, t: (i * T + t, 0)),
                   pl.BlockSpec((NB, CONV_W - 1, MIX_W), lambda i, t: (i, 0, 0))),
        scratch_shapes=[pltpu.VMEM((NB, TL + 40, MIX_W), F32),
                        pltpu.VMEM((8, TL + 32, MIX_W), F32),
                        pltpu.VMEM((CONV_W, 8, MIX_W), F32)],
        compiler_params=_cp(("arbitrary", "arbitrary")),
    )(proj0, proj0, state, conv_w, conv_b, ln_g, ln_b)


def _s5_kernel(u_ref, wb_ref, wk_ref, wc_ref, d_ref, ar_ref, ai_ref, x0r_ref, x0i_ref,
               y_ref, xfr_ref, xfi_ref, xs_ref, cr_ref, ci_ref, *, NB, TL, T):
    t = pl.program_id(2)
    RB = NB * TL
    GL = TL // 8
    u = u_ref[...]
    ub = u.astype(BF16)
    big = _dot(ub, wb_ref[0])
    xs_ref[0] = big[:, :1024]
    xs_ref[1] = big[:, 1024:]
    rid = jnp.bitwise_and(_iota((RB, 256), 0), 7)
    lags = [ub] + [jnp.where(rid >= d, pltpu.roll(u, d, 0), 0.0).astype(BF16) for d in range(1, 8)]
    y_loc = _dot(jnp.concatenate(lags, axis=1), wk_ref[0])
    ar = ar_ref[...]
    ai = ai_ref[...]

    def cmul(pr, pi, qr, qi):
        return pr * qr - pi * qi, pr * qi + pi * qr

    a1 = (ar, ai)
    a2 = cmul(*a1, *a1)
    a4 = cmul(*a2, *a2)
    a3 = cmul(*a2, *a1)
    a5 = cmul(*a4, *a1)
    a6 = cmul(*a4, *a2)
    a7 = cmul(*a6, *a1)
    a8 = cmul(*a4, *a4)
    a0 = (jnp.ones_like(ar), jnp.zeros_like(ai))
    r8 = _iota((8, 1024), 0)

    def table(powers):
        tr = jnp.zeros((8, 1024), F32)
        ti = jnp.zeros((8, 1024), F32)
        for k, (pr, pi) in enumerate(powers):
            tr = jnp.where(r8 == k, pr, tr)
            ti = jnp.where(r8 == k, pi, ti)
        return tr, ti

    pwr, pwi = table((a1, a2, a3, a4, a5, a6, a7, a8))
    qwr, qwi = table((a7, a6, a5, a4, a3, a2, a1, a0))
    a8r, a8i = a8

    first = t == 0

    def seq_body(nb, carry0):
        x0r = x0r_ref[nb]
        x0i = x0i_ref[nb]
        if T > 1:
            c_r = jnp.where(first, x0r, cr_ref[0:1, :])
            c_i = jnp.where(first, x0i, ci_ref[0:1, :])
        else:
            c_r, c_i = x0r, x0i

        def grp(g, c):
            c_r, c_i = c
            off = pl.multiple_of(nb * TL + g * 8, 8)
            vr = xs_ref[0, pl.ds(off, 8), :]
            vi = xs_ref[1, pl.ds(off, 8), :]
            er = jnp.sum(qwr * vr - qwi * vi, axis=0, keepdims=True)
            ei = jnp.sum(qwr * vi + qwi * vr, axis=0, keepdims=True)
            br = jnp.broadcast_to(c_r, (8, 1024))
            bi = jnp.broadcast_to(c_i, (8, 1024))
            xs_ref[0, pl.ds(off, 8), :] = pwr * br - pwi * bi
            xs_ref[1, pl.ds(off, 8), :] = pwr * bi + pwi * br
            return a8r * c_r - a8i * c_i + er, a8r * c_i + a8i * c_r + ei

        c_r, c_i = lax.fori_loop(0, GL, grp, (c_r, c_i))
        if T > 1:
            cr_ref[...] = jnp.broadcast_to(c_r, (8, 1024))
            ci_ref[...] = jnp.broadcast_to(c_i, (8, 1024))

        @pl.when(t == T - 1)
        def _():
            xfr_ref[nb] = c_r
            xfi_ref[nb] = c_i

        return carry0

    lax.fori_loop(0, NB, seq_body, 0)
    y = (_dot(xs_ref[0].astype(BF16), wc_ref[0, 0:1024, :])
         + _dot(xs_ref[1].astype(BF16), wc_ref[0, 1024:2048, :]))
    y_ref[...] = _gelu_tanh(y + y_loc + d_ref[...] * u)


def _s5_call(proj0, wb, wk, wc, dvec, ar, ai, x0r, x0i, *, N, L, NB, TL):
    T = L // TL
    RB = NB * TL
    assert NB == 1 or T == 1
    ub = 3072 // 256
    st = jax.ShapeDtypeStruct((N, 1, 4096), F32)
    return pl.pallas_call(
        functools.partial(_s5_kernel, NB=NB, TL=TL, T=T),
        out_shape=(jax.ShapeDtypeStruct((N * L, MIX_W), F32), st, st),
        grid=(N // NB, 4, T),
        in_specs=[pl.BlockSpec((RB, 256), lambda i, j, t: (i * T + t, ub + j)),
                  pl.BlockSpec((1, 256, 2048), lambda i, j, t: (j, 0, 0)),
                  pl.BlockSpec((1, 2048, 256), lambda i, j, t: (j, 0, 0)),
                  pl.BlockSpec((1, 2048, 256), lambda i, j, t: (j, 0, 0)),
                  pl.BlockSpec((1, 256), lambda i, j, t: (0, j)),
                  pl.BlockSpec((1, 1024), lambda i, j, t: (0, j)),
                  pl.BlockSpec((1, 1024), lambda i, j, t: (0, j)),
                  pl.BlockSpec((NB, 1, 1024), lambda i, j, t: (i, 0, j)),
                  pl.BlockSpec((NB, 1, 1024), lambda i, j, t: (i, 0, j))],
        out_specs=(pl.BlockSpec((RB, 256), lambda i, j, t: (i * T + t, j)),
                   pl.BlockSpec((NB, 1, 1024), lambda i, j, t: (i, 0, j)),
                   pl.BlockSpec((NB, 1, 1024), lambda i, j, t: (i, 0, j))),
        scratch_shapes=[pltpu.VMEM((2, RB, 1024), F32),
                        pltpu.VMEM((8, 1024), F32),
                        pltpu.VMEM((8, 1024), F32)],
        compiler_params=_cp(("arbitrary", "arbitrary", "arbitrary")),
    )(proj0, wb, wk, wc, dvec, ar, ai, x0r, x0i)


def _mlstm_kernel(q_ref, k_ref, v_ref, o_ref, z_ref, gt_ref, gb_ref, hg_ref, c0_ref, n0_ref, m0_ref,
                  y_ref, c_ref, n_ref, m_ref, cs_ref, ns_ref, ms_ref, *, NB, TL, T):
    for nb in range(NB):
        rows = lambda ref: ref.at[pl.ds(nb * TL, TL)]
        one = lambda ref: ref.at[pl.ds(nb, 1)]
        _mlstm_seq(rows(q_ref), rows(k_ref), rows(v_ref), rows(o_ref), rows(z_ref), rows(gt_ref), gb_ref, hg_ref,
                   one(c0_ref), one(n0_ref), one(m0_ref), rows(y_ref), one(c_ref), one(n_ref), one(m_ref),
                   cs_ref.at[nb], ns_ref.at[nb], ms_ref.at[nb], TL=TL, T=T)


def _mlstm_seq(q_ref, k_ref, v_ref, o_ref, z_ref, gt_ref, gb_ref, hg_ref, c0_ref, n0_ref, m0_ref,
               y_ref, c_ref, n_ref, m_ref, cs_ref, ns_ref, ms_ref, *, TL, T):
    t = pl.program_id(1)

    @pl.when(t == 0)
    def _():
        cs_ref[...] = c0_ref[0]
        ns_ref[...] = n0_ref[0]
        ms_ref[...] = m0_ref[0]

    G = gt_ref[...] + gb_ref[...]
    B = _row_cumsum(-_softplus(-G), TL)
    Bs = pltpu.roll(B, 124, 1)
    A = G - Bs
    CM = _row_cummax(A, TL)
    ms = ms_ref[...]
    dg = _iota((8, 128), 0) == _iota((8, 128), 1)
    mrow = jnp.sum(jnp.where(dg, ms, 0.0), axis=0, keepdims=True)
    M = jnp.maximum(mrow, CM)
    MT = Bs + M
    sel = dg.astype(BF16)
    a_hi, a_mid, a_lo = _split3(A)
    Arow = _dot_nt(sel, a_hi) + _dot_nt(sel, a_mid) + _dot_nt(sel, a_lo)
    causal = _iota((TL, TL), 0) >= _iota((TL, TL), 1)
    H = range(M_HEADS)
    sl = [slice(M_D * h, M_D * (h + 1)) for h in H]
    q = [q_ref[:, sl[h]] * (M_D ** -0.5) for h in H]
    qb = [x.astype(BF16) for x in q]
    kf = [k_ref[:, sl[h]] for h in H]
    kb = [x.astype(BF16) for x in kf]
    vf = [v_ref[:, sl[h]] for h in H]
    c_old = [cs_ref[h] for h in H]
    n_old = [ns_ref[h:h + 1, :] for h in H]
    m_col = [M[:, h:h + 1] for h in H]
    mt_col = [MT[:, h:h + 1] for h in H]
    b_col = [Bs[:, h:h + 1] for h in H]
    m_prev = [mrow[:, h:h + 1] for h in H]
    dm = [jnp.exp(jnp.where(causal, Arow[h:h + 1, :] - m_col[h], -jnp.inf)) for h in H]
    s = [_dot_nt(qb[h], kb[h]) * dm[h] for h in H]
    inter = [jnp.exp(m_prev[h] - m_col[h]) for h in H]
    h_intra = [_dot(s[h].astype(BF16), vf[h].astype(BF16)) for h in H]
    h_inter = [_dot(qb[h], c_old[h].astype(BF16)) * inter[h] for h in H]
    n_all = [jnp.sum(s[h], axis=1, keepdims=True) + jnp.sum(q[h] * n_old[h], axis=1, keepdims=True) * inter[h]
             for h in H]
    hh = [(h_intra[h] + h_inter[h]) / jnp.maximum(jnp.abs(n_all[h]), jnp.exp(-mt_col[h])) for h in H]
    m_new = [mt_col[h][TL - 1:TL, :] for h in H]
    b_end = [b_col[h][TL - 1:TL, :] for h in H]
    dec = [jnp.exp(m_prev[h] + b_end[h] - m_new[h]) for h in H]
    w_s = [jnp.exp(b_end[h] - b_col[h] + G[:, h:h + 1] - m_new[h]) for h in H]
    c_new = [c_old[h] * dec[h] + _dot_tn(kb[h], (vf[h] * w_s[h]).astype(BF16)) for h in H]
    n_new = [n_old[h] * dec[h] + jnp.sum(kf[h] * w_s[h], axis=0, keepdims=True) for h in H]
    outs = []
    for h in H:
        mu = jnp.mean(hh[h], axis=-1, keepdims=True)
        hc = hh[h] - mu
        var = jnp.mean(hc * hc, axis=-1, keepdims=True)
        hn = hc * lax.rsqrt(var + LN_EPS) * hg_ref[:, sl[h]]
        outs.append(hn * _sigmoid(o_ref[:, sl[h]]) * _silu(z_ref[:, sl[h]]))
    for h in H:
        cs_ref[h] = c_new[h]
        ns_ref[h:h + 1, :] = n_new[h]
        ms_ref[h:h + 1, :] = jnp.broadcast_to(m_new[h], (1, 128))
        y_ref[:, sl[h]] = outs[h]

    @pl.when(t == T - 1)
    def _():
        c_ref[0] = cs_ref[...]
        n_ref[0] = ns_ref[...]
        m_ref[0] = ms_ref[...]


def _mlstm_call(proj1, gate_b, hn_g, c0, n0, m0, *, N, L, NB, TL):
    T = L // TL
    RB = NB * TL
    assert NB == 1 or T == 1

    def col(cb):
        return pl.BlockSpec((RB, MIX_W), lambda i, t: (i * T + t, cb))

    st_specs = (pl.BlockSpec((NB, M_HEADS, M_D, M_D), lambda i, t: (i, 0, 0, 0)),
                pl.BlockSpec((NB, M_HEADS, M_D), lambda i, t: (i, 0, 0)),
                pl.BlockSpec((NB, 8, 128), lambda i, t: (i, 0, 0)))
    return pl.pallas_call(
        functools.partial(_mlstm_kernel, NB=NB, TL=TL, T=T),
        out_shape=(jax.ShapeDtypeStruct((N * L, MIX_W), F32),
                   jax.ShapeDtypeStruct((N, M_HEADS, M_D, M_D), F32),
                   jax.ShapeDtypeStruct((N, M_HEADS, M_D), F32),
                   jax.ShapeDtypeStruct((N, 8, 128), F32)),
        grid=(N // NB, T),
        in_specs=[col(0), col(1), col(2), col(3), col(4),
                  pl.BlockSpec((RB, 128), lambda i, t: (i * T + t, P1_GATE // 128)),
                  pl.BlockSpec((1, 128), lambda i, t: (0, 0)),
                  pl.BlockSpec((1, MIX_W), lambda i, t: (0, 0))] + list(st_specs),
        out_specs=(pl.BlockSpec((RB, MIX_W), lambda i, t: (i * T + t, 0)),) + st_specs,
        scratch_shapes=[pltpu.VMEM((NB, M_HEADS, M_D, M_D), F32),
                        pltpu.VMEM((NB, M_HEADS, M_D), F32),
                        pltpu.VMEM((NB, 8, 128), F32)],
        compiler_params=_cp(("arbitrary", "arbitrary")),
    )(proj1, proj1, proj1, proj1, proj1, proj1, gate_b, hn_g, c0, n0, m0)


N_RA_OUT = 9


def _rwkv_a_kernel(pr_ref, pk_ref, pv_ref, pwa_ref, hr_ref, hk_ref, hv_ref, hwa_ref, st_ref, stwa_ref,
                   mu_ref, muwa_ref, w0_ref, w2_ref, a0_ref, a2_ref, kkp_ref, ka_ref, rk_ref,
                   ah_ref, rh_ref, bh_ref, kh_ref, vo_ref, ul_ref, yl_ref, dc_ref, bo_ref, shr_ref, shw_ref,
                   s_at, s_rt, s_bt, s_kt, s_v, s_cum, *, RB, CT, N, L, U):
    HS = R_HEADS * CT
    HG = 128 // CT
    NG = R_HEADS // HG
    GW = HG * R_K
    NCH = RB // CT
    short = L == CT
    rid = _iota((RB, 1), 0)
    grow = pl.program_id(0) * RB + rid

    if short:
        spread = (_shr(_iota((RB, RB // CT), 0), _log2(CT)) == _iota((RB, RB // CT), 1)).astype(BF16)

    def shifted(p_ref, h_ref, s_ref, lo, hi, mu):
        p = p_ref[...]
        prev = pltpu.roll(p, 1, 0)
        if short:
            hi3, mid3, lo3 = _split3(s_ref[:, lo:hi])
            first_prev = _dot(spread, hi3) + _dot(spread, mid3) + _dot(spread, lo3)
            prev = jnp.where(jnp.bitwise_and(rid, CT - 1) == 0, first_prev, prev)
        else:
            prev = jnp.where(rid == 0, h_ref[7:8, :], prev)
            for n in range(N):
                prev = jnp.where(grow == n * L, s_ref[n:n + 1, lo:hi], prev)
        return p + (prev - p) * mu

    raw = ((pr_ref, shr_ref, 0), (pk_ref, shr_ref, 1024), (pv_ref, shr_ref, 2048), (pwa_ref, shw_ref, 0))
    if short:
        nseq = RB // CT
        sel = (_iota((nseq, RB), 1) == _iota((nseq, RB), 0) * CT + (CT - 1)).astype(BF16)
        for src, dst, lo in raw:
            hi, mid, low = _split3(src[...])
            dst[:, lo:lo + src.shape[1]] = _dot(sel, hi) + _dot(sel, mid) + _dot(sel, low)
    else:
        for n in range(N):
            tile, off = divmod(n * L + L - 1, RB)

            @pl.when(pl.program_id(0) == tile)
            def _():
                for src, dst, lo in raw:
                    dst[n:n + 1, lo:lo + src.shape[1]] = src[off:off + 1, :]

    r = shifted(pr_ref, hr_ref, st_ref, 0, 1024, mu_ref[:, 0:1024])
    k = shifted(pk_ref, hk_ref, st_ref, 1024, 2048, mu_ref[:, 1024:2048])
    v = shifted(pv_ref, hv_ref, st_ref, 2048, 3072, mu_ref[:, 2048:3072])
    wa = shifted(pwa_ref, hwa_ref, stwa_ref, 0, 128, muwa_ref[...])
    w = -_softplus(-(w0_ref[...] + _dot(jnp.tanh(wa).astype(BF16), w2_ref[...]))) - 0.5
    wlog = -jnp.exp(w)
    a = _sigmoid(a0_ref[...] + _dot(wa.astype(BF16), a2_ref[...]))
    kk = k * kkp_ref[...]
    kk = kk / jnp.maximum(jnp.sqrt(_segsum(kk * kk, R_K)), 1e-12)
    kmod = k * (1.0 + (a - 1.0) * ka_ref[...])
    bo_ref[...] = _segsum(r * kmod * rk_ref[...], R_K) * v
    cum = _row_cumsum(wlog, CT)
    einv = jnp.exp(-cum)
    s_at[...] = (-kk) * jnp.exp(cum - wlog)
    s_rt[...] = r * jnp.exp(cum)
    s_bt[...] = kk * a * einv
    s_kt[...] = kmod * einv
    s_v[...] = v
    s_cum[...] = cum

    mdt = BF16 if CT % 16 == 0 else F32
    be_mask = (_shr(_iota((128, GW), 0), _log2(CT)) == _shr(_iota((128, GW), 1), _log2(R_K))).astype(mdt)
    bd_mask = (_shr(_iota((HS, HS), 0), _log2(CT)) == _shr(_iota((HS, HS), 1), _log2(CT))).astype(mdt)
    tt = _iota((CT, HS), 0)
    ss = jnp.bitwise_and(_iota((CT, HS), 1), CT - 1)
    strict = tt > ss
    incl = tt >= ss
    eye_c = (tt == ss).astype(F32)
    cat0 = lambda *xs: jnp.concatenate(xs, axis=0)

    def blockexp(x):
        return [(jnp.concatenate([x[:, GW * g:GW * (g + 1)].astype(mdt)] * HG, axis=0) * be_mask).astype(BF16)
                for g in range(NG)]

    def gram(lhs, be):
        lb = lhs.astype(BF16)
        return jnp.concatenate([_dot_nt(lb[:, GW * g:GW * (g + 1)], be[g]) for g in range(NG)], axis=1)

    def apply(cmp, be):
        cb = cmp.astype(BF16)
        return jnp.concatenate([_dot(cb[:, 128 * g:128 * (g + 1)], be[g]) for g in range(NG)], axis=1)

    def bdiag(x):
        return (jnp.concatenate([x.astype(mdt)] * R_HEADS, axis=0) * bd_mask).astype(BF16)

    def bdiag_hl(x):
        if mdt == BF16:
            return tuple(bdiag(part) for part in _split2(x))
        return _split2(jnp.concatenate([x] * R_HEADS, axis=0) * bd_mask)

    def mm_hl(stack, wh, wl):
        sh, sl = _split2(stack)
        n = stack.shape[0]
        full = _dot(cat0(sh, sl), wh)
        return full[:n] + full[n:] + _dot(sh, wl)

    def chunks(i, carry):
        rows = [pl.ds(pl.multiple_of((i * U + u) * CT, CT), CT) for u in range(U)]
        ld = lambda ref: [ref[rw, :] for rw in rows]
        at, rt, bt, kt, vv, cm = ld(s_at), ld(s_rt), ld(s_bt), ld(s_kt), ld(s_v), ld(s_cum)
        each = lambda f, *xs: [f(*a_) for a_ in zip(*xs)]
        ar_ = each(cat0, at, rt)
        gb = each(lambda l_, y_: gram(l_, blockexp(y_)), ar_, bt)
        gk = each(lambda l_, y_: gram(l_, blockexp(y_)), ar_, kt)
        a_ab = each(lambda m: jnp.where(strict, m[:CT], 0.0), gb)
        a_rb = each(lambda m: jnp.where(incl, m[CT:], 0.0), gb)
        a_ak = each(lambda m: jnp.where(strict, m[:CT], 0.0), gk)
        a_rk = each(lambda m: jnp.where(incl, m[CT:], 0.0), gk)
        p = each(lambda m: eye_c + m, a_ab)
        x = a_ab
        q = a_rb
        w_hl = each(bdiag_hl, x)
        res = each(lambda x_, q_, w_: mm_hl(cat0(x_, q_), *w_), x, q, w_hl)
        x = each(lambda r_: r_[:CT], res)
        q = each(lambda q_, r_: q_ + r_[CT:], q, res)
        pw = 2
        while pw < CT:
            w_hl = each(bdiag_hl, x)
            if 2 * pw >= CT:
                res = each(lambda p_, q_, w_: mm_hl(cat0(p_, q_), *w_), p, q, w_hl)
                q = each(lambda q_, r_: q_ + r_[CT:], q, res)
            else:
                res = each(lambda p_, x_, q_, w_: mm_hl(cat0(p_, x_, q_), *w_), p, x, q, w_hl)
                x = each(lambda r_: r_[CT:2 * CT], res)
                q = each(lambda q_, r_: q_ + r_[2 * CT:], q, res)
            p = each(lambda p_, r_: p_ + r_[:CT], p, res)
            pw *= 2
        tq = each(cat0, p, q)
        res = each(lambda m, k_: _dot(m.astype(BF16), bdiag(k_)), tq, a_ak)
        ty = each(lambda r_, k_: cat0(r_[:CT], r_[CT:] + k_), res, a_rk)
        o1 = each(lambda m, y_: apply(m, blockexp(y_)), tq, at)
        o2 = each(lambda m, y_: apply(m, blockexp(y_)), ty, vv)
        ect = each(lambda c_: jnp.exp(c_[CT - 1:CT, :]), cm)
        for u, rw in enumerate(rows):
            ah_ref[rw, :] = o1[u][:CT]
            rh_ref[rw, :] = rt[u] + o1[u][CT:]
            ul_ref[rw, :] = o2[u][:CT]
            yl_ref[rw, :] = o2[u][CT:]
            bh_ref[rw, :] = bt[u] * ect[u]
            kh_ref[rw, :] = kt[u] * ect[u]
            vo_ref[rw, :] = vv[u]
            dc_ref[rw, :] = jnp.broadcast_to(ect[u], (CT, 1024))
        return carry

    lax.fori_loop(0, NCH // U, chunks, 0)


def _rwkv_a_call(proj1, st_rkv, st_wa, wts, *, N, L, RB, CT, U):
    short = L == CT
    rows = N * L
    assert rows % RB == 0 and (RB // CT) % U == 0

    def col(cb, width=MIX_W):
        return pl.BlockSpec((RB, width), lambda i: (i, cb))

    def halo(cb, width=MIX_W):
        return pl.BlockSpec((8, width), lambda i: (jnp.maximum(i * (RB // 8) - 1, 0), cb))

    if short:
        st_specs = [pl.BlockSpec((RB // CT, 3072), lambda i: (i, 0)), pl.BlockSpec((RB // CT, 128), lambda i: (i, 0))]
    else:
        st_specs = [pl.BlockSpec((N, 3072), lambda i: (0, 0)), pl.BlockSpec((N, 128), lambda i: (0, 0))]

    def full(shape):
        return pl.BlockSpec(shape, lambda i: (0,) * len(shape))

    o_spec = pl.BlockSpec((RB, MIX_W), lambda i: (i, 0))
    if short:
        sh_specs = (pl.BlockSpec((RB // CT, 3072), lambda i: (i, 0)), pl.BlockSpec((RB // CT, 128), lambda i: (i, 0)))
    else:
        sh_specs = (pl.BlockSpec((N, 3072), lambda i: (0, 0)), pl.BlockSpec((N, 128), lambda i: (0, 0)))
    outs = pl.pallas_call(
        functools.partial(_rwkv_a_kernel, RB=RB, CT=CT, N=N, L=L, U=U),
        out_shape=((jax.ShapeDtypeStruct((rows, MIX_W), F32),) * N_RA_OUT
                   + (jax.ShapeDtypeStruct((N, 3072), F32), jax.ShapeDtypeStruct((N, 128), F32))),
        grid=(rows // RB,),
        in_specs=([col(P1_R // 1024), col(P1_RK // 1024), col(P1_RV // 1024), col(P1_WA // 128, 128),
                   halo(P1_R // 1024), halo(P1_RK // 1024), halo(P1_RV // 1024), halo(P1_WA // 128, 128)]
                  + st_specs
                  + [full((1, 3072)), full((1, 128)), full((1, 1024)), full((128, 1024)), full((1, 1024)),
                     full((128, 1024)), full((1, 1024)), full((1, 1024)), full((1, 1024))]),
        out_specs=(o_spec,) * N_RA_OUT + sh_specs,
        scratch_shapes=[pltpu.VMEM((RB, MIX_W), F32)] * 6,
        compiler_params=_cp(("arbitrary",)),
    )(*([proj1] * 8 + [st_rkv, st_wa] + list(wts)))
    return outs[:N_RA_OUT], jnp.concatenate(outs[N_RA_OUT:], axis=1)


def _rwkv_b_kernel(ah_ref, rh_ref, bh_ref, kh_ref, v_ref, ul_ref, yl_ref, dc_ref, bo_ref, z_ref, lg_ref, lb_ref,
                   s0_ref, y_ref, so_ref, sbd_ref, yb_ref, *, NBLK, TLB, CT, T):
    t = pl.program_id(1)
    bd_mask = (_shr(_iota((256, 256), 0), 6) == _shr(_iota((256, 256), 1), 6)).astype(F32)

    @pl.when(t == 0)
    def _():
        for nb in range(NBLK):
            for j in range(4):
                x = s0_ref[nb, 256 * j:256 * (j + 1), :]
                x = jnp.concatenate([x, x], axis=1)
                sbd_ref[4 * nb + j] = jnp.concatenate([x, x], axis=1) * bd_mask

    chains = [(nb, j, slice(256 * j, 256 * (j + 1))) for nb in range(NBLK) for j in range(4)]
    for c in range(TLB // CT):
        rows = slice(c * CT, (c + 1) * CT)
        sbs = [sbd_ref[4 * nb + j] for nb, j, cs in chains]
        outs = [_dot_nt(jnp.concatenate([ah_ref[nb, rows, cs], rh_ref[nb, rows, cs]], axis=0).astype(BF16),
                        sb.astype(BF16))
                for (nb, j, cs), sb in zip(chains, sbs)]
        upds = [_dot_tn(jnp.concatenate([o[:CT] + ul_ref[nb, rows, cs], v_ref[nb, rows, cs]],
                                        axis=0).astype(BF16),
                        jnp.concatenate([bh_ref[nb, rows, cs], kh_ref[nb, rows, cs]], axis=0).astype(BF16))
                for (nb, j, cs), o in zip(chains, outs)]
        for (nb, j, cs), sb, o, upd in zip(chains, sbs, outs, upds):
            yb_ref[nb, rows, cs] = o[CT:] + yl_ref[nb, rows, cs]
            sbd_ref[4 * nb + j] = sb * dc_ref[nb, c * CT:c * CT + 1, cs] + upd * bd_mask

    @pl.when(t == T - 1)
    def _():
        for nb in range(NBLK):
            outs = []
            for j in range(4):
                sb = sbd_ref[4 * nb + j]
                half = sb[:, :128] + sb[:, 128:]
                outs.append(half[:, :64] + half[:, 64:])
            so_ref[nb] = jnp.concatenate(outs, axis=0)

    cat = lambda ref: jnp.concatenate([ref[nb] for nb in range(NBLK)], axis=0)
    y = cat(yb_ref)
    mu = _segsum(y, R_K) * (1.0 / R_K)
    yc = y - mu
    var = _segsum(yc * yc, R_K) * (1.0 / R_K)
    yn = yc * lax.rsqrt(var + R_LN_EPS) * lg_ref[...] + lb_ref[...] + cat(bo_ref)
    out = yn * _silu(cat(z_ref))
    for nb in range(NBLK):
        y_ref[nb] = out[nb * TLB:(nb + 1) * TLB, :]


def _rwkv_b_call(ra, proj1, ln_g, ln_b, s0, *, N, L, NBLK, TLB, CT):
    T = L // TLB
    blk = lambda cb: pl.BlockSpec((NBLK, TLB, MIX_W), lambda i, t: (i, t, cb))
    s_spec = pl.BlockSpec((NBLK, 1024, 64), lambda i, t: (i, 0, 0))
    ra3 = [a.reshape(N, L, MIX_W) for a in ra]
    y, s_new = pl.pallas_call(
        functools.partial(_rwkv_b_kernel, NBLK=NBLK, TLB=TLB, CT=CT, T=T),
        out_shape=(jax.ShapeDtypeStruct((N, L, MIX_W), F32),
                   jax.ShapeDtypeStruct((N, 1024, 64), F32)),
        grid=(N // NBLK, T),
        in_specs=([blk(0)] * N_RA_OUT
                  + [blk(P1_ZD // 1024),
                     pl.BlockSpec((1, MIX_W), lambda i, t: (0, 0)),
                     pl.BlockSpec((1, MIX_W), lambda i, t: (0, 0)),
                     s_spec]),
        out_specs=(blk(0), s_spec),
        scratch_shapes=[pltpu.VMEM((4 * NBLK, 256, 256), F32),
                        pltpu.VMEM((NBLK, TLB, MIX_W), F32)],
        compiler_params=_cp(("arbitrary", "arbitrary")),
    )(*(ra3 + [proj1.reshape(N, L, P1_N), ln_g, ln_b, s0]))
    return y.reshape(N * L, MIX_W), s_new


W1_TR = 512


def _regroup_w1_kernel(w_ref, g_ref, o_ref):
    j = pl.program_id(0)
    last = P1_N // W1_TR - 1

    @pl.when(j < last)
    def _():
        o_ref[...] = w_ref[...].astype(BF16)

    @pl.when(j == last)
    def _():
        tail = jnp.concatenate([w_ref[0:128, :], g_ref[...], jnp.zeros((W1_TR - 136, D_MODEL), F32)], axis=0)
        o_ref[...] = tail.astype(BF16)


def _regroup_w1(w1):
    wt = w1.T

    def src_row(j):
        return jnp.where(j < 8, W1_TR * j,
                         jnp.where(j < 16, 4104 + W1_TR * (j - 8),
                                   jnp.where(j < 18, 8328 + W1_TR * (j - 16), 8200)))

    return pl.pallas_call(
        _regroup_w1_kernel,
        out_shape=jax.ShapeDtypeStruct((P1_N, D_MODEL), BF16),
        grid=(P1_N // W1_TR,),
        in_specs=[pl.BlockSpec((pl.Element(W1_TR), pl.Element(D_MODEL)),
                               lambda j: (pl.multiple_of(src_row(j), 8), 0)),
                  pl.BlockSpec((pl.Element(8), pl.Element(D_MODEL)), lambda j: (4096, 0))],
        out_specs=pl.BlockSpec((W1_TR, D_MODEL), lambda j: (j, 0)),
        compiler_params=_cp(("parallel",)),
    )(wt, wt)


def _s5_weights(lam_re, lam_im, log_dt, b_re, b_im, c_re, c_im):
    dt = jnp.exp(log_dt)[:, None]
    mag = jnp.exp(lam_re * dt)
    ar = mag * jnp.cos(lam_im * dt)
    ai = mag * jnp.sin(lam_im * dt)
    den = lam_re * lam_re + lam_im * lam_im
    qr = ((ar - 1.0) * lam_re + ai * lam_im) / den
    qi = (ai * lam_re - (ar - 1.0) * lam_im) / den
    bbr = qr[..., None] * b_re - qi[..., None] * b_im
    bbi = qr[..., None] * b_im + qi[..., None] * b_re
    hp = lax.Precision.HIGHEST
    spread_p = (jnp.arange(1024)[None, :] % S5_STATE == jnp.arange(S5_STATE)[:, None]).astype(F32)
    spread_h = (jnp.arange(256)[None, :] % S5_GROUP == jnp.arange(S5_GROUP)[:, None]).astype(F32)
    grp_in = (jnp.arange(256)[:, None] // S5_GROUP == jnp.arange(1024)[None, :] // S5_STATE)
    grp_out = (jnp.arange(1024)[:, None] // S5_STATE == jnp.arange(256)[None, :] // S5_GROUP)

    def in_blocks(bb):
        rows = bb.transpose(0, 2, 1).reshape(4, 256, S5_STATE)
        return jnp.matmul(rows, spread_p, precision=hp) * grp_in

    def out_blocks(cc):
        rows = cc.transpose(0, 2, 1).reshape(4, 1024, S5_GROUP)
        return jnp.matmul(rows, spread_h, precision=hp) * grp_out

    wb = jnp.concatenate([in_blocks(bbr), in_blocks(bbi)], axis=2).astype(BF16)
    wc = jnp.concatenate([out_blocks(c_re), out_blocks(-c_im)], axis=1).astype(BF16)
    pr, pi = jnp.ones_like(ar), jnp.zeros_like(ai)
    lag = []
    for _ in range(8):
        cpr = c_re * pr[:, None, :] - c_im * pi[:, None, :]
        cpi = c_re * pi[:, None, :] + c_im * pr[:, None, :]
        lag.append(jnp.einsum('gop,gph->goh', cpr, bbr, precision=lax.Precision.HIGHEST)
                   - jnp.einsum('gop,gph->goh', cpi, bbi, precision=lax.Precision.HIGHEST))
        pr, pi = pr * ar - pi * ai, pr * ai + pi * ar
    kd = jnp.stack(lag).reshape(8, 4, 16, S5_GROUP, S5_GROUP)
    kd = kd.transpose(1, 0, 2, 4, 3).reshape(4, 8, 256, S5_GROUP)
    spread = (jnp.arange(256)[None, :] % S5_GROUP == jnp.arange(S5_GROUP)[:, None]).astype(F32)
    same_group = (jnp.arange(256)[:, None] // S5_GROUP == jnp.arange(256)[None, :] // S5_GROUP)
    wk = jnp.matmul(kd, spread, precision=lax.Precision.HIGHEST) * same_group
    wk = wk.reshape(4, 8 * 256, 256).astype(BF16)
    return ar.reshape(1, 4096), ai.reshape(1, 4096), wb, wk, wc


CFG = {
    "P": dict(N=BATCH, L=P_LEN, tm_mm=1376, tm_mm1=2752, tm=688, tm_ln=512, drop_meta=True,
              conv=dict(NB=1, TL=344), s5=dict(NB=1, TL=688), mlstm=dict(NB=1, TL=344),
              ra=dict(RB=192, CT=16, U=12), rb=dict(NBLK=4, TLB=48, CT=16)),
    "S": dict(N=DEC_BATCH, L=DEC_SEQ, tm_mm=1024, tm_mm1=1024, tm=512, tm_ln=256, drop_meta=False,
              conv=dict(NB=16, TL=8), s5=dict(NB=32, TL=8), mlstm=dict(NB=4, TL=8),
              ra=dict(RB=256, CT=8, U=8), rb=dict(NBLK=8, TLB=8, CT=8)),
}


def _trunk(x, st, w, cfg):
    n, l = cfg["N"], cfg["L"]
    proj0 = _matmul(x, w["w_in0"], cfg["tm_mm"], 1024)
    act, conv_new = _conv_call(proj0, st["conv"], w["conv_w"], w["conv_b"], w["a_ln_g"], w["a_ln_b"],
                               N=n, L=l, **cfg["conv"])
    mix_a = _pw_gate(act, w["pw"], proj0, tm=cfg["tm_mm"])
    yb, xr, xi = _s5_call(proj0, w["s5_wb"], w["s5_wk"], w["s5_wc"], w["s5_d"], w["s5_ar"], w["s5_ai"],
                          st["ssm_re"].reshape(n, 1, 4096), st["ssm_im"].reshape(n, 1, 4096),
                          N=n, L=l, **cfg["s5"])
    mix_b = _glu_gate(yb, w["glu_w"], w["glu_b"], proj0, tm=cfg["tm_mm"])
    x1, x1_bf16 = _out_ln(x, mix_a, mix_b, w["w_out0"], w["ln_g0"], w["ln_b0"], tm=cfg["tm_ln"], with_bf16=True)

    proj1 = _matmul_nt(x1_bf16, w["w_in1"], cfg["tm_mm1"], 512)
    m0 = jnp.pad(jnp.broadcast_to(st["m"][:, :, None], (n, M_HEADS, 128)), ((0, 0), (0, 4), (0, 0)))
    mix_c, c_new, n_new, m_new = _mlstm_call(proj1, w["gate_b"], w["hn_g"], st["c"], st["n"], m0,
                                             N=n, L=l, **cfg["mlstm"])
    sh = st["shift"]
    ra, shift_new = _rwkv_a_call(proj1, sh[:, :3072], sh[:, 3072:], w["rwkv"], N=n, L=l, **cfg["ra"])
    mix_d, s_new = _rwkv_b_call(ra, proj1, w["r_ln_g"], w["r_ln_b"], st["s"].reshape(n, 1024, 64),
                                N=n, L=l, **cfg["rb"])
    final_ln = _out_ln_prompt if cfg["drop_meta"] else functools.partial(_out_ln, tm=cfg["tm_ln"])
    y = final_ln(x1, mix_c, mix_d, w["w_out1"], w["ln_g1"], w["ln_b1"])

    states = (conv_new[None],
              xr.reshape(n, S5_GROUPS, S5_STATE)[None],
              xi.reshape(n, S5_GROUPS, S5_STATE)[None],
              c_new[None], n_new[None], m_new[:, :M_HEADS, 0][None],
              s_new.reshape(n, R_HEADS, R_K, R_K)[None],
              shift_new[None])
    return y, states


def kernel(x_prompt, x_sample, state_conv, state_ssm_re, state_ssm_im, state_mlstm_c, state_mlstm_n, state_mlstm_m, state_rwkv_s, state_rwkv_shift, meta_tokens, ev_w_in, a_conv_w, a_conv_b, a_ln_g, a_ln_b, a_pw, s5_lambda_re, s5_lambda_im, s5_log_dt, s5_b_re, s5_b_im, s5_c_re, s5_c_im, s5_d, s5_glu_w, s5_glu_b, ev_w_out, ev_ln_g, ev_ln_b, od_w_in, m_ig_b, m_fg_b, m_hn_g, r_mu, r_w0, r_w2, r_a0, r_a2, r_kk, r_ka, r_rk, r_ln_g, r_ln_b, od_w_out, od_ln_g, od_ln_b):
    nb = x_prompt.shape[0]
    row = lambda vec: vec.reshape(1, -1)
    zeros = lambda *s: jnp.zeros(s, F32)

    ar, ai, wb, wk, wc = _s5_weights(s5_lambda_re[0], s5_lambda_im[0], s5_log_dt[0], s5_b_re[0], s5_b_im[0],
                                     s5_c_re[0], s5_c_im[0])
    w_in1 = _regroup_w1(od_w_in[0])
    mu = r_mu[0]
    w = dict(
        w_in0=ev_w_in[0].astype(BF16), conv_w=a_conv_w[0], conv_b=row(a_conv_b[0]),
        a_ln_g=row(a_ln_g[0]), a_ln_b=row(a_ln_b[0]), pw=a_pw[0].astype(BF16),
        s5_wb=wb, s5_wk=wk, s5_wc=wc, s5_d=row(s5_d[0]), s5_ar=ar, s5_ai=ai,
        glu_w=s5_glu_w[0].astype(BF16), glu_b=row(s5_glu_b[0]),
        w_out0=ev_w_out[0].astype(BF16), ln_g0=row(ev_ln_g[0]), ln_b0=row(ev_ln_b[0]),
        w_in1=w_in1,
        gate_b=jnp.concatenate([m_ig_b[0], m_fg_b[0], jnp.zeros((120,), F32)]).reshape(1, 128),
        hn_g=row(m_hn_g[0]),
        rwkv=[row(mu[:3072]), row(mu[3072:]), row(r_w0[0]),
              jnp.concatenate([r_w2[0], jnp.zeros((64, MIX_W), F32)], axis=0).astype(BF16),
              row(r_a0[0]),
              jnp.concatenate([jnp.zeros((64, MIX_W), F32), r_a2[0]], axis=0).astype(BF16),
              row(r_kk[0]), row(r_ka[0]), row(r_rk[0])],
        r_ln_g=row(r_ln_g[0]), r_ln_b=row(r_ln_b[0]),
        w_out1=od_w_out[0].astype(BF16), ln_g1=row(od_ln_g[0]), ln_b1=row(od_ln_b[0]),
    )

    x_p = jnp.concatenate([jnp.broadcast_to(meta_tokens[None], (nb, N_META, D_MODEL)), x_prompt],
                          axis=1).reshape(nb * P_LEN, D_MODEL)
    st_p = dict(conv=zeros(nb, CONV_W - 1, MIX_W), ssm_re=zeros(nb, 4096), ssm_im=zeros(nb, 4096),
                c=zeros(nb, M_HEADS, M_D, M_D), n=zeros(nb, M_HEADS, M_D), m=zeros(nb, M_HEADS),
                s=zeros(nb, R_HEADS, R_K, R_K), shift=zeros(nb, 3200))
    y_p, states_p = _trunk(x_p, st_p, w, CFG["P"])

    st_s = dict(conv=state_conv[0], ssm_re=state_ssm_re[0], ssm_im=state_ssm_im[0],
                c=state_mlstm_c[0], n=state_mlstm_n[0], m=state_mlstm_m[0],
                s=state_rwkv_s[0], shift=state_rwkv_shift[0])
    y_s, states_s = _trunk(x_sample.reshape(DEC_BATCH * DEC_SEQ, D_MODEL), st_s, w, CFG["S"])

    y_prompt = y_p
    y_sample = y_s.reshape(DEC_BATCH, DEC_SEQ, D_MODEL)
    return (y_prompt, y_sample) + states_p + states_s
```

```python
import functools
import math

import jax
import jax.numpy as jnp
from jax import lax
from jax.experimental import pallas as pl
from jax.experimental.pallas import tpu as pltpu

F32 = jnp.float32
BF16 = jnp.bfloat16

D_MODEL = 2048
MIX_W = 1024
N_META = 16
CONV_W = 31
S5_GROUP = 16
S5_GROUPS = 64
S5_STATE = 64
M_HEADS = 4
M_D = 256
R_HEADS = 16
R_K = 64
LN_EPS = 1e-5
R_LN_EPS = 64e-5
DEPTH = 2
ALPHA = (2 * DEPTH) ** 0.25

BATCH = 4
SEQ = 2048
P_LEN = N_META + SEQ
DEC_BATCH = 128
DEC_SEQ = 8

P1_Q, P1_K, P1_V, P1_O, P1_ZC = 0, 1024, 2048, 3072, 4096
P1_R, P1_RK, P1_RV, P1_ZD, P1_WA, P1_GATE = 5120, 6144, 7168, 8192, 9216, 9344
P1_N = 9728

VMEM_LIMIT = 48 * 1024 * 1024


def _cp(sem):
    return pltpu.CompilerParams(dimension_semantics=sem, vmem_limit_bytes=VMEM_LIMIT)


def _dot(a, b):
    return jnp.dot(a, b, preferred_element_type=F32)


def _dot_nt(a, b):
    return lax.dot_general(a, b, (((1,), (1,)), ((), ())), preferred_element_type=F32)


def _dot_tn(a, b):
    return lax.dot_general(a, b, (((0,), (0,)), ((), ())), preferred_element_type=F32)


def _split2(x):
    hi = x.astype(BF16)
    lo = (x - hi.astype(F32)).astype(BF16)
    return hi, lo


def _split3(x):
    hi = x.astype(BF16)
    r1 = x - hi.astype(F32)
    mid = r1.astype(BF16)
    lo = (r1 - mid.astype(F32)).astype(BF16)
    return hi, mid, lo


def _sigmoid(x):
    return jax.nn.sigmoid(x)


def _silu(x):
    return x * jax.nn.sigmoid(x)


def _softplus(x):
    return jnp.maximum(x, 0.0) + jnp.log(1.0 + jnp.exp(-jnp.abs(x)))


def _gelu_tanh(x):
    c = math.sqrt(2.0 / math.pi)
    return x * (0.5 * (1.0 + jnp.tanh(c * (x + 0.044715 * (x * x * x)))))


def _iota(shape, axis):
    return lax.broadcasted_iota(jnp.int32, shape, axis)


def _shr(x, k):
    return lax.shift_right_logical(x, jnp.int32(k))


def _log2(n):
    k = int(round(math.log2(n)))
    assert 1 << k == n
    return k


def _block_ones(n, seg, dtype):
    r = _shr(_iota((n, n), 0), _log2(seg))
    c = _shr(_iota((n, n), 1), _log2(seg))
    return (r == c).astype(dtype)


def _segsum(x, seg):
    g = _block_ones(256, seg, BF16)
    outs = []
    for j in range(x.shape[1] // 256):
        hi, lo = _split2(x[:, 256 * j:256 * (j + 1)])
        outs.append(_dot(hi, g) + _dot(lo, g))
    return jnp.concatenate(outs, axis=1)


def _row_cumsum(x, period):
    rows = x.shape[0]
    rid = _iota(x.shape, 0)
    if period < rows:
        rid = jnp.bitwise_and(rid, period - 1)
    d = 1
    while d < min(period, rows):
        x = x + jnp.where(rid >= d, pltpu.roll(x, d, 0), 0.0)
        d *= 2
    return x


def _row_cummax(x, period):
    rows = x.shape[0]
    rid = _iota(x.shape, 0)
    if period < rows:
        rid = jnp.bitwise_and(rid, period - 1)
    d = 1
    while d < min(period, rows):
        x = jnp.maximum(x, jnp.where(rid >= d, pltpu.roll(x, d, 0), -jnp.inf))
        d *= 2
    return x


def _mm_kernel(x_ref, w_ref, o_ref):
    o_ref[...] = _dot(x_ref[...].astype(BF16), w_ref[...])


def _matmul(x, w, tm, tn):
    r, k = x.shape
    n = w.shape[1]
    return pl.pallas_call(
        _mm_kernel,
        out_shape=jax.ShapeDtypeStruct((r, n), F32),
        grid=(pl.cdiv(r, tm), n // tn),
        in_specs=[pl.BlockSpec((tm, k), lambda i, j: (i, 0)),
                  pl.BlockSpec((k, tn), lambda i, j: (0, j))],
        out_specs=pl.BlockSpec((tm, tn), lambda i, j: (i, j)),
        compiler_params=_cp(("parallel", "arbitrary")),
    )(x, w)


def _mm_nt_kernel(x_ref, w_ref, o_ref):
    o_ref[...] = _dot_nt(x_ref[...].astype(BF16), w_ref[...])


def _matmul_nt(x, w_t, tm, tn):
    r, k = x.shape
    n = w_t.shape[0]
    return pl.pallas_call(
        _mm_nt_kernel,
        out_shape=jax.ShapeDtypeStruct((r, n), F32),
        grid=(pl.cdiv(r, tm), n // tn),
        in_specs=[pl.BlockSpec((tm, k), lambda i, j: (i, 0)),
                  pl.BlockSpec((tn, k), lambda i, j: (j, 0))],
        out_specs=pl.BlockSpec((tm, tn), lambda i, j: (i, j)),
        compiler_params=_cp(("parallel", "arbitrary")),
    )(x, w_t)


def _pw_kernel(a_ref, w_ref, z_ref, o_ref):
    o_ref[...] = _dot(a_ref[...].astype(BF16), w_ref[...]) * _silu(z_ref[...])


def _pw_gate(act, pw, proj0, tm, tn=512):
    r = act.shape[0]
    zb = 2048 // tn
    return pl.pallas_call(
        _pw_kernel,
        out_shape=jax.ShapeDtypeStruct((r, MIX_W), F32),
        grid=(pl.cdiv(r, tm), MIX_W // tn),
        in_specs=[pl.BlockSpec((tm, MIX_W), lambda i, j: (i, 0)),
                  pl.BlockSpec((MIX_W, tn), lambda i, j: (0, j)),
                  pl.BlockSpec((tm, tn), lambda i, j: (i, zb + j))],
        out_specs=pl.BlockSpec((tm, tn), lambda i, j: (i, j)),
        compiler_params=_cp(("parallel", "arbitrary")),
    )(act, pw, proj0)


def _glu_kernel(y_ref, wv_ref, wg_ref, bv_ref, bg_ref, z_ref, o_ref):
    y = y_ref[...].astype(BF16)
    v = _dot(y, wv_ref[...]) + bv_ref[...]
    g = _dot(y, wg_ref[...]) + bg_ref[...]
    o_ref[...] = v * _sigmoid(g) * _silu(z_ref[...])


def _glu_gate(yb, glu_w, glu_b, proj0, tm, tn=512):
    r = yb.shape[0]
    nb = MIX_W // tn
    zb = 4096 // tn
    return pl.pallas_call(
        _glu_kernel,
        out_shape=jax.ShapeDtypeStruct((r, MIX_W), F32),
        grid=(pl.cdiv(r, tm), nb),
        in_specs=[pl.BlockSpec((tm, MIX_W), lambda i, j: (i, 0)),
                  pl.BlockSpec((MIX_W, tn), lambda i, j: (0, j)),
                  pl.BlockSpec((MIX_W, tn), lambda i, j: (0, nb + j)),
                  pl.BlockSpec((1, tn), lambda i, j: (0, j)),
                  pl.BlockSpec((1, tn), lambda i, j: (0, nb + j)),
                  pl.BlockSpec((tm, tn), lambda i, j: (i, zb + j))],
        out_specs=pl.BlockSpec((tm, tn), lambda i, j: (i, j)),
        compiler_params=_cp(("parallel", "arbitrary")),
    )(yb, glu_w, glu_w, glu_b, glu_b, proj0)


def _out_ln_kernel(x_ref, ma_ref, mb_ref, wa_ref, wb_ref, g_ref, b_ref, o_ref):
    out = _dot(ma_ref[...].astype(BF16), wa_ref[...]) + _dot(mb_ref[...].astype(BF16), wb_ref[...])
    y = ALPHA * x_ref[...] + out
    mu = jnp.mean(y, axis=-1, keepdims=True)
    yc = y - mu
    var = jnp.mean(yc * yc, axis=-1, keepdims=True)
    o_ref[...] = yc * lax.rsqrt(var + LN_EPS) * g_ref[...] + b_ref[...]


def _out_ln_bf16_kernel(x_ref, ma_ref, mb_ref, wa_ref, wb_ref, g_ref, b_ref, o_ref, ob_ref):
    _out_ln_kernel(x_ref, ma_ref, mb_ref, wa_ref, wb_ref, g_ref, b_ref, o_ref)
    ob_ref[...] = o_ref[...].astype(BF16)


def _out_ln(x, mix_a, mix_b, w_out, ln_g, ln_b, tm, with_bf16=False):
    r = x.shape[0]
    o_spec = pl.BlockSpec((tm, D_MODEL), lambda i: (i, 0))
    return pl.pallas_call(
        _out_ln_bf16_kernel if with_bf16 else _out_ln_kernel,
        out_shape=((jax.ShapeDtypeStruct((r, D_MODEL), F32), jax.ShapeDtypeStruct((r, D_MODEL), BF16))
                   if with_bf16 else jax.ShapeDtypeStruct((r, D_MODEL), F32)),
        grid=(pl.cdiv(r, tm),),
        in_specs=[pl.BlockSpec((tm, D_MODEL), lambda i: (i, 0)),
                  pl.BlockSpec((tm, MIX_W), lambda i: (i, 0)),
                  pl.BlockSpec((tm, MIX_W), lambda i: (i, 0)),
                  pl.BlockSpec((MIX_W, D_MODEL), lambda i: (0, 0), pipeline_mode=pl.Buffered(1)),
                  pl.BlockSpec((MIX_W, D_MODEL), lambda i: (1, 0), pipeline_mode=pl.Buffered(1)),
                  pl.BlockSpec((1, D_MODEL), lambda i: (0, 0)),
                  pl.BlockSpec((1, D_MODEL), lambda i: (0, 0))],
        out_specs=(o_spec, o_spec) if with_bf16 else o_spec,
        compiler_params=_cp(("parallel",)),
    )(x, mix_a, mix_b, w_out, w_out, ln_g, ln_b)


def _out_ln_prompt(x, mix_a, mix_b, w_out, ln_g, ln_b, tm=512):
    tiles = SEQ // tm

    def rows(width):
        return pl.BlockSpec((pl.Element(tm), pl.Element(width)),
                            lambda n, t: (pl.multiple_of(n * P_LEN + N_META + t * tm, 8), 0))

    return pl.pallas_call(
        _out_ln_kernel,
        out_shape=jax.ShapeDtypeStruct((BATCH * SEQ, D_MODEL), F32),
        grid=(BATCH, tiles),
        in_specs=[rows(D_MODEL), rows(MIX_W), rows(MIX_W),
                  pl.BlockSpec((MIX_W, D_MODEL), lambda n, t: (0, 0), pipeline_mode=pl.Buffered(1)),
                  pl.BlockSpec((MIX_W, D_MODEL), lambda n, t: (1, 0), pipeline_mode=pl.Buffered(1)),
                  pl.BlockSpec((1, D_MODEL), lambda n, t: (0, 0)),
                  pl.BlockSpec((1, D_MODEL), lambda n, t: (0, 0))],
        out_specs=pl.BlockSpec((tm, D_MODEL), lambda n, t: (n * tiles + t, 0)),
        compiler_params=_cp(("parallel", "arbitrary")),
    )(x, mix_a, mix_b, w_out, w_out, ln_g, ln_b).reshape(BATCH, SEQ, D_MODEL)


def _conv_kernel(u_ref, g_ref, z_ref, st_ref, w_ref, cb_ref, lg_ref, lb_ref, pw_ref, act_ref, nst_ref,
                 hp_ref, hs_ref, wb_ref, *, NB, TL, T):
    t = pl.program_id(1)
    for j in range(CONV_W):
        wb_ref[j] = jnp.broadcast_to(w_ref[j:j + 1, :], (8, MIX_W))
    for nb in range(NB):
        base = nb * TL

        @pl.when(t == 0)
        def _():
            hp_ref[nb, 0:2, :] = jnp.zeros((2, MIX_W), F32)
            hp_ref[nb, 2:32, :] = st_ref[nb]

        hp_ref[nb, TL + 32:TL + 40, :] = jnp.zeros((8, MIX_W), F32)
        hp_ref[nb, 32:32 + TL, :] = u_ref[base:base + TL, :] * _sigmoid(g_ref[base:base + TL, :])
        for b in range(8):
            hs_ref[b] = hp_ref[nb, b:b + TL + 32, :]

        def taps(r0, groups):
            acc = [None] * groups
            for j in range(CONV_W):
                o = j + 2
                wj = wb_ref[j]
                for g in range(groups):
                    term = wj * hs_ref[o % 8, pl.ds(r0 + 8 * (o // 8 + g), 8), :]
                    acc[g] = term if acc[g] is None else acc[g] + term
            for g in range(groups):
                act_ref[pl.ds(base + r0 + 8 * g, 8), :] = acc[g]

        def chunk(c, carry):
            taps(pl.multiple_of(c * 16, 16), 2)
            return carry

        lax.fori_loop(0, TL // 16, chunk, 0)
        if TL % 16:
            taps(TL - 8, 1)

        @pl.when(t == T - 1)
        def _():
            nst_ref[nb] = hp_ref[nb, TL + 2:TL + 32, :]

        if T > 1:
            hp_ref[nb, 0:32, :] = hp_ref[nb, TL:TL + 32, :]

    y = act_ref[...] + cb_ref[...]
    mu = jnp.mean(y, axis=-1, keepdims=True)
    yc = y - mu
    var = jnp.mean(yc * yc, axis=-1, keepdims=True)
    act = _silu(yc * lax.rsqrt(var + LN_EPS) * lg_ref[...] + lb_ref[...])
    act_ref[...] = _dot(act.astype(BF16), pw_ref[...]) * _silu(z_ref[...])


def _conv_call(proj0, state, conv_w, conv_b, ln_g, ln_b, pw, *, N, L, NB, TL):
    T = L // TL
    RB = NB * TL
    assert NB == 1 or T == 1
    return pl.pallas_call(
        functools.partial(_conv_kernel, NB=NB, TL=TL, T=T),
        out_shape=(jax.ShapeDtypeStruct((N * L, MIX_W), F32),
                   jax.ShapeDtypeStruct((N, CONV_W - 1, MIX_W), F32)),
        grid=(N // NB, T),
        in_specs=[pl.BlockSpec((RB, MIX_W), lambda i, t: (i * T + t, 0)),
                  pl.BlockSpec((RB, MIX_W), lambda i, t: (i * T + t, 1)),
                  pl.BlockSpec((RB, MIX_W), lambda i, t: (i * T + t, 2)),
                  pl.BlockSpec((NB, CONV_W - 1, MIX_W), lambda i, t: (i, 0, 0)),
                  pl.BlockSpec((CONV_W, MIX_W), lambda i, t: (0, 0)),
                  pl.BlockSpec((1, MIX_W), lambda i, t: (0, 0)),
                  pl.BlockSpec((1, MIX_W), lambda i, t: (0, 0)),
                  pl.BlockSpec((1, MIX_W), lambda i, t: (0, 0)),
                  pl.BlockSpec((MIX_W, MIX_W), lambda i, t: (0, 0), pipeline_mode=pl.Buffered(1))],
        out_specs=(pl.BlockSpec((RB, MIX_W), lambda i, t: (i * T + t, 0)),
                   pl.BlockSpec((NB, CONV_W - 1, MIX_W), lambda i, t: (i, 0, 0))),
        scratch_shapes=[pltpu.VMEM((NB, TL + 40, MIX_W), F32),
                        pltpu.VMEM((8, TL + 32, MIX_W), F32),
                        pltpu.VMEM((CONV_W, 8, MIX_W), F32)],
        compiler_params=_cp(("arbitrary", "arbitrary")),
    )(proj0, proj0, proj0, state, conv_w, conv_b, ln_g, ln_b, pw)


def _s5_kernel(u_ref, wb_ref, wk_ref, wc_ref, d_ref, ar_ref, ai_ref, x0r_ref, x0i_ref,
               y_ref, xfr_ref, xfi_ref, xs_ref, cr_ref, ci_ref, *, NB, TL, T):
    t = pl.program_id(2)
    RB = NB * TL
    GL = TL // 8
    u = u_ref[...]
    ub = u.astype(BF16)
    big = _dot(ub, wb_ref[0])
    xs_ref[0] = big[:, :1024]
    xs_ref[1] = big[:, 1024:]
    rid = jnp.bitwise_and(_iota((RB, 256), 0), 7)
    lags = [ub] + [jnp.where(rid >= d, pltpu.roll(u, d, 0), 0.0).astype(BF16) for d in range(1, 8)]
    y_loc = _dot(jnp.concatenate(lags, axis=1), wk_ref[0])
    ar = ar_ref[...]
    ai = ai_ref[...]

    def cmul(pr, pi, qr, qi):
        return pr * qr - pi * qi, pr * qi + pi * qr

    a1 = (ar, ai)
    a2 = cmul(*a1, *a1)
    a4 = cmul(*a2, *a2)
    a3 = cmul(*a2, *a1)
    a5 = cmul(*a4, *a1)
    a6 = cmul(*a4, *a2)
    a7 = cmul(*a6, *a1)
    a8 = cmul(*a4, *a4)
    a0 = (jnp.ones_like(ar), jnp.zeros_like(ai))
    r8 = _iota((8, 1024), 0)

    def table(powers):
        tr = jnp.zeros((8, 1024), F32)
        ti = jnp.zeros((8, 1024), F32)
        for k, (pr, pi) in enumerate(powers):
            tr = jnp.where(r8 == k, pr, tr)
            ti = jnp.where(r8 == k, pi, ti)
        return tr, ti

    pwr, pwi = table((a1, a2, a3, a4, a5, a6, a7, a8))
    qwr, qwi = table((a7, a6, a5, a4, a3, a2, a1, a0))
    a8r, a8i = a8

    first = t == 0

    def seq_body(nb, carry0):
        x0r = x0r_ref[nb]
        x0i = x0i_ref[nb]
        if T > 1:
            c_r = jnp.where(first, x0r, cr_ref[0:1, :])
            c_i = jnp.where(first, x0i, ci_ref[0:1, :])
        else:
            c_r, c_i = x0r, x0i

        def grp(g, c):
            c_r, c_i = c
            off = pl.multiple_of(nb * TL + g * 8, 8)
            vr = xs_ref[0, pl.ds(off, 8), :]
            vi = xs_ref[1, pl.ds(off, 8), :]
            er = jnp.sum(qwr * vr - qwi * vi, axis=0, keepdims=True)
            ei = jnp.sum(qwr * vi + qwi * vr, axis=0, keepdims=True)
            br = jnp.broadcast_to(c_r, (8, 1024))
            bi = jnp.broadcast_to(c_i, (8, 1024))
            xs_ref[0, pl.ds(off, 8), :] = pwr * br - pwi * bi
            xs_ref[1, pl.ds(off, 8), :] = pwr * bi + pwi * br
            return a8r * c_r - a8i * c_i + er, a8r * c_i + a8i * c_r + ei

        c_r, c_i = lax.fori_loop(0, GL, grp, (c_r, c_i))
        if T > 1:
            cr_ref[...] = jnp.broadcast_to(c_r, (8, 1024))
            ci_ref[...] = jnp.broadcast_to(c_i, (8, 1024))

        @pl.when(t == T - 1)
        def _():
            xfr_ref[nb] = c_r
            xfi_ref[nb] = c_i

        return carry0

    lax.fori_loop(0, NB, seq_body, 0)
    y = (_dot(xs_ref[0].astype(BF16), wc_ref[0, 0:1024, :])
         + _dot(xs_ref[1].astype(BF16), wc_ref[0, 1024:2048, :]))
    y_ref[...] = _gelu_tanh(y + y_loc + d_ref[...] * u)


def _s5_call(proj0, wb, wk, wc, dvec, ar, ai, x0r, x0i, *, N, L, NB, TL):
    T = L // TL
    RB = NB * TL
    assert NB == 1 or T == 1
    ub = 3072 // 256
    st = jax.ShapeDtypeStruct((N, 1, 4096), F32)
    return pl.pallas_call(
        functools.partial(_s5_kernel, NB=NB, TL=TL, T=T),
        out_shape=(jax.ShapeDtypeStruct((N * L, MIX_W), F32), st, st),
        grid=(N // NB, 4, T),
        in_specs=[pl.BlockSpec((RB, 256), lambda i, j, t: (i * T + t, ub + j)),
                  pl.BlockSpec((1, 256, 2048), lambda i, j, t: (j, 0, 0)),
                  pl.BlockSpec((1, 2048, 256), lambda i, j, t: (j, 0, 0)),
                  pl.BlockSpec((1, 2048, 256), lambda i, j, t: (j, 0, 0)),
                  pl.BlockSpec((1, 256), lambda i, j, t: (0, j)),
                  pl.BlockSpec((1, 1024), lambda i, j, t: (0, j)),
                  pl.BlockSpec((1, 1024), lambda i, j, t: (0, j)),
                  pl.BlockSpec((NB, 1, 1024), lambda i, j, t: (i, 0, j)),
                  pl.BlockSpec((NB, 1, 1024), lambda i, j, t: (i, 0, j))],
        out_specs=(pl.BlockSpec((RB, 256), lambda i, j, t: (i * T + t, j)),
                   pl.BlockSpec((NB, 1, 1024), lambda i, j, t: (i, 0, j)),
                   pl.BlockSpec((NB, 1, 1024), lambda i, j, t: (i, 0, j))),
        scratch_shapes=[pltpu.VMEM((2, RB, 1024), F32),
                        pltpu.VMEM((8, 1024), F32),
                        pltpu.VMEM((8, 1024), F32)],
        compiler_params=_cp(("arbitrary", "arbitrary", "arbitrary")),
    )(proj0, wb, wk, wc, dvec, ar, ai, x0r, x0i)


def _mlstm_kernel(q_ref, k_ref, v_ref, o_ref, z_ref, gt_ref, gb_ref, hg_ref, c0_ref, n0_ref, m0_ref,
                  y_ref, c_ref, n_ref, m_ref, cs_ref, ns_ref, ms_ref, *, NB, TL, T):
    for nb in range(NB):
        rows = lambda ref: ref.at[pl.ds(nb * TL, TL)]
        one = lambda ref: ref.at[pl.ds(nb, 1)]
        _mlstm_seq(rows(q_ref), rows(k_ref), rows(v_ref), rows(o_ref), rows(z_ref), rows(gt_ref), gb_ref, hg_ref,
                   one(c0_ref), one(n0_ref), one(m0_ref), rows(y_ref), one(c_ref), one(n_ref), one(m_ref),
                   cs_ref.at[nb], ns_ref.at[nb], ms_ref.at[nb], TL=TL, T=T)


def _mlstm_seq(q_ref, k_ref, v_ref, o_ref, z_ref, gt_ref, gb_ref, hg_ref, c0_ref, n0_ref, m0_ref,
               y_ref, c_ref, n_ref, m_ref, cs_ref, ns_ref, ms_ref, *, TL, T):
    t = pl.program_id(1)

    @pl.when(t == 0)
    def _():
        cs_ref[...] = c0_ref[0]
        ns_ref[...] = n0_ref[0]
        ms_ref[...] = m0_ref[0]

    G = gt_ref[...] + gb_ref[...]
    B = _row_cumsum(-_softplus(-G), TL)
    Bs = pltpu.roll(B, 124, 1)
    A = G - Bs
    CM = _row_cummax(A, TL)
    ms = ms_ref[...]
    dg = _iota((8, 128), 0) == _iota((8, 128), 1)
    mrow = jnp.sum(jnp.where(dg, ms, 0.0), axis=0, keepdims=True)
    M = jnp.maximum(mrow, CM)
    MT = Bs + M
    sel = dg.astype(BF16)
    a_hi, a_mid, a_lo = _split3(A)
    Arow = _dot_nt(sel, a_hi) + _dot_nt(sel, a_mid) + _dot_nt(sel, a_lo)
    causal = _iota((TL, TL), 0) >= _iota((TL, TL), 1)
    H = range(M_HEADS)
    sl = [slice(M_D * h, M_D * (h + 1)) for h in H]
    q = [q_ref[:, sl[h]] * (M_D ** -0.5) for h in H]
    qb = [x.astype(BF16) for x in q]
    kf = [k_ref[:, sl[h]] for h in H]
    kb = [x.astype(BF16) for x in kf]
    vf = [v_ref[:, sl[h]] for h in H]
    c_old = [cs_ref[h] for h in H]
    n_old = [ns_ref[h:h + 1, :] for h in H]
    m_col = [M[:, h:h + 1] for h in H]
    mt_col = [MT[:, h:h + 1] for h in H]
    b_col = [Bs[:, h:h + 1] for h in H]
    m_prev = [mrow[:, h:h + 1] for h in H]
    dm = [jnp.exp(jnp.where(causal, Arow[h:h + 1, :] - m_col[h], -jnp.inf)) for h in H]
    s = [_dot_nt(qb[h], kb[h]) * dm[h] for h in H]
    inter = [jnp.exp(m_prev[h] - m_col[h]) for h in H]
    h_intra = [_dot(s[h].astype(BF16), vf[h].astype(BF16)) for h in H]
    h_inter = [_dot(qb[h], c_old[h].astype(BF16)) * inter[h] for h in H]
    n_all = [jnp.sum(s[h], axis=1, keepdims=True) + jnp.sum(q[h] * n_old[h], axis=1, keepdims=True) * inter[h]
             for h in H]
    hh = [(h_intra[h] + h_inter[h]) / jnp.maximum(jnp.abs(n_all[h]), jnp.exp(-mt_col[h])) for h in H]
    m_new = [mt_col[h][TL - 1:TL, :] for h in H]
    b_end = [b_col[h][TL - 1:TL, :] for h in H]
    dec = [jnp.exp(m_prev[h] + b_end[h] - m_new[h]) for h in H]
    w_s = [jnp.exp(b_end[h] - b_col[h] + G[:, h:h + 1] - m_new[h]) for h in H]
    c_new = [c_old[h] * dec[h] + _dot_tn(kb[h], (vf[h] * w_s[h]).astype(BF16)) for h in H]
    n_new = [n_old[h] * dec[h] + jnp.sum(kf[h] * w_s[h], axis=0, keepdims=True) for h in H]
    outs = []
    for h in H:
        mu = jnp.mean(hh[h], axis=-1, keepdims=True)
        hc = hh[h] - mu
        var = jnp.mean(hc * hc, axis=-1, keepdims=True)
        hn = hc * lax.rsqrt(var + LN_EPS) * hg_ref[:, sl[h]]
        outs.append(hn * _sigmoid(o_ref[:, sl[h]]) * _silu(z_ref[:, sl[h]]))
    for h in H:
        cs_ref[h] = c_new[h]
        ns_ref[h:h + 1, :] = n_new[h]
        ms_ref[h:h + 1, :] = jnp.broadcast_to(m_new[h], (1, 128))
        y_ref[:, sl[h]] = outs[h]

    @pl.when(t == T - 1)
    def _():
        c_ref[0] = cs_ref[...]
        n_ref[0] = ns_ref[...]
        m_ref[0] = ms_ref[...]


def _mlstm_call(proj1, gate_b, hn_g, c0, n0, m0, *, N, L, NB, TL):
    T = L // TL
    RB = NB * TL
    assert NB == 1 or T == 1

    def col(cb):
        return pl.BlockSpec((RB, MIX_W), lambda i, t: (i * T + t, cb))

    st_specs = (pl.BlockSpec((NB, M_HEADS, M_D, M_D), lambda i, t: (i, 0, 0, 0)),
                pl.BlockSpec((NB, M_HEADS, M_D), lambda i, t: (i, 0, 0)),
                pl.BlockSpec((NB, 8, 128), lambda i, t: (i, 0, 0)))
    return pl.pallas_call(
        functools.partial(_mlstm_kernel, NB=NB, TL=TL, T=T),
        out_shape=(jax.ShapeDtypeStruct((N * L, MIX_W), F32),
                   jax.ShapeDtypeStruct((N, M_HEADS, M_D, M_D), F32),
                   jax.ShapeDtypeStruct((N, M_HEADS, M_D), F32),
                   jax.ShapeDtypeStruct((N, 8, 128), F32)),
        grid=(N // NB, T),
        in_specs=[col(0), col(1), col(2), col(3), col(4),
                  pl.BlockSpec((RB, 128), lambda i, t: (i * T + t, P1_GATE // 128)),
                  pl.BlockSpec((1, 128), lambda i, t: (0, 0)),
                  pl.BlockSpec((1, MIX_W), lambda i, t: (0, 0))] + list(st_specs),
        out_specs=(pl.BlockSpec((RB, MIX_W), lambda i, t: (i * T + t, 0)),) + st_specs,
        scratch_shapes=[pltpu.VMEM((NB, M_HEADS, M_D, M_D), F32),
                        pltpu.VMEM((NB, M_HEADS, M_D), F32),
                        pltpu.VMEM((NB, 8, 128), F32)],
        compiler_params=_cp(("arbitrary", "arbitrary")),
    )(proj1, proj1, proj1, proj1, proj1, proj1, gate_b, hn_g, c0, n0, m0)


N_RA_OUT = 9


def _rwkv_a_kernel(pr_ref, pk_ref, pv_ref, pwa_ref, hr_ref, hk_ref, hv_ref, hwa_ref, st_ref, stwa_ref,
                   mu_ref, muwa_ref, w0_ref, w2_ref, a0_ref, a2_ref, kkp_ref, ka_ref, rk_ref,
                   ah_ref, rh_ref, bh_ref, kh_ref, vo_ref, ul_ref, yl_ref, dc_ref, bo_ref, shr_ref, shw_ref,
                   s_at, s_rt, s_bt, s_kt, s_v, s_cum, *, RB, CT, N, L, U):
    HS = R_HEADS * CT
    HG = 128 // CT
    NG = R_HEADS // HG
    GW = HG * R_K
    NCH = RB // CT
    short = L == CT
    rid = _iota((RB, 1), 0)
    grow = pl.program_id(0) * RB + rid

    if short:
        spread = (_shr(_iota((RB, RB // CT), 0), _log2(CT)) == _iota((RB, RB // CT), 1)).astype(BF16)

    def shifted(p_ref, h_ref, s_ref, lo, hi, mu):
        p = p_ref[...]
        prev = pltpu.roll(p, 1, 0)
        if short:
            hi3, mid3, lo3 = _split3(s_ref[:, lo:hi])
            first_prev = _dot(spread, hi3) + _dot(spread, mid3) + _dot(spread, lo3)
            prev = jnp.where(jnp.bitwise_and(rid, CT - 1) == 0, first_prev, prev)
        else:
            prev = jnp.where(rid == 0, h_ref[7:8, :], prev)
            for n in range(N):
                prev = jnp.where(grow == n * L, s_ref[n:n + 1, lo:hi], prev)
        return p + (prev - p) * mu

    raw = ((pr_ref, shr_ref, 0), (pk_ref, shr_ref, 1024), (pv_ref, shr_ref, 2048), (pwa_ref, shw_ref, 0))
    if short:
        nseq = RB // CT
        sel = (_iota((nseq, RB), 1) == _iota((nseq, RB), 0) * CT + (CT - 1)).astype(BF16)
        for src, dst, lo in raw:
            hi, mid, low = _split3(src[...])
            dst[:, lo:lo + src.shape[1]] = _dot(sel, hi) + _dot(sel, mid) + _dot(sel, low)
    else:
        for n in range(N):
            tile, off = divmod(n * L + L - 1, RB)

            @pl.when(pl.program_id(0) == tile)
            def _():
                for src, dst, lo in raw:
                    dst[n:n + 1, lo:lo + src.shape[1]] = src[off:off + 1, :]

    r = shifted(pr_ref, hr_ref, st_ref, 0, 1024, mu_ref[:, 0:1024])
    k = shifted(pk_ref, hk_ref, st_ref, 1024, 2048, mu_ref[:, 1024:2048])
    v = shifted(pv_ref, hv_ref, st_ref, 2048, 3072, mu_ref[:, 2048:3072])
    wa = shifted(pwa_ref, hwa_ref, stwa_ref, 0, 128, muwa_ref[...])
    w = -_softplus(-(w0_ref[...] + _dot(jnp.tanh(wa).astype(BF16), w2_ref[...]))) - 0.5
    wlog = -jnp.exp(w)
    a = _sigmoid(a0_ref[...] + _dot(wa.astype(BF16), a2_ref[...]))
    kk = k * kkp_ref[...]
    kk = kk / jnp.maximum(jnp.sqrt(_segsum(kk * kk, R_K)), 1e-12)
    kmod = k * (1.0 + (a - 1.0) * ka_ref[...])
    bo_ref[...] = _segsum(r * kmod * rk_ref[...], R_K) * v
    cum = _row_cumsum(wlog, CT)
    einv = jnp.exp(-cum)
    s_at[...] = (-kk) * jnp.exp(cum - wlog)
    s_rt[...] = r * jnp.exp(cum)
    s_bt[...] = kk * a * einv
    s_kt[...] = kmod * einv
    s_v[...] = v
    s_cum[...] = cum

    mdt = BF16 if CT % 16 == 0 else F32
    be_mask = (_shr(_iota((128, GW), 0), _log2(CT)) == _shr(_iota((128, GW), 1), _log2(R_K))).astype(mdt)
    bd_mask = (_shr(_iota((HS, HS), 0), _log2(CT)) == _shr(_iota((HS, HS), 1), _log2(CT))).astype(mdt)
    tt = _iota((CT, HS), 0)
    ss = jnp.bitwise_and(_iota((CT, HS), 1), CT - 1)
    strict = tt > ss
    incl = tt >= ss
    eye_c = (tt == ss).astype(F32)
    cat0 = lambda *xs: jnp.concatenate(xs, axis=0)

    def blockexp(x):
        return [(jnp.concatenate([x[:, GW * g:GW * (g + 1)].astype(mdt)] * HG, axis=0) * be_mask).astype(BF16)
                for g in range(NG)]

    def gram(lhs, be):
        lb = lhs.astype(BF16)
        return jnp.concatenate([_dot_nt(lb[:, GW * g:GW * (g + 1)], be[g]) for g in range(NG)], axis=1)

    def apply(cmp, be):
        cb = cmp.astype(BF16)
        return jnp.concatenate([_dot(cb[:, 128 * g:128 * (g + 1)], be[g]) for g in range(NG)], axis=1)

    def bdiag(x):
        return (jnp.concatenate([x.astype(mdt)] * R_HEADS, axis=0) * bd_mask).astype(BF16)

    def bdiag_hl(x):
        if mdt == BF16:
            return tuple(bdiag(part) for part in _split2(x))
        return _split2(jnp.concatenate([x] * R_HEADS, axis=0) * bd_mask)

    def mm_hl(stack, wh, wl):
        sh, sl = _split2(stack)
        n = stack.shape[0]
        full = _dot(cat0(sh, sl), wh)
        return full[:n] + full[n:] + _dot(sh, wl)

    def chunks(i, carry):
        rows = [pl.ds(pl.multiple_of((i * U + u) * CT, CT), CT) for u in range(U)]
        ld = lambda ref: [ref[rw, :] for rw in rows]
        at, rt, bt, kt, vv, cm = ld(s_at), ld(s_rt), ld(s_bt), ld(s_kt), ld(s_v), ld(s_cum)
        each = lambda f, *xs: [f(*a_) for a_ in zip(*xs)]
        ar_ = each(cat0, at, rt)
        gb = each(lambda l_, y_: gram(l_, blockexp(y_)), ar_, bt)
        gk = each(lambda l_, y_: gram(l_, blockexp(y_)), ar_, kt)
        a_ab = each(lambda m: jnp.where(strict, m[:CT], 0.0), gb)
        a_rb = each(lambda m: jnp.where(incl, m[CT:], 0.0), gb)
        a_ak = each(lambda m: jnp.where(strict, m[:CT], 0.0), gk)
        a_rk = each(lambda m: jnp.where(incl, m[CT:], 0.0), gk)
        p = each(lambda m: eye_c + m, a_ab)
        x = a_ab
        q = a_rb
        w_hl = each(bdiag_hl, x)
        res = each(lambda x_, q_, w_: mm_hl(cat0(x_, q_), *w_), x, q, w_hl)
        x = each(lambda r_: r_[:CT], res)
        q = each(lambda q_, r_: q_ + r_[CT:], q, res)
        pw = 2
        while pw < CT:
            w_hl = each(bdiag_hl, x)
            if 2 * pw >= CT:
                res = each(lambda p_, q_, w_: mm_hl(cat0(p_, q_), *w_), p, q, w_hl)
                q = each(lambda q_, r_: q_ + r_[CT:], q, res)
            else:
                res = each(lambda p_, x_, q_, w_: mm_hl(cat0(p_, x_, q_), *w_), p, x, q, w_hl)
                x = each(lambda r_: r_[CT:2 * CT], res)
                q = each(lambda q_, r_: q_ + r_[2 * CT:], q, res)
            p = each(lambda p_, r_: p_ + r_[:CT], p, res)
            pw *= 2
        tq = each(cat0, p, q)
        res = each(lambda m, k_: _dot(m.astype(BF16), bdiag(k_)), tq, a_ak)
        ty = each(lambda r_, k_: cat0(r_[:CT], r_[CT:] + k_), res, a_rk)
        o1 = each(lambda m, y_: apply(m, blockexp(y_)), tq, at)
        o2 = each(lambda m, y_: apply(m, blockexp(y_)), ty, vv)
        ect = each(lambda c_: jnp.exp(c_[CT - 1:CT, :]), cm)
        for u, rw in enumerate(rows):
            ah_ref[rw, :] = o1[u][:CT]
            rh_ref[rw, :] = rt[u] + o1[u][CT:]
            ul_ref[rw, :] = o2[u][:CT]
            yl_ref[rw, :] = o2[u][CT:]
            bh_ref[rw, :] = bt[u] * ect[u]
            kh_ref[rw, :] = kt[u] * ect[u]
            vo_ref[rw, :] = vv[u]
            dc_ref[rw, :] = jnp.broadcast_to(ect[u], (CT, 1024))
        return carry

    lax.fori_loop(0, NCH // U, chunks, 0)


def _rwkv_a_call(proj1, st_rkv, st_wa, wts, *, N, L, RB, CT, U):
    short = L == CT
    rows = N * L
    assert rows % RB == 0 and (RB // CT) % U == 0

    def col(cb, width=MIX_W):
        return pl.BlockSpec((RB, width), lambda i: (i, cb))

    def halo(cb, width=MIX_W):
        return pl.BlockSpec((8, width), lambda i: (jnp.maximum(i * (RB // 8) - 1, 0), cb))

    if short:
        st_specs = [pl.BlockSpec((RB // CT, 3072), lambda i: (i, 0)), pl.BlockSpec((RB // CT, 128), lambda i: (i, 0))]
    else:
        st_specs = [pl.BlockSpec((N, 3072), lambda i: (0, 0)), pl.BlockSpec((N, 128), lambda i: (0, 0))]

    def full(shape):
        return pl.BlockSpec(shape, lambda i: (0,) * len(shape))

    o_spec = pl.BlockSpec((RB, MIX_W), lambda i: (i, 0))
    if short:
        sh_specs = (pl.BlockSpec((RB // CT, 3072), lambda i: (i, 0)), pl.BlockSpec((RB // CT, 128), lambda i: (i, 0)))
    else:
        sh_specs = (pl.BlockSpec((N, 3072), lambda i: (0, 0)), pl.BlockSpec((N, 128), lambda i: (0, 0)))
    outs = pl.pallas_call(
        functools.partial(_rwkv_a_kernel, RB=RB, CT=CT, N=N, L=L, U=U),
        out_shape=((jax.ShapeDtypeStruct((rows, MIX_W), F32),) * N_RA_OUT
                   + (jax.ShapeDtypeStruct((N, 3072), F32), jax.ShapeDtypeStruct((N, 128), F32))),
        grid=(rows // RB,),
        in_specs=([col(P1_R // 1024), col(P1_RK // 1024), col(P1_RV // 1024), col(P1_WA // 128, 128),
                   halo(P1_R // 1024), halo(P1_RK // 1024), halo(P1_RV // 1024), halo(P1_WA // 128, 128)]
                  + st_specs
                  + [full((1, 3072)), full((1, 128)), full((1, 1024)), full((128, 1024)), full((1, 1024)),
                     full((128, 1024)), full((1, 1024)), full((1, 1024)), full((1, 1024))]),
        out_specs=(o_spec,) * N_RA_OUT + sh_specs,
        scratch_shapes=[pltpu.VMEM((RB, MIX_W), F32)] * 6,
        compiler_params=_cp(("arbitrary",)),
    )(*([proj1] * 8 + [st_rkv, st_wa] + list(wts)))
    return outs[:N_RA_OUT], jnp.concatenate(outs[N_RA_OUT:], axis=1)


def _rwkv_b_kernel(ah_ref, rh_ref, bh_ref, kh_ref, v_ref, ul_ref, yl_ref, dc_ref, bo_ref, z_ref, lg_ref, lb_ref,
                   s0_ref, y_ref, so_ref, sbd_ref, yb_ref, *, NBLK, TLB, CT, T):
    t = pl.program_id(1)
    bd_mask = (_shr(_iota((256, 256), 0), 6) == _shr(_iota((256, 256), 1), 6)).astype(F32)

    @pl.when(t == 0)
    def _():
        for nb in range(NBLK):
            for j in range(4):
                x = s0_ref[nb, 256 * j:256 * (j + 1), :]
                x = jnp.concatenate([x, x], axis=1)
                sbd_ref[4 * nb + j] = jnp.concatenate([x, x], axis=1) * bd_mask

    chains = [(nb, j, slice(256 * j, 256 * (j + 1))) for nb in range(NBLK) for j in range(4)]
    for c in range(TLB // CT):
        rows = slice(c * CT, (c + 1) * CT)
        sbs = [sbd_ref[4 * nb + j] for nb, j, cs in chains]
        outs = [_dot_nt(jnp.concatenate([ah_ref[nb, rows, cs], rh_ref[nb, rows, cs]], axis=0).astype(BF16),
                        sb.astype(BF16))
                for (nb, j, cs), sb in zip(chains, sbs)]
        upds = [_dot_tn(jnp.concatenate([o[:CT] + ul_ref[nb, rows, cs], v_ref[nb, rows, cs]],
                                        axis=0).astype(BF16),
                        jnp.concatenate([bh_ref[nb, rows, cs], kh_ref[nb, rows, cs]], axis=0).astype(BF16))
                for (nb, j, cs), o in zip(chains, outs)]
        for (nb, j, cs), sb, o, upd in zip(chains, sbs, outs, upds):
            yb_ref[nb, rows, cs] = o[CT:] + yl_ref[nb, rows, cs]
            sbd_ref[4 * nb + j] = sb * dc_ref[nb, c * CT:c * CT + 1, cs] + upd * bd_mask

    @pl.when(t == T - 1)
    def _():
        for nb in range(NBLK):
            outs = []
            for j in range(4):
                sb = sbd_ref[4 * nb + j]
                half = sb[:, :128] + sb[:, 128:]
                outs.append(half[:, :64] + half[:, 64:])
            so_ref[nb] = jnp.concatenate(outs, axis=0)

    cat = lambda ref: jnp.concatenate([ref[nb] for nb in range(NBLK)], axis=0)
    y = cat(yb_ref)
    mu = _segsum(y, R_K) * (1.0 / R_K)
    yc = y - mu
    var = _segsum(yc * yc, R_K) * (1.0 / R_K)
    yn = yc * lax.rsqrt(var + R_LN_EPS) * lg_ref[...] + lb_ref[...] + cat(bo_ref)
    out = yn * _silu(cat(z_ref))
    for nb in range(NBLK):
        y_ref[nb] = out[nb * TLB:(nb + 1) * TLB, :]


def _rwkv_b_call(ra, proj1, ln_g, ln_b, s0, *, N, L, NBLK, TLB, CT):
    T = L // TLB
    blk = lambda cb: pl.BlockSpec((NBLK, TLB, MIX_W), lambda i, t: (i, t, cb))
    s_spec = pl.BlockSpec((NBLK, 1024, 64), lambda i, t: (i, 0, 0))
    ra3 = [a.reshape(N, L, MIX_W) for a in ra]
    y, s_new = pl.pallas_call(
        functools.partial(_rwkv_b_kernel, NBLK=NBLK, TLB=TLB, CT=CT, T=T),
        out_shape=(jax.ShapeDtypeStruct((N, L, MIX_W), F32),
                   jax.ShapeDtypeStruct((N, 1024, 64), F32)),
        grid=(N // NBLK, T),
        in_specs=([blk(0)] * N_RA_OUT
                  + [blk(P1_ZD // 1024),
                     pl.BlockSpec((1, MIX_W), lambda i, t: (0, 0)),
                     pl.BlockSpec((1, MIX_W), lambda i, t: (0, 0)),
                     s_spec]),
        out_specs=(blk(0), s_spec),
        scratch_shapes=[pltpu.VMEM((4 * NBLK, 256, 256), F32),
                        pltpu.VMEM((NBLK, TLB, MIX_W), F32)],
        compiler_params=_cp(("arbitrary", "arbitrary")),
    )(*(ra3 + [proj1.reshape(N, L, P1_N), ln_g, ln_b, s0]))
    return y.reshape(N * L, MIX_W), s_new


W1_TR = 512


def _regroup_w1_kernel(w_ref, g_ref, o_ref):
    j = pl.program_id(0)
    last = P1_N // W1_TR - 1

    @pl.when(j < last)
    def _():
        o_ref[...] = w_ref[...].astype(BF16)

    @pl.when(j == last)
    def _():
        tail = jnp.concatenate([w_ref[0:128, :], g_ref[...], jnp.zeros((W1_TR - 136, D_MODEL), F32)], axis=0)
        o_ref[...] = tail.astype(BF16)


def _regroup_w1(w1):
    wt = w1.T

    def src_row(j):
        return jnp.where(j < 8, W1_TR * j,
                         jnp.where(j < 16, 4104 + W1_TR * (j - 8),
                                   jnp.where(j < 18, 8328 + W1_TR * (j - 16), 8200)))

    return pl.pallas_call(
        _regroup_w1_kernel,
        out_shape=jax.ShapeDtypeStruct((P1_N, D_MODEL), BF16),
        grid=(P1_N // W1_TR,),
        in_specs=[pl.BlockSpec((pl.Element(W1_TR), pl.Element(D_MODEL)),
                               lambda j: (pl.multiple_of(src_row(j), 8), 0)),
                  pl.BlockSpec((pl.Element(8), pl.Element(D_MODEL)), lambda j: (4096, 0))],
        out_specs=pl.BlockSpec((W1_TR, D_MODEL), lambda j: (j, 0)),
        compiler_params=_cp(("parallel",)),
    )(wt, wt)


def _s5_weights(lam_re, lam_im, log_dt, b_re, b_im, c_re, c_im):
    dt = jnp.exp(log_dt)[:, None]
    mag = jnp.exp(lam_re * dt)
    ar = mag * jnp.cos(lam_im * dt)
    ai = mag * jnp.sin(lam_im * dt)
    den = lam_re * lam_re + lam_im * lam_im
    qr = ((ar - 1.0) * lam_re + ai * lam_im) / den
    qi = (ai * lam_re - (ar - 1.0) * lam_im) / den
    bbr = qr[..., None] * b_re - qi[..., None] * b_im
    bbi = qr[..., None] * b_im + qi[..., None] * b_re
    hp = lax.Precision.HIGHEST
    spread_p = (jnp.arange(1024)[None, :] % S5_STATE == jnp.arange(S5_STATE)[:, None]).astype(F32)
    spread_h = (jnp.arange(256)[None, :] % S5_GROUP == jnp.arange(S5_GROUP)[:, None]).astype(F32)
    grp_in = (jnp.arange(256)[:, None] // S5_GROUP == jnp.arange(1024)[None, :] // S5_STATE)
    grp_out = (jnp.arange(1024)[:, None] // S5_STATE == jnp.arange(256)[None, :] // S5_GROUP)

    def in_blocks(bb):
        rows = bb.transpose(0, 2, 1).reshape(4, 256, S5_STATE)
        return jnp.matmul(rows, spread_p, precision=hp) * grp_in

    def out_blocks(cc):
        rows = cc.transpose(0, 2, 1).reshape(4, 1024, S5_GROUP)
        return jnp.matmul(rows, spread_h, precision=hp) * grp_out

    wb = jnp.concatenate([in_blocks(bbr), in_blocks(bbi)], axis=2).astype(BF16)
    wc = jnp.concatenate([out_blocks(c_re), out_blocks(-c_im)], axis=1).astype(BF16)
    pr, pi = jnp.ones_like(ar), jnp.zeros_like(ai)
    lag = []
    for _ in range(8):
        cpr = c_re * pr[:, None, :] - c_im * pi[:, None, :]
        cpi = c_re * pi[:, None, :] + c_im * pr[:, None, :]
        lag.append(jnp.einsum('gop,gph->goh', cpr, bbr, precision=lax.Precision.HIGHEST)
                   - jnp.einsum('gop,gph->goh', cpi, bbi, precision=lax.Precision.HIGHEST))
        pr, pi = pr * ar - pi * ai, pr * ai + pi * ar
    kd = jnp.stack(lag).reshape(8, 4, 16, S5_GROUP, S5_GROUP)
    kd = kd.transpose(1, 0, 2, 4, 3).reshape(4, 8, 256, S5_GROUP)
    spread = (jnp.arange(256)[None, :] % S5_GROUP == jnp.arange(S5_GROUP)[:, None]).astype(F32)
    same_group = (jnp.arange(256)[:, None] // S5_GROUP == jnp.arange(256)[None, :] // S5_GROUP)
    wk = jnp.matmul(kd, spread, precision=lax.Precision.HIGHEST) * same_group
    wk = wk.reshape(4, 8 * 256, 256).astype(BF16)
    return ar.reshape(1, 4096), ai.reshape(1, 4096), wb, wk, wc


CFG = {
    "P": dict(N=BATCH, L=P_LEN, tm_mm=1376, tm_mm1=2752, tm=688, tm_ln=512, drop_meta=True,
              conv=dict(NB=1, TL=344), s5=dict(NB=1, TL=688), mlstm=dict(NB=1, TL=344),
              ra=dict(RB=192, CT=16, U=12), rb=dict(NBLK=4, TLB=48, CT=16)),
    "S": dict(N=DEC_BATCH, L=DEC_SEQ, tm_mm=1024, tm_mm1=1024, tm=512, tm_ln=256, drop_meta=False,
              conv=dict(NB=16, TL=8), s5=dict(NB=32, TL=8), mlstm=dict(NB=4, TL=8),
              ra=dict(RB=256, CT=8, U=8), rb=dict(NBLK=8, TLB=8, CT=8)),
}


def _trunk(x, st, w, cfg):
    n, l = cfg["N"], cfg["L"]
    proj0 = _matmul(x, w["w_in0"], cfg["tm_mm"], 1024)
    mix_a, conv_new = _conv_call(proj0, st["conv"], w["conv_w"], w["conv_b"], w["a_ln_g"], w["a_ln_b"], w["pw"],
                                 N=n, L=l, **cfg["conv"])
    yb, xr, xi = _s5_call(proj0, w["s5_wb"], w["s5_wk"], w["s5_wc"], w["s5_d"], w["s5_ar"], w["s5_ai"],
                          st["ssm_re"].reshape(n, 1, 4096), st["ssm_im"].reshape(n, 1, 4096),
                          N=n, L=l, **cfg["s5"])
    mix_b = _glu_gate(yb, w["glu_w"], w["glu_b"], proj0, tm=cfg["tm_mm"])
    x1, x1_bf16 = _out_ln(x, mix_a, mix_b, w["w_out0"], w["ln_g0"], w["ln_b0"], tm=cfg["tm_ln"], with_bf16=True)

    proj1 = _matmul_nt(x1_bf16, w["w_in1"], cfg["tm_mm1"], 512)
    m0 = jnp.pad(jnp.broadcast_to(st["m"][:, :, None], (n, M_HEADS, 128)), ((0, 0), (0, 4), (0, 0)))
    mix_c, c_new, n_new, m_new = _mlstm_call(proj1, w["gate_b"], w["hn_g"], st["c"], st["n"], m0,
                                             N=n, L=l, **cfg["mlstm"])
    sh = st["shift"]
    ra, shift_new = _rwkv_a_call(proj1, sh[:, :3072], sh[:, 3072:], w["rwkv"], N=n, L=l, **cfg["ra"])
    mix_d, s_new = _rwkv_b_call(ra, proj1, w["r_ln_g"], w["r_ln_b"], st["s"].reshape(n, 1024, 64),
                                N=n, L=l, **cfg["rb"])
    final_ln = _out_ln_prompt if cfg["drop_meta"] else functools.partial(_out_ln, tm=cfg["tm_ln"])
    y = final_ln(x1, mix_c, mix_d, w["w_out1"], w["ln_g1"], w["ln_b1"])

    states = (conv_new[None],
              xr.reshape(n, S5_GROUPS, S5_STATE)[None],
              xi.reshape(n, S5_GROUPS, S5_STATE)[None],
              c_new[None], n_new[None], m_new[:, :M_HEADS, 0][None],
              s_new.reshape(n, R_HEADS, R_K, R_K)[None],
              shift_new[None])
    return y, states


def kernel(x_prompt, x_sample, state_conv, state_ssm_re, state_ssm_im, state_mlstm_c, state_mlstm_n, state_mlstm_m, state_rwkv_s, state_rwkv_shift, meta_tokens, ev_w_in, a_conv_w, a_conv_b, a_ln_g, a_ln_b, a_pw, s5_lambda_re, s5_lambda_im, s5_log_dt, s5_b_re, s5_b_im, s5_c_re, s5_c_im, s5_d, s5_glu_w, s5_glu_b, ev_w_out, ev_ln_g, ev_ln_b, od_w_in, m_ig_b, m_fg_b, m_hn_g, r_mu, r_w0, r_w2, r_a0, r_a2, r_kk, r_ka, r_rk, r_ln_g, r_ln_b, od_w_out, od_ln_g, od_ln_b):
    nb = x_prompt.shape[0]
    row = lambda vec: vec.reshape(1, -1)
    zeros = lambda *s: jnp.zeros(s, F32)

    ar, ai, wb, wk, wc = _s5_weights(s5_lambda_re[0], s5_lambda_im[0], s5_log_dt[0], s5_b_re[0], s5_b_im[0],
                                     s5_c_re[0], s5_c_im[0])
    w_in1 = _regroup_w1(od_w_in[0])
    mu = r_mu[0]
    w = dict(
        w_in0=ev_w_in[0].astype(BF16), conv_w=a_conv_w[0], conv_b=row(a_conv_b[0]),
        a_ln_g=row(a_ln_g[0]), a_ln_b=row(a_ln_b[0]), pw=a_pw[0].astype(BF16),
        s5_wb=wb, s5_wk=wk, s5_wc=wc, s5_d=row(s5_d[0]), s5_ar=ar, s5_ai=ai,
        glu_w=s5_glu_w[0].astype(BF16), glu_b=row(s5_glu_b[0]),
        w_out0=ev_w_out[0].astype(BF16), ln_g0=row(ev_ln_g[0]), ln_b0=row(ev_ln_b[0]),
        w_in1=w_in1,
        gate_b=jnp.concatenate([m_ig_b[0], m_fg_b[0], jnp.zeros((120,), F32)]).reshape(1, 128),
        hn_g=row(m_hn_g[0]),
        rwkv=[row(mu[:3072]), row(mu[3072:]), row(r_w0[0]),
              jnp.concatenate([r_w2[0], jnp.zeros((64, MIX_W), F32)], axis=0).astype(BF16),
              row(r_a0[0]),
              jnp.concatenate([jnp.zeros((64, MIX_W), F32), r_a2[0]], axis=0).astype(BF16),
              row(r_kk[0]), row(r_ka[0]), row(r_rk[0])],
        r_ln_g=row(r_ln_g[0]), r_ln_b=row(r_ln_b[0]),
        w_out1=od_w_out[0].astype(BF16), ln_g1=row(od_ln_g[0]), ln_b1=row(od_ln_b[0]),
    )

    x_p = jnp.concatenate([jnp.broadcast_to(meta_tokens[None], (nb, N_META, D_MODEL)), x_prompt],
                          axis=1).reshape(nb * P_LEN, D_MODEL)
    st_p = dict(conv=zeros(nb, CONV_W - 1, MIX_W), ssm_re=zeros(nb, 4096), ssm_im=zeros(nb, 4096),
                c=zeros(nb, M_HEADS, M_D, M_D), n=zeros(nb, M_HEADS, M_D), m=zeros(nb, M_HEADS),
                s=zeros(nb, R_HEADS, R_K, R_K), shift=zeros(nb, 3200))
    y_p, states_p = _trunk(x_p, st_p, w, CFG["P"])

    st_s = dict(conv=state_conv[0], ssm_re=state_ssm_re[0], ssm_im=state_ssm_im[0],
                c=state_mlstm_c[0], n=state_mlstm_n[0], m=state_mlstm_m[0],
                s=state_rwkv_s[0], shift=state_rwkv_shift[0])
    y_s, states_s = _trunk(x_sample.reshape(DEC_BATCH * DEC_SEQ, D_MODEL), st_s, w, CFG["S"])

    y_prompt = y_p
    y_sample = y_s.reshape(DEC_BATCH, DEC_SEQ, D_MODEL)
    return (y_prompt, y_sample) + states_p + states_s
```

```python
import functools
import math

import jax
import jax.numpy as jnp
from jax import lax
from jax.experimental import pallas as pl
from jax.experimental.pallas import tpu as pltpu

F32 = jnp.float32
BF16 = jnp.bfloat16

D_MODEL = 2048
MIX_W = 1024
N_META = 16
CONV_W = 31
S5_GROUP = 16
S5_GROUPS = 64
S5_STATE = 64
M_HEADS = 4
M_D = 256
R_HEADS = 16
R_K = 64
LN_EPS = 1e-5
R_LN_EPS = 64e-5
DEPTH = 2
ALPHA = (2 * DEPTH) ** 0.25

BATCH = 4
SEQ = 2048
P_LEN = N_META + SEQ
DEC_BATCH = 128
DEC_SEQ = 8

P1_Q, P1_K, P1_V, P1_O, P1_ZC = 0, 1024, 2048, 3072, 4096
P1_R, P1_RK, P1_RV, P1_ZD, P1_WA, P1_GATE = 5120, 6144, 7168, 8192, 9216, 9344
P1_N = 9728

VMEM_LIMIT = 48 * 1024 * 1024


def _cp(sem):
    return pltpu.CompilerParams(dimension_semantics=sem, vmem_limit_bytes=VMEM_LIMIT)


def _dot(a, b):
    return jnp.dot(a, b, preferred_element_type=F32)


def _dot_nt(a, b):
    return lax.dot_general(a, b, (((1,), (1,)), ((), ())), preferred_element_type=F32)


def _dot_tn(a, b):
    return lax.dot_general(a, b, (((0,), (0,)), ((), ())), preferred_element_type=F32)


def _split2(x):
    hi = x.astype(BF16)
    lo = (x - hi.astype(F32)).astype(BF16)
    return hi, lo


def _split3(x):
    hi = x.astype(BF16)
    r1 = x - hi.astype(F32)
    mid = r1.astype(BF16)
    lo = (r1 - mid.astype(F32)).astype(BF16)
    return hi, mid, lo


def _sigmoid(x):
    return jax.nn.sigmoid(x)


def _silu(x):
    return x * jax.nn.sigmoid(x)


def _softplus(x):
    return jnp.maximum(x, 0.0) + jnp.log(1.0 + jnp.exp(-jnp.abs(x)))


def _gelu_tanh(x):
    c = math.sqrt(2.0 / math.pi)
    return x * (0.5 * (1.0 + jnp.tanh(c * (x + 0.044715 * (x * x * x)))))


def _iota(shape, axis):
    return lax.broadcasted_iota(jnp.int32, shape, axis)


def _shr(x, k):
    return lax.shift_right_logical(x, jnp.int32(k))


def _log2(n):
    k = int(round(math.log2(n)))
    assert 1 << k == n
    return k


def _block_ones(n, seg, dtype):
    r = _shr(_iota((n, n), 0), _log2(seg))
    c = _shr(_iota((n, n), 1), _log2(seg))
    return (r == c).astype(dtype)


def _segsum(x, seg):
    g = _block_ones(256, seg, BF16)
    outs = []
    for j in range(x.shape[1] // 256):
        hi, lo = _split2(x[:, 256 * j:256 * (j + 1)])
        outs.append(_dot(hi, g) + _dot(lo, g))
    return jnp.concatenate(outs, axis=1)


def _row_cumsum(x, period):
    rows = x.shape[0]
    rid = _iota(x.shape, 0)
    if period < rows:
        rid = jnp.bitwise_and(rid, period - 1)
    d = 1
    while d < min(period, rows):
        x = x + jnp.where(rid >= d, pltpu.roll(x, d, 0), 0.0)
        d *= 2
    return x


def _row_cummax(x, period):
    rows = x.shape[0]
    rid = _iota(x.shape, 0)
    if period < rows:
        rid = jnp.bitwise_and(rid, period - 1)
    d = 1
    while d < min(period, rows):
        x = jnp.maximum(x, jnp.where(rid >= d, pltpu.roll(x, d, 0), -jnp.inf))
        d *= 2
    return x


def _mm_kernel(x_ref, w_ref, o_ref):
    o_ref[...] = _dot(x_ref[...].astype(BF16), w_ref[...])


def _matmul(x, w, tm, tn):
    r, k = x.shape
    n = w.shape[1]
    return pl.pallas_call(
        _mm_kernel,
        out_shape=jax.ShapeDtypeStruct((r, n), F32),
        grid=(pl.cdiv(r, tm), n // tn),
        in_specs=[pl.BlockSpec((tm, k), lambda i, j: (i, 0)),
                  pl.BlockSpec((k, tn), lambda i, j: (0, j))],
        out_specs=pl.BlockSpec((tm, tn), lambda i, j: (i, j)),
        compiler_params=_cp(("parallel", "arbitrary")),
    )(x, w)


def _mm_nt_kernel(x_ref, w_ref, o_ref):
    o_ref[...] = _dot_nt(x_ref[...].astype(BF16), w_ref[...])


def _matmul_nt(x, w_t, tm, tn):
    r, k = x.shape
    n = w_t.shape[0]
    return pl.pallas_call(
        _mm_nt_kernel,
        out_shape=jax.ShapeDtypeStruct((r, n), F32),
        grid=(pl.cdiv(r, tm), n // tn),
        in_specs=[pl.BlockSpec((tm, k), lambda i, j: (i, 0)),
                  pl.BlockSpec((tn, k), lambda i, j: (j, 0))],
        out_specs=pl.BlockSpec((tm, tn), lambda i, j: (i, j)),
        compiler_params=_cp(("parallel", "arbitrary")),
    )(x, w_t)


def _glu_kernel(y_ref, wv_ref, wg_ref, bv_ref, bg_ref, z_ref, o_ref):
    y = y_ref[...].astype(BF16)
    v = _dot(y, wv_ref[...]) + bv_ref[...]
    g = _dot(y, wg_ref[...]) + bg_ref[...]
    o_ref[...] = (v * _sigmoid(g) * _silu(z_ref[...])).astype(o_ref.dtype)


def _glu_gate(yb, glu_w, glu_b, proj0, tm, tn=512):
    r = yb.shape[0]
    nb = MIX_W // tn
    zb = 4096 // tn
    return pl.pallas_call(
        _glu_kernel,
        out_shape=jax.ShapeDtypeStruct((r, MIX_W), BF16),
        grid=(pl.cdiv(r, tm), nb),
        in_specs=[pl.BlockSpec((tm, MIX_W), lambda i, j: (i, 0)),
                  pl.BlockSpec((MIX_W, tn), lambda i, j: (0, j)),
                  pl.BlockSpec((MIX_W, tn), lambda i, j: (0, nb + j)),
                  pl.BlockSpec((1, tn), lambda i, j: (0, j)),
                  pl.BlockSpec((1, tn), lambda i, j: (0, nb + j)),
                  pl.BlockSpec((tm, tn), lambda i, j: (i, zb + j))],
        out_specs=pl.BlockSpec((tm, tn), lambda i, j: (i, j)),
        compiler_params=_cp(("parallel", "arbitrary")),
    )(yb, glu_w, glu_w, glu_b, glu_b, proj0)


def _out_ln_kernel(x_ref, ma_ref, mb_ref, wa_ref, wb_ref, g_ref, b_ref, o_ref):
    out = _dot(ma_ref[...].astype(BF16), wa_ref[...]) + _dot(mb_ref[...].astype(BF16), wb_ref[...])
    y = ALPHA * x_ref[...] + out
    mu = jnp.mean(y, axis=-1, keepdims=True)
    yc = y - mu
    var = jnp.mean(yc * yc, axis=-1, keepdims=True)
    o_ref[...] = yc * lax.rsqrt(var + LN_EPS) * g_ref[...] + b_ref[...]


def _out_ln_bf16_kernel(x_ref, ma_ref, mb_ref, wa_ref, wb_ref, g_ref, b_ref, o_ref, ob_ref):
    _out_ln_kernel(x_ref, ma_ref, mb_ref, wa_ref, wb_ref, g_ref, b_ref, o_ref)
    ob_ref[...] = o_ref[...].astype(BF16)


def _out_ln(x, mix_a, mix_b, w_out, ln_g, ln_b, tm, with_bf16=False):
    r = x.shape[0]
    o_spec = pl.BlockSpec((tm, D_MODEL), lambda i: (i, 0))
    return pl.pallas_call(
        _out_ln_bf16_kernel if with_bf16 else _out_ln_kernel,
        out_shape=((jax.ShapeDtypeStruct((r, D_MODEL), F32), jax.ShapeDtypeStruct((r, D_MODEL), BF16))
                   if with_bf16 else jax.ShapeDtypeStruct((r, D_MODEL), F32)),
        grid=(pl.cdiv(r, tm),),
        in_specs=[pl.BlockSpec((tm, D_MODEL), lambda i: (i, 0)),
                  pl.BlockSpec((tm, MIX_W), lambda i: (i, 0)),
                  pl.BlockSpec((tm, MIX_W), lambda i: (i, 0)),
                  pl.BlockSpec((MIX_W, D_MODEL), lambda i: (0, 0), pipeline_mode=pl.Buffered(1)),
                  pl.BlockSpec((MIX_W, D_MODEL), lambda i: (1, 0), pipeline_mode=pl.Buffered(1)),
                  pl.BlockSpec((1, D_MODEL), lambda i: (0, 0)),
                  pl.BlockSpec((1, D_MODEL), lambda i: (0, 0))],
        out_specs=(o_spec, o_spec) if with_bf16 else o_spec,
        compiler_params=_cp(("parallel",)),
    )(x, mix_a, mix_b, w_out, w_out, ln_g, ln_b)


def _out_ln_prompt(x, mix_a, mix_b, w_out, ln_g, ln_b, tm=512):
    tiles = SEQ // tm

    def rows(width):
        return pl.BlockSpec((pl.Element(tm), pl.Element(width)),
                            lambda n, t: (pl.multiple_of(n * P_LEN + N_META + t * tm, 8), 0))

    return pl.pallas_call(
        _out_ln_kernel,
        out_shape=jax.ShapeDtypeStruct((BATCH * SEQ, D_MODEL), F32),
        grid=(BATCH, tiles),
        in_specs=[rows(D_MODEL), rows(MIX_W), rows(MIX_W),
                  pl.BlockSpec((MIX_W, D_MODEL), lambda n, t: (0, 0), pipeline_mode=pl.Buffered(1)),
                  pl.BlockSpec((MIX_W, D_MODEL), lambda n, t: (1, 0), pipeline_mode=pl.Buffered(1)),
                  pl.BlockSpec((1, D_MODEL), lambda n, t: (0, 0)),
                  pl.BlockSpec((1, D_MODEL), lambda n, t: (0, 0))],
        out_specs=pl.BlockSpec((tm, D_MODEL), lambda n, t: (n * tiles + t, 0)),
        compiler_params=_cp(("parallel", "arbitrary")),
    )(x, mix_a, mix_b, w_out, w_out, ln_g, ln_b).reshape(BATCH, SEQ, D_MODEL)


def _conv_kernel(u_ref, g_ref, z_ref, st_ref, w_ref, cb_ref, lg_ref, lb_ref, pw_ref, act_ref, nst_ref,
                 hp_ref, hs_ref, wb_ref, *, NB, TL, T):
    t = pl.program_id(1)
    for j in range(CONV_W):
        wb_ref[j] = jnp.broadcast_to(w_ref[j:j + 1, :], (8, MIX_W))
    for nb in range(NB):
        base = nb * TL

        @pl.when(t == 0)
        def _():
            hp_ref[nb, 0:2, :] = jnp.zeros((2, MIX_W), F32)
            hp_ref[nb, 2:32, :] = st_ref[nb]

        hp_ref[nb, TL + 32:TL + 40, :] = jnp.zeros((8, MIX_W), F32)
        hp_ref[nb, 32:32 + TL, :] = u_ref[base:base + TL, :] * _sigmoid(g_ref[base:base + TL, :])
        for b in range(8):
            hs_ref[b] = hp_ref[nb, b:b + TL + 32, :]

        def taps(r0, groups):
            acc = [None] * groups
            for j in range(CONV_W):
                o = j + 2
                wj = wb_ref[j]
                for g in range(groups):
                    term = wj * hs_ref[o % 8, pl.ds(r0 + 8 * (o // 8 + g), 8), :]
                    acc[g] = term if acc[g] is None else acc[g] + term
            for g in range(groups):
                act_ref[pl.ds(base + r0 + 8 * g, 8), :] = acc[g]

        def chunk(c, carry):
            taps(pl.multiple_of(c * 16, 16), 2)
            return carry

        lax.fori_loop(0, TL // 16, chunk, 0)
        if TL % 16:
            taps(TL - 8, 1)

        @pl.when(t == T - 1)
        def _():
            nst_ref[nb] = hp_ref[nb, TL + 2:TL + 32, :]

        if T > 1:
            hp_ref[nb, 0:32, :] = hp_ref[nb, TL:TL + 32, :]

    y = act_ref[...] + cb_ref[...]
    mu = jnp.mean(y, axis=-1, keepdims=True)
    yc = y - mu
    var = jnp.mean(yc * yc, axis=-1, keepdims=True)
    act = _silu(yc * lax.rsqrt(var + LN_EPS) * lg_ref[...] + lb_ref[...])
    act_ref[...] = _dot(act.astype(BF16), pw_ref[...]) * _silu(z_ref[...])


def _conv_call(proj0, state, conv_w, conv_b, ln_g, ln_b, pw, *, N, L, NB, TL):
    T = L // TL
    RB = NB * TL
    assert NB == 1 or T == 1
    return pl.pallas_call(
        functools.partial(_conv_kernel, NB=NB, TL=TL, T=T),
        out_shape=(jax.ShapeDtypeStruct((N * L, MIX_W), F32),
                   jax.ShapeDtypeStruct((N, CONV_W - 1, MIX_W), F32)),
        grid=(N // NB, T),
        in_specs=[pl.BlockSpec((RB, MIX_W), lambda i, t: (i * T + t, 0)),
                  pl.BlockSpec((RB, MIX_W), lambda i, t: (i * T + t, 1)),
                  pl.BlockSpec((RB, MIX_W), lambda i, t: (i * T + t, 2)),
                  pl.BlockSpec((NB, CONV_W - 1, MIX_W), lambda i, t: (i, 0, 0)),
                  pl.BlockSpec((CONV_W, MIX_W), lambda i, t: (0, 0)),
                  pl.BlockSpec((1, MIX_W), lambda i, t: (0, 0)),
                  pl.BlockSpec((1, MIX_W), lambda i, t: (0, 0)),
                  pl.BlockSpec((1, MIX_W), lambda i, t: (0, 0)),
                  pl.BlockSpec((MIX_W, MIX_W), lambda i, t: (0, 0), pipeline_mode=pl.Buffered(1))],
        out_specs=(pl.BlockSpec((RB, MIX_W), lambda i, t: (i * T + t, 0)),
                   pl.BlockSpec((NB, CONV_W - 1, MIX_W), lambda i, t: (i, 0, 0))),
        scratch_shapes=[pltpu.VMEM((NB, TL + 40, MIX_W), F32),
                        pltpu.VMEM((8, TL + 32, MIX_W), F32),
                        pltpu.VMEM((CONV_W, 8, MIX_W), F32)],
        compiler_params=_cp(("arbitrary", "arbitrary")),
    )(proj0, proj0, proj0, state, conv_w, conv_b, ln_g, ln_b, pw)


def _s5_kernel(u_ref, wb_ref, wk_ref, wc_ref, d_ref, ar_ref, ai_ref, x0r_ref, x0i_ref,
               y_ref, xfr_ref, xfi_ref, xs_ref, cr_ref, ci_ref, *, NB, TL, T):
    t = pl.program_id(2)
    RB = NB * TL
    GL = TL // 8
    u = u_ref[...]
    ub = u.astype(BF16)
    big = _dot(ub, wb_ref[0])
    xs_ref[0] = big[:, :1024]
    xs_ref[1] = big[:, 1024:]
    rid = jnp.bitwise_and(_iota((RB, 256), 0), 7)
    lags = [ub] + [jnp.where(rid >= d, pltpu.roll(u, d, 0), 0.0).astype(BF16) for d in range(1, 8)]
    y_loc = _dot(jnp.concatenate(lags, axis=1), wk_ref[0])
    ar = ar_ref[...]
    ai = ai_ref[...]

    def cmul(pr, pi, qr, qi):
        return pr * qr - pi * qi, pr * qi + pi * qr

    a1 = (ar, ai)
    a2 = cmul(*a1, *a1)
    a4 = cmul(*a2, *a2)
    a3 = cmul(*a2, *a1)
    a5 = cmul(*a4, *a1)
    a6 = cmul(*a4, *a2)
    a7 = cmul(*a6, *a1)
    a8 = cmul(*a4, *a4)
    a0 = (jnp.ones_like(ar), jnp.zeros_like(ai))
    r8 = _iota((8, 1024), 0)

    def table(powers):
        tr = jnp.zeros((8, 1024), F32)
        ti = jnp.zeros((8, 1024), F32)
        for k, (pr, pi) in enumerate(powers):
            tr = jnp.where(r8 == k, pr, tr)
            ti = jnp.where(r8 == k, pi, ti)
        return tr, ti

    pwr, pwi = table((a1, a2, a3, a4, a5, a6, a7, a8))
    qwr, qwi = table((a7, a6, a5, a4, a3, a2, a1, a0))
    a8r, a8i = a8

    first = t == 0

    def seq_body(nb, carry0):
        x0r = x0r_ref[nb]
        x0i = x0i_ref[nb]
        if T > 1:
            c_r = jnp.where(first, x0r, cr_ref[0:1, :])
            c_i = jnp.where(first, x0i, ci_ref[0:1, :])
        else:
            c_r, c_i = x0r, x0i

        def grp(g, c):
            c_r, c_i = c
            off = pl.multiple_of(nb * TL + g * 8, 8)
            vr = xs_ref[0, pl.ds(off, 8), :]
            vi = xs_ref[1, pl.ds(off, 8), :]
            er = jnp.sum(qwr * vr - qwi * vi, axis=0, keepdims=True)
            ei = jnp.sum(qwr * vi + qwi * vr, axis=0, keepdims=True)
            br = jnp.broadcast_to(c_r, (8, 1024))
            bi = jnp.broadcast_to(c_i, (8, 1024))
            xs_ref[0, pl.ds(off, 8), :] = pwr * br - pwi * bi
            xs_ref[1, pl.ds(off, 8), :] = pwr * bi + pwi * br
            return a8r * c_r - a8i * c_i + er, a8r * c_i + a8i * c_r + ei

        c_r, c_i = lax.fori_loop(0, GL, grp, (c_r, c_i))
        if T > 1:
            cr_ref[...] = jnp.broadcast_to(c_r, (8, 1024))
            ci_ref[...] = jnp.broadcast_to(c_i, (8, 1024))

        @pl.when(t == T - 1)
        def _():
            xfr_ref[nb] = c_r
            xfi_ref[nb] = c_i

        return carry0

    lax.fori_loop(0, NB, seq_body, 0)
    y = (_dot(xs_ref[0].astype(BF16), wc_ref[0, 0:1024, :])
         + _dot(xs_ref[1].astype(BF16), wc_ref[0, 1024:2048, :]))
    y_ref[...] = _gelu_tanh(y + y_loc + d_ref[...] * u)


def _s5_call(proj0, wb, wk, wc, dvec, ar, ai, x0r, x0i, *, N, L, NB, TL):
    T = L // TL
    RB = NB * TL
    assert NB == 1 or T == 1
    ub = 3072 // 256
    st = jax.ShapeDtypeStruct((N, 1, 4096), F32)
    return pl.pallas_call(
        functools.partial(_s5_kernel, NB=NB, TL=TL, T=T),
        out_shape=(jax.ShapeDtypeStruct((N * L, MIX_W), F32), st, st),
        grid=(N // NB, 4, T),
        in_specs=[pl.BlockSpec((RB, 256), lambda i, j, t: (i * T + t, ub + j)),
                  pl.BlockSpec((1, 256, 2048), lambda i, j, t: (j, 0, 0)),
                  pl.BlockSpec((1, 2048, 256), lambda i, j, t: (j, 0, 0)),
                  pl.BlockSpec((1, 2048, 256), lambda i, j, t: (j, 0, 0)),
                  pl.BlockSpec((1, 256), lambda i, j, t: (0, j)),
                  pl.BlockSpec((1, 1024), lambda i, j, t: (0, j)),
                  pl.BlockSpec((1, 1024), lambda i, j, t: (0, j)),
                  pl.BlockSpec((NB, 1, 1024), lambda i, j, t: (i, 0, j)),
                  pl.BlockSpec((NB, 1, 1024), lambda i, j, t: (i, 0, j))],
        out_specs=(pl.BlockSpec((RB, 256), lambda i, j, t: (i * T + t, j)),
                   pl.BlockSpec((NB, 1, 1024), lambda i, j, t: (i, 0, j)),
                   pl.BlockSpec((NB, 1, 1024), lambda i, j, t: (i, 0, j))),
        scratch_shapes=[pltpu.VMEM((2, RB, 1024), F32),
                        pltpu.VMEM((8, 1024), F32),
                        pltpu.VMEM((8, 1024), F32)],
        compiler_params=_cp(("arbitrary", "arbitrary", "arbitrary")),
    )(proj0, wb, wk, wc, dvec, ar, ai, x0r, x0i)


def _mlstm_kernel(q_ref, k_ref, v_ref, o_ref, z_ref, gt_ref, gb_ref, hg_ref, c0_ref, n0_ref, m0_ref,
                  y_ref, c_ref, n_ref, m_ref, cs_ref, ns_ref, ms_ref, *, NB, TL, T):
    for nb in range(NB):
        rows = lambda ref: ref.at[pl.ds(nb * TL, TL)]
        one = lambda ref: ref.at[pl.ds(nb, 1)]
        _mlstm_seq(rows(q_ref), rows(k_ref), rows(v_ref), rows(o_ref), rows(z_ref), rows(gt_ref), gb_ref, hg_ref,
                   one(c0_ref), one(n0_ref), one(m0_ref), rows(y_ref), one(c_ref), one(n_ref), one(m_ref),
                   cs_ref.at[nb], ns_ref.at[nb], ms_ref.at[nb], TL=TL, T=T)


def _mlstm_seq(q_ref, k_ref, v_ref, o_ref, z_ref, gt_ref, gb_ref, hg_ref, c0_ref, n0_ref, m0_ref,
               y_ref, c_ref, n_ref, m_ref, cs_ref, ns_ref, ms_ref, *, TL, T):
    t = pl.program_id(1)

    @pl.when(t == 0)
    def _():
        cs_ref[...] = c0_ref[0]
        ns_ref[...] = n0_ref[0]
        ms_ref[...] = m0_ref[0]

    G = gt_ref[...] + gb_ref[...]
    B = _row_cumsum(-_softplus(-G), TL)
    Bs = pltpu.roll(B, 124, 1)
    A = G - Bs
    CM = _row_cummax(A, TL)
    ms = ms_ref[...]
    dg = _iota((8, 128), 0) == _iota((8, 128), 1)
    mrow = jnp.sum(jnp.where(dg, ms, 0.0), axis=0, keepdims=True)
    M = jnp.maximum(mrow, CM)
    MT = Bs + M
    sel = dg.astype(BF16)
    a_hi, a_mid, a_lo = _split3(A)
    Arow = _dot_nt(sel, a_hi) + _dot_nt(sel, a_mid) + _dot_nt(sel, a_lo)
    causal = _iota((TL, TL), 0) >= _iota((TL, TL), 1)
    H = range(M_HEADS)
    sl = [slice(M_D * h, M_D * (h + 1)) for h in H]
    q = [q_ref[:, sl[h]] * (M_D ** -0.5) for h in H]
    qb = [x.astype(BF16) for x in q]
    kf = [k_ref[:, sl[h]] for h in H]
    kb = [x.astype(BF16) for x in kf]
    vf = [v_ref[:, sl[h]] for h in H]
    c_old = [cs_ref[h] for h in H]
    n_old = [ns_ref[h:h + 1, :] for h in H]
    m_col = [M[:, h:h + 1] for h in H]
    mt_col = [MT[:, h:h + 1] for h in H]
    b_col = [Bs[:, h:h + 1] for h in H]
    m_prev = [mrow[:, h:h + 1] for h in H]
    dm = [jnp.exp(jnp.where(causal, Arow[h:h + 1, :] - m_col[h], -jnp.inf)) for h in H]
    s = [_dot_nt(qb[h], kb[h]) * dm[h] for h in H]
    inter = [jnp.exp(m_prev[h] - m_col[h]) for h in H]
    h_intra = [_dot(s[h].astype(BF16), vf[h].astype(BF16)) for h in H]
    h_inter = [_dot(qb[h], c_old[h].astype(BF16)) * inter[h] for h in H]
    n_all = [jnp.sum(s[h], axis=1, keepdims=True) + jnp.sum(q[h] * n_old[h], axis=1, keepdims=True) * inter[h]
             for h in H]
    hh = [(h_intra[h] + h_inter[h]) / jnp.maximum(jnp.abs(n_all[h]), jnp.exp(-mt_col[h])) for h in H]
    m_new = [mt_col[h][TL - 1:TL, :] for h in H]
    b_end = [b_col[h][TL - 1:TL, :] for h in H]
    dec = [jnp.exp(m_prev[h] + b_end[h] - m_new[h]) for h in H]
    w_s = [jnp.exp(b_end[h] - b_col[h] + G[:, h:h + 1] - m_new[h]) for h in H]
    c_new = [c_old[h] * dec[h] + _dot_tn(kb[h], (vf[h] * w_s[h]).astype(BF16)) for h in H]
    n_new = [n_old[h] * dec[h] + jnp.sum(kf[h] * w_s[h], axis=0, keepdims=True) for h in H]
    outs = []
    for h in H:
        mu = jnp.mean(hh[h], axis=-1, keepdims=True)
        hc = hh[h] - mu
        var = jnp.mean(hc * hc, axis=-1, keepdims=True)
        hn = hc * lax.rsqrt(var + LN_EPS) * hg_ref[:, sl[h]]
        outs.append(hn * _sigmoid(o_ref[:, sl[h]]) * _silu(z_ref[:, sl[h]]))
    for h in H:
        cs_ref[h] = c_new[h]
        ns_ref[h:h + 1, :] = n_new[h]
        ms_ref[h:h + 1, :] = jnp.broadcast_to(m_new[h], (1, 128))
        y_ref[:, sl[h]] = outs[h]

    @pl.when(t == T - 1)
    def _():
        c_ref[0] = cs_ref[...]
        n_ref[0] = ns_ref[...]
        m_ref[0] = ms_ref[...]


def _mlstm_call(proj1, gate_b, hn_g, c0, n0, m0, *, N, L, NB, TL):
    T = L // TL
    RB = NB * TL
    assert NB == 1 or T == 1

    def col(cb):
        return pl.BlockSpec((RB, MIX_W), lambda i, t: (i * T + t, cb))

    st_specs = (pl.BlockSpec((NB, M_HEADS, M_D, M_D), lambda i, t: (i, 0, 0, 0)),
                pl.BlockSpec((NB, M_HEADS, M_D), lambda i, t: (i, 0, 0)),
                pl.BlockSpec((NB, 8, 128), lambda i, t: (i, 0, 0)))
    return pl.pallas_call(
        functools.partial(_mlstm_kernel, NB=NB, TL=TL, T=T),
        out_shape=(jax.ShapeDtypeStruct((N * L, MIX_W), F32),
                   jax.ShapeDtypeStruct((N, M_HEADS, M_D, M_D), F32),
                   jax.ShapeDtypeStruct((N, M_HEADS, M_D), F32),
                   jax.ShapeDtypeStruct((N, 8, 128), F32)),
        grid=(N // NB, T),
        in_specs=[col(0), col(1), col(2), col(3), col(4),
                  pl.BlockSpec((RB, 128), lambda i, t: (i * T + t, P1_GATE // 128)),
                  pl.BlockSpec((1, 128), lambda i, t: (0, 0)),
                  pl.BlockSpec((1, MIX_W), lambda i, t: (0, 0))] + list(st_specs),
        out_specs=(pl.BlockSpec((RB, MIX_W), lambda i, t: (i * T + t, 0)),) + st_specs,
        scratch_shapes=[pltpu.VMEM((NB, M_HEADS, M_D, M_D), F32),
                        pltpu.VMEM((NB, M_HEADS, M_D), F32),
                        pltpu.VMEM((NB, 8, 128), F32)],
        compiler_params=_cp(("arbitrary", "arbitrary")),
    )(proj1, proj1, proj1, proj1, proj1, proj1, gate_b, hn_g, c0, n0, m0)


N_RA_OUT = 9


def _rwkv_a_kernel(pr_ref, pk_ref, pv_ref, pwa_ref, hr_ref, hk_ref, hv_ref, hwa_ref, st_ref, stwa_ref,
                   mu_ref, muwa_ref, w0_ref, w2_ref, a0_ref, a2_ref, kkp_ref, ka_ref, rk_ref,
                   ah_ref, rh_ref, bh_ref, kh_ref, vo_ref, ul_ref, yl_ref, dc_ref, bo_ref, shr_ref, shw_ref,
                   s_at, s_rt, s_bt, s_kt, s_v, s_cum, *, RB, CT, N, L, U):
    HS = R_HEADS * CT
    HG = 128 // CT
    NG = R_HEADS // HG
    GW = HG * R_K
    NCH = RB // CT
    short = L == CT
    rid = _iota((RB, 1), 0)
    grow = pl.program_id(0) * RB + rid

    if short:
        spread = (_shr(_iota((RB, RB // CT), 0), _log2(CT)) == _iota((RB, RB // CT), 1)).astype(BF16)

    def shifted(p_ref, h_ref, s_ref, lo, hi, mu):
        p = p_ref[...]
        prev = pltpu.roll(p, 1, 0)
        if short:
            hi3, mid3, lo3 = _split3(s_ref[:, lo:hi])
            first_prev = _dot(spread, hi3) + _dot(spread, mid3) + _dot(spread, lo3)
            prev = jnp.where(jnp.bitwise_and(rid, CT - 1) == 0, first_prev, prev)
        else:
            prev = jnp.where(rid == 0, h_ref[7:8, :], prev)
            for n in range(N):
                prev = jnp.where(grow == n * L, s_ref[n:n + 1, lo:hi], prev)
        return p + (prev - p) * mu

    raw = ((pr_ref, shr_ref, 0), (pk_ref, shr_ref, 1024), (pv_ref, shr_ref, 2048), (pwa_ref, shw_ref, 0))
    if short:
        nseq = RB // CT
        sel = (_iota((nseq, RB), 1) == _iota((nseq, RB), 0) * CT + (CT - 1)).astype(BF16)
        for src, dst, lo in raw:
            hi, mid, low = _split3(src[...])
            dst[:, lo:lo + src.shape[1]] = _dot(sel, hi) + _dot(sel, mid) + _dot(sel, low)
    else:
        for n in range(N):
            tile, off = divmod(n * L + L - 1, RB)

            @pl.when(pl.program_id(0) == tile)
            def _():
                for src, dst, lo in raw:
                    dst[n:n + 1, lo:lo + src.shape[1]] = src[off:off + 1, :]

    r = shifted(pr_ref, hr_ref, st_ref, 0, 1024, mu_ref[:, 0:1024])
    k = shifted(pk_ref, hk_ref, st_ref, 1024, 2048, mu_ref[:, 1024:2048])
    v = shifted(pv_ref, hv_ref, st_ref, 2048, 3072, mu_ref[:, 2048:3072])
    wa = shifted(pwa_ref, hwa_ref, stwa_ref, 0, 128, muwa_ref[...])
    w = -_softplus(-(w0_ref[...] + _dot(jnp.tanh(wa).astype(BF16), w2_ref[...]))) - 0.5
    wlog = -jnp.exp(w)
    a = _sigmoid(a0_ref[...] + _dot(wa.astype(BF16), a2_ref[...]))
    kk = k * kkp_ref[...]
    kk = kk / jnp.maximum(jnp.sqrt(_segsum(kk * kk, R_K)), 1e-12)
    kmod = k * (1.0 + (a - 1.0) * ka_ref[...])
    bo_ref[...] = _segsum(r * kmod * rk_ref[...], R_K) * v
    cum = _row_cumsum(wlog, CT)
    einv = jnp.exp(-cum)
    s_at[...] = (-kk) * jnp.exp(cum - wlog)
    s_rt[...] = r * jnp.exp(cum)
    s_bt[...] = kk * a * einv
    s_kt[...] = kmod * einv
    s_v[...] = v
    s_cum[...] = cum

    mdt = BF16 if CT % 16 == 0 else F32
    be_mask = (_shr(_iota((128, GW), 0), _log2(CT)) == _shr(_iota((128, GW), 1), _log2(R_K))).astype(mdt)
    bd_mask = (_shr(_iota((HS, HS), 0), _log2(CT)) == _shr(_iota((HS, HS), 1), _log2(CT))).astype(mdt)
    tt = _iota((CT, HS), 0)
    ss = jnp.bitwise_and(_iota((CT, HS), 1), CT - 1)
    strict = tt > ss
    incl = tt >= ss
    eye_c = (tt == ss).astype(F32)
    cat0 = lambda *xs: jnp.concatenate(xs, axis=0)

    def blockexp(x):
        return [(jnp.concatenate([x[:, GW * g:GW * (g + 1)].astype(mdt)] * HG, axis=0) * be_mask).astype(BF16)
                for g in range(NG)]

    def gram(lhs, be):
        lb = lhs.astype(BF16)
        return jnp.concatenate([_dot_nt(lb[:, GW * g:GW * (g + 1)], be[g]) for g in range(NG)], axis=1)

    def apply(cmp, be):
        cb = cmp.astype(BF16)
        return jnp.concatenate([_dot(cb[:, 128 * g:128 * (g + 1)], be[g]) for g in range(NG)], axis=1)

    def bdiag(x):
        return (jnp.concatenate([x.astype(mdt)] * R_HEADS, axis=0) * bd_mask).astype(BF16)

    def bdiag_hl(x):
        if mdt == BF16:
            return tuple(bdiag(part) for part in _split2(x))
        return _split2(jnp.concatenate([x] * R_HEADS, axis=0) * bd_mask)

    def mm_hl(stack, wh, wl):
        sh, sl = _split2(stack)
        n = stack.shape[0]
        full = _dot(cat0(sh, sl), wh)
        return full[:n] + full[n:] + _dot(sh, wl)

    def chunks(i, carry):
        rows = [pl.ds(pl.multiple_of((i * U + u) * CT, CT), CT) for u in range(U)]
        ld = lambda ref: [ref[rw, :] for rw in rows]
        at, rt, bt, kt, vv, cm = ld(s_at), ld(s_rt), ld(s_bt), ld(s_kt), ld(s_v), ld(s_cum)
        each = lambda f, *xs: [f(*a_) for a_ in zip(*xs)]
        ar_ = each(cat0, at, rt)
        gb = each(lambda l_, y_: gram(l_, blockexp(y_)), ar_, bt)
        gk = each(lambda l_, y_: gram(l_, blockexp(y_)), ar_, kt)
        a_ab = each(lambda m: jnp.where(strict, m[:CT], 0.0), gb)
        a_rb = each(lambda m: jnp.where(incl, m[CT:], 0.0), gb)
        a_ak = each(lambda m: jnp.where(strict, m[:CT], 0.0), gk)
        a_rk = each(lambda m: jnp.where(incl, m[CT:], 0.0), gk)
        p = each(lambda m: eye_c + m, a_ab)
        x = a_ab
        q = a_rb
        w_hl = each(bdiag_hl, x)
        res = each(lambda x_, q_, w_: mm_hl(cat0(x_, q_), *w_), x, q, w_hl)
        x = each(lambda r_: r_[:CT], res)
        q = each(lambda q_, r_: q_ + r_[CT:], q, res)
        pw = 2
        while pw < CT:
            w_hl = each(bdiag_hl, x)
            if 2 * pw >= CT:
                res = each(lambda p_, q_, w_: mm_hl(cat0(p_, q_), *w_), p, q, w_hl)
                q = each(lambda q_, r_: q_ + r_[CT:], q, res)
            else:
                res = each(lambda p_, x_, q_, w_: mm_hl(cat0(p_, x_, q_), *w_), p, x, q, w_hl)
                x = each(lambda r_: r_[CT:2 * CT], res)
                q = each(lambda q_, r_: q_ + r_[2 * CT:], q, res)
            p = each(lambda p_, r_: p_ + r_[:CT], p, res)
            pw *= 2
        tq = each(cat0, p, q)
        res = each(lambda m, k_: _dot(m.astype(BF16), bdiag(k_)), tq, a_ak)
        ty = each(lambda r_, k_: cat0(r_[:CT], r_[CT:] + k_), res, a_rk)
        o1 = each(lambda m, y_: apply(m, blockexp(y_)), tq, at)
        o2 = each(lambda m, y_: apply(m, blockexp(y_)), ty, vv)
        ect = each(lambda c_: jnp.exp(c_[CT - 1:CT, :]), cm)
        for u, rw in enumerate(rows):
            ah_ref[rw, :] = o1[u][:CT]
            rh_ref[rw, :] = rt[u] + o1[u][CT:]
            ul_ref[rw, :] = o2[u][:CT]
            yl_ref[rw, :] = o2[u][CT:]
            bh_ref[rw, :] = bt[u] * ect[u]
            kh_ref[rw, :] = kt[u] * ect[u]
            vo_ref[rw, :] = vv[u]
            dc_ref[rw, :] = jnp.broadcast_to(ect[u], (CT, 1024))
        return carry

    lax.fori_loop(0, NCH // U, chunks, 0)


def _rwkv_a_call(proj1, st_rkv, st_wa, wts, *, N, L, RB, CT, U):
    short = L == CT
    rows = N * L
    assert rows % RB == 0 and (RB // CT) % U == 0

    def col(cb, width=MIX_W):
        return pl.BlockSpec((RB, width), lambda i: (i, cb))

    def halo(cb, width=MIX_W):
        return pl.BlockSpec((8, width), lambda i: (jnp.maximum(i * (RB // 8) - 1, 0), cb))

    if short:
        st_specs = [pl.BlockSpec((RB // CT, 3072), lambda i: (i, 0)), pl.BlockSpec((RB // CT, 128), lambda i: (i, 0))]
    else:
        st_specs = [pl.BlockSpec((N, 3072), lambda i: (0, 0)), pl.BlockSpec((N, 128), lambda i: (0, 0))]

    def full(shape):
        return pl.BlockSpec(shape, lambda i: (0,) * len(shape))

    o_spec = pl.BlockSpec((RB, MIX_W), lambda i: (i, 0))
    if short:
        sh_specs = (pl.BlockSpec((RB // CT, 3072), lambda i: (i, 0)), pl.BlockSpec((RB // CT, 128), lambda i: (i, 0)))
    else:
        sh_specs = (pl.BlockSpec((N, 3072), lambda i: (0, 0)), pl.BlockSpec((N, 128), lambda i: (0, 0)))
    outs = pl.pallas_call(
        functools.partial(_rwkv_a_kernel, RB=RB, CT=CT, N=N, L=L, U=U),
        out_shape=((jax.ShapeDtypeStruct((rows, MIX_W), F32),) * N_RA_OUT
                   + (jax.ShapeDtypeStruct((N, 3072), F32), jax.ShapeDtypeStruct((N, 128), F32))),
        grid=(rows // RB,),
        in_specs=([col(P1_R // 1024), col(P1_RK // 1024), col(P1_RV // 1024), col(P1_WA // 128, 128),
                   halo(P1_R // 1024), halo(P1_RK // 1024), halo(P1_RV // 1024), halo(P1_WA // 128, 128)]
                  + st_specs
                  + [full((1, 3072)), full((1, 128)), full((1, 1024)), full((128, 1024)), full((1, 1024)),
                     full((128, 1024)), full((1, 1024)), full((1, 1024)), full((1, 1024))]),
        out_specs=(o_spec,) * N_RA_OUT + sh_specs,
        scratch_shapes=[pltpu.VMEM((RB, MIX_W), F32)] * 6,
        compiler_params=_cp(("arbitrary",)),
    )(*([proj1] * 8 + [st_rkv, st_wa] + list(wts)))
    return outs[:N_RA_OUT], jnp.concatenate(outs[N_RA_OUT:], axis=1)


def _rwkv_b_kernel(ah_ref, rh_ref, bh_ref, kh_ref, v_ref, ul_ref, yl_ref, dc_ref, bo_ref, z_ref, lg_ref, lb_ref,
                   s0_ref, y_ref, so_ref, sbd_ref, yb_ref, *, NBLK, TLB, CT, T):
    t = pl.program_id(1)
    bd_mask = (_shr(_iota((256, 256), 0), 6) == _shr(_iota((256, 256), 1), 6)).astype(F32)

    @pl.when(t == 0)
    def _():
        for nb in range(NBLK):
            for j in range(4):
                x = s0_ref[nb, 256 * j:256 * (j + 1), :]
                x = jnp.concatenate([x, x], axis=1)
                sbd_ref[4 * nb + j] = jnp.concatenate([x, x], axis=1) * bd_mask

    chains = [(nb, j, slice(256 * j, 256 * (j + 1))) for nb in range(NBLK) for j in range(4)]
    for c in range(TLB // CT):
        rows = slice(c * CT, (c + 1) * CT)
        sbs = [sbd_ref[4 * nb + j] for nb, j, cs in chains]
        outs = [_dot_nt(jnp.concatenate([ah_ref[nb, rows, cs], rh_ref[nb, rows, cs]], axis=0).astype(BF16),
                        sb.astype(BF16))
                for (nb, j, cs), sb in zip(chains, sbs)]
        upds = [_dot_tn(jnp.concatenate([o[:CT] + ul_ref[nb, rows, cs], v_ref[nb, rows, cs]],
                                        axis=0).astype(BF16),
                        jnp.concatenate([bh_ref[nb, rows, cs], kh_ref[nb, rows, cs]], axis=0).astype(BF16))
                for (nb, j, cs), o in zip(chains, outs)]
        for (nb, j, cs), sb, o, upd in zip(chains, sbs, outs, upds):
            yb_ref[nb, rows, cs] = o[CT:] + yl_ref[nb, rows, cs]
            sbd_ref[4 * nb + j] = sb * dc_ref[nb, c * CT:c * CT + 1, cs] + upd * bd_mask

    @pl.when(t == T - 1)
    def _():
        for nb in range(NBLK):
            outs = []
            for j in range(4):
                sb = sbd_ref[4 * nb + j]
                half = sb[:, :128] + sb[:, 128:]
                outs.append(half[:, :64] + half[:, 64:])
            so_ref[nb] = jnp.concatenate(outs, axis=0)

    cat = lambda ref: jnp.concatenate([ref[nb] for nb in range(NBLK)], axis=0)
    y = cat(yb_ref)
    mu = _segsum(y, R_K) * (1.0 / R_K)
    yc = y - mu
    var = _segsum(yc * yc, R_K) * (1.0 / R_K)
    yn = yc * lax.rsqrt(var + R_LN_EPS) * lg_ref[...] + lb_ref[...] + cat(bo_ref)
    out = yn * _silu(cat(z_ref))
    for nb in range(NBLK):
        y_ref[nb] = out[nb * TLB:(nb + 1) * TLB, :]


def _rwkv_b_call(ra, proj1, ln_g, ln_b, s0, *, N, L, NBLK, TLB, CT):
    T = L // TLB
    blk = lambda cb: pl.BlockSpec((NBLK, TLB, MIX_W), lambda i, t: (i, t, cb))
    s_spec = pl.BlockSpec((NBLK, 1024, 64), lambda i, t: (i, 0, 0))
    ra3 = [a.reshape(N, L, MIX_W) for a in ra]
    y, s_new = pl.pallas_call(
        functools.partial(_rwkv_b_kernel, NBLK=NBLK, TLB=TLB, CT=CT, T=T),
        out_shape=(jax.ShapeDtypeStruct((N, L, MIX_W), F32),
                   jax.ShapeDtypeStruct((N, 1024, 64), F32)),
        grid=(N // NBLK, T),
        in_specs=([blk(0)] * N_RA_OUT
                  + [blk(P1_ZD // 1024),
                     pl.BlockSpec((1, MIX_W), lambda i, t: (0, 0)),
                     pl.BlockSpec((1, MIX_W), lambda i, t: (0, 0)),
                     s_spec]),
        out_specs=(blk(0), s_spec),
        scratch_shapes=[pltpu.VMEM((4 * NBLK, 256, 256), F32),
                        pltpu.VMEM((NBLK, TLB, MIX_W), F32)],
        compiler_params=_cp(("arbitrary", "arbitrary")),
    )(*(ra3 + [proj1.reshape(N, L, P1_N), ln_g, ln_b, s0]))
    return y.reshape(N * L, MIX_W), s_new


W1_TR = 512


def _regroup_w1_kernel(w_ref, g_ref, o_ref):
    j = pl.program_id(0)
    last = P1_N // W1_TR - 1

    @pl.when(j < last)
    def _():
        o_ref[...] = w_ref[...].astype(BF16)

    @pl.when(j == last)
    def _():
        tail = jnp.concatenate([w_ref[0:128, :], g_ref[...], jnp.zeros((W1_TR - 136, D_MODEL), F32)], axis=0)
        o_ref[...] = tail.astype(BF16)


def _regroup_w1(w1):
    wt = w1.T

    def src_row(j):
        return jnp.where(j < 8, W1_TR * j,
                         jnp.where(j < 16, 4104 + W1_TR * (j - 8),
                                   jnp.where(j < 18, 8328 + W1_TR * (j - 16), 8200)))

    return pl.pallas_call(
        _regroup_w1_kernel,
        out_shape=jax.ShapeDtypeStruct((P1_N, D_MODEL), BF16),
        grid=(P1_N // W1_TR,),
        in_specs=[pl.BlockSpec((pl.Element(W1_TR), pl.Element(D_MODEL)),
                               lambda j: (pl.multiple_of(src_row(j), 8), 0)),
                  pl.BlockSpec((pl.Element(8), pl.Element(D_MODEL)), lambda j: (4096, 0))],
        out_specs=pl.BlockSpec((W1_TR, D_MODEL), lambda j: (j, 0)),
        compiler_params=_cp(("parallel",)),
    )(wt, wt)


def _s5_weights(lam_re, lam_im, log_dt, b_re, b_im, c_re, c_im):
    dt = jnp.exp(log_dt)[:, None]
    mag = jnp.exp(lam_re * dt)
    ar = mag * jnp.cos(lam_im * dt)
    ai = mag * jnp.sin(lam_im * dt)
    den = lam_re * lam_re + lam_im * lam_im
    qr = ((ar - 1.0) * lam_re + ai * lam_im) / den
    qi = (ai * lam_re - (ar - 1.0) * lam_im) / den
    bbr = qr[..., None] * b_re - qi[..., None] * b_im
    bbi = qr[..., None] * b_im + qi[..., None] * b_re
    hp = lax.Precision.HIGHEST
    spread_p = (jnp.arange(1024)[None, :] % S5_STATE == jnp.arange(S5_STATE)[:, None]).astype(F32)
    spread_h = (jnp.arange(256)[None, :] % S5_GROUP == jnp.arange(S5_GROUP)[:, None]).astype(F32)
    grp_in = (jnp.arange(256)[:, None] // S5_GROUP == jnp.arange(1024)[None, :] // S5_STATE)
    grp_out = (jnp.arange(1024)[:, None] // S5_STATE == jnp.arange(256)[None, :] // S5_GROUP)

    def in_blocks(bb):
        rows = bb.transpose(0, 2, 1).reshape(4, 256, S5_STATE)
        return jnp.matmul(rows, spread_p, precision=hp) * grp_in

    def out_blocks(cc):
        rows = cc.transpose(0, 2, 1).reshape(4, 1024, S5_GROUP)
        return jnp.matmul(rows, spread_h, precision=hp) * grp_out

    wb = jnp.concatenate([in_blocks(bbr), in_blocks(bbi)], axis=2).astype(BF16)
    wc = jnp.concatenate([out_blocks(c_re), out_blocks(-c_im)], axis=1).astype(BF16)
    pr, pi = jnp.ones_like(ar), jnp.zeros_like(ai)
    lag = []
    for _ in range(8):
        cpr = c_re * pr[:, None, :] - c_im * pi[:, None, :]
        cpi = c_re * pi[:, None, :] + c_im * pr[:, None, :]
        lag.append(jnp.einsum('gop,gph->goh', cpr, bbr, precision=lax.Precision.HIGHEST)
                   - jnp.einsum('gop,gph->goh', cpi, bbi, precision=lax.Precision.HIGHEST))
        pr, pi = pr * ar - pi * ai, pr * ai + pi * ar
    kd = jnp.stack(lag).reshape(8, 4, 16, S5_GROUP, S5_GROUP)
    kd = kd.transpose(1, 0, 2, 4, 3).reshape(4, 8, 256, S5_GROUP)
    spread = (jnp.arange(256)[None, :] % S5_GROUP == jnp.arange(S5_GROUP)[:, None]).astype(F32)
    same_group = (jnp.arange(256)[:, None] // S5_GROUP == jnp.arange(256)[None, :] // S5_GROUP)
    wk = jnp.matmul(kd, spread, precision=lax.Precision.HIGHEST) * same_group
    wk = wk.reshape(4, 8 * 256, 256).astype(BF16)
    return ar.reshape(1, 4096), ai.reshape(1, 4096), wb, wk, wc


CFG = {
    "P": dict(N=BATCH, L=P_LEN, tm_mm=1376, tm_mm1=2752, tm=688, tm_ln=512, drop_meta=True,
              conv=dict(NB=1, TL=344), s5=dict(NB=1, TL=688), mlstm=dict(NB=1, TL=344),
              ra=dict(RB=192, CT=16, U=12), rb=dict(NBLK=4, TLB=48, CT=16)),
    "S": dict(N=DEC_BATCH, L=DEC_SEQ, tm_mm=1024, tm_mm1=1024, tm=512, tm_ln=256, drop_meta=False,
              conv=dict(NB=16, TL=8), s5=dict(NB=32, TL=8), mlstm=dict(NB=4, TL=8),
              ra=dict(RB=256, CT=8, U=8), rb=dict(NBLK=8, TLB=8, CT=8)),
}


def _trunk(x, st, w, cfg):
    n, l = cfg["N"], cfg["L"]
    proj0 = _matmul(x, w["w_in0"], cfg["tm_mm"], 1024)
    mix_a, conv_new = _conv_call(proj0, st["conv"], w["conv_w"], w["conv_b"], w["a_ln_g"], w["a_ln_b"], w["pw"],
                                 N=n, L=l, **cfg["conv"])
    yb, xr, xi = _s5_call(proj0, w["s5_wb"], w["s5_wk"], w["s5_wc"], w["s5_d"], w["s5_ar"], w["s5_ai"],
                          st["ssm_re"].reshape(n, 1, 4096), st["ssm_im"].reshape(n, 1, 4096),
                          N=n, L=l, **cfg["s5"])
    mix_b = _glu_gate(yb, w["glu_w"], w["glu_b"], proj0, tm=cfg["tm_mm"])
    x1, x1_bf16 = _out_ln(x, mix_a, mix_b, w["w_out0"], w["ln_g0"], w["ln_b0"], tm=cfg["tm_ln"], with_bf16=True)

    proj1 = _matmul_nt(x1_bf16, w["w_in1"], cfg["tm_mm1"], 512)
    m0 = jnp.pad(jnp.broadcast_to(st["m"][:, :, None], (n, M_HEADS, 128)), ((0, 0), (0, 4), (0, 0)))
    mix_c, c_new, n_new, m_new = _mlstm_call(proj1, w["gate_b"], w["hn_g"], st["c"], st["n"], m0,
                                             N=n, L=l, **cfg["mlstm"])
    sh = st["shift"]
    ra, shift_new = _rwkv_a_call(proj1, sh[:, :3072], sh[:, 3072:], w["rwkv"], N=n, L=l, **cfg["ra"])
    mix_d, s_new = _rwkv_b_call(ra, proj1, w["r_ln_g"], w["r_ln_b"], st["s"].reshape(n, 1024, 64),
                                N=n, L=l, **cfg["rb"])
    final_ln = _out_ln_prompt if cfg["drop_meta"] else functools.partial(_out_ln, tm=cfg["tm_ln"])
    y = final_ln(x1, mix_c, mix_d, w["w_out1"], w["ln_g1"], w["ln_b1"])

    states = (conv_new[None],
              xr.reshape(n, S5_GROUPS, S5_STATE)[None],
              xi.reshape(n, S5_GROUPS, S5_STATE)[None],
              c_new[None], n_new[None], m_new[:, :M_HEADS, 0][None],
              s_new.reshape(n, R_HEADS, R_K, R_K)[None],
              shift_new[None])
    return y, states


def kernel(x_prompt, x_sample, state_conv, state_ssm_re, state_ssm_im, state_mlstm_c, state_mlstm_n, state_mlstm_m, state_rwkv_s, state_rwkv_shift, meta_tokens, ev_w_in, a_conv_w, a_conv_b, a_ln_g, a_ln_b, a_pw, s5_lambda_re, s5_lambda_im, s5_log_dt, s5_b_re, s5_b_im, s5_c_re, s5_c_im, s5_d, s5_glu_w, s5_glu_b, ev_w_out, ev_ln_g, ev_ln_b, od_w_in, m_ig_b, m_fg_b, m_hn_g, r_mu, r_w0, r_w2, r_a0, r_a2, r_kk, r_ka, r_rk, r_ln_g, r_ln_b, od_w_out, od_ln_g, od_ln_b):
    nb = x_prompt.shape[0]
    row = lambda vec: vec.reshape(1, -1)
    zeros = lambda *s: jnp.zeros(s, F32)

    ar, ai, wb, wk, wc = _s5_weights(s5_lambda_re[0], s5_lambda_im[0], s5_log_dt[0], s5_b_re[0], s5_b_im[0],
                                     s5_c_re[0], s5_c_im[0])
    w_in1 = _regroup_w1(od_w_in[0])
    mu = r_mu[0]
    w = dict(
        w_in0=ev_w_in[0].astype(BF16), conv_w=a_conv_w[0], conv_b=row(a_conv_b[0]),
        a_ln_g=row(a_ln_g[0]), a_ln_b=row(a_ln_b[0]), pw=a_pw[0].astype(BF16),
        s5_wb=wb, s5_wk=wk, s5_wc=wc, s5_d=row(s5_d[0]), s5_ar=ar, s5_ai=ai,
        glu_w=s5_glu_w[0].astype(BF16), glu_b=row(s5_glu_b[0]),
        w_out0=ev_w_out[0].astype(BF16), ln_g0=row(ev_ln_g[0]), ln_b0=row(ev_ln_b[0]),
        w_in1=w_in1,
        gate_b=jnp.concatenate([m_ig_b[0], m_fg_b[0], jnp.zeros((120,), F32)]).reshape(1, 128),
        hn_g=row(m_hn_g[0]),
        rwkv=[row(mu[:3072]), row(mu[3072:]), row(r_w0[0]),
              jnp.concatenate([r_w2[0], jnp.zeros((64, MIX_W), F32)], axis=0).astype(BF16),
              row(r_a0[0]),
              jnp.concatenate([jnp.zeros((64, MIX_W), F32), r_a2[0]], axis=0).astype(BF16),
              row(r_kk[0]), row(r_ka[0]), row(r_rk[0])],
        r_ln_g=row(r_ln_g[0]), r_ln_b=row(r_ln_b[0]),
        w_out1=od_w_out[0].astype(BF16), ln_g1=row(od_ln_g[0]), ln_b1=row(od_ln_b[0]),
    )

    x_p = jnp.concatenate([jnp.broadcast_to(meta_tokens[None], (nb, N_META, D_MODEL)), x_prompt],
                          axis=1).reshape(nb * P_LEN, D_MODEL)
    st_p = dict(conv=zeros(nb, CONV_W - 1, MIX_W), ssm_re=zeros(nb, 4096), ssm_im=zeros(nb, 4096),
                c=zeros(nb, M_HEADS, M_D, M_D), n=zeros(nb, M_HEADS, M_D), m=zeros(nb, M_HEADS),
                s=zeros(nb, R_HEADS, R_K, R_K), shift=zeros(nb, 3200))
    y_p, states_p = _trunk(x_p, st_p, w, CFG["P"])

    st_s = dict(conv=state_conv[0], ssm_re=state_ssm_re[0], ssm_im=state_ssm_im[0],
                c=state_mlstm_c[0], n=state_mlstm_n[0], m=state_mlstm_m[0],
                s=state_rwkv_s[0], shift=state_rwkv_shift[0])
    y_s, states_s = _trunk(x_sample.reshape(DEC_BATCH * DEC_SEQ, D_MODEL), st_s, w, CFG["S"])

    y_prompt = y_p
    y_sample = y_s.reshape(DEC_BATCH, DEC_SEQ, D_MODEL)
    return (y_prompt, y_sample) + states_p + states_s
```
